```python
import jax, jax.numpy as jnp
from jax import lax
import numpy as np

D_MODEL = 1024
BATCH = 16
SEQ = 4096
DEPTH = 2
DEC_BATCH = 8
DEC_SEQ = 32
PAST_LEN = 1024

CHUNK = 64
Q_BLOCK = 128
EPS = 1e-6
H_SB = 4
HD_SB = D_MODEL // 16
H_ML = 4
DK_ML = D_MODEL // 16
DV_ML = D_MODEL // 16
H_GDN = 4
DK_GDN = D_MODEL // 16
DV_GDN = D_MODEL // 16
CONV_W = 4
H_GLA = 4
DK_GLA = D_MODEL // 32
DV_GLA = D_MODEL // 16
GLA_RANK = 16
GLA_TAU = 16.0
D_FF = 4 * D_MODEL
W_SB = H_SB * HD_SB
W_ML = H_ML * DV_ML
W_GDN = H_GDN * DV_GDN
W_GLA = H_GLA * DV_GLA
GDN_CONV_DIM = 2 * H_GDN * DK_GDN + H_GDN * DV_GDN
IN_SIZES = (W_SB, W_SB, W_SB,
            H_ML * DK_ML, H_ML * DK_ML, W_ML, H_ML, H_ML, W_ML,
            GDN_CONV_DIM, H_GDN, H_GDN, W_GDN,
            H_GLA * DK_GLA, H_GLA * DK_GLA, W_GLA, GLA_RANK, W_GLA)
N_IN = sum(IN_SIZES)

kernel_name = 'hybrid_stream_encoder_step'


def rmsnorm(x, g):
    xf = x.astype(jnp.float32)
    y = xf * lax.rsqrt(jnp.mean(xf * xf, axis=-1, keepdims=True) + EPS)
    return (y * g.astype(jnp.float32)).astype(x.dtype)


def head_rmsnorm(x, g):
    B, L, H, d = x.shape
    y = x * lax.rsqrt(jnp.mean(x * x, axis=-1, keepdims=True) + EPS)
    return y.reshape(B, L, H * d) * g


def l2norm(x):
    return x * lax.rsqrt(jnp.sum(x * x, axis=-1, keepdims=True) + EPS)


def to_chunks(a, c):
    B, L = a.shape[0], a.shape[1]
    return jnp.swapaxes(a.reshape((B, L // c, c) + a.shape[2:]), 0, 1)


def from_chunks(a):
    nc, B, c = a.shape[:3]
    return jnp.swapaxes(a, 0, 1).reshape((B, nc * c) + a.shape[3:])


def sb_attend(q, k, v, q_pos, k_pos):
    d = q.shape[-1]
    z = jnp.einsum('bqhd,bkhd->bhqk', q, k) * (d ** -0.5)
    causal = k_pos[None, :] < q_pos[:, None]
    log_1mb = jnp.where(causal, jax.nn.log_sigmoid(-z), 0.0)
    csum = jnp.cumsum(log_1mb, axis=-1)
    log_a = jax.nn.log_sigmoid(z) + (csum[..., -1:] - csum)
    a = jnp.where(causal, jnp.exp(log_a), 0.0)
    return jnp.einsum('bhqk,bkhd->bqhd', a, v)


def sb_prompt(q, k, v):
    B, L, H, d = q.shape
    nb = L // Q_BLOCK
    qb = jnp.swapaxes(q.reshape(B, nb, Q_BLOCK, H, d), 0, 1)
    pos = jnp.arange(L, dtype=jnp.int32)
    qp = pos.reshape(nb, Q_BLOCK)
    out = lax.map(lambda a: sb_attend(a[0], k, v, a[1], pos), (qb, qp))
    return jnp.swapaxes(out, 0, 1).reshape(B, L, H, d)


def mlstm_chunked(q, k, v, i_pre, f_pre, C0, n0, m0, chunk):
    k = k * (k.shape[-1] ** -0.5)
    log_f = jax.nn.log_sigmoid(f_pre)
    incl = jnp.tril(jnp.ones((chunk, chunk), bool))

    def step(carry, xs):
        C, n, m = carry
        qc, kc, vc, ic, lfc = xs
        b = jnp.cumsum(lfc, axis=1).transpose(0, 2, 1)
        it = ic.transpose(0, 2, 1)
        log_d = jnp.where(incl, b[..., :, None] - b[..., None, :] + it[..., None, :], -jnp.inf)
        inter = b + m[..., None]
        m_t = jnp.maximum(inter, jnp.max(log_d, axis=-1))
        sc = jnp.einsum('bthd,bshd->bhts', qc, kc) * jnp.exp(log_d - m_t[..., None])
        si = jnp.exp(inter - m_t)
        num = jnp.einsum('bhts,bshv->bhtv', sc, vc) + si[..., None] * jnp.einsum('bthd,bhdv->bhtv', qc, C)
        den = jnp.sum(sc, axis=-1) + si * jnp.einsum('bthd,bhd->bht', qc, n)
        hc = num / jnp.maximum(jnp.abs(den), jnp.exp(-m_t))[..., None]
        b_end = b[..., -1]
        w_log = b_end[..., None] - b + it
        m_new = jnp.maximum(b_end + m, jnp.max(w_log, axis=-1))
        w = jnp.exp(w_log - m_new[..., None])
        dec = jnp.exp(b_end + m - m_new)
        C_new = dec[..., None, None] * C + jnp.einsum('bhs,bshd,bshv->bhdv', w, kc, vc)
        n_new = dec[..., None] * n + jnp.einsum('bhs,bshd->bhd', w, kc)
        return (C_new, n_new, m_new), hc.transpose(0, 2, 1, 3)

    xs = tuple(to_chunks(a, chunk) for a in (q, k, v, i_pre, log_f))
    (C, n, m), h = lax.scan(step, (C0, n0, m0), xs)
    return from_chunks(h), C, n, m


def gdn_chunked(q, k, v, beta, log_g, S0, chunk):
    strict = jnp.tril(jnp.ones((chunk, chunk), bool), -1)
    incl = jnp.tril(jnp.ones((chunk, chunk), bool))
    eye = jnp.eye(chunk, dtype=jnp.float32)

    def step(S, xs):
        qc, kc, vc, bc, gc = xs
        b = jnp.cumsum(gc, axis=1).transpose(0, 2, 1)
        bt = bc.transpose(0, 2, 1)
        diff = b[..., :, None] - b[..., None, :]
        kk = jnp.einsum('bthd,bshd->bhts', kc, kc)
        low = jnp.where(strict, bt[..., :, None] * jnp.exp(jnp.where(strict, diff, 0.0)) * kk, 0.0)
        rhs = bt[..., None] * (vc.transpose(0, 2, 1, 3) - jnp.exp(b)[..., None] * jnp.einsum('bthd,bhdv->bhtv', kc, S))
        u = lax.linalg.triangular_solve(low + eye, rhs, left_side=True, lower=True, unit_diagonal=True)
        qk = jnp.where(incl, jnp.exp(jnp.where(incl, diff, 0.0)) * jnp.einsum('bthd,bshd->bhts', qc, kc), 0.0)
        o = jnp.exp(b)[..., None] * jnp.einsum('bthd,bhdv->bhtv', qc, S) + jnp.einsum('bhts,bhsv->bhtv', qk, u)
        b_end = b[..., -1]
        S_new = jnp.exp(b_end)[..., None, None] * S + jnp.einsum('bhs,bshd,bhsv->bhdv', jnp.exp(b_end[..., None] - b), kc, u)
        return S_new, o.transpose(0, 2, 1, 3)

    xs = tuple(to_chunks(a, chunk) for a in (q, k, v, beta, log_g))
    S, o = lax.scan(step, S0, xs)
    return from_chunks(o), S


def gla_chunked(q, k, v, log_a, S0, chunk):
    incl = jnp.tril(jnp.ones((chunk, chunk), bool))[None, :, :, None, None]

    def step(S, xs):
        qc, kc, vc, gc = xs
        b = jnp.cumsum(gc, axis=1)
        dec = jnp.exp(jnp.where(incl, b[:, :, None] - b[:, None, :], -jnp.inf))
        a = jnp.einsum('bthd,bshd,btshd->bhts', qc, kc, dec)
        o = jnp.einsum('bhts,bshv->bthv', a, vc) + jnp.einsum('bthd,bhdv->bthv', qc * jnp.exp(b), S)
        b_end = b[:, -1]
        S_new = jnp.exp(b_end)[..., None] * S + jnp.einsum('bshd,bshv->bhdv', kc * jnp.exp(b_end[:, None] - b), vc)
        return S_new, o

    xs = tuple(to_chunks(a, chunk) for a in (q, k, v, log_a))
    S, o = lax.scan(step, S0, xs)
    return from_chunks(o), S


def causal_conv(x, buf, w):
    L = x.shape[1]
    xp = jnp.concatenate([buf, x], axis=1)
    y = xp[:, 0:L] * w[0]
    for i in range(1, CONV_W):
        y = y + xp[:, i:i + L] * w[i]
    return jax.nn.silu(y), xp[:, -(CONV_W - 1):]


def token_mixers(h, mw, states, chunk):
    (w_in, ml_ib, ml_fb, ml_ng, gdn_cw, gdn_alog, gdn_dtb, gdn_ng, gla_w2, gla_b, gla_ng, w_out) = mw
    (sb_k_past, sb_v_past, ml_C, ml_n, ml_m, gdn_S, gdn_buf, gla_S) = states
    B, L, _ = h.shape
    f32 = jnp.float32
    p = (h @ w_in).astype(f32)
    idx = np.cumsum(np.array(IN_SIZES))[:-1].tolist()
    (sq, sk, sv, mq, mk, mv, mi, mf, mo, gqkv, gb, ga, gz, lq, lk, lv, lg, lr) = jnp.split(p, idx, axis=-1)
    heads = lambda a, n: a.reshape(B, L, n, -1)
    q_sb, k_sb, v_sb = heads(sq, H_SB), heads(sk, H_SB), heads(sv, H_SB)
    if sb_k_past is None:
        o_sb = sb_prompt(q_sb, k_sb, v_sb)
    else:
        P = sb_k_past.shape[1]
        k_all = jnp.concatenate([sb_k_past.astype(f32), k_sb], axis=1)
        v_all = jnp.concatenate([sb_v_past.astype(f32), v_sb], axis=1)
        o_sb = sb_attend(q_sb, k_all, v_all, P + jnp.arange(L, dtype=jnp.int32), jnp.arange(P + L, dtype=jnp.int32))
    h_ml, ml_C, ml_n, ml_m = mlstm_chunked(heads(mq, H_ML), heads(mk, H_ML), heads(mv, H_ML),
                                           mi + ml_ib, mf + ml_fb, ml_C.astype(f32), ml_n.astype(f32),
                                           ml_m.astype(f32), chunk)
    o_ml = head_rmsnorm(h_ml, ml_ng) * jax.nn.sigmoid(mo)
    qkv, gdn_buf = causal_conv(gqkv, gdn_buf.astype(f32), gdn_cw)
    gq, gk, gv = jnp.split(qkv, [H_GDN * DK_GDN, 2 * H_GDN * DK_GDN], axis=-1)
    log_decay = -jnp.exp(gdn_alog) * jax.nn.softplus(ga + gdn_dtb)
    o_g, gdn_S = gdn_chunked(l2norm(heads(gq, H_GDN)) * (DK_GDN ** -0.5), l2norm(heads(gk, H_GDN)),
                             heads(gv, H_GDN), jax.nn.sigmoid(gb), log_decay, gdn_S.astype(f32), chunk)
    o_gdn = head_rmsnorm(o_g, gdn_ng) * jax.nn.silu(gz)
    log_a = jax.nn.log_sigmoid(lg @ gla_w2 + gla_b) / GLA_TAU
    o_l, gla_S = gla_chunked(heads(lq, H_GLA) * (DK_GLA ** -0.5), heads(lk, H_GLA), heads(lv, H_GLA),
                             heads(log_a, H_GLA), gla_S.astype(f32), chunk)
    o_gla = head_rmsnorm(o_l, gla_ng) * jax.nn.silu(lr)
    o = jnp.concatenate([o_sb.reshape(B, L, W_SB), o_ml, o_gdn, o_gla], axis=-1).astype(h.dtype)
    new_states = (k_sb.astype(h.dtype), v_sb.astype(h.dtype), ml_C, ml_n, ml_m, gdn_S, gdn_buf, gla_S)
    return o @ w_out, new_states


def encoder_layer(x, c, lw, states, chunk):
    (n1, n2, wada, bada, w1, w2) = lw[:4] + lw[-2:]
    mw = lw[4:-2]
    mod = jax.nn.silu(c) @ wada + bada
    sh1, sc1, g1, sh2, sc2, g2 = [m[:, None, :] for m in jnp.split(mod, 6, axis=-1)]
    h = rmsnorm(x, n1) * (1 + sc1) + sh1
    mix, new_states = token_mixers(h, mw, states, chunk)
    x = x + g1 * mix
    h = rmsnorm(x, n2) * (1 + sc2) + sh2
    x = x + g2 * (jnp.square(jax.nn.relu(h @ w1)) @ w2)
    return x, new_states


def setup_inputs(seed: int = 0) -> dict:
    key = jax.random.key(seed)
    ks = iter(jax.random.split(key, 40))
    nrm = lambda shape, s: s * jax.random.normal(next(ks), shape, jnp.float32)
    D = D_MODEL
    return {
        'x_prompt': nrm((BATCH, SEQ, D), 1.0),
        'x_sample': nrm((DEC_BATCH, DEC_SEQ, D), 1.0),
        'cache_sb_k': nrm((DEPTH, DEC_BATCH, PAST_LEN, H_SB, HD_SB), 1.0),
        'cache_sb_v': nrm((DEPTH, DEC_BATCH, PAST_LEN, H_SB, HD_SB), 1.0),
        'state_mlstm_C': nrm((DEPTH, DEC_BATCH, H_ML, DK_ML, DV_ML), 0.1),
        'state_mlstm_n': nrm((DEPTH, DEC_BATCH, H_ML, DK_ML), 0.1),
        'state_mlstm_m': nrm((DEPTH, DEC_BATCH, H_ML), 0.5),
        'state_gdn_S': nrm((DEPTH, DEC_BATCH, H_GDN, DK_GDN, DV_GDN), 0.1),
        'state_gdn_conv': nrm((DEPTH, DEC_BATCH, CONV_W - 1, GDN_CONV_DIM), 1.0),
        'state_gla_S': nrm((DEPTH, DEC_BATCH, H_GLA, DK_GLA, DV_GLA), 0.1),
        'c_prompt': nrm((BATCH, D), 1.0),
        'c_sample': nrm((DEC_BATCH, D), 1.0),
        'norm1_g': 1.0 + nrm((DEPTH, D), 0.02),
        'norm2_g': 1.0 + nrm((DEPTH, D), 0.02),
        'w_ada': nrm((DEPTH, D, 6 * D), 0.5 * D ** -0.5),
        'b_ada': nrm((DEPTH, 6 * D), 0.02),
        'w_in': nrm((DEPTH, D, N_IN), D ** -0.5),
        'mlstm_i_bias': nrm((DEPTH, H_ML), 0.1),
        'mlstm_f_bias': 4.0 + nrm((DEPTH, H_ML), 1.0),
        'mlstm_norm_g': 1.0 + nrm((DEPTH, W_ML), 0.02),
        'gdn_conv_w': nrm((DEPTH, CONV_W, GDN_CONV_DIM), CONV_W ** -0.5),
        'gdn_a_log': jnp.log(jax.random.uniform(next(ks), (DEPTH, H_GDN), jnp.float32, 1.0, 16.0)),
        'gdn_dt_bias': nrm((DEPTH, H_GDN), 0.1),
        'gdn_norm_g': 1.0 + nrm((DEPTH, W_GDN), 0.02),
        'gla_w_gate2': nrm((DEPTH, GLA_RANK, H_GLA * DK_GLA), GLA_RANK ** -0.5),
        'gla_gate_bias': nrm((DEPTH, H_GLA * DK_GLA), 0.1),
        'gla_norm_g': 1.0 + nrm((DEPTH, W_GLA), 0.02),
        'w_out': nrm((DEPTH, D, D), D ** -0.5),
        'w_ff1': nrm((DEPTH, D, D_FF), D ** -0.5),
        'w_ff2': nrm((DEPTH, D_FF, D), D_FF ** -0.5),
        'final_g': 1.0 + nrm((D,), 0.02),
    }


def reference(x_prompt, x_sample, cache_sb_k, cache_sb_v, state_mlstm_C, state_mlstm_n, state_mlstm_m,
              state_gdn_S, state_gdn_conv, state_gla_S, c_prompt, c_sample, norm1_g, norm2_g, w_ada, b_ada,
              w_in, mlstm_i_bias, mlstm_f_bias, mlstm_norm_g, gdn_conv_w, gdn_a_log, gdn_dt_bias, gdn_norm_g,
              gla_w_gate2, gla_gate_bias, gla_norm_g, w_out, w_ff1, w_ff2, final_g):
    f32 = jnp.float32
    Bp = x_prompt.shape[0]
    Ls = x_sample.shape[1]
    xp, xs = x_prompt, x_sample
    p_list, s_list = [], []
    for l in range(DEPTH):
        lw = (norm1_g[l], norm2_g[l], w_ada[l], b_ada[l], w_in[l], mlstm_i_bias[l], mlstm_f_bias[l],
              mlstm_norm_g[l], gdn_conv_w[l], gdn_a_log[l], gdn_dt_bias[l], gdn_norm_g[l], gla_w_gate2[l],
              gla_gate_bias[l], gla_norm_g[l], w_out[l], w_ff1[l], w_ff2[l])
        fresh = (None, None, jnp.zeros((Bp, H_ML, DK_ML, DV_ML), f32), jnp.zeros((Bp, H_ML, DK_ML), f32),
                 jnp.zeros((Bp, H_ML), f32), jnp.zeros((Bp, H_GDN, DK_GDN, DV_GDN), f32),
                 jnp.zeros((Bp, CONV_W - 1, GDN_CONV_DIM), f32), jnp.zeros((Bp, H_GLA, DK_GLA, DV_GLA), f32))
        past = (cache_sb_k[l], cache_sb_v[l], state_mlstm_C[l], state_mlstm_n[l], state_mlstm_m[l],
                state_gdn_S[l], state_gdn_conv[l], state_gla_S[l])
        xp, st_p = encoder_layer(xp, c_prompt, lw, fresh, CHUNK)
        xs, st_s = encoder_layer(xs, c_sample, lw, past, Ls)
        p_list.append(st_p)
        s_list.append(st_s)
    y_prompt = rmsnorm(xp, final_g)
    y_sample = rmsnorm(xs, final_g)
    p_sb_k, p_sb_v, p_ml_C, p_ml_n, p_ml_m, p_gdn_S, p_gdn_conv, p_gla_S = [jnp.stack([st[i] for st in p_list]) for i in range(8)]
    s_sb_k, s_sb_v, s_ml_C, s_ml_n, s_ml_m, s_gdn_S, s_gdn_conv, s_gla_S = [jnp.stack([st[i] for st in s_list]) for i in range(8)]
    return (y_prompt, y_sample, p_sb_k, p_sb_v, p_ml_C, p_ml_n, p_ml_m, p_gdn_S, p_gdn_conv, p_gla_S,
            s_sb_k, s_sb_v, s_ml_C, s_ml_n, s_ml_m, s_gdn_S, s_gdn_conv, s_gla_S)
```

```python
import functools

import jax
import jax.numpy as jnp
from jax import lax
from jax.experimental import pallas as pl
from jax.experimental.pallas import tpu as pltpu

F32 = jnp.float32
BF16 = jnp.bfloat16
EPS = 1e-6
N_HEADS = 4
HEAD_V = 64
GLA_DK = 32
CONV_W = 4
GLA_TAU = 16.0
W_GROUP = N_HEADS * HEAD_V
LANES = 128
VMEM_LIMIT = 56 * 1024 * 1024

PB_SQ, PB_MQ, PB_MK, PB_MV, PB_MO, PB_GQ, PB_GK, PB_GV, PB_GZ, PB_LV, PB_LR = range(11)
PB128_LQ, PB128_LK, PB128_GATES = 22, 23, 24
P_WIDTH = 25 * LANES
G_MI, G_MF, G_GB, G_GA, G_LG = 0, 4, 8, 12, 16
GLA_RANK = 16


def _dot(a, b):
    return jnp.dot(a.astype(BF16), b.astype(BF16), preferred_element_type=F32)


def _dot_nt(a, b):
    return lax.dot_general(a.astype(BF16), b.astype(BF16), (((1,), (1,)), ((), ())),
                           preferred_element_type=F32)


def _dot_tn(a, b):
    return lax.dot_general(a.astype(BF16), b.astype(BF16), (((0,), (0,)), ((), ())),
                           preferred_element_type=F32)


def _split3(x):
    x1 = x.astype(BF16)
    r1 = x - x1.astype(F32)
    x2 = r1.astype(BF16)
    x3 = (r1 - x2.astype(F32)).astype(BF16)
    return x1, x2, x3


def _split2(x):
    x1 = x.astype(BF16)
    x2 = (x - x1.astype(F32)).astype(BF16)
    return x1, x2


def _dot01_left(m01, x):
    return sum(jnp.dot(m01, p, preferred_element_type=F32) for p in _split3(x))


def _transpose_exact(x):
    eye = (lax.broadcasted_iota(jnp.int32, (LANES, LANES), 0)
           == lax.broadcasted_iota(jnp.int32, (LANES, LANES), 1)).astype(BF16)
    return sum(lax.dot_general(eye, p, (((1,), (1,)), ((), ())), preferred_element_type=F32)
               for p in _split3(x))


def _dot3x(a, b):
    a1, a2 = _split2(a)
    b1, b2 = _split2(b)
    d = lambda u, v: jnp.dot(u, v, preferred_element_type=F32)
    return d(a1, b1) + (d(a1, b2) + d(a2, b1))


def _log_sigmoid(x):
    return jnp.minimum(x, 0.0) - jnp.log(1.0 + jnp.exp(-jnp.abs(x)))


def _sigmoid(x):
    return 1.0 / (1.0 + jnp.exp(-x))


def _silu(x):
    return x * _sigmoid(x)


def _softplus(x):
    return jnp.maximum(x, 0.0) + jnp.log(1.0 + jnp.exp(-jnp.abs(x)))


def _tri(c, strict=False):
    r = lax.broadcasted_iota(jnp.int32, (c, c), 0)
    s = lax.broadcasted_iota(jnp.int32, (c, c), 1)
    return (r > s) if strict else (r >= s)


def _head_norm_gate(o, g_row, gate):
    y = o * lax.rsqrt(jnp.mean(o * o, axis=-1, keepdims=True) + EPS)
    return y * g_row * gate


def _hs(h, w=HEAD_V):
    return slice(h * w, (h + 1) * w)


def _ada_kernel(c_ref, w_ref, b_ref, o_ref):
    o_ref[0] = _dot(_silu(c_ref[...]), w_ref[0]) + b_ref[0]


def _ada_call(c_all, w_ada, b_ada, tn=512):
    depth, d, n = w_ada.shape
    rows = c_all.shape[0]
    return pl.pallas_call(
        _ada_kernel,
        grid=(depth, n // tn),
        in_specs=[pl.BlockSpec((rows, d), lambda l, j: (0, 0)),
                  pl.BlockSpec((1, d, tn), lambda l, j: (l, 0, j)),
                  pl.BlockSpec((1, 1, tn), lambda l, j: (l, 0, j))],
        out_specs=pl.BlockSpec((1, rows, tn), lambda l, j: (l, 0, j)),
        out_shape=jax.ShapeDtypeStruct((depth, rows, n), F32),
        compiler_params=pltpu.CompilerParams(dimension_semantics=("parallel", "parallel"),
                                             vmem_limit_bytes=VMEM_LIMIT),
        name="ada_mod",
    )(c_all, w_ada, b_ada.reshape(depth, 1, n))


def _modulated_norm(x, g_row, scale_row, shift_row):
    y = x * lax.rsqrt(jnp.mean(x * x, axis=-1, keepdims=True) + EPS) * g_row
    return y * (1.0 + scale_row) + shift_row


def _inproj_kernel(x_ref, mod_ref, g_ref, w_ref, k_ref, v_ref, p_ref, *, col_chunk):
    mod = mod_ref[0]
    h = _modulated_norm(x_ref[0], g_ref[...], mod[1:2], mod[0:1]).astype(BF16)
    k_ref[0] = jnp.dot(h, w_ref[:, 0:W_GROUP], preferred_element_type=F32)
    v_ref[0] = jnp.dot(h, w_ref[:, W_GROUP:2 * W_GROUP], preferred_element_type=F32)
    base = 2 * W_GROUP
    for n0 in range(0, P_WIDTH, col_chunk):
        n1 = min(n0 + col_chunk, P_WIDTH)
        p_ref[0, :, n0:n1] = jnp.dot(h, w_ref[:, base + n0:base + n1], preferred_element_type=F32)


def _inproj_call(x, mod, g, w_perm, tm):
    b, l, d = x.shape
    n_all = w_perm.shape[1]
    return pl.pallas_call(
        functools.partial(_inproj_kernel, col_chunk=512),
        grid=(b, l // tm),
        in_specs=[pl.BlockSpec((1, tm, d), lambda i, j: (i, j, 0)),
                  pl.BlockSpec((1, 6, d), lambda i, j: (i, 0, 0)),
                  pl.BlockSpec((1, d), lambda i, j: (0, 0)),
                  pl.BlockSpec((d, n_all), lambda i, j: (0, 0), pipeline_mode=pl.Buffered(1))],
        out_specs=[pl.BlockSpec((1, tm, W_GROUP), lambda i, j: (i, j, 0)),
                   pl.BlockSpec((1, tm, W_GROUP), lambda i, j: (i, j, 0)),
                   pl.BlockSpec((1, tm, P_WIDTH), lambda i, j: (i, j, 0))],
        out_shape=[jax.ShapeDtypeStruct((b, l, W_GROUP), F32),
                   jax.ShapeDtypeStruct((b, l, W_GROUP), F32),
                   jax.ShapeDtypeStruct((b, l, P_WIDTH), F32)],
        compiler_params=pltpu.CompilerParams(dimension_semantics=("parallel", "parallel"),
                                             vmem_limit_bytes=VMEM_LIMIT),
        name="norm_inproj",
    )(x, mod, g, w_perm)


def _sb_kernel(q_ref, k_ref, v_ref, o_ref, *, tq, tk, q_offset):
    i = pl.program_id(1)
    q0 = q_offset + i * tq
    n_kb = (q0 + tq + tk - 1) // tk
    rj = lax.broadcasted_iota(jnp.int32, (tk, 2 * tk), 0)
    cs = lax.broadcasted_iota(jnp.int32, (tk, 2 * tk), 1)
    suffix_and_total = ((rj > cs) | (cs >= tk)).astype(BF16)
    q_pos = q0 + lax.broadcasted_iota(jnp.int32, (tq, tk), 0)
    k_off = lax.broadcasted_iota(jnp.int32, (tq, tk), 1)
    q_all = q_ref[0] * (HEAD_V ** -0.5)
    for h in range(N_HEADS):
        qh = q_all[:, _hs(h)].astype(BF16)

        def body(t, carry, qh=qh, h=h):
            acc, run = carry
            j = n_kb - 1 - t
            s0 = pl.multiple_of(j * tk, tk)
            kb = k_ref[0, pl.ds(s0, tk), _hs(h)].astype(BF16)
            vb = v_ref[0, pl.ds(s0, tk), _hs(h)].astype(BF16)
            z = lax.dot_general(qh, kb, (((1,), (1,)), ((), ())), preferred_element_type=F32)
            causal = (s0 + k_off) < q_pos
            lg1m = jnp.where(causal, _log_sigmoid(-z), 0.0)
            hi, lo = _split2(lg1m)
            st = (jnp.dot(hi, suffix_and_total, preferred_element_type=F32)
                  + jnp.dot(lo, suffix_and_total, preferred_element_type=F32))
            log_a = (z + lg1m) + (st[:, :tk] + run)
            a = jnp.where(causal, jnp.exp(log_a), 0.0)
            acc = acc + jnp.dot(a.astype(BF16), vb, preferred_element_type=F32)
            return acc, run + st[:, tk:]

        acc, _ = lax.fori_loop(0, n_kb, body,
                               (jnp.zeros((tq, HEAD_V), F32), jnp.zeros((tq, tk), F32)))
        o_ref[0, :, _hs(h)] = acc


def _sb_call(q_src, q_block, k_all, v_all, l, tq, tk, q_offset):
    b = q_src.shape[0]
    lk = k_all.shape[1]
    return pl.pallas_call(
        functools.partial(_sb_kernel, tq=tq, tk=tk, q_offset=q_offset),
        grid=(b, l // tq),
        in_specs=[pl.BlockSpec((1, tq, W_GROUP), lambda i, j: (i, j, q_block)),
                  pl.BlockSpec((1, lk, W_GROUP), lambda i, j: (i, 0, 0)),
                  pl.BlockSpec((1, lk, W_GROUP), lambda i, j: (i, 0, 0))],
        out_specs=pl.BlockSpec((1, tq, W_GROUP), lambda i, j: (i, j, 0)),
        out_shape=jax.ShapeDtypeStruct((b, l, W_GROUP), F32),
        compiler_params=pltpu.CompilerParams(dimension_semantics=("parallel", "parallel"),
                                             vmem_limit_bytes=VMEM_LIMIT),
        name="sb_attention",
    )(q_src, k_all, v_all)


def _mlstm_kernel(q_ref, k_ref, v_ref, og_ref, g_ref, bias_ref, ng_ref, c0_ref, n0_ref, m0_ref,
                  h_ref, c_out, n_out, m_out, c_s, n_s, m_s, *, c):
    j = pl.program_id(1)

    @pl.when(j == 0)
    def _():
        c_s[...] = c0_ref[0]
        n_s[...] = n0_ref[0]
        m_s[...] = m0_ref[0]

    gates = g_ref[0] + bias_ref[...]
    lane = lax.broadcasted_iota(jnp.int32, gates.shape, 1)
    is_f = (lane >= G_MF) & (lane < G_MF + N_HEADS)
    x = jnp.where(is_f, _log_sigmoid(gates), gates)
    incl = _tri(c)
    csum = _dot01_left(incl.astype(BF16), x)
    x_t = _transpose_exact(x)
    csum_t = _transpose_exact(csum)
    q_all, k_all, v_all, og = q_ref[0], k_ref[0] * (HEAD_V ** -0.5), v_ref[0], og_ref[0]
    ng = ng_ref[...]
    for h in range(N_HEADS):
        bc = csum[:, G_MF + h:G_MF + h + 1]
        br = csum_t[G_MF + h:G_MF + h + 1, :]
        ic = x[:, G_MI + h:G_MI + h + 1]
        ir = x_t[G_MI + h:G_MI + h + 1, :]
        m_prev = m_s[0:1, h:h + 1]
        qh, kh, vh = q_all[:, _hs(h)], k_all[:, _hs(h)], v_all[:, _hs(h)]
        c_h = c_s[h]
        n_h = n_s[h:h + 1, :]
        log_d = jnp.where(incl, bc - br + ir, -jnp.inf)
        inter = bc + m_prev
        m_t = jnp.maximum(inter, jnp.max(log_d, axis=-1, keepdims=True))
        sc = _dot_nt(qh, kh) * jnp.exp(log_d - m_t)
        si = jnp.exp(inter - m_t)
        num = _dot(sc, vh) + si * _dot(qh, c_h)
        den = jnp.sum(sc, axis=-1, keepdims=True) + si * jnp.sum(qh * n_h, axis=-1, keepdims=True)
        hc = num / jnp.maximum(jnp.abs(den), jnp.exp(-m_t))
        b_end = bc[c - 1:c, :]
        m_new = jnp.maximum(b_end + m_prev, jnp.max(b_end - br + ir, axis=-1, keepdims=True))
        kw = kh * jnp.exp(b_end - bc + ic - m_new)
        dec = jnp.exp(b_end + m_prev - m_new)
        c_s[h] = dec * c_h + _dot_tn(kw, vh)
        n_s[h:h + 1, :] = dec * n_h + jnp.sum(kw, axis=0, keepdims=True)
        m_s[0:1, h:h + 1] = m_new
        h_ref[0, :, _hs(h)] = _head_norm_gate(hc, ng[:, _hs(h)], _sigmoid(og[:, _hs(h)]))

    @pl.when(j == pl.num_programs(1) - 1)
    def _():
        c_out[0] = c_s[...]
        n_out[0] = n_s[...]
        m_out[0] = m_s[...]


def _mlstm_call(p, bias_row, ng, c0, n0, m0, c):
    b, l, _ = p.shape
    nc = l // c
    blk = lambda cb: pl.BlockSpec((1, c, W_GROUP), lambda i, j, cb=cb: (i, j, cb))
    full = lambda a: pl.BlockSpec((1,) + a.shape[1:], lambda i, j: (i,) + (0,) * (a.ndim - 1))
    m0 = m0.reshape(b, 1, N_HEADS)
    out_shape = [jax.ShapeDtypeStruct((b, l, W_GROUP), F32),
                 jax.ShapeDtypeStruct(c0.shape, F32),
                 jax.ShapeDtypeStruct(n0.shape, F32),
                 jax.ShapeDtypeStruct(m0.shape, F32)]
    h, c_new, n_new, m_new = pl.pallas_call(
        functools.partial(_mlstm_kernel, c=c),
        grid=(b, nc),
        in_specs=[blk(PB_MQ), blk(PB_MK), blk(PB_MV), blk(PB_MO),
                  pl.BlockSpec((1, c, LANES), lambda i, j: (i, j, PB128_GATES)),
                  pl.BlockSpec((1, LANES), lambda i, j: (0, 0)),
                  pl.BlockSpec((1, W_GROUP), lambda i, j: (0, 0)),
                  full(c0), full(n0), full(m0)],
        out_specs=[pl.BlockSpec((1, c, W_GROUP), lambda i, j: (i, j, 0)),
                   full(c0), full(n0), full(m0)],
        out_shape=out_shape,
        scratch_shapes=[pltpu.VMEM(c0.shape[1:], F32), pltpu.VMEM(n0.shape[1:], F32),
                        pltpu.VMEM((1, N_HEADS), F32)],
        compiler_params=pltpu.CompilerParams(dimension_semantics=("parallel", "arbitrary"),
                                             vmem_limit_bytes=VMEM_LIMIT),
        name="mlstm",
    )(p, p, p, p, p, bias_row, ng, c0, n0, m0)
    return h, c_new, n_new, m_new.reshape(b, N_HEADS)


def _unit_lower_inverse(a, c):
    eye = (lax.broadcasted_iota(jnp.int32, (c, c), 0)
           == lax.broadcasted_iota(jnp.int32, (c, c), 1)).astype(F32)
    inv = eye - a
    power = a
    span = 2
    while span < c:
        power = _dot3x(power, power)
        inv = inv + _dot3x(inv, power)
        span *= 2
    return inv


def _gdn_kernel(q_ref, k_ref, v_ref, z_ref, g_ref, bias_ref, alog_ref, cw_ref, ng_ref, s0_ref, cb0_ref,
                o_ref, s_out, cb_out, s_s, prev_s, *, c):
    j = pl.program_id(1)

    @pl.when(j == 0)
    def _():
        s_s[...] = s0_ref[0]
        prev_s[...] = jnp.zeros_like(prev_s)
        prev_s[c - 8:c, :] = cb0_ref[0]

    x = jnp.concatenate([q_ref[0], k_ref[0], v_ref[0]], axis=-1)
    prev = prev_s[...]
    row = lax.broadcasted_iota(jnp.int32, x.shape, 0)
    cw = cw_ref[...]
    y = x * cw[CONV_W - 1:CONV_W, :]
    for d in range(1, CONV_W):
        shifted = jnp.where(row < d, pltpu.roll(prev, d, 0), pltpu.roll(x, d, 0))
        y = y + shifted * cw[CONV_W - 1 - d:CONV_W - d, :]
    prev_s[...] = x
    qkv = _silu(y)

    gates = g_ref[0] + bias_ref[...]
    lane = lax.broadcasted_iota(jnp.int32, gates.shape, 1)
    is_a = (lane >= G_GA) & (lane < G_GA + N_HEADS)
    xg = jnp.where(is_a, -jnp.exp(alog_ref[...]) * _softplus(gates), _sigmoid(gates))
    incl = _tri(c)
    strict = _tri(c, strict=True)
    csum = _dot01_left(incl.astype(BF16), xg)
    csum_t = _transpose_exact(csum)
    zg = z_ref[0]
    ng = ng_ref[...]
    for h in range(N_HEADS):
        qh = qkv[:, _hs(h)]
        kh = qkv[:, W_GROUP + h * HEAD_V:W_GROUP + (h + 1) * HEAD_V]
        vh = qkv[:, 2 * W_GROUP + h * HEAD_V:2 * W_GROUP + (h + 1) * HEAD_V]
        qh = qh * lax.rsqrt(jnp.sum(qh * qh, axis=-1, keepdims=True) + EPS) * (HEAD_V ** -0.5)
        kh = kh * lax.rsqrt(jnp.sum(kh * kh, axis=-1, keepdims=True) + EPS)
        beta = xg[:, G_GB + h:G_GB + h + 1]
        bc = csum[:, G_GA + h:G_GA + h + 1]
        br = csum_t[G_GA + h:G_GA + h + 1, :]
        s_h = s_s[h]
        decay = jnp.exp(jnp.where(incl, bc - br, 0.0))
        low = jnp.where(strict, beta * decay * _dot_nt(kh, kh), 0.0)
        eb = jnp.exp(bc)
        rhs = beta * (vh - eb * _dot(kh, s_h))
        u = _dot3x(_unit_lower_inverse(low, c), rhs)
        qk = jnp.where(incl, decay * _dot_nt(qh, kh), 0.0)
        o = eb * _dot(qh, s_h) + _dot(qk, u)
        b_end = bc[c - 1:c, :]
        s_s[h] = jnp.exp(b_end) * s_h + _dot_tn(kh * jnp.exp(b_end - bc), u)
        o_ref[0, :, _hs(h)] = _head_norm_gate(o, ng[:, _hs(h)], _silu(zg[:, _hs(h)]))

    @pl.when(j == pl.num_programs(1) - 1)
    def _():
        s_out[0] = s_s[...]
        cb_out[0] = prev_s[c - 8:c, :]


def _gdn_call(p, bias_row, alog_row, conv_w, ng, s0, conv0, c):
    b, l, _ = p.shape
    nc = l // c
    blk = lambda cb: pl.BlockSpec((1, c, W_GROUP), lambda i, j, cb=cb: (i, j, cb))
    full = lambda a: pl.BlockSpec((1,) + a.shape[1:], lambda i, j: (i,) + (0,) * (a.ndim - 1))
    conv_dim = conv_w.shape[1]
    conv0_pad = jnp.concatenate([jnp.zeros((b, 8 - (CONV_W - 1), conv_dim), F32), conv0], axis=1)
    o, s_new, conv_new = pl.pallas_call(
        functools.partial(_gdn_kernel, c=c),
        grid=(b, nc),
        in_specs=[blk(PB_GQ), blk(PB_GK), blk(PB_GV), blk(PB_GZ),
                  pl.BlockSpec((1, c, LANES), lambda i, j: (i, j, PB128_GATES)),
                  pl.BlockSpec((1, LANES), lambda i, j: (0, 0)),
                  pl.BlockSpec((1, LANES), lambda i, j: (0, 0)),
                  pl.BlockSpec((CONV_W, conv_dim), lambda i, j: (0, 0)),
                  pl.BlockSpec((1, W_GROUP), lambda i, j: (0, 0)),
                  full(s0), full(conv0_pad)],
        out_specs=[pl.BlockSpec((1, c, W_GROUP), lambda i, j: (i, j, 0)),
                   full(s0), full(conv0_pad)],
        out_shape=[jax.ShapeDtypeStruct((b, l, W_GROUP), F32),
                   jax.ShapeDtypeStruct(s0.shape, F32),
                   jax.ShapeDtypeStruct(conv0_pad.shape, F32)],
        scratch_shapes=[pltpu.VMEM(s0.shape[1:], F32), pltpu.VMEM((c, conv_dim), F32)],
        compiler_params=pltpu.CompilerParams(dimension_semantics=("parallel", "arbitrary"),
                                             vmem_limit_bytes=VMEM_LIMIT),
        name="gdn",
    )(p, p, p, p, p, bias_row, alog_row, conv_w, ng, s0, conv0_pad)
    return o, s_new, conv_new[:, 8 - (CONV_W - 1):, :]


def _gla_kernel(q_ref, k_ref, v_ref, r_ref, g_ref, w2_ref, gb_ref, ng_ref, st0_ref,
                o_ref, st_out, st_s, b_s, q_s, a_s, *, c):
    j = pl.program_id(1)
    dk_all = N_HEADS * GLA_DK

    @pl.when(j == 0)
    def _():
        st_s[...] = st0_ref[0]

    log_a = _log_sigmoid(_dot(g_ref[0], w2_ref[...]) + gb_ref[...]) / GLA_TAU
    incl = _tri(c)
    b = _dot01_left(incl.astype(BF16), log_a)
    q = q_ref[0] * (GLA_DK ** -0.5)
    k = k_ref[0]
    v = v_ref[0]
    b_s[...] = b
    q_s[...] = q
    hr = lax.broadcasted_iota(jnp.int32, (dk_all, W_GROUP), 0) // GLA_DK
    hc = lax.broadcasted_iota(jnp.int32, (dk_all, W_GROUP), 1) // HEAD_V
    head_sum = (hr == hc).astype(BF16)
    s_idx = lax.broadcasted_iota(jnp.int32, (c, dk_all), 0)

    def row_body(t, carry):
        bt = b_s[pl.ds(t, 1), :]
        qt = q_s[pl.ds(t, 1), :]
        w = jnp.where(s_idx <= t, jnp.exp(jnp.minimum(bt - b, 0.0)) * k * qt, 0.0)
        hi, lo = _split2(w)
        a_exp = (jnp.dot(hi, head_sum, preferred_element_type=F32)
                 + jnp.dot(lo, head_sum, preferred_element_type=F32))
        a_s[pl.ds(t, 1), :] = jnp.sum(a_exp * v, axis=0, keepdims=True)
        return carry

    lax.fori_loop(0, c, row_body, 0)
    st = st_s[...]
    o = a_s[...] + _dot_nt(q * jnp.exp(b), st)
    b_end = b[c - 1:c, :]
    blk_r = lax.broadcasted_iota(jnp.int32, st.shape, 0) // HEAD_V
    blk_c = lax.broadcasted_iota(jnp.int32, st.shape, 1) // GLA_DK
    upd = jnp.where(blk_r == blk_c, _dot_tn(v, k * jnp.exp(b_end - b)), 0.0)
    st_s[...] = st * jnp.exp(b_end) + upd
    r = r_ref[0]
    ng = ng_ref[...]
    for h in range(N_HEADS):
        o_ref[0, :, _hs(h)] = _head_norm_gate(o[:, _hs(h)], ng[:, _hs(h)], _silu(r[:, _hs(h)]))

    @pl.when(j == pl.num_programs(1) - 1)
    def _():
        st_out[0] = st_s[...]


def _gla_call(p, w2_pad, gb_row, ng, s0, c):
    b, l, _ = p.shape
    nc = l // c
    dk_all = N_HEADS * GLA_DK
    st0 = jnp.zeros((b, N_HEADS, HEAD_V, N_HEADS, GLA_DK), F32)
    st0 = st0.at[:, jnp.arange(N_HEADS), :, jnp.arange(N_HEADS), :].set(
        jnp.transpose(s0, (1, 0, 3, 2)))
    st0 = st0.reshape(b, W_GROUP, dk_all)
    blk = lambda cb: pl.BlockSpec((1, c, W_GROUP), lambda i, j, cb=cb: (i, j, cb))
    blk128 = lambda cb: pl.BlockSpec((1, c, LANES), lambda i, j, cb=cb: (i, j, cb))
    o, st_new = pl.pallas_call(
        functools.partial(_gla_kernel, c=c),
        grid=(b, nc),
        in_specs=[blk128(PB128_LQ), blk128(PB128_LK), blk(PB_LV), blk(PB_LR), blk128(PB128_GATES),
                  pl.BlockSpec((LANES, dk_all), lambda i, j: (0, 0)),
                  pl.BlockSpec((1, dk_all), lambda i, j: (0, 0)),
                  pl.BlockSpec((1, W_GROUP), lambda i, j: (0, 0)),
                  pl.BlockSpec((1, W_GROUP, dk_all), lambda i, j: (i, 0, 0))],
        out_specs=[pl.BlockSpec((1, c, W_GROUP), lambda i, j: (i, j, 0)),
                   pl.BlockSpec((1, W_GROUP, dk_all), lambda i, j: (i, 0, 0))],
        out_shape=[jax.ShapeDtypeStruct((b, l, W_GROUP), F32),
                   jax.ShapeDtypeStruct((b, W_GROUP, dk_all), F32)],
        scratch_shapes=[pltpu.VMEM((W_GROUP, dk_all), F32), pltpu.VMEM((c, dk_all), F32),
                        pltpu.VMEM((c, dk_all), F32), pltpu.VMEM((c, W_GROUP), F32)],
        compiler_params=pltpu.CompilerParams(dimension_semantics=("parallel", "arbitrary"),
                                             vmem_limit_bytes=VMEM_LIMIT),
        name="gla",
    )(p, p, p, p, p, w2_pad, gb_row, ng, st0)
    st5 = st_new.reshape(b, N_HEADS, HEAD_V, N_HEADS, GLA_DK)
    s_new = jnp.transpose(st5[:, jnp.arange(N_HEADS), :, jnp.arange(N_HEADS), :], (1, 0, 3, 2))
    return o, s_new


def _outffn_kernel(x_ref, a_ref, b_ref, c_ref, d_ref, mod_ref, g2_ref, gf_ref, wo_ref, w1_ref, w2_ref,
                   o_ref, *, ff_chunk, final):
    mod = mod_ref[0]
    mixed = jnp.concatenate([a_ref[0], b_ref[0], c_ref[0], d_ref[0]], axis=-1).astype(BF16)
    x = x_ref[0] + mod[2:3] * jnp.dot(mixed, wo_ref[...], preferred_element_type=F32)
    h = _modulated_norm(x, g2_ref[...], mod[4:5], mod[3:4]).astype(BF16)
    d_ff = w1_ref.shape[1]
    acc = jnp.zeros(x.shape, F32)
    for f0 in range(0, d_ff, ff_chunk):
        a = jnp.maximum(jnp.dot(h, w1_ref[:, f0:f0 + ff_chunk], preferred_element_type=F32), 0.0)
        acc = acc + jnp.dot((a * a).astype(BF16), w2_ref[f0:f0 + ff_chunk, :],
                            preferred_element_type=F32)
    x = x + mod[5:6] * acc
    if final:
        x = x * lax.rsqrt(jnp.mean(x * x, axis=-1, keepdims=True) + EPS) * gf_ref[...]
    o_ref[0] = x


def _outffn_call(x, mixers, mod, g2, gf, wo, w1, w2, tm, final):
    b, l, d = x.shape
    d_ff = w1.shape[1]
    tok = lambda w: pl.BlockSpec((1, tm, w), lambda i, j: (i, j, 0))
    const = lambda shape: pl.BlockSpec(shape, lambda i, j: (0,) * len(shape),
                                       pipeline_mode=pl.Buffered(1))
    return pl.pallas_call(
        functools.partial(_outffn_kernel, ff_chunk=1024, final=final),
        grid=(b, l // tm),
        in_specs=[tok(d), tok(W_GROUP), tok(W_GROUP), tok(W_GROUP), tok(W_GROUP),
                  pl.BlockSpec((1, 6, d), lambda i, j: (i, 0, 0)),
                  pl.BlockSpec((1, d), lambda i, j: (0, 0)),
                  pl.BlockSpec((1, d), lambda i, j: (0, 0)),
                  const((d, d)), const((d, d_ff)), const((d_ff, d))],
        out_specs=tok(d),
        out_shape=jax.ShapeDtypeStruct((b, l, d), F32),
        compiler_params=pltpu.CompilerParams(dimension_semantics=("parallel", "parallel"),
                                             vmem_limit_bytes=VMEM_LIMIT),
        name="outproj_ffn",
    )(x, *mixers, mod, g2, gf, wo, w1, w2)


def _permute_w_in(w_in):
    d = w_in.shape[0]
    sizes = (256, 256, 256, 256, 256, 256, 4, 4, 256, 768, 4, 4, 256, 128, 128, 256, 16, 256)
    offs = [0]
    for s in sizes:
        offs.append(offs[-1] + s)
    (sq, sk, sv, mq, mk, mv, mi, mf, mo, gqkv, gb, ga, gz, lq, lk, lv, lg, lr) = [
        w_in[:, offs[i]:offs[i + 1]] for i in range(len(sizes))]
    pad = jnp.zeros((d, LANES - (4 * N_HEADS + GLA_RANK)), w_in.dtype)
    return jnp.concatenate([sk, sv, sq, mq, mk, mv, mo, gqkv, gz, lv, lr, lq, lk,
                            mi, mf, gb, ga, lg, pad], axis=1)


def _row128(pieces):
    row = jnp.zeros((LANES,), F32)
    for off, vec in pieces:
        row = row.at[off:off + vec.shape[0]].set(vec)
    return row.reshape(1, LANES)


def _layer(x, mod, lw, states, chunk, tm, sb_tq, sb_tk, final, final_g):
    (n1, n2, w_in_p, gate_bias, ml_ng, conv_w, alog_row, gdn_ng, w2_pad, gla_gb, gla_ng,
     w_out, w_ff1, w_ff2) = lw
    (sb_k_past, sb_v_past, ml_c, ml_n, ml_m, gdn_s, gdn_buf, gla_s) = states
    b, l, _ = x.shape
    k_new, v_new, p = _inproj_call(x, mod, n1, w_in_p, tm)
    if sb_k_past is None:
        k_all, v_all, q_offset = k_new, v_new, 0
    else:
        past = sb_k_past.shape[1]
        lk = -(-(past + l) // sb_tk) * sb_tk
        padz = jnp.zeros((b, lk - past - l, W_GROUP), F32)
        k_all = jnp.concatenate([sb_k_past.reshape(b, past, W_GROUP), k_new, padz], axis=1)
        v_all = jnp.concatenate([sb_v_past.reshape(b, past, W_GROUP), v_new, padz], axis=1)
        q_offset = past
    o_sb = _sb_call(p, PB_SQ, k_all, v_all, l, sb_tq, sb_tk, q_offset)
    o_ml, ml_c, ml_n, ml_m = _mlstm_call(p, gate_bias, ml_ng, ml_c, ml_n, ml_m, chunk)
    o_gdn, gdn_s, gdn_buf = _gdn_call(p, gate_bias, alog_row, conv_w, gdn_ng, gdn_s, gdn_buf, chunk)
    o_gla, gla_s = _gla_call(p, w2_pad, gla_gb, gla_ng, gla_s, chunk)
    x = _outffn_call(x, (o_sb, o_ml, o_gdn, o_gla), mod, n2, final_g, w_out, w_ff1, w_ff2, tm, final)
    hk = lambda a: a.reshape(b, l, N_HEADS, HEAD_V)
    return x, (hk(k_new), hk(v_new), ml_c, ml_n, ml_m, gdn_s, gdn_buf, gla_s)


def kernel(x_prompt, x_sample, cache_sb_k, cache_sb_v, state_mlstm_C, state_mlstm_n, state_mlstm_m, state_gdn_S, state_gdn_conv, state_gla_S, c_prompt, c_sample, norm1_g, norm2_g, w_ada, b_ada, w_in, mlstm_i_bias, mlstm_f_bias, mlstm_norm_g, gdn_conv_w, gdn_a_log, gdn_dt_bias, gdn_norm_g, gla_w_gate2, gla_gate_bias, gla_norm_g, w_out, w_ff1, w_ff2, final_g):
    depth = w_in.shape[0]
    bp, lp, d = x_prompt.shape
    bs, ls, _ = x_sample.shape
    dk_all = N_HEADS * GLA_DK
    mod_all = _ada_call(jnp.concatenate([c_prompt, c_sample], axis=0), w_ada, b_ada)
    mod_all = mod_all.reshape(depth, bp + bs, 6, d)
    final_row = final_g.reshape(1, d)
    xp, xs = x_prompt, x_sample
    p_list, s_list = [], []
    for l in range(depth):
        gate_bias = _row128([(G_MI, mlstm_i_bias[l]), (G_MF, mlstm_f_bias[l]), (G_GA, gdn_dt_bias[l])])
        alog_row = _row128([(G_GA, gdn_a_log[l])])
        w2_pad = jnp.zeros((LANES, dk_all), F32).at[G_LG:G_LG + GLA_RANK, :].set(gla_w_gate2[l])
        lw = (norm1_g[l].reshape(1, d), norm2_g[l].reshape(1, d), _permute_w_in(w_in[l]).astype(BF16),
              gate_bias, mlstm_norm_g[l].reshape(1, W_GROUP), gdn_conv_w[l], alog_row,
              gdn_norm_g[l].reshape(1, W_GROUP), w2_pad.astype(BF16),
              gla_gate_bias[l].reshape(1, dk_all), gla_norm_g[l].reshape(1, W_GROUP),
              w_out[l].astype(BF16), w_ff1[l].astype(BF16), w_ff2[l].astype(BF16))
        fresh = (None, None, jnp.zeros((bp,) + state_mlstm_C.shape[2:], F32),
                 jnp.zeros((bp,) + state_mlstm_n.shape[2:], F32),
                 jnp.zeros((bp,) + state_mlstm_m.shape[2:], F32),
                 jnp.zeros((bp,) + state_gdn_S.shape[2:], F32),
                 jnp.zeros((bp,) + state_gdn_conv.shape[2:], F32),
                 jnp.zeros((bp,) + state_gla_S.shape[2:], F32))
        past = (cache_sb_k[l], cache_sb_v[l], state_mlstm_C[l], state_mlstm_n[l], state_mlstm_m[l],
                state_gdn_S[l], state_gdn_conv[l], state_gla_S[l])
        final = l == depth - 1
        xp, st_p = _layer(xp, mod_all[l, :bp], lw, fresh, 64, min(512, lp), min(128, lp), 128,
                          final, final_row)
        xs, st_s = _layer(xs, mod_all[l, bp:], lw, past, ls, ls, ls, 128, final, final_row)
        p_list.append(st_p)
        s_list.append(st_s)
    stacked_p = [jnp.stack([st[i] for st in p_list]) for i in range(8)]
    stacked_s = [jnp.stack([st[i] for st in s_list]) for i in range(8)]
    return (xp, xs, *stacked_p, *stacked_s)
```

```python
import functools

import jax
import jax.numpy as jnp
from jax import lax
from jax.experimental import pallas as pl
from jax.experimental.pallas import tpu as pltpu

F32 = jnp.float32
BF16 = jnp.bfloat16
EPS = 1e-6
N_HEADS = 4
HEAD_V = 64
GLA_DK = 32
CONV_W = 4
GLA_TAU = 16.0
W_GROUP = N_HEADS * HEAD_V
LANES = 128
VMEM_LIMIT = 56 * 1024 * 1024

PB_SQ, PB_MQ, PB_MK, PB_MV, PB_MO, PB_GQ, PB_GK, PB_GV, PB_GZ, PB_LV, PB_LR = range(11)
PB128_LQ, PB128_LK, PB128_GATES = 22, 23, 24
P_WIDTH = 25 * LANES
G_MI, G_MF, G_GB, G_GA, G_LG = 0, 4, 8, 12, 16
GLA_RANK = 16


def _dot(a, b):
    return jnp.dot(a.astype(BF16), b.astype(BF16), preferred_element_type=F32)


def _dot_nt(a, b):
    return lax.dot_general(a.astype(BF16), b.astype(BF16), (((1,), (1,)), ((), ())),
                           preferred_element_type=F32)


def _dot_tn(a, b):
    return lax.dot_general(a.astype(BF16), b.astype(BF16), (((0,), (0,)), ((), ())),
                           preferred_element_type=F32)


def _split3(x):
    x1 = x.astype(BF16)
    r1 = x - x1.astype(F32)
    x2 = r1.astype(BF16)
    x3 = (r1 - x2.astype(F32)).astype(BF16)
    return x1, x2, x3


def _split2(x):
    x1 = x.astype(BF16)
    x2 = (x - x1.astype(F32)).astype(BF16)
    return x1, x2


def _dot01_left(m01, x):
    return sum(jnp.dot(m01, p, preferred_element_type=F32) for p in _split3(x))


def _transpose_exact(x):
    eye = (lax.broadcasted_iota(jnp.int32, (LANES, LANES), 0)
           == lax.broadcasted_iota(jnp.int32, (LANES, LANES), 1)).astype(BF16)
    return sum(lax.dot_general(eye, p, (((1,), (1,)), ((), ())), preferred_element_type=F32)
               for p in _split3(x))


def _dot3x(a, b):
    a1, a2 = _split2(a)
    b1, b2 = _split2(b)
    d = lambda u, v: jnp.dot(u, v, preferred_element_type=F32)
    return d(a1, b1) + (d(a1, b2) + d(a2, b1))


def _log_sigmoid(x):
    return jnp.minimum(x, 0.0) - jnp.log(1.0 + jnp.exp(-jnp.abs(x)))


def _sigmoid(x):
    return 1.0 / (1.0 + jnp.exp(-x))


def _silu(x):
    return x * _sigmoid(x)


def _softplus(x):
    return jnp.maximum(x, 0.0) + jnp.log(1.0 + jnp.exp(-jnp.abs(x)))


def _tri(c, strict=False):
    r = lax.broadcasted_iota(jnp.int32, (c, c), 0)
    s = lax.broadcasted_iota(jnp.int32, (c, c), 1)
    return (r > s) if strict else (r >= s)


def _head_norm_gate(o, g_row, gate):
    y = o * lax.rsqrt(jnp.mean(o * o, axis=-1, keepdims=True) + EPS)
    return y * g_row * gate


def _hs(h, w=HEAD_V):
    return slice(h * w, (h + 1) * w)


def _ada_kernel(c_ref, w_ref, b_ref, o_ref):
    o_ref[0] = _dot(_silu(c_ref[...]), w_ref[0]) + b_ref[0]


def _ada_call(c_all, w_ada, b_ada, tn=512):
    depth, d, n = w_ada.shape
    rows = c_all.shape[0]
    return pl.pallas_call(
        _ada_kernel,
        grid=(depth, n // tn),
        in_specs=[pl.BlockSpec((rows, d), lambda l, j: (0, 0)),
                  pl.BlockSpec((1, d, tn), lambda l, j: (l, 0, j)),
                  pl.BlockSpec((1, 1, tn), lambda l, j: (l, 0, j))],
        out_specs=pl.BlockSpec((1, rows, tn), lambda l, j: (l, 0, j)),
        out_shape=jax.ShapeDtypeStruct((depth, rows, n), F32),
        compiler_params=pltpu.CompilerParams(dimension_semantics=("parallel", "parallel"),
                                             vmem_limit_bytes=VMEM_LIMIT),
        name="ada_mod",
    )(c_all, w_ada, b_ada.reshape(depth, 1, n))


def _modulated_norm(x, g_row, scale_row, shift_row):
    y = x * lax.rsqrt(jnp.mean(x * x, axis=-1, keepdims=True) + EPS) * g_row
    return y * (1.0 + scale_row) + shift_row


def _inproj_kernel(x_ref, mod_ref, g_ref, w_ref, k_ref, v_ref, p_ref, *, col_chunk):
    mod = mod_ref[0]
    h = _modulated_norm(x_ref[0], g_ref[...], mod[1:2], mod[0:1]).astype(BF16)
    k_ref[0] = jnp.dot(h, w_ref[:, 0:W_GROUP], preferred_element_type=F32)
    v_ref[0] = jnp.dot(h, w_ref[:, W_GROUP:2 * W_GROUP], preferred_element_type=F32)
    base = 2 * W_GROUP
    for n0 in range(0, P_WIDTH, col_chunk):
        n1 = min(n0 + col_chunk, P_WIDTH)
        p_ref[0, :, n0:n1] = jnp.dot(h, w_ref[:, base + n0:base + n1], preferred_element_type=F32)


def _inproj_call(x, mod, g, w_perm, tm):
    b, l, d = x.shape
    n_all = w_perm.shape[1]
    return pl.pallas_call(
        functools.partial(_inproj_kernel, col_chunk=512),
        grid=(b, l // tm),
        in_specs=[pl.BlockSpec((1, tm, d), lambda i, j: (i, j, 0)),
                  pl.BlockSpec((1, 6, d), lambda i, j: (i, 0, 0)),
                  pl.BlockSpec((1, d), lambda i, j: (0, 0)),
                  pl.BlockSpec((d, n_all), lambda i, j: (0, 0), pipeline_mode=pl.Buffered(1))],
        out_specs=[pl.BlockSpec((1, tm, W_GROUP), lambda i, j: (i, j, 0)),
                   pl.BlockSpec((1, tm, W_GROUP), lambda i, j: (i, j, 0)),
                   pl.BlockSpec((1, tm, P_WIDTH), lambda i, j: (i, j, 0))],
        out_shape=[jax.ShapeDtypeStruct((b, l, W_GROUP), F32),
                   jax.ShapeDtypeStruct((b, l, W_GROUP), F32),
                   jax.ShapeDtypeStruct((b, l, P_WIDTH), F32)],
        compiler_params=pltpu.CompilerParams(dimension_semantics=("parallel", "parallel"),
                                             vmem_limit_bytes=VMEM_LIMIT),
        name="norm_inproj",
    )(x, mod, g, w_perm)


SB_DEAD_LOG = -104.0


def _sb_kernel(q_ref, k_ref, v_ref, o_ref, *, tq, tk, q_offset):
    i = pl.program_id(1)
    q0 = q_offset + i * tq
    j_diag = q0 // tk
    rj = lax.broadcasted_iota(jnp.int32, (tk, 2 * tk), 0)
    cs = lax.broadcasted_iota(jnp.int32, (tk, 2 * tk), 1)
    suffix_and_total = ((rj > cs) | (cs >= tk)).astype(BF16)
    q_all = q_ref[0] * (HEAD_V ** -0.5)
    qs = [q_all[:, _hs(h)].astype(BF16) for h in range(N_HEADS)]

    def block(j, accs, runs, causal):
        s0 = pl.multiple_of(j * tk, tk)
        heads = range(N_HEADS)
        kbs = [k_ref[0, pl.ds(s0, tk), _hs(h)].astype(BF16) for h in heads]
        vbs = [v_ref[0, pl.ds(s0, tk), _hs(h)].astype(BF16) for h in heads]
        zs = [lax.dot_general(qs[h], kbs[h], (((1,), (1,)), ((), ())), preferred_element_type=F32)
              for h in heads]
        lg1ms = [_log_sigmoid(-z) for z in zs]
        if causal is not None:
            lg1ms = [jnp.where(causal, x, 0.0) for x in lg1ms]
        splits = [_split2(x) for x in lg1ms]
        sts = [jnp.dot(hi, suffix_and_total, preferred_element_type=F32)
               + jnp.dot(lo, suffix_and_total, preferred_element_type=F32) for hi, lo in splits]
        probs = [jnp.exp((zs[h] + lg1ms[h]) + (sts[h][:, :tk] + runs[h])) for h in heads]
        if causal is not None:
            probs = [jnp.where(causal, a, 0.0) for a in probs]
        new_accs = [accs[h] + jnp.dot(probs[h].astype(BF16), vbs[h], preferred_element_type=F32)
                    for h in heads]
        new_runs = [runs[h] + sts[h][:, tk:] for h in heads]
        return tuple(new_accs), tuple(new_runs)

    def live_of(runs):
        return jnp.max(jnp.maximum(jnp.maximum(runs[0], runs[1]), jnp.maximum(runs[2], runs[3])))

    q_pos = q0 + lax.broadcasted_iota(jnp.int32, (tq, tk), 0)
    k_pos = j_diag * tk + lax.broadcasted_iota(jnp.int32, (tq, tk), 1)
    zeros = lambda w: tuple(jnp.zeros((tq, w), F32) for _ in range(N_HEADS))
    accs, runs = block(j_diag, zeros(HEAD_V), zeros(tk), k_pos < q_pos)

    def cond(carry):
        j, live, _, _ = carry
        return (j >= 0) & (live > SB_DEAD_LOG)

    def body(carry):
        j, _, accs, runs = carry
        accs, runs = block(j, accs, runs, None)
        return j - 1, live_of(runs), accs, runs

    _, _, accs, _ = lax.while_loop(cond, body, (j_diag - 1, live_of(runs), accs, runs))
    for h in range(N_HEADS):
        o_ref[0, :, _hs(h)] = accs[h]


def _sb_call(q_src, q_block, k_all, v_all, l, tq, tk, q_offset):
    assert tk % tq == 0 and q_offset % tk == 0
    b = q_src.shape[0]
    lk = k_all.shape[1]
    return pl.pallas_call(
        functools.partial(_sb_kernel, tq=tq, tk=tk, q_offset=q_offset),
        grid=(b, l // tq),
        in_specs=[pl.BlockSpec((1, tq, W_GROUP), lambda i, j: (i, j, q_block)),
                  pl.BlockSpec((1, lk, W_GROUP), lambda i, j: (i, 0, 0)),
                  pl.BlockSpec((1, lk, W_GROUP), lambda i, j: (i, 0, 0))],
        out_specs=pl.BlockSpec((1, tq, W_GROUP), lambda i, j: (i, j, 0)),
        out_shape=jax.ShapeDtypeStruct((b, l, W_GROUP), F32),
        compiler_params=pltpu.CompilerParams(dimension_semantics=("parallel", "parallel"),
                                             vmem_limit_bytes=VMEM_LIMIT),
        name="sb_attention",
    )(q_src, k_all, v_all)


def _mlstm_kernel(q_ref, k_ref, v_ref, og_ref, g_ref, bias_ref, ng_ref, c0_ref, n0_ref, m0_ref,
                  h_ref, c_out, n_out, m_out, c_s, n_s, m_s, *, c):
    j = pl.program_id(1)

    @pl.when(j == 0)
    def _():
        c_s[...] = c0_ref[0]
        n_s[...] = n0_ref[0]
        m_s[...] = m0_ref[0]

    gates = g_ref[0] + bias_ref[...]
    lane = lax.broadcasted_iota(jnp.int32, gates.shape, 1)
    is_f = (lane >= G_MF) & (lane < G_MF + N_HEADS)
    x = jnp.where(is_f, _log_sigmoid(gates), gates)
    incl = _tri(c)
    csum = _dot01_left(incl.astype(BF16), x)
    x_t = _transpose_exact(x)
    csum_t = _transpose_exact(csum)
    q_all, k_all, v_all, og = q_ref[0], k_ref[0] * (HEAD_V ** -0.5), v_ref[0], og_ref[0]
    ng = ng_ref[...]
    heads = range(N_HEADS)
    bcs = [csum[:, G_MF + h:G_MF + h + 1] for h in heads]
    brs = [csum_t[G_MF + h:G_MF + h + 1, :] for h in heads]
    ics = [x[:, G_MI + h:G_MI + h + 1] for h in heads]
    irs = [x_t[G_MI + h:G_MI + h + 1, :] for h in heads]
    m_prevs = [m_s[0:1, h:h + 1] for h in heads]
    qs = [q_all[:, _hs(h)] for h in heads]
    ks = [k_all[:, _hs(h)] for h in heads]
    vs = [v_all[:, _hs(h)] for h in heads]
    c_hs = [c_s[h] for h in heads]
    n_hs = [n_s[h:h + 1, :] for h in heads]
    qk = [_dot_nt(qs[h], ks[h]) for h in heads]
    q_c = [_dot(qs[h], c_hs[h]) for h in heads]
    log_ds = [jnp.where(incl, bcs[h] - brs[h] + irs[h], -jnp.inf) for h in heads]
    inters = [bcs[h] + m_prevs[h] for h in heads]
    m_ts = [jnp.maximum(inters[h], jnp.max(log_ds[h], axis=-1, keepdims=True)) for h in heads]
    scs = [qk[h] * jnp.exp(log_ds[h] - m_ts[h]) for h in heads]
    sis = [jnp.exp(inters[h] - m_ts[h]) for h in heads]
    nums = [_dot(scs[h], vs[h]) + sis[h] * q_c[h] for h in heads]
    dens = [jnp.sum(scs[h], axis=-1, keepdims=True)
            + sis[h] * jnp.sum(qs[h] * n_hs[h], axis=-1, keepdims=True) for h in heads]
    hcs = [nums[h] / jnp.maximum(jnp.abs(dens[h]), jnp.exp(-m_ts[h])) for h in heads]
    b_ends = [bcs[h][c - 1:c, :] for h in heads]
    m_news = [jnp.maximum(b_ends[h] + m_prevs[h],
                          jnp.max(b_ends[h] - brs[h] + irs[h], axis=-1, keepdims=True)) for h in heads]
    kws = [ks[h] * jnp.exp(b_ends[h] - bcs[h] + ics[h] - m_news[h]) for h in heads]
    decs = [jnp.exp(b_ends[h] + m_prevs[h] - m_news[h]) for h in heads]
    c_upd = [_dot_tn(kws[h], vs[h]) for h in heads]
    for h in heads:
        c_s[h] = decs[h] * c_hs[h] + c_upd[h]
        n_s[h:h + 1, :] = decs[h] * n_hs[h] + jnp.sum(kws[h], axis=0, keepdims=True)
        m_s[0:1, h:h + 1] = m_news[h]
        h_ref[0, :, _hs(h)] = _head_norm_gate(hcs[h], ng[:, _hs(h)], _sigmoid(og[:, _hs(h)]))

    @pl.when(j == pl.num_programs(1) - 1)
    def _():
        c_out[0] = c_s[...]
        n_out[0] = n_s[...]
        m_out[0] = m_s[...]


def _mlstm_call(p, bias_row, ng, c0, n0, m0, c):
    b, l, _ = p.shape
    nc = l // c
    blk = lambda cb: pl.BlockSpec((1, c, W_GROUP), lambda i, j, cb=cb: (i, j, cb))
    full = lambda a: pl.BlockSpec((1,) + a.shape[1:], lambda i, j: (i,) + (0,) * (a.ndim - 1))
    m0 = m0.reshape(b, 1, N_HEADS)
    out_shape = [jax.ShapeDtypeStruct((b, l, W_GROUP), F32),
                 jax.ShapeDtypeStruct(c0.shape, F32),
                 jax.ShapeDtypeStruct(n0.shape, F32),
                 jax.ShapeDtypeStruct(m0.shape, F32)]
    h, c_new, n_new, m_new = pl.pallas_call(
        functools.partial(_mlstm_kernel, c=c),
        grid=(b, nc),
        in_specs=[blk(PB_MQ), blk(PB_MK), blk(PB_MV), blk(PB_MO),
                  pl.BlockSpec((1, c, LANES), lambda i, j: (i, j, PB128_GATES)),
                  pl.BlockSpec((1, LANES), lambda i, j: (0, 0)),
                  pl.BlockSpec((1, W_GROUP), lambda i, j: (0, 0)),
                  full(c0), full(n0), full(m0)],
        out_specs=[pl.BlockSpec((1, c, W_GROUP), lambda i, j: (i, j, 0)),
                   full(c0), full(n0), full(m0)],
        out_shape=out_shape,
        scratch_shapes=[pltpu.VMEM(c0.shape[1:], F32), pltpu.VMEM(n0.shape[1:], F32),
                        pltpu.VMEM((1, N_HEADS), F32)],
        compiler_params=pltpu.CompilerParams(dimension_semantics=("parallel", "arbitrary"),
                                             vmem_limit_bytes=VMEM_LIMIT),
        name="mlstm",
    )(p, p, p, p, p, bias_row, ng, c0, n0, m0)
    return h, c_new, n_new, m_new.reshape(b, N_HEADS)


def _unit_lower_inverses(mats, c):
    eye = (lax.broadcasted_iota(jnp.int32, (c, c), 0)
           == lax.broadcasted_iota(jnp.int32, (c, c), 1)).astype(F32)
    invs = [eye - a for a in mats]
    powers = list(mats)
    span = 2
    while span < c:
        powers = [_dot3x(p, p) for p in powers]
        invs = [t + _dot3x(t, p) for t, p in zip(invs, powers)]
        span *= 2
    return invs


def _gdn_kernel(q_ref, k_ref, v_ref, z_ref, g_ref, bias_ref, alog_ref, cw_ref, ng_ref, s0_ref, cb0_ref,
                o_ref, s_out, cb_out, s_s, prev_s, *, c):
    j = pl.program_id(1)

    @pl.when(j == 0)
    def _():
        s_s[...] = s0_ref[0]
        prev_s[...] = jnp.zeros_like(prev_s)
        prev_s[c - 8:c, :] = cb0_ref[0]

    x = jnp.concatenate([q_ref[0], k_ref[0], v_ref[0]], axis=-1)
    prev = prev_s[...]
    row = lax.broadcasted_iota(jnp.int32, x.shape, 0)
    cw = cw_ref[...]
    y = x * cw[CONV_W - 1:CONV_W, :]
    for d in range(1, CONV_W):
        shifted = jnp.where(row < d, pltpu.roll(prev, d, 0), pltpu.roll(x, d, 0))
        y = y + shifted * cw[CONV_W - 1 - d:CONV_W - d, :]
    prev_s[...] = x
    qkv = _silu(y)

    gates = g_ref[0] + bias_ref[...]
    lane = lax.broadcasted_iota(jnp.int32, gates.shape, 1)
    is_a = (lane >= G_GA) & (lane < G_GA + N_HEADS)
    xg = jnp.where(is_a, -jnp.exp(alog_ref[...]) * _softplus(gates), _sigmoid(gates))
    incl = _tri(c)
    strict = _tri(c, strict=True)
    csum = _dot01_left(incl.astype(BF16), xg)
    csum_t = _transpose_exact(csum)
    zg = z_ref[0]
    ng = ng_ref[...]
    heads = range(N_HEADS)
    l2 = lambda a: a * lax.rsqrt(jnp.sum(a * a, axis=-1, keepdims=True) + EPS)
    qs = [l2(qkv[:, _hs(h)]) * (HEAD_V ** -0.5) for h in heads]
    ks = [l2(qkv[:, W_GROUP + h * HEAD_V:W_GROUP + (h + 1) * HEAD_V]) for h in heads]
    vs = [qkv[:, 2 * W_GROUP + h * HEAD_V:2 * W_GROUP + (h + 1) * HEAD_V] for h in heads]
    betas = [xg[:, G_GB + h:G_GB + h + 1] for h in heads]
    bcs = [csum[:, G_GA + h:G_GA + h + 1] for h in heads]
    brs = [csum_t[G_GA + h:G_GA + h + 1, :] for h in heads]
    states = [s_s[h] for h in heads]
    decays = [jnp.exp(jnp.where(incl, bcs[h] - brs[h], 0.0)) for h in heads]
    kks = [_dot_nt(ks[h], ks[h]) for h in heads]
    qks = [_dot_nt(qs[h], ks[h]) for h in heads]
    k_s = [_dot(ks[h], states[h]) for h in heads]
    q_s = [_dot(qs[h], states[h]) for h in heads]
    lows = [jnp.where(strict, betas[h] * decays[h] * kks[h], 0.0) for h in heads]
    invs = _unit_lower_inverses(lows, c)
    ebs = [jnp.exp(bcs[h]) for h in heads]
    rhss = [betas[h] * (vs[h] - ebs[h] * k_s[h]) for h in heads]
    us = [_dot3x(invs[h], rhss[h]) for h in heads]
    qkm = [jnp.where(incl, decays[h] * qks[h], 0.0) for h in heads]
    outs = [ebs[h] * q_s[h] + _dot(qkm[h], us[h]) for h in heads]
    b_ends = [bcs[h][c - 1:c, :] for h in heads]
    upds = [_dot_tn(ks[h] * jnp.exp(b_ends[h] - bcs[h]), us[h]) for h in heads]
    for h in heads:
        s_s[h] = jnp.exp(b_ends[h]) * states[h] + upds[h]
        o_ref[0, :, _hs(h)] = _head_norm_gate(outs[h], ng[:, _hs(h)], _silu(zg[:, _hs(h)]))

    @pl.when(j == pl.num_programs(1) - 1)
    def _():
        s_out[0] = s_s[...]
        cb_out[0] = prev_s[c - 8:c, :]


def _gdn_call(p, bias_row, alog_row, conv_w, ng, s0, conv0, c):
    b, l, _ = p.shape
    nc = l // c
    blk = lambda cb: pl.BlockSpec((1, c, W_GROUP), lambda i, j, cb=cb: (i, j, cb))
    full = lambda a: pl.BlockSpec((1,) + a.shape[1:], lambda i, j: (i,) + (0,) * (a.ndim - 1))
    conv_dim = conv_w.shape[1]
    conv0_pad = jnp.concatenate([jnp.zeros((b, 8 - (CONV_W - 1), conv_dim), F32), conv0], axis=1)
    o, s_new, conv_new = pl.pallas_call(
        functools.partial(_gdn_kernel, c=c),
        grid=(b, nc),
        in_specs=[blk(PB_GQ), blk(PB_GK), blk(PB_GV), blk(PB_GZ),
                  pl.BlockSpec((1, c, LANES), lambda i, j: (i, j, PB128_GATES)),
                  pl.BlockSpec((1, LANES), lambda i, j: (0, 0)),
                  pl.BlockSpec((1, LANES), lambda i, j: (0, 0)),
                  pl.BlockSpec((CONV_W, conv_dim), lambda i, j: (0, 0)),
                  pl.BlockSpec((1, W_GROUP), lambda i, j: (0, 0)),
                  full(s0), full(conv0_pad)],
        out_specs=[pl.BlockSpec((1, c, W_GROUP), lambda i, j: (i, j, 0)),
                   full(s0), full(conv0_pad)],
        out_shape=[jax.ShapeDtypeStruct((b, l, W_GROUP), F32),
                   jax.ShapeDtypeStruct(s0.shape, F32),
                   jax.ShapeDtypeStruct(conv0_pad.shape, F32)],
        scratch_shapes=[pltpu.VMEM(s0.shape[1:], F32), pltpu.VMEM((c, conv_dim), F32)],
        compiler_params=pltpu.CompilerParams(dimension_semantics=("parallel", "arbitrary"),
                                             vmem_limit_bytes=VMEM_LIMIT),
        name="gdn",
    )(p, p, p, p, p, bias_row, alog_row, conv_w, ng, s0, conv0_pad)
    return o, s_new, conv_new[:, 8 - (CONV_W - 1):, :]


def _gla_kernel(q_ref, k_ref, v_ref, r_ref, g_ref, w2_ref, gb_ref, ng_ref, st0_ref,
                o_ref, st_out, st_s, b_s, q_s, a_s, *, c):
    j = pl.program_id(1)
    dk_all = N_HEADS * GLA_DK

    @pl.when(j == 0)
    def _():
        st_s[...] = st0_ref[0]

    log_a = _log_sigmoid(_dot(g_ref[0], w2_ref[...]) + gb_ref[...]) / GLA_TAU
    incl = _tri(c)
    b = _dot01_left(incl.astype(BF16), log_a)
    q = q_ref[0] * (GLA_DK ** -0.5)
    k = k_ref[0]
    v = v_ref[0]
    b_s[...] = b
    q_s[...] = q
    hr = lax.broadcasted_iota(jnp.int32, (dk_all, W_GROUP), 0) // GLA_DK
    hc = lax.broadcasted_iota(jnp.int32, (dk_all, W_GROUP), 1) // HEAD_V
    head_sum = (hr == hc).astype(BF16)
    s_idx = lax.broadcasted_iota(jnp.int32, (c, dk_all), 0)

    def row_body(t, carry):
        bt = b_s[pl.ds(t, 1), :]
        qt = q_s[pl.ds(t, 1), :]
        w = jnp.where(s_idx <= t, jnp.exp(jnp.minimum(bt - b, 0.0)) * k * qt, 0.0)
        hi, lo = _split2(w)
        a_exp = (jnp.dot(hi, head_sum, preferred_element_type=F32)
                 + jnp.dot(lo, head_sum, preferred_element_type=F32))
        a_s[pl.ds(t, 1), :] = jnp.sum(a_exp * v, axis=0, keepdims=True)
        return carry

    lax.fori_loop(0, c, row_body, 0, unroll=8)
    st = st_s[...]
    o = a_s[...] + _dot_nt(q * jnp.exp(b), st)
    b_end = b[c - 1:c, :]
    blk_r = lax.broadcasted_iota(jnp.int32, st.shape, 0) // HEAD_V
    blk_c = lax.broadcasted_iota(jnp.int32, st.shape, 1) // GLA_DK
    upd = jnp.where(blk_r == blk_c, _dot_tn(v, k * jnp.exp(b_end - b)), 0.0)
    st_s[...] = st * jnp.exp(b_end) + upd
    r = r_ref[0]
    ng = ng_ref[...]
    for h in range(N_HEADS):
        o_ref[0, :, _hs(h)] = _head_norm_gate(o[:, _hs(h)], ng[:, _hs(h)], _silu(r[:, _hs(h)]))

    @pl.when(j == pl.num_programs(1) - 1)
    def _():
        st_out[0] = st_s[...]


def _gla_call(p, w2_pad, gb_row, ng, s0, c):
    b, l, _ = p.shape
    nc = l // c
    dk_all = N_HEADS * GLA_DK
    st0 = jnp.zeros((b, N_HEADS, HEAD_V, N_HEADS, GLA_DK), F32)
    st0 = st0.at[:, jnp.arange(N_HEADS), :, jnp.arange(N_HEADS), :].set(
        jnp.transpose(s0, (1, 0, 3, 2)))
    st0 = st0.reshape(b, W_GROUP, dk_all)
    blk = lambda cb: pl.BlockSpec((1, c, W_GROUP), lambda i, j, cb=cb: (i, j, cb))
    blk128 = lambda cb: pl.BlockSpec((1, c, LANES), lambda i, j, cb=cb: (i, j, cb))
    o, st_new = pl.pallas_call(
        functools.partial(_gla_kernel, c=c),
        grid=(b, nc),
        in_specs=[blk128(PB128_LQ), blk128(PB128_LK), blk(PB_LV), blk(PB_LR), blk128(PB128_GATES),
                  pl.BlockSpec((LANES, dk_all), lambda i, j: (0, 0)),
                  pl.BlockSpec((1, dk_all), lambda i, j: (0, 0)),
                  pl.BlockSpec((1, W_GROUP), lambda i, j: (0, 0)),
                  pl.BlockSpec((1, W_GROUP, dk_all), lambda i, j: (i, 0, 0))],
        out_specs=[pl.BlockSpec((1, c, W_GROUP), lambda i, j: (i, j, 0)),
                   pl.BlockSpec((1, W_GROUP, dk_all), lambda i, j: (i, 0, 0))],
        out_shape=[jax.ShapeDtypeStruct((b, l, W_GROUP), F32),
                   jax.ShapeDtypeStruct((b, W_GROUP, dk_all), F32)],
        scratch_shapes=[pltpu.VMEM((W_GROUP, dk_all), F32), pltpu.VMEM((c, dk_all), F32),
                        pltpu.VMEM((c, dk_all), F32), pltpu.VMEM((c, W_GROUP), F32)],
        compiler_params=pltpu.CompilerParams(dimension_semantics=("parallel", "arbitrary"),
                                             vmem_limit_bytes=VMEM_LIMIT),
        name="gla",
    )(p, p, p, p, p, w2_pad, gb_row, ng, st0)
    st5 = st_new.reshape(b, N_HEADS, HEAD_V, N_HEADS, GLA_DK)
    s_new = jnp.transpose(st5[:, jnp.arange(N_HEADS), :, jnp.arange(N_HEADS), :], (1, 0, 3, 2))
    return o, s_new


def _outffn_kernel(x_ref, a_ref, b_ref, c_ref, d_ref, mod_ref, g2_ref, gf_ref, wo_ref, w1_ref, w2_ref,
                   o_ref, *, ff_chunk, final):
    mod = mod_ref[0]
    mixed = jnp.concatenate([a_ref[0], b_ref[0], c_ref[0], d_ref[0]], axis=-1).astype(BF16)
    x = x_ref[0] + mod[2:3] * jnp.dot(mixed, wo_ref[...], preferred_element_type=F32)
    h = _modulated_norm(x, g2_ref[...], mod[4:5], mod[3:4]).astype(BF16)
    d_ff = w1_ref.shape[1]
    acc = jnp.zeros(x.shape, F32)
    for f0 in range(0, d_ff, ff_chunk):
        a = jnp.maximum(jnp.dot(h, w1_ref[:, f0:f0 + ff_chunk], preferred_element_type=F32), 0.0)
        acc = acc + jnp.dot((a * a).astype(BF16), w2_ref[f0:f0 + ff_chunk, :],
                            preferred_element_type=F32)
    x = x + mod[5:6] * acc
    if final:
        x = x * lax.rsqrt(jnp.mean(x * x, axis=-1, keepdims=True) + EPS) * gf_ref[...]
    o_ref[0] = x


def _outffn_call(x, mixers, mod, g2, gf, wo, w1, w2, tm, final):
    b, l, d = x.shape
    d_ff = w1.shape[1]
    tok = lambda w: pl.BlockSpec((1, tm, w), lambda i, j: (i, j, 0))
    const = lambda shape: pl.BlockSpec(shape, lambda i, j: (0,) * len(shape),
                                       pipeline_mode=pl.Buffered(1))
    return pl.pallas_call(
        functools.partial(_outffn_kernel, ff_chunk=1024, final=final),
        grid=(b, l // tm),
        in_specs=[tok(d), tok(W_GROUP), tok(W_GROUP), tok(W_GROUP), tok(W_GROUP),
                  pl.BlockSpec((1, 6, d), lambda i, j: (i, 0, 0)),
                  pl.BlockSpec((1, d), lambda i, j: (0, 0)),
                  pl.BlockSpec((1, d), lambda i, j: (0, 0)),
                  const((d, d)), const((d, d_ff)), const((d_ff, d))],
        out_specs=tok(d),
        out_shape=jax.ShapeDtypeStruct((b, l, d), F32),
        compiler_params=pltpu.CompilerParams(dimension_semantics=("parallel", "parallel"),
                                             vmem_limit_bytes=VMEM_LIMIT),
        name="outproj_ffn",
    )(x, *mixers, mod, g2, gf, wo, w1, w2)


def _permute_w_in(w_in):
    d = w_in.shape[0]
    sizes = (256, 256, 256, 256, 256, 256, 4, 4, 256, 768, 4, 4, 256, 128, 128, 256, 16, 256)
    offs = [0]
    for s in sizes:
        offs.append(offs[-1] + s)
    (sq, sk, sv, mq, mk, mv, mi, mf, mo, gqkv, gb, ga, gz, lq, lk, lv, lg, lr) = [
        w_in[:, offs[i]:offs[i + 1]] for i in range(len(sizes))]
    pad = jnp.zeros((d, LANES - (4 * N_HEADS + GLA_RANK)), w_in.dtype)
    return jnp.concatenate([sk, sv, sq, mq, mk, mv, mo, gqkv, gz, lv, lr, lq, lk,
                            mi, mf, gb, ga, lg, pad], axis=1)


def _row128(pieces):
    row = jnp.zeros((LANES,), F32)
    for off, vec in pieces:
        row = row.at[off:off + vec.shape[0]].set(vec)
    return row.reshape(1, LANES)


def _layer(x, mod, lw, states, chunk, tm, sb_tq, sb_tk, final, final_g):
    (n1, n2, w_in_p, gate_bias, ml_ng, conv_w, alog_row, gdn_ng, w2_pad, gla_gb, gla_ng,
     w_out, w_ff1, w_ff2) = lw
    (sb_k_past, sb_v_past, ml_c, ml_n, ml_m, gdn_s, gdn_buf, gla_s) = states
    b, l, _ = x.shape
    k_new, v_new, p = _inproj_call(x, mod, n1, w_in_p, tm)
    if sb_k_past is None:
        k_all, v_all, q_offset = k_new, v_new, 0
    else:
        past = sb_k_past.shape[1]
        lk = -(-(past + l) // sb_tk) * sb_tk
        padz = jnp.zeros((b, lk - past - l, W_GROUP), F32)
        k_all = jnp.concatenate([sb_k_past.reshape(b, past, W_GROUP), k_new, padz], axis=1)
        v_all = jnp.concatenate([sb_v_past.reshape(b, past, W_GROUP), v_new, padz], axis=1)
        q_offset = past
    o_sb = _sb_call(p, PB_SQ, k_all, v_all, l, sb_tq, sb_tk, q_offset)
    o_ml, ml_c, ml_n, ml_m = _mlstm_call(p, gate_bias, ml_ng, ml_c, ml_n, ml_m, chunk)
    o_gdn, gdn_s, gdn_buf = _gdn_call(p, gate_bias, alog_row, conv_w, gdn_ng, gdn_s, gdn_buf, chunk)
    o_gla, gla_s = _gla_call(p, w2_pad, gla_gb, gla_ng, gla_s, chunk)
    x = _outffn_call(x, (o_sb, o_ml, o_gdn, o_gla), mod, n2, final_g, w_out, w_ff1, w_ff2, tm, final)
    hk = lambda a: a.reshape(b, l, N_HEADS, HEAD_V)
    return x, (hk(k_new), hk(v_new), ml_c, ml_n, ml_m, gdn_s, gdn_buf, gla_s)


def kernel(x_prompt, x_sample, cache_sb_k, cache_sb_v, state_mlstm_C, state_mlstm_n, state_mlstm_m, state_gdn_S, state_gdn_conv, state_gla_S, c_prompt, c_sample, norm1_g, norm2_g, w_ada, b_ada, w_in, mlstm_i_bias, mlstm_f_bias, mlstm_norm_g, gdn_conv_w, gdn_a_log, gdn_dt_bias, gdn_norm_g, gla_w_gate2, gla_gate_bias, gla_norm_g, w_out, w_ff1, w_ff2, final_g):
    depth = w_in.shape[0]
    bp, lp, d = x_prompt.shape
    bs, ls, _ = x_sample.shape
    dk_all = N_HEADS * GLA_DK
    mod_all = _ada_call(jnp.concatenate([c_prompt, c_sample], axis=0), w_ada, b_ada)
    mod_all = mod_all.reshape(depth, bp + bs, 6, d)
    final_row = final_g.reshape(1, d)
    xp, xs = x_prompt, x_sample
    p_list, s_list = [], []
    for l in range(depth):
        gate_bias = _row128([(G_MI, mlstm_i_bias[l]), (G_MF, mlstm_f_bias[l]), (G_GA, gdn_dt_bias[l])])
        alog_row = _row128([(G_GA, gdn_a_log[l])])
        w2_pad = jnp.zeros((LANES, dk_all), F32).at[G_LG:G_LG + GLA_RANK, :].set(gla_w_gate2[l])
        lw = (norm1_g[l].reshape(1, d), norm2_g[l].reshape(1, d), _permute_w_in(w_in[l]).astype(BF16),
              gate_bias, mlstm_norm_g[l].reshape(1, W_GROUP), gdn_conv_w[l], alog_row,
              gdn_norm_g[l].reshape(1, W_GROUP), w2_pad.astype(BF16),
              gla_gate_bias[l].reshape(1, dk_all), gla_norm_g[l].reshape(1, W_GROUP),
              w_out[l].astype(BF16), w_ff1[l].astype(BF16), w_ff2[l].astype(BF16))
        fresh = (None, None, jnp.zeros((bp,) + state_mlstm_C.shape[2:], F32),
                 jnp.zeros((bp,) + state_mlstm_n.shape[2:], F32),
                 jnp.zeros((bp,) + state_mlstm_m.shape[2:], F32),
                 jnp.zeros((bp,) + state_gdn_S.shape[2:], F32),
                 jnp.zeros((bp,) + state_gdn_conv.shape[2:], F32),
                 jnp.zeros((bp,) + state_gla_S.shape[2:], F32))
        past = (cache_sb_k[l], cache_sb_v[l], state_mlstm_C[l], state_mlstm_n[l], state_mlstm_m[l],
                state_gdn_S[l], state_gdn_conv[l], state_gla_S[l])
        final = l == depth - 1
        xp, st_p = _layer(xp, mod_all[l, :bp], lw, fresh, 64, min(512, lp), min(128, lp), 128,
                          final, final_row)
        xs, st_s = _layer(xs, mod_all[l, bp:], lw, past, ls, ls, ls, 128, final, final_row)
        p_list.append(st_p)
        s_list.append(st_s)
    stacked_p = [jnp.stack([st[i] for st in p_list]) for i in range(8)]
    stacked_s = [jnp.stack([st[i] for st in s_list]) for i in range(8)]
    return (xp, xs, *stacked_p, *stacked_s)
```

```python
import functools
import math

import jax
import jax.numpy as jnp
from jax import lax
from jax.experimental import pallas as pl
from jax.experimental.pallas import tpu as pltpu

F32 = jnp.float32
BF16 = jnp.bfloat16
EPS = 1e-6
N_HEADS = 4
HEAD_V = 64
GLA_DK = 32
CONV_W = 4
GLA_TAU = 16.0
W_GROUP = N_HEADS * HEAD_V
LANES = 128
VMEM_LIMIT = 56 * 1024 * 1024
MLSTM_SEQS_PER_STEP = 1
GDN_SEQS_PER_STEP = 4
GLA_SEQS_PER_STEP = 4

PB_SQ, PB_MQ, PB_MK, PB_MV, PB_MO, PB_GQ, PB_GK, PB_GV, PB_GZ, PB_LV, PB_LR = range(11)
PB128_LQ, PB128_LK, PB128_GATES = 22, 23, 24
P_WIDTH = 25 * LANES
G_MI, G_MF, G_GB, G_GA, G_LG = 0, 4, 8, 12, 16
GLA_RANK = 16


def _dot(a, b):
    return jnp.dot(a.astype(BF16), b.astype(BF16), preferred_element_type=F32)


def _dot_nt(a, b):
    return lax.dot_general(a.astype(BF16), b.astype(BF16), (((1,), (1,)), ((), ())),
                           preferred_element_type=F32)


def _dot_tn(a, b):
    return lax.dot_general(a.astype(BF16), b.astype(BF16), (((0,), (0,)), ((), ())),
                           preferred_element_type=F32)


def _split3(x):
    x1 = x.astype(BF16)
    r1 = x - x1.astype(F32)
    x2 = r1.astype(BF16)
    x3 = (r1 - x2.astype(F32)).astype(BF16)
    return x1, x2, x3


def _split2(x):
    x1 = x.astype(BF16)
    x2 = (x - x1.astype(F32)).astype(BF16)
    return x1, x2


def _dot01_left(m01, x):
    return sum(jnp.dot(m01, p, preferred_element_type=F32) for p in _split3(x))


def _transpose_exact(x):
    eye = (lax.broadcasted_iota(jnp.int32, (LANES, LANES), 0)
           == lax.broadcasted_iota(jnp.int32, (LANES, LANES), 1)).astype(BF16)
    return sum(lax.dot_general(eye, p, (((1,), (1,)), ((), ())), preferred_element_type=F32)
               for p in _split3(x))


def _dot_split(a, b):
    a1, a2 = _split2(a)
    b1, b2 = _split2(b)
    d = lambda u, v: jnp.dot(u, v, preferred_element_type=F32)
    return d(a1, b1) + (d(a1, b2) + d(a2, b1))


def _log_sigmoid(x):
    return jnp.minimum(x, 0.0) - jnp.log(1.0 + jnp.exp(-jnp.abs(x)))


def _sigmoid(x):
    return 1.0 / (1.0 + jnp.exp(-x))


def _silu(x):
    return x * _sigmoid(x)


def _softplus(x):
    return jnp.maximum(x, 0.0) + jnp.log(1.0 + jnp.exp(-jnp.abs(x)))


def _tri(c, strict=False):
    r = lax.broadcasted_iota(jnp.int32, (c, c), 0)
    s = lax.broadcasted_iota(jnp.int32, (c, c), 1)
    return (r > s) if strict else (r >= s)


def _head_norm_gate(o, g_row, gate):
    y = o * lax.rsqrt(jnp.mean(o * o, axis=-1, keepdims=True) + EPS)
    return y * g_row * gate


def _hs(h, w=HEAD_V):
    return slice(h * w, (h + 1) * w)


def _ada_kernel(c_ref, w_ref, b_ref, o_ref):
    o_ref[0] = _dot(_silu(c_ref[...]), w_ref[0]) + b_ref[0]


def _ada_call(c_all, w_ada, b_ada, tn=512):
    depth, d, n = w_ada.shape
    rows = c_all.shape[0]
    return pl.pallas_call(
        _ada_kernel,
        grid=(depth, n // tn),
        in_specs=[pl.BlockSpec((rows, d), lambda l, j: (0, 0)),
                  pl.BlockSpec((1, d, tn), lambda l, j: (l, 0, j)),
                  pl.BlockSpec((1, 1, tn), lambda l, j: (l, 0, j))],
        out_specs=pl.BlockSpec((1, rows, tn), lambda l, j: (l, 0, j)),
        out_shape=jax.ShapeDtypeStruct((depth, rows, n), F32),
        compiler_params=pltpu.CompilerParams(dimension_semantics=("parallel", "parallel"),
                                             vmem_limit_bytes=VMEM_LIMIT),
        name="ada_mod",
    )(c_all, w_ada, b_ada.reshape(depth, 1, n))


def _modulated_norm(x, g_row, scale_row, shift_row):
    y = x * lax.rsqrt(jnp.mean(x * x, axis=-1, keepdims=True) + EPS) * g_row
    return y * (1.0 + scale_row) + shift_row


def _inproj_kernel(x_ref, mod_ref, g_ref, w_ref, k_ref, v_ref, p_ref, *, col_chunk):
    mod = mod_ref[0]
    h = _modulated_norm(x_ref[0], g_ref[...], mod[1:2], mod[0:1]).astype(BF16)
    k_ref[0] = jnp.dot(h, w_ref[:, 0:W_GROUP], preferred_element_type=F32)
    v_ref[0] = jnp.dot(h, w_ref[:, W_GROUP:2 * W_GROUP], preferred_element_type=F32)
    base = 2 * W_GROUP
    for n0 in range(0, P_WIDTH, col_chunk):
        n1 = min(n0 + col_chunk, P_WIDTH)
        p_ref[0, :, n0:n1] = jnp.dot(h, w_ref[:, base + n0:base + n1], preferred_element_type=F32)


def _inproj_call(x, mod, g, w_perm, tm):
    b, l, d = x.shape
    n_all = w_perm.shape[1]
    return pl.pallas_call(
        functools.partial(_inproj_kernel, col_chunk=512),
        grid=(b, l // tm),
        in_specs=[pl.BlockSpec((1, tm, d), lambda i, j: (i, j, 0)),
                  pl.BlockSpec((1, 6, d), lambda i, j: (i, 0, 0)),
                  pl.BlockSpec((1, d), lambda i, j: (0, 0)),
                  pl.BlockSpec((d, n_all), lambda i, j: (0, 0), pipeline_mode=pl.Buffered(1))],
        out_specs=[pl.BlockSpec((1, tm, W_GROUP), lambda i, j: (i, j, 0)),
                   pl.BlockSpec((1, tm, W_GROUP), lambda i, j: (i, j, 0)),
                   pl.BlockSpec((1, tm, P_WIDTH), lambda i, j: (i, j, 0))],
        out_shape=[jax.ShapeDtypeStruct((b, l, W_GROUP), F32),
                   jax.ShapeDtypeStruct((b, l, W_GROUP), F32),
                   jax.ShapeDtypeStruct((b, l, P_WIDTH), F32)],
        compiler_params=pltpu.CompilerParams(dimension_semantics=("parallel", "parallel"),
                                             vmem_limit_bytes=VMEM_LIMIT),
        name="norm_inproj",
    )(x, mod, g, w_perm)


SB_DEAD_LOG = -104.0


def _sb_kernel(q_ref, k_ref, v_ref, o_ref, *, tq, tk, q_offset):
    i = pl.program_id(1)
    q0 = q_offset + i * tq
    j_diag = q0 // tk
    rj = lax.broadcasted_iota(jnp.int32, (tk, 2 * tk), 0)
    cs = lax.broadcasted_iota(jnp.int32, (tk, 2 * tk), 1)
    suffix_and_total = ((rj > cs) | (cs >= tk)).astype(BF16)
    q_all = q_ref[0] * (HEAD_V ** -0.5)
    qs = [q_all[:, _hs(h)].astype(BF16) for h in range(N_HEADS)]

    def block(j, accs, runs, causal):
        s0 = pl.multiple_of(j * tk, tk)
        heads = range(N_HEADS)
        kbs = [k_ref[0, pl.ds(s0, tk), _hs(h)].astype(BF16) for h in heads]
        vbs = [v_ref[0, pl.ds(s0, tk), _hs(h)].astype(BF16) for h in heads]
        zs = [lax.dot_general(qs[h], kbs[h], (((1,), (1,)), ((), ())), preferred_element_type=F32)
              for h in heads]
        lg1ms = [_log_sigmoid(-z) for z in zs]
        if causal is not None:
            lg1ms = [jnp.where(causal, x, 0.0) for x in lg1ms]
        splits = [_split2(x) for x in lg1ms]
        sts = [jnp.dot(hi, suffix_and_total, preferred_element_type=F32)
               + jnp.dot(lo, suffix_and_total, preferred_element_type=F32) for hi, lo in splits]
        probs = [jnp.exp((zs[h] + lg1ms[h]) + (sts[h][:, :tk] + runs[h])) for h in heads]
        if causal is not None:
            probs = [jnp.where(causal, a, 0.0) for a in probs]
        new_accs = [accs[h] + jnp.dot(probs[h].astype(BF16), vbs[h], preferred_element_type=F32)
                    for h in heads]
        new_runs = [runs[h] + sts[h][:, tk:] for h in heads]
        return tuple(new_accs), tuple(new_runs)

    def live_of(runs):
        return jnp.max(jnp.maximum(jnp.maximum(runs[0], runs[1]), jnp.maximum(runs[2], runs[3])))

    q_pos = q0 + lax.broadcasted_iota(jnp.int32, (tq, tk), 0)
    k_pos = j_diag * tk + lax.broadcasted_iota(jnp.int32, (tq, tk), 1)
    zeros = lambda w: tuple(jnp.zeros((tq, w), F32) for _ in range(N_HEADS))
    accs, runs = block(j_diag, zeros(HEAD_V), zeros(tk), k_pos < q_pos)

    def cond(carry):
        j, live, _, _ = carry
        return (j >= 0) & (live > SB_DEAD_LOG)

    def body(carry):
        j, _, accs, runs = carry
        accs, runs = block(j, accs, runs, None)
        return j - 1, live_of(runs), accs, runs

    _, _, accs, _ = lax.while_loop(cond, body, (j_diag - 1, live_of(runs), accs, runs))
    for h in range(N_HEADS):
        o_ref[0, :, _hs(h)] = accs[h]


def _sb_call(q_src, q_block, k_all, v_all, l, tq, tk, q_offset):
    assert tk % tq == 0 and q_offset % tk == 0
    b = q_src.shape[0]
    lk = k_all.shape[1]
    return pl.pallas_call(
        functools.partial(_sb_kernel, tq=tq, tk=tk, q_offset=q_offset),
        grid=(b, l // tq),
        in_specs=[pl.BlockSpec((1, tq, W_GROUP), lambda i, j: (i, j, q_block)),
                  pl.BlockSpec((1, lk, W_GROUP), lambda i, j: (i, 0, 0)),
                  pl.BlockSpec((1, lk, W_GROUP), lambda i, j: (i, 0, 0))],
        out_specs=pl.BlockSpec((1, tq, W_GROUP), lambda i, j: (i, j, 0)),
        out_shape=jax.ShapeDtypeStruct((b, l, W_GROUP), F32),
        compiler_params=pltpu.CompilerParams(dimension_semantics=("parallel", "parallel"),
                                             vmem_limit_bytes=VMEM_LIMIT),
        name="sb_attention",
    )(q_src, k_all, v_all)


def _mlstm_kernel(q_ref, k_ref, v_ref, og_ref, g_ref, bias_ref, ng_ref, c0_ref, n0_ref, m0_ref,
                  h_ref, c_out, n_out, m_out, c_s, n_s, m_s, *, c, nb):
    j = pl.program_id(1)

    @pl.when(j == 0)
    def _():
        c_s[...] = c0_ref[...]
        n_s[...] = n0_ref[...]
        m_s[...] = m0_ref[...]

    seqs = range(nb)
    lane = lax.broadcasted_iota(jnp.int32, (c, LANES), 1)
    is_f = (lane >= G_MF) & (lane < G_MF + N_HEADS)
    incl = _tri(c)
    incl_b = incl.astype(BF16)
    gates = [g_ref[i] + bias_ref[...] for i in seqs]
    xs = [jnp.where(is_f, _log_sigmoid(g), g) for g in gates]
    csums = [_dot01_left(incl_b, x) for x in xs]
    x_ts = [_transpose_exact(x) for x in xs]
    csum_ts = [_transpose_exact(cs) for cs in csums]
    ng = ng_ref[...]
    units = [(i, h) for i in seqs for h in range(N_HEADS)]
    heads = range(len(units))
    bcs = [csums[i][:, G_MF + h:G_MF + h + 1] for i, h in units]
    brs = [csum_ts[i][G_MF + h:G_MF + h + 1, :] for i, h in units]
    ics = [xs[i][:, G_MI + h:G_MI + h + 1] for i, h in units]
    irs = [x_ts[i][G_MI + h:G_MI + h + 1, :] for i, h in units]
    m_prevs = [m_s[i, 0:1, h:h + 1] for i, h in units]
    qs = [q_ref[i, :, _hs(h)] for i, h in units]
    ks = [k_ref[i, :, _hs(h)] * (HEAD_V ** -0.5) for i, h in units]
    vs = [v_ref[i, :, _hs(h)] for i, h in units]
    c_hs = [c_s[i, h] for i, h in units]
    n_hs = [n_s[i, h:h + 1, :] for i, h in units]
    qk = [_dot_nt(qs[h], ks[h]) for h in heads]
    q_c = [_dot(qs[h], c_hs[h]) for h in heads]
    log_ds = [jnp.where(incl, bcs[h] - brs[h] + irs[h], -jnp.inf) for h in heads]
    inters = [bcs[h] + m_prevs[h] for h in heads]
    m_ts = [jnp.maximum(inters[h], jnp.max(log_ds[h], axis=-1, keepdims=True)) for h in heads]
    scs = [qk[h] * jnp.exp(log_ds[h] - m_ts[h]) for h in heads]
    sis = [jnp.exp(inters[h] - m_ts[h]) for h in heads]
    nums = [_dot(scs[h], vs[h]) + sis[h] * q_c[h] for h in heads]
    dens = [jnp.sum(scs[h], axis=-1, keepdims=True)
            + sis[h] * jnp.sum(qs[h] * n_hs[h], axis=-1, keepdims=True) for h in heads]
    hcs = [nums[h] / jnp.maximum(jnp.abs(dens[h]), jnp.exp(-m_ts[h])) for h in heads]
    b_ends = [bcs[h][c - 1:c, :] for h in heads]
    m_news = [jnp.maximum(b_ends[h] + m_prevs[h],
                          jnp.max(b_ends[h] - brs[h] + irs[h], axis=-1, keepdims=True)) for h in heads]
    kws = [ks[h] * jnp.exp(b_ends[h] - bcs[h] + ics[h] - m_news[h]) for h in heads]
    decs = [jnp.exp(b_ends[h] + m_prevs[h] - m_news[h]) for h in heads]
    c_upd = [_dot_tn(kws[h], vs[h]) for h in heads]
    for u, (i, h) in enumerate(units):
        c_s[i, h] = decs[u] * c_hs[u] + c_upd[u]
        n_s[i, h:h + 1, :] = decs[u] * n_hs[u] + jnp.sum(kws[u], axis=0, keepdims=True)
        m_s[i, 0:1, h:h + 1] = m_news[u]
        h_ref[i, :, _hs(h)] = _head_norm_gate(hcs[u], ng[:, _hs(h)], _sigmoid(og_ref[i, :, _hs(h)]))

    @pl.when(j == pl.num_programs(1) - 1)
    def _():
        c_out[...] = c_s[...]
        n_out[...] = n_s[...]
        m_out[...] = m_s[...]


def _mlstm_call(p, bias_row, ng, c0, n0, m0, c, nb):
    b, l, _ = p.shape
    nb = math.gcd(nb, b)
    nc = l // c
    blk = lambda cb: pl.BlockSpec((nb, c, W_GROUP), lambda i, j, cb=cb: (i, j, cb))
    full = lambda a: pl.BlockSpec((nb,) + a.shape[1:], lambda i, j: (i,) + (0,) * (a.ndim - 1))
    m0 = m0.reshape(b, 1, N_HEADS)
    out_shape = [jax.ShapeDtypeStruct((b, l, W_GROUP), F32),
                 jax.ShapeDtypeStruct(c0.shape, F32),
                 jax.ShapeDtypeStruct(n0.shape, F32),
                 jax.ShapeDtypeStruct(m0.shape, F32)]
    h, c_new, n_new, m_new = pl.pallas_call(
        functools.partial(_mlstm_kernel, c=c, nb=nb),
        grid=(b // nb, nc),
        in_specs=[blk(PB_MQ), blk(PB_MK), blk(PB_MV), blk(PB_MO),
                  pl.BlockSpec((nb, c, LANES), lambda i, j: (i, j, PB128_GATES)),
                  pl.BlockSpec((1, LANES), lambda i, j: (0, 0)),
                  pl.BlockSpec((1, W_GROUP), lambda i, j: (0, 0)),
                  full(c0), full(n0), full(m0)],
        out_specs=[pl.BlockSpec((nb, c, W_GROUP), lambda i, j: (i, j, 0)),
                   full(c0), full(n0), full(m0)],
        out_shape=out_shape,
        scratch_shapes=[pltpu.VMEM((nb,) + c0.shape[1:], F32), pltpu.VMEM((nb,) + n0.shape[1:], F32),
                        pltpu.VMEM((nb, 1, N_HEADS), F32)],
        compiler_params=pltpu.CompilerParams(dimension_semantics=("parallel", "arbitrary"),
                                             vmem_limit_bytes=VMEM_LIMIT),
        name="mlstm",
    )(p, p, p, p, p, bias_row, ng, c0, n0, m0)
    return h, c_new, n_new, m_new.reshape(b, N_HEADS)


def _unit_lower_inverses(mats, c):
    eye = (lax.broadcasted_iota(jnp.int32, (c, c), 0)
           == lax.broadcasted_iota(jnp.int32, (c, c), 1)).astype(F32)
    tps = [jnp.concatenate([eye - a, _dot_split(a, a)], axis=0) for a in mats]
    top = lax.broadcasted_iota(jnp.int32, (2 * c, c), 0) < c
    for _ in range(c.bit_length() - 3):
        tps = [jnp.where(top, tp, 0.0) + _dot_split(tp, tp[c:, :]) for tp in tps]
    return [tp[:c, :] + _dot_split(tp[:c, :], tp[c:, :]) for tp in tps]


def _gdn_kernel(q_ref, k_ref, v_ref, z_ref, g_ref, bias_ref, alog_ref, cw_ref, ng_ref, s0_ref, cb0_ref,
                o_ref, s_out, cb_out, s_s, prev_s, *, c, nb):
    j = pl.program_id(1)
    seqs = range(nb)

    @pl.when(j == 0)
    def _():
        s_s[...] = s0_ref[...]
        prev_s[...] = jnp.zeros_like(prev_s)
        prev_s[:, c - 8:c, :] = cb0_ref[...]

    cw = cw_ref[...]
    row = lax.broadcasted_iota(jnp.int32, (c, cw.shape[1]), 0)
    qkvs = []
    for i in seqs:
        x = jnp.concatenate([q_ref[i], k_ref[i], v_ref[i]], axis=-1)
        prev = prev_s[i]
        y = x * cw[CONV_W - 1:CONV_W, :]
        for d in range(1, CONV_W):
            shifted = jnp.where(row < d, pltpu.roll(prev, d, 0), pltpu.roll(x, d, 0))
            y = y + shifted * cw[CONV_W - 1 - d:CONV_W - d, :]
        prev_s[i] = x
        qkvs.append(_silu(y))

    lane = lax.broadcasted_iota(jnp.int32, (c, LANES), 1)
    is_a = (lane >= G_GA) & (lane < G_GA + N_HEADS)
    incl = _tri(c)
    incl_b = incl.astype(BF16)
    strict = _tri(c, strict=True)
    gates = [g_ref[i] + bias_ref[...] for i in seqs]
    xgs = [jnp.where(is_a, -jnp.exp(alog_ref[...]) * _softplus(g), _sigmoid(g)) for g in gates]
    csums = [_dot01_left(incl_b, xg) for xg in xgs]
    csum_ts = [_transpose_exact(cs) for cs in csums]
    ng = ng_ref[...]
    units = [(i, h) for i in seqs for h in range(N_HEADS)]
    heads = range(len(units))
    l2 = lambda a: a * lax.rsqrt(jnp.sum(a * a, axis=-1, keepdims=True) + EPS)
    qs = [l2(qkvs[i][:, _hs(h)]) * (HEAD_V ** -0.5) for i, h in units]
    ks = [l2(qkvs[i][:, W_GROUP + h * HEAD_V:W_GROUP + (h + 1) * HEAD_V]) for i, h in units]
    vs = [qkvs[i][:, 2 * W_GROUP + h * HEAD_V:2 * W_GROUP + (h + 1) * HEAD_V] for i, h in units]
    betas = [xgs[i][:, G_GB + h:G_GB + h + 1] for i, h in units]
    bcs = [csums[i][:, G_GA + h:G_GA + h + 1] for i, h in units]
    brs = [csum_ts[i][G_GA + h:G_GA + h + 1, :] for i, h in units]
    states = [s_s[i, h] for i, h in units]
    decays = [jnp.exp(jnp.where(incl, bcs[h] - brs[h], 0.0)) for h in heads]
    kks = [_dot_nt(ks[h], ks[h]) for h in heads]
    qks = [_dot_nt(qs[h], ks[h]) for h in heads]
    k_s = [_dot(ks[h], states[h]) for h in heads]
    q_s = [_dot(qs[h], states[h]) for h in heads]
    lows = [jnp.where(strict, betas[h] * decays[h] * kks[h], 0.0) for h in heads]
    invs = _unit_lower_inverses(lows, c)
    ebs = [jnp.exp(bcs[h]) for h in heads]
    rhss = [betas[h] * (vs[h] - ebs[h] * k_s[h]) for h in heads]
    us = [_dot_split(invs[h], rhss[h]) for h in heads]
    qkm = [jnp.where(incl, decays[h] * qks[h], 0.0) for h in heads]
    outs = [ebs[h] * q_s[h] + _dot(qkm[h], us[h]) for h in heads]
    b_ends = [bcs[h][c - 1:c, :] for h in heads]
    upds = [_dot_tn(ks[h] * jnp.exp(b_ends[h] - bcs[h]), us[h]) for h in heads]
    for u, (i, h) in enumerate(units):
        s_s[i, h] = jnp.exp(b_ends[u]) * states[u] + upds[u]
        o_ref[i, :, _hs(h)] = _head_norm_gate(outs[u], ng[:, _hs(h)], _silu(z_ref[i, :, _hs(h)]))

    @pl.when(j == pl.num_programs(1) - 1)
    def _():
        s_out[...] = s_s[...]
        cb_out[...] = prev_s[:, c - 8:c, :]


def _gdn_call(p, bias_row, alog_row, conv_w, ng, s0, conv0, c, nb):
    b, l, _ = p.shape
    nb = math.gcd(nb, b)
    nc = l // c
    blk = lambda cb: pl.BlockSpec((nb, c, W_GROUP), lambda i, j, cb=cb: (i, j, cb))
    full = lambda a: pl.BlockSpec((nb,) + a.shape[1:], lambda i, j: (i,) + (0,) * (a.ndim - 1))
    conv_dim = conv_w.shape[1]
    conv0_pad = jnp.concatenate([jnp.zeros((b, 8 - (CONV_W - 1), conv_dim), F32), conv0], axis=1)
    o, s_new, conv_new = pl.pallas_call(
        functools.partial(_gdn_kernel, c=c, nb=nb),
        grid=(b // nb, nc),
        in_specs=[blk(PB_GQ), blk(PB_GK), blk(PB_GV), blk(PB_GZ),
                  pl.BlockSpec((nb, c, LANES), lambda i, j: (i, j, PB128_GATES)),
                  pl.BlockSpec((1, LANES), lambda i, j: (0, 0)),
                  pl.BlockSpec((1, LANES), lambda i, j: (0, 0)),
                  pl.BlockSpec((CONV_W, conv_dim), lambda i, j: (0, 0)),
                  pl.BlockSpec((1, W_GROUP), lambda i, j: (0, 0)),
                  full(s0), full(conv0_pad)],
        out_specs=[pl.BlockSpec((nb, c, W_GROUP), lambda i, j: (i, j, 0)),
                   full(s0), full(conv0_pad)],
        out_shape=[jax.ShapeDtypeStruct((b, l, W_GROUP), F32),
                   jax.ShapeDtypeStruct(s0.shape, F32),
                   jax.ShapeDtypeStruct(conv0_pad.shape, F32)],
        scratch_shapes=[pltpu.VMEM((nb,) + s0.shape[1:], F32), pltpu.VMEM((nb, c, conv_dim), F32)],
        compiler_params=pltpu.CompilerParams(dimension_semantics=("parallel", "arbitrary"),
                                             vmem_limit_bytes=VMEM_LIMIT),
        name="gdn",
    )(p, p, p, p, p, bias_row, alog_row, conv_w, ng, s0, conv0_pad)
    return o, s_new, conv_new[:, 8 - (CONV_W - 1):, :]


def _gla_kernel(q_ref, k_ref, v_ref, r_ref, g_ref, w2_ref, gb_ref, ng_ref, st0_ref,
                o_ref, st_out, st_s, b_s, q_s, a_s, *, c, nb):
    j = pl.program_id(1)
    dk_all = N_HEADS * GLA_DK
    seqs = range(nb)

    @pl.when(j == 0)
    def _():
        st_s[...] = st0_ref[...]

    incl = _tri(c).astype(BF16)
    log_as = [_log_sigmoid(_dot(g_ref[i], w2_ref[...]) + gb_ref[...]) / GLA_TAU for i in seqs]
    bs = [_dot01_left(incl, la) for la in log_as]
    qs = [q_ref[i] * (GLA_DK ** -0.5) for i in seqs]
    ks = [k_ref[i] for i in seqs]
    vs = [v_ref[i] for i in seqs]
    for i in seqs:
        b_s[i] = bs[i]
        q_s[i] = qs[i]
    hr = lax.broadcasted_iota(jnp.int32, (8, dk_all), 0)
    hc = lax.broadcasted_iota(jnp.int32, (8, dk_all), 1) // GLA_DK
    head_sel = (hr == hc).astype(BF16)
    sub = lax.broadcasted_iota(jnp.int32, (8, dk_all), 0)
    a_s[...] = jnp.zeros_like(a_s)
    for t in range(c):
        n = 8 * (t // 8 + 1)
        for i in seqs:
            bt = b_s[i, t:t + 1, :]
            qt = q_s[i, t:t + 1, :]
            w = jnp.exp(bt - bs[i][:n]) * ks[i][:n] * qt
            last = jnp.where(sub <= t % 8, w[n - 8:], 0.0)
            w = last if n == 8 else jnp.concatenate([w[:n - 8], last], axis=0)
            rows = _dot_nt(head_sel, w)
            for h in range(N_HEADS):
                a_s[i, h, t:t + 1, 0:n] = rows[h:h + 1, :]
    sts = [st_s[i] for i in seqs]
    inter = [_dot_nt(qs[i] * jnp.exp(bs[i]), sts[i]) for i in seqs]
    outs = [jnp.concatenate([_dot(a_s[i, h], vs[i][:, _hs(h)]) for h in range(N_HEADS)], axis=-1)
            + inter[i] for i in seqs]
    b_ends = [bs[i][c - 1:c, :] for i in seqs]
    blk_r = lax.broadcasted_iota(jnp.int32, sts[0].shape, 0) // HEAD_V
    blk_c = lax.broadcasted_iota(jnp.int32, sts[0].shape, 1) // GLA_DK
    upds = [jnp.where(blk_r == blk_c, _dot_tn(vs[i], ks[i] * jnp.exp(b_ends[i] - bs[i])), 0.0)
            for i in seqs]
    ng = ng_ref[...]
    for i in seqs:
        st_s[i] = sts[i] * jnp.exp(b_ends[i]) + upds[i]
        r = r_ref[i]
        for h in range(N_HEADS):
            o_ref[i, :, _hs(h)] = _head_norm_gate(outs[i][:, _hs(h)], ng[:, _hs(h)],
                                                  _silu(r[:, _hs(h)]))

    @pl.when(j == pl.num_programs(1) - 1)
    def _():
        st_out[...] = st_s[...]


def _gla_call(p, w2_pad, gb_row, ng, s0, c, nb):
    b, l, _ = p.shape
    nb = math.gcd(nb, b)
    nc = l // c
    dk_all = N_HEADS * GLA_DK
    st0 = jnp.zeros((b, N_HEADS, HEAD_V, N_HEADS, GLA_DK), F32)
    st0 = st0.at[:, jnp.arange(N_HEADS), :, jnp.arange(N_HEADS), :].set(
        jnp.transpose(s0, (1, 0, 3, 2)))
    st0 = st0.reshape(b, W_GROUP, dk_all)
    blk = lambda cb: pl.BlockSpec((nb, c, W_GROUP), lambda i, j, cb=cb: (i, j, cb))
    blk128 = lambda cb: pl.BlockSpec((nb, c, LANES), lambda i, j, cb=cb: (i, j, cb))
    o, st_new = pl.pallas_call(
        functools.partial(_gla_kernel, c=c, nb=nb),
        grid=(b // nb, nc),
        in_specs=[blk128(PB128_LQ), blk128(PB128_LK), blk(PB_LV), blk(PB_LR), blk128(PB128_GATES),
                  pl.BlockSpec((LANES, dk_all), lambda i, j: (0, 0)),
                  pl.BlockSpec((1, dk_all), lambda i, j: (0, 0)),
                  pl.BlockSpec((1, W_GROUP), lambda i, j: (0, 0)),
                  pl.BlockSpec((nb, W_GROUP, dk_all), lambda i, j: (i, 0, 0))],
        out_specs=[pl.BlockSpec((nb, c, W_GROUP), lambda i, j: (i, j, 0)),
                   pl.BlockSpec((nb, W_GROUP, dk_all), lambda i, j: (i, 0, 0))],
        out_shape=[jax.ShapeDtypeStruct((b, l, W_GROUP), F32),
                   jax.ShapeDtypeStruct((b, W_GROUP, dk_all), F32)],
        scratch_shapes=[pltpu.VMEM((nb, W_GROUP, dk_all), F32), pltpu.VMEM((nb, c, dk_all), F32),
                        pltpu.VMEM((nb, c, dk_all), F32), pltpu.VMEM((nb, N_HEADS, c, c), F32)],
        compiler_params=pltpu.CompilerParams(dimension_semantics=("parallel", "arbitrary"),
                                             vmem_limit_bytes=VMEM_LIMIT),
        name="gla",
    )(p, p, p, p, p, w2_pad, gb_row, ng, st0)
    st5 = st_new.reshape(b, N_HEADS, HEAD_V, N_HEADS, GLA_DK)
    s_new = jnp.transpose(st5[:, jnp.arange(N_HEADS), :, jnp.arange(N_HEADS), :], (1, 0, 3, 2))
    return o, s_new


def _outffn_kernel(x_ref, a_ref, b_ref, c_ref, d_ref, mod_ref, g2_ref, gf_ref, wo_ref, w1_ref, w2_ref,
                   o_ref, *, ff_chunk, final):
    mod = mod_ref[0]
    mixed = jnp.concatenate([a_ref[0], b_ref[0], c_ref[0], d_ref[0]], axis=-1).astype(BF16)
    x = x_ref[0] + mod[2:3] * jnp.dot(mixed, wo_ref[...], preferred_element_type=F32)
    h = _modulated_norm(x, g2_ref[...], mod[4:5], mod[3:4]).astype(BF16)
    d_ff = w1_ref.shape[1]
    acc = jnp.zeros(x.shape, F32)
    for f0 in range(0, d_ff, ff_chunk):
        a = jnp.maximum(jnp.dot(h, w1_ref[:, f0:f0 + ff_chunk], preferred_element_type=F32), 0.0)
        acc = acc + jnp.dot((a * a).astype(BF16), w2_ref[f0:f0 + ff_chunk, :],
                            preferred_element_type=F32)
    x = x + mod[5:6] * acc
    if final:
        x = x * lax.rsqrt(jnp.mean(x * x, axis=-1, keepdims=True) + EPS) * gf_ref[...]
    o_ref[0] = x


def _outffn_call(x, mixers, mod, g2, gf, wo, w1, w2, tm, final):
    b, l, d = x.shape
    d_ff = w1.shape[1]
    tok = lambda w: pl.BlockSpec((1, tm, w), lambda i, j: (i, j, 0))
    const = lambda shape: pl.BlockSpec(shape, lambda i, j: (0,) * len(shape),
                                       pipeline_mode=pl.Buffered(1))
    return pl.pallas_call(
        functools.partial(_outffn_kernel, ff_chunk=1024, final=final),
        grid=(b, l // tm),
        in_specs=[tok(d), tok(W_GROUP), tok(W_GROUP), tok(W_GROUP), tok(W_GROUP),
                  pl.BlockSpec((1, 6, d), lambda i, j: (i, 0, 0)),
                  pl.BlockSpec((1, d), lambda i, j: (0, 0)),
                  pl.BlockSpec((1, d), lambda i, j: (0, 0)),
                  const((d, d)), const((d, d_ff)), const((d_ff, d))],
        out_specs=tok(d),
        out_shape=jax.ShapeDtypeStruct((b, l, d), F32),
        compiler_params=pltpu.CompilerParams(dimension_semantics=("parallel", "parallel"),
                                             vmem_limit_bytes=VMEM_LIMIT),
        name="outproj_ffn",
    )(x, *mixers, mod, g2, gf, wo, w1, w2)


def _permute_w_in(w_in):
    d = w_in.shape[0]
    sizes = (256, 256, 256, 256, 256, 256, 4, 4, 256, 768, 4, 4, 256, 128, 128, 256, 16, 256)
    offs = [0]
    for s in sizes:
        offs.append(offs[-1] + s)
    (sq, sk, sv, mq, mk, mv, mi, mf, mo, gqkv, gb, ga, gz, lq, lk, lv, lg, lr) = [
        w_in[:, offs[i]:offs[i + 1]] for i in range(len(sizes))]
    pad = jnp.zeros((d, LANES - (4 * N_HEADS + GLA_RANK)), w_in.dtype)
    return jnp.concatenate([sk, sv, sq, mq, mk, mv, mo, gqkv, gz, lv, lr, lq, lk,
                            mi, mf, gb, ga, lg, pad], axis=1)


def _row128(pieces):
    row = jnp.zeros((LANES,), F32)
    for off, vec in pieces:
        row = row.at[off:off + vec.shape[0]].set(vec)
    return row.reshape(1, LANES)


def _layer(x, mod, lw, states, chunk, tm, sb_tq, sb_tk, final, final_g):
    (n1, n2, w_in_p, gate_bias, ml_ng, conv_w, alog_row, gdn_ng, w2_pad, gla_gb, gla_ng,
     w_out, w_ff1, w_ff2) = lw
    (sb_k_past, sb_v_past, ml_c, ml_n, ml_m, gdn_s, gdn_buf, gla_s) = states
    b, l, _ = x.shape
    k_new, v_new, p = _inproj_call(x, mod, n1, w_in_p, tm)
    if sb_k_past is None:
        k_all, v_all, q_offset = k_new, v_new, 0
    else:
        past = sb_k_past.shape[1]
        lk = -(-(past + l) // sb_tk) * sb_tk
        padz = jnp.zeros((b, lk - past - l, W_GROUP), F32)
        k_all = jnp.concatenate([sb_k_past.reshape(b, past, W_GROUP), k_new, padz], axis=1)
        v_all = jnp.concatenate([sb_v_past.reshape(b, past, W_GROUP), v_new, padz], axis=1)
        q_offset = past
    o_sb = _sb_call(p, PB_SQ, k_all, v_all, l, sb_tq, sb_tk, q_offset)
    o_ml, ml_c, ml_n, ml_m = _mlstm_call(p, gate_bias, ml_ng, ml_c, ml_n, ml_m, chunk, MLSTM_SEQS_PER_STEP)
    o_gdn, gdn_s, gdn_buf = _gdn_call(p, gate_bias, alog_row, conv_w, gdn_ng, gdn_s, gdn_buf, chunk,
                                      GDN_SEQS_PER_STEP)
    o_gla, gla_s = _gla_call(p, w2_pad, gla_gb, gla_ng, gla_s, chunk, GLA_SEQS_PER_STEP)
    x = _outffn_call(x, (o_sb, o_ml, o_gdn, o_gla), mod, n2, final_g, w_out, w_ff1, w_ff2, tm, final)
    hk = lambda a: a.reshape(b, l, N_HEADS, HEAD_V)
    return x, (hk(k_new), hk(v_new), ml_c, ml_n, ml_m, gdn_s, gdn_buf, gla_s)


def kernel(x_prompt, x_sample, cache_sb_k, cache_sb_v, state_mlstm_C, state_mlstm_n, state_mlstm_m, state_gdn_S, state_gdn_conv, state_gla_S, c_prompt, c_sample, norm1_g, norm2_g, w_ada, b_ada, w_in, mlstm_i_bias, mlstm_f_bias, mlstm_norm_g, gdn_conv_w, gdn_a_log, gdn_dt_bias, gdn_norm_g, gla_w_gate2, gla_gate_bias, gla_norm_g, w_out, w_ff1, w_ff2, final_g):
    depth = w_in.shape[0]
    bp, lp, d = x_prompt.shape
    bs, ls, _ = x_sample.shape
    dk_all = N_HEADS * GLA_DK
    mod_all = _ada_call(jnp.concatenate([c_prompt, c_sample], axis=0), w_ada, b_ada)
    mod_all = mod_all.reshape(depth, bp + bs, 6, d)
    final_row = final_g.reshape(1, d)
    xp, xs = x_prompt, x_sample
    p_list, s_list = [], []
    for l in range(depth):
        gate_bias = _row128([(G_MI, mlstm_i_bias[l]), (G_MF, mlstm_f_bias[l]), (G_GA, gdn_dt_bias[l])])
        alog_row = _row128([(G_GA, gdn_a_log[l])])
        w2_pad = jnp.zeros((LANES, dk_all), F32).at[G_LG:G_LG + GLA_RANK, :].set(gla_w_gate2[l])
        lw = (norm1_g[l].reshape(1, d), norm2_g[l].reshape(1, d), _permute_w_in(w_in[l]).astype(BF16),
              gate_bias, mlstm_norm_g[l].reshape(1, W_GROUP), gdn_conv_w[l], alog_row,
              gdn_norm_g[l].reshape(1, W_GROUP), w2_pad.astype(BF16),
              gla_gate_bias[l].reshape(1, dk_all), gla_norm_g[l].reshape(1, W_GROUP),
              w_out[l].astype(BF16), w_ff1[l].astype(BF16), w_ff2[l].astype(BF16))
        fresh = (None, None, jnp.zeros((bp,) + state_mlstm_C.shape[2:], F32),
                 jnp.zeros((bp,) + state_mlstm_n.shape[2:], F32),
                 jnp.zeros((bp,) + state_mlstm_m.shape[2:], F32),
                 jnp.zeros((bp,) + state_gdn_S.shape[2:], F32),
                 jnp.zeros((bp,) + state_gdn_conv.shape[2:], F32),
                 jnp.zeros((bp,) + state_gla_S.shape[2:], F32))
        past = (cache_sb_k[l], cache_sb_v[l], state_mlstm_C[l], state_mlstm_n[l], state_mlstm_m[l],
                state_gdn_S[l], state_gdn_conv[l], state_gla_S[l])
        final = l == depth - 1
        xp, st_p = _layer(xp, mod_all[l, :bp], lw, fresh, 64, min(512, lp), min(128, lp), 128,
                          final, final_row)
        xs, st_s = _layer(xs, mod_all[l, bp:], lw, past, ls, ls, ls, 128, final, final_row)
        p_list.append(st_p)
        s_list.append(st_s)
    stacked_p = [jnp.stack([st[i] for st in p_list]) for i in range(8)]
    stacked_s = [jnp.stack([st[i] for st in s_list]) for i in range(8)]
    return (xp, xs, *stacked_p, *stacked_s)
```

```python
import functools
import math
from typing import NamedTuple

import jax
import jax.numpy as jnp
from jax import lax
from jax.experimental import pallas as pl
from jax.experimental.pallas import tpu as pltpu

F32 = jnp.float32
BF16 = jnp.bfloat16
EPS = 1e-6
N_HEADS = 4
HEAD_V = 64
GLA_DK = 32
CONV_W = 4
GLA_TAU = 16.0
W_GROUP = N_HEADS * HEAD_V
LANES = 128
VMEM_LIMIT = 56 * 1024 * 1024
MLSTM_SEQS_PER_STEP = 1
GDN_SEQS_PER_STEP = 4
GLA_SEQS_PER_STEP = 4

PB_SQ, PB_MQ, PB_MK, PB_MV, PB_MO, PB_GQ, PB_GK, PB_GV, PB_GZ, PB_LV, PB_LR = range(11)
PB128_LQ, PB128_LK, PB128_GATES = 22, 23, 24
P_WIDTH = 25 * LANES
G_MI, G_MF, G_GB, G_GA, G_LG = 0, 4, 8, 12, 16
GLA_RANK = 16


def _dot(a, b):
    return jnp.dot(a.astype(BF16), b.astype(BF16), preferred_element_type=F32)


def _dot_nt(a, b):
    return lax.dot_general(a.astype(BF16), b.astype(BF16), (((1,), (1,)), ((), ())),
                           preferred_element_type=F32)


def _dot_tn(a, b):
    return lax.dot_general(a.astype(BF16), b.astype(BF16), (((0,), (0,)), ((), ())),
                           preferred_element_type=F32)


def _split3(x):
    x1 = x.astype(BF16)
    r1 = x - x1.astype(F32)
    x2 = r1.astype(BF16)
    x3 = (r1 - x2.astype(F32)).astype(BF16)
    return x1, x2, x3


def _split2(x):
    x1 = x.astype(BF16)
    x2 = (x - x1.astype(F32)).astype(BF16)
    return x1, x2


def _dot01_left(m01, x):
    return sum(jnp.dot(m01, p, preferred_element_type=F32) for p in _split3(x))


def _transpose_exact(x):
    eye = (lax.broadcasted_iota(jnp.int32, (LANES, LANES), 0)
           == lax.broadcasted_iota(jnp.int32, (LANES, LANES), 1)).astype(BF16)
    return sum(lax.dot_general(eye, p, (((1,), (1,)), ((), ())), preferred_element_type=F32)
               for p in _split3(x))


def _dot_split(a, b):
    a1, a2 = _split2(a)
    b1, b2 = _split2(b)
    d = lambda u, v: jnp.dot(u, v, preferred_element_type=F32)
    return d(a1, b1) + (d(a1, b2) + d(a2, b1))


def _log_sigmoid(x):
    return jnp.minimum(x, 0.0) - jnp.log(1.0 + jnp.exp(-jnp.abs(x)))


def _sigmoid(x):
    return 1.0 / (1.0 + jnp.exp(-x))


def _silu(x):
    return x * _sigmoid(x)


def _softplus(x):
    return jnp.maximum(x, 0.0) + jnp.log(1.0 + jnp.exp(-jnp.abs(x)))


def _tri(c, strict=False):
    r = lax.broadcasted_iota(jnp.int32, (c, c), 0)
    s = lax.broadcasted_iota(jnp.int32, (c, c), 1)
    return (r > s) if strict else (r >= s)


def _head_norm_gate(o, g_row, gate):
    y = o * lax.rsqrt(jnp.mean(o * o, axis=-1, keepdims=True) + EPS)
    return y * g_row * gate


def _hs(h, w=HEAD_V):
    return slice(h * w, (h + 1) * w)


def _ada_kernel(c_ref, w_ref, b_ref, o_ref):
    o_ref[0] = _dot(_silu(c_ref[...]), w_ref[0]) + b_ref[0]


def _ada_call(c_all, w_ada, b_ada, tn=512):
    depth, d, n = w_ada.shape
    rows = c_all.shape[0]
    return pl.pallas_call(
        _ada_kernel,
        grid=(depth, n // tn),
        in_specs=[pl.BlockSpec((rows, d), lambda l, j: (0, 0)),
                  pl.BlockSpec((1, d, tn), lambda l, j: (l, 0, j)),
                  pl.BlockSpec((1, 1, tn), lambda l, j: (l, 0, j))],
        out_specs=pl.BlockSpec((1, rows, tn), lambda l, j: (l, 0, j)),
        out_shape=jax.ShapeDtypeStruct((depth, rows, n), F32),
        compiler_params=pltpu.CompilerParams(dimension_semantics=("parallel", "parallel"),
                                             vmem_limit_bytes=VMEM_LIMIT),
        name="ada_mod",
    )(c_all, w_ada, b_ada.reshape(depth, 1, n))


def _modulated_norm(x, g_row, scale_row, shift_row):
    y = x * lax.rsqrt(jnp.mean(x * x, axis=-1, keepdims=True) + EPS) * g_row
    return y * (1.0 + scale_row) + shift_row


def _inproj_kernel(x_ref, mod_ref, g_ref, w_ref, k_ref, v_ref, p_ref, *, col_chunk):
    mod = mod_ref[0]
    h = _modulated_norm(x_ref[0], g_ref[...], mod[1:2], mod[0:1]).astype(BF16)
    k_ref[0] = jnp.dot(h, w_ref[:, 0:W_GROUP], preferred_element_type=F32)
    v_ref[0] = jnp.dot(h, w_ref[:, W_GROUP:2 * W_GROUP], preferred_element_type=F32)
    base = 2 * W_GROUP
    for n0 in range(0, P_WIDTH, col_chunk):
        n1 = min(n0 + col_chunk, P_WIDTH)
        p_ref[0, :, n0:n1] = jnp.dot(h, w_ref[:, base + n0:base + n1], preferred_element_type=F32)


def _inproj_call(x, mod, g, w_perm, tm):
    b, l, d = x.shape
    n_all = w_perm.shape[1]
    return pl.pallas_call(
        functools.partial(_inproj_kernel, col_chunk=512),
        grid=(b, l // tm),
        in_specs=[pl.BlockSpec((1, tm, d), lambda i, j: (i, j, 0)),
                  pl.BlockSpec((1, 6, d), lambda i, j: (i, 0, 0)),
                  pl.BlockSpec((1, d), lambda i, j: (0, 0)),
                  pl.BlockSpec((d, n_all), lambda i, j: (0, 0), pipeline_mode=pl.Buffered(1))],
        out_specs=[pl.BlockSpec((1, tm, W_GROUP), lambda i, j: (i, j, 0)),
                   pl.BlockSpec((1, tm, W_GROUP), lambda i, j: (i, j, 0)),
                   pl.BlockSpec((1, tm, P_WIDTH), lambda i, j: (i, j, 0))],
        out_shape=[jax.ShapeDtypeStruct((b, l, W_GROUP), F32),
                   jax.ShapeDtypeStruct((b, l, W_GROUP), F32),
                   jax.ShapeDtypeStruct((b, l, P_WIDTH), F32)],
        compiler_params=pltpu.CompilerParams(dimension_semantics=("parallel", "parallel"),
                                             vmem_limit_bytes=VMEM_LIMIT),
        name="norm_inproj",
    )(x, mod, g, w_perm)


SB_DEAD_LOG = -104.0


def _sb_kernel(q_ref, k_ref, v_ref, o_ref, *, tq, tk, q_offset):
    i = pl.program_id(1)
    q0 = q_offset + i * tq
    j_top = (q0 + tq - 1) // tk
    n_masked = max(tq // tk, 1)
    rj = lax.broadcasted_iota(jnp.int32, (tk, tk + LANES), 0)
    cs = lax.broadcasted_iota(jnp.int32, (tk, tk + LANES), 1)
    suffix_and_total = ((rj > cs) | (cs >= tk)).astype(BF16)
    q_all = q_ref[0] * (HEAD_V ** -0.5)
    qs = [q_all[:, _hs(h)].astype(BF16) for h in range(N_HEADS)]

    def block(j, accs, runs, causal):
        s0 = pl.multiple_of(j * tk, tk)
        heads = range(N_HEADS)
        kbs = [k_ref[0, pl.ds(s0, tk), _hs(h)].astype(BF16) for h in heads]
        vbs = [v_ref[0, pl.ds(s0, tk), _hs(h)].astype(BF16) for h in heads]
        zs = [lax.dot_general(qs[h], kbs[h], (((1,), (1,)), ((), ())), preferred_element_type=F32)
              for h in heads]
        lg1ms = [_log_sigmoid(-z) for z in zs]
        if causal is not None:
            lg1ms = [jnp.where(causal, x, 0.0) for x in lg1ms]
        splits = [_split2(x) for x in lg1ms]
        sts = [jnp.dot(hi, suffix_and_total, preferred_element_type=F32)
               + jnp.dot(lo, suffix_and_total, preferred_element_type=F32) for hi, lo in splits]
        wide = lambda r: r if tk == LANES else jnp.concatenate([r] * (tk // LANES), axis=1)
        probs = [jnp.exp((zs[h] + lg1ms[h]) + (sts[h][:, :tk] + wide(runs[h]))) for h in heads]
        if causal is not None:
            probs = [jnp.where(causal, a, 0.0) for a in probs]
        new_accs = [accs[h] + jnp.dot(probs[h].astype(BF16), vbs[h], preferred_element_type=F32)
                    for h in heads]
        new_runs = [runs[h] + sts[h][:, tk:] for h in heads]
        return tuple(new_accs), tuple(new_runs)

    def live_of(runs):
        return jnp.max(jnp.maximum(jnp.maximum(runs[0], runs[1]), jnp.maximum(runs[2], runs[3])))

    q_pos = q0 + lax.broadcasted_iota(jnp.int32, (tq, tk), 0)
    k_off = lax.broadcasted_iota(jnp.int32, (tq, tk), 1)
    zeros = lambda w: tuple(jnp.zeros((tq, w), F32) for _ in range(N_HEADS))
    accs, runs = zeros(HEAD_V), zeros(LANES)
    for m in range(n_masked):
        accs, runs = block(j_top - m, accs, runs, (j_top - m) * tk + k_off < q_pos)

    def cond(carry):
        j, live, _, _ = carry
        return (j >= 0) & (live > SB_DEAD_LOG)

    def body(carry):
        j, _, accs, runs = carry
        accs, runs = block(j, accs, runs, None)
        return j - 1, live_of(runs), accs, runs

    _, _, accs, _ = lax.while_loop(cond, body, (j_top - n_masked, live_of(runs), accs, runs))
    for h in range(N_HEADS):
        o_ref[0, :, _hs(h)] = accs[h]


def _sb_call(q_src, q_block, k_all, v_all, l, tq, tk, q_offset):
    assert (tq % tk == 0 or tk % tq == 0) and q_offset % max(tq, tk) == 0 and tk % LANES == 0
    b = q_src.shape[0]
    lk = k_all.shape[1]
    return pl.pallas_call(
        functools.partial(_sb_kernel, tq=tq, tk=tk, q_offset=q_offset),
        grid=(b, l // tq),
        in_specs=[pl.BlockSpec((1, tq, W_GROUP), lambda i, j: (i, j, q_block)),
                  pl.BlockSpec((1, lk, W_GROUP), lambda i, j: (i, 0, 0)),
                  pl.BlockSpec((1, lk, W_GROUP), lambda i, j: (i, 0, 0))],
        out_specs=pl.BlockSpec((1, tq, W_GROUP), lambda i, j: (i, j, 0)),
        out_shape=jax.ShapeDtypeStruct((b, l, W_GROUP), F32),
        compiler_params=pltpu.CompilerParams(dimension_semantics=("parallel", "parallel"),
                                             vmem_limit_bytes=VMEM_LIMIT),
        name="sb_attention",
    )(q_src, k_all, v_all)


def _mlstm_kernel(q_ref, k_ref, v_ref, og_ref, g_ref, bias_ref, ng_ref, c0_ref, n0_ref, m0_ref,
                  h_ref, c_out, n_out, m_out, c_s, n_s, m_s, *, c, nb):
    j = pl.program_id(1)

    @pl.when(j == 0)
    def _():
        c_s[...] = c0_ref[...]
        n_s[...] = n0_ref[...]
        m_s[...] = m0_ref[...]

    seqs = range(nb)
    lane = lax.broadcasted_iota(jnp.int32, (c, LANES), 1)
    is_f = (lane >= G_MF) & (lane < G_MF + N_HEADS)
    incl = _tri(c)
    incl_b = incl.astype(BF16)
    gates = [g_ref[i] + bias_ref[...] for i in seqs]
    xs = [jnp.where(is_f, _log_sigmoid(g), g) for g in gates]
    csums = [_dot01_left(incl_b, x) for x in xs]
    x_ts = [_transpose_exact(x) for x in xs]
    csum_ts = [_transpose_exact(cs) for cs in csums]
    ng = ng_ref[...]
    units = [(i, h) for i in seqs for h in range(N_HEADS)]
    heads = range(len(units))
    bcs = [csums[i][:, G_MF + h:G_MF + h + 1] for i, h in units]
    brs = [csum_ts[i][G_MF + h:G_MF + h + 1, :] for i, h in units]
    ics = [xs[i][:, G_MI + h:G_MI + h + 1] for i, h in units]
    irs = [x_ts[i][G_MI + h:G_MI + h + 1, :] for i, h in units]
    m_prevs = [m_s[i, 0:1, h:h + 1] for i, h in units]
    qs = [q_ref[i, :, _hs(h)] for i, h in units]
    ks = [k_ref[i, :, _hs(h)] * (HEAD_V ** -0.5) for i, h in units]
    vs = [v_ref[i, :, _hs(h)] for i, h in units]
    c_hs = [c_s[i, h] for i, h in units]
    n_hs = [n_s[i, h:h + 1, :] for i, h in units]
    qk = [_dot_nt(qs[h], ks[h]) for h in heads]
    q_c = [_dot(qs[h], c_hs[h]) for h in heads]
    log_ds = [jnp.where(incl, bcs[h] - brs[h] + irs[h], -jnp.inf) for h in heads]
    inters = [bcs[h] + m_prevs[h] for h in heads]
    m_ts = [jnp.maximum(inters[h], jnp.max(log_ds[h], axis=-1, keepdims=True)) for h in heads]
    scs = [qk[h] * jnp.exp(log_ds[h] - m_ts[h]) for h in heads]
    sis = [jnp.exp(inters[h] - m_ts[h]) for h in heads]
    nums = [_dot(scs[h], vs[h]) + sis[h] * q_c[h] for h in heads]
    dens = [jnp.sum(scs[h], axis=-1, keepdims=True)
            + sis[h] * jnp.sum(qs[h] * n_hs[h], axis=-1, keepdims=True) for h in heads]
    hcs = [nums[h] / jnp.maximum(jnp.abs(dens[h]), jnp.exp(-m_ts[h])) for h in heads]
    b_ends = [bcs[h][c - 1:c, :] for h in heads]
    m_news = [jnp.maximum(b_ends[h] + m_prevs[h],
                          jnp.max(b_ends[h] - brs[h] + irs[h], axis=-1, keepdims=True)) for h in heads]
    kws = [ks[h] * jnp.exp(b_ends[h] - bcs[h] + ics[h] - m_news[h]) for h in heads]
    decs = [jnp.exp(b_ends[h] + m_prevs[h] - m_news[h]) for h in heads]
    c_upd = [_dot_tn(kws[h], vs[h]) for h in heads]
    for u, (i, h) in enumerate(units):
        c_s[i, h] = decs[u] * c_hs[u] + c_upd[u]
        n_s[i, h:h + 1, :] = decs[u] * n_hs[u] + jnp.sum(kws[u], axis=0, keepdims=True)
        m_s[i, 0:1, h:h + 1] = m_news[u]
        h_ref[i, :, _hs(h)] = _head_norm_gate(hcs[u], ng[:, _hs(h)], _sigmoid(og_ref[i, :, _hs(h)]))

    @pl.when(j == pl.num_programs(1) - 1)
    def _():
        c_out[...] = c_s[...]
        n_out[...] = n_s[...]
        m_out[...] = m_s[...]


def _mlstm_call(p, bias_row, ng, c0, n0, m0, c, nb):
    b, l, _ = p.shape
    nb = math.gcd(nb, b)
    nc = l // c
    blk = lambda cb: pl.BlockSpec((nb, c, W_GROUP), lambda i, j, cb=cb: (i, j, cb))
    full = lambda a: pl.BlockSpec((nb,) + a.shape[1:], lambda i, j: (i,) + (0,) * (a.ndim - 1))
    m0 = m0.reshape(b, 1, N_HEADS)
    out_shape = [jax.ShapeDtypeStruct((b, l, W_GROUP), F32),
                 jax.ShapeDtypeStruct(c0.shape, F32),
                 jax.ShapeDtypeStruct(n0.shape, F32),
                 jax.ShapeDtypeStruct(m0.shape, F32)]
    h, c_new, n_new, m_new = pl.pallas_call(
        functools.partial(_mlstm_kernel, c=c, nb=nb),
        grid=(b // nb, nc),
        in_specs=[blk(PB_MQ), blk(PB_MK), blk(PB_MV), blk(PB_MO),
                  pl.BlockSpec((nb, c, LANES), lambda i, j: (i, j, PB128_GATES)),
                  pl.BlockSpec((1, LANES), lambda i, j: (0, 0)),
                  pl.BlockSpec((1, W_GROUP), lambda i, j: (0, 0)),
                  full(c0), full(n0), full(m0)],
        out_specs=[pl.BlockSpec((nb, c, W_GROUP), lambda i, j: (i, j, 0)),
                   full(c0), full(n0), full(m0)],
        out_shape=out_shape,
        scratch_shapes=[pltpu.VMEM((nb,) + c0.shape[1:], F32), pltpu.VMEM((nb,) + n0.shape[1:], F32),
                        pltpu.VMEM((nb, 1, N_HEADS), F32)],
        compiler_params=pltpu.CompilerParams(dimension_semantics=("parallel", "arbitrary"),
                                             vmem_limit_bytes=VMEM_LIMIT),
        name="mlstm",
    )(p, p, p, p, p, bias_row, ng, c0, n0, m0)
    return h, c_new, n_new, m_new.reshape(b, N_HEADS)


def _unit_lower_inverses(mats, c):
    eye = (lax.broadcasted_iota(jnp.int32, (c, c), 0)
           == lax.broadcasted_iota(jnp.int32, (c, c), 1)).astype(F32)
    tps = [jnp.concatenate([eye - a, _dot_split(a, a)], axis=0) for a in mats]
    top = lax.broadcasted_iota(jnp.int32, (2 * c, c), 0) < c
    for _ in range(c.bit_length() - 3):
        tps = [jnp.where(top, tp, 0.0) + _dot_split(tp, tp[c:, :]) for tp in tps]
    return [tp[:c, :] + _dot_split(tp[:c, :], tp[c:, :]) for tp in tps]


def _gdn_kernel(q_ref, k_ref, v_ref, z_ref, g_ref, bias_ref, alog_ref, cw_ref, ng_ref, s0_ref, cb0_ref,
                o_ref, s_out, cb_out, s_s, prev_s, *, c, nb):
    j = pl.program_id(1)
    seqs = range(nb)

    @pl.when(j == 0)
    def _():
        s_s[...] = s0_ref[...]
        prev_s[...] = jnp.zeros_like(prev_s)
        prev_s[:, c - 8:c, :] = cb0_ref[...]

    cw = cw_ref[...]
    row = lax.broadcasted_iota(jnp.int32, (c, cw.shape[1]), 0)
    qkvs = []
    for i in seqs:
        x = jnp.concatenate([q_ref[i], k_ref[i], v_ref[i]], axis=-1)
        prev = prev_s[i]
        y = x * cw[CONV_W - 1:CONV_W, :]
        for d in range(1, CONV_W):
            shifted = jnp.where(row < d, pltpu.roll(prev, d, 0), pltpu.roll(x, d, 0))
            y = y + shifted * cw[CONV_W - 1 - d:CONV_W - d, :]
        prev_s[i] = x
        qkvs.append(_silu(y))

    lane = lax.broadcasted_iota(jnp.int32, (c, LANES), 1)
    is_a = (lane >= G_GA) & (lane < G_GA + N_HEADS)
    incl = _tri(c)
    incl_b = incl.astype(BF16)
    strict = _tri(c, strict=True)
    gates = [g_ref[i] + bias_ref[...] for i in seqs]
    xgs = [jnp.where(is_a, -jnp.exp(alog_ref[...]) * _softplus(g), _sigmoid(g)) for g in gates]
    csums = [_dot01_left(incl_b, xg) for xg in xgs]
    csum_ts = [_transpose_exact(cs) for cs in csums]
    ng = ng_ref[...]
    units = [(i, h) for i in seqs for h in range(N_HEADS)]
    heads = range(len(units))
    l2 = lambda a: a * lax.rsqrt(jnp.sum(a * a, axis=-1, keepdims=True) + EPS)
    qs = [l2(qkvs[i][:, _hs(h)]) * (HEAD_V ** -0.5) for i, h in units]
    ks = [l2(qkvs[i][:, W_GROUP + h * HEAD_V:W_GROUP + (h + 1) * HEAD_V]) for i, h in units]
    vs = [qkvs[i][:, 2 * W_GROUP + h * HEAD_V:2 * W_GROUP + (h + 1) * HEAD_V] for i, h in units]
    betas = [xgs[i][:, G_GB + h:G_GB + h + 1] for i, h in units]
    bcs = [csums[i][:, G_GA + h:G_GA + h + 1] for i, h in units]
    brs = [csum_ts[i][G_GA + h:G_GA + h + 1, :] for i, h in units]
    states = [s_s[i, h] for i, h in units]
    decays = [jnp.exp(jnp.where(incl, bcs[h] - brs[h], 0.0)) for h in heads]
    kks = [_dot_nt(ks[h], ks[h]) for h in heads]
    qks = [_dot_nt(qs[h], ks[h]) for h in heads]
    k_s = [_dot(ks[h], states[h]) for h in heads]
    q_s = [_dot(qs[h], states[h]) for h in heads]
    lows = [jnp.where(strict, betas[h] * decays[h] * kks[h], 0.0) for h in heads]
    invs = _unit_lower_inverses(lows, c)
    ebs = [jnp.exp(bcs[h]) for h in heads]
    rhss = [betas[h] * (vs[h] - ebs[h] * k_s[h]) for h in heads]
    us = [_dot_split(invs[h], rhss[h]) for h in heads]
    qkm = [jnp.where(incl, decays[h] * qks[h], 0.0) for h in heads]
    outs = [ebs[h] * q_s[h] + _dot(qkm[h], us[h]) for h in heads]
    b_ends = [bcs[h][c - 1:c, :] for h in heads]
    upds = [_dot_tn(ks[h] * jnp.exp(b_ends[h] - bcs[h]), us[h]) for h in heads]
    for u, (i, h) in enumerate(units):
        s_s[i, h] = jnp.exp(b_ends[u]) * states[u] + upds[u]
        o_ref[i, :, _hs(h)] = _head_norm_gate(outs[u], ng[:, _hs(h)], _silu(z_ref[i, :, _hs(h)]))

    @pl.when(j == pl.num_programs(1) - 1)
    def _():
        s_out[...] = s_s[...]
        cb_out[...] = prev_s[:, c - 8:c, :]


def _gdn_call(p, bias_row, alog_row, conv_w, ng, s0, conv0, c, nb):
    b, l, _ = p.shape
    nb = math.gcd(nb, b)
    nc = l // c
    blk = lambda cb: pl.BlockSpec((nb, c, W_GROUP), lambda i, j, cb=cb: (i, j, cb))
    full = lambda a: pl.BlockSpec((nb,) + a.shape[1:], lambda i, j: (i,) + (0,) * (a.ndim - 1))
    conv_dim = conv_w.shape[1]
    conv0_pad = jnp.concatenate([jnp.zeros((b, 8 - (CONV_W - 1), conv_dim), F32), conv0], axis=1)
    o, s_new, conv_new = pl.pallas_call(
        functools.partial(_gdn_kernel, c=c, nb=nb),
        grid=(b // nb, nc),
        in_specs=[blk(PB_GQ), blk(PB_GK), blk(PB_GV), blk(PB_GZ),
                  pl.BlockSpec((nb, c, LANES), lambda i, j: (i, j, PB128_GATES)),
                  pl.BlockSpec((1, LANES), lambda i, j: (0, 0)),
                  pl.BlockSpec((1, LANES), lambda i, j: (0, 0)),
                  pl.BlockSpec((CONV_W, conv_dim), lambda i, j: (0, 0)),
                  pl.BlockSpec((1, W_GROUP), lambda i, j: (0, 0)),
                  full(s0), full(conv0_pad)],
        out_specs=[pl.BlockSpec((nb, c, W_GROUP), lambda i, j: (i, j, 0)),
                   full(s0), full(conv0_pad)],
        out_shape=[jax.ShapeDtypeStruct((b, l, W_GROUP), F32),
                   jax.ShapeDtypeStruct(s0.shape, F32),
                   jax.ShapeDtypeStruct(conv0_pad.shape, F32)],
        scratch_shapes=[pltpu.VMEM((nb,) + s0.shape[1:], F32), pltpu.VMEM((nb, c, conv_dim), F32)],
        compiler_params=pltpu.CompilerParams(dimension_semantics=("parallel", "arbitrary"),
                                             vmem_limit_bytes=VMEM_LIMIT),
        name="gdn",
    )(p, p, p, p, p, bias_row, alog_row, conv_w, ng, s0, conv0_pad)
    return o, s_new, conv_new[:, 8 - (CONV_W - 1):, :]


def _gla_kernel(q_ref, k_ref, v_ref, r_ref, g_ref, w2_ref, gb_ref, ng_ref, st0_ref,
                o_ref, st_out, st_s, b_s, q_s, a_s, *, c, nb):
    j = pl.program_id(1)
    dk_all = N_HEADS * GLA_DK
    seqs = range(nb)

    @pl.when(j == 0)
    def _():
        st_s[...] = st0_ref[...]

    incl = _tri(c).astype(BF16)
    log_as = [_log_sigmoid(_dot(g_ref[i], w2_ref[...]) + gb_ref[...]) / GLA_TAU for i in seqs]
    bs = [_dot01_left(incl, la) for la in log_as]
    qs = [q_ref[i] * (GLA_DK ** -0.5) for i in seqs]
    ks = [k_ref[i] for i in seqs]
    vs = [v_ref[i] for i in seqs]
    for i in seqs:
        b_s[i] = bs[i]
        q_s[i] = qs[i]
    hr = lax.broadcasted_iota(jnp.int32, (8, dk_all), 0)
    hc = lax.broadcasted_iota(jnp.int32, (8, dk_all), 1) // GLA_DK
    head_sel = (hr == hc).astype(BF16)
    sub = lax.broadcasted_iota(jnp.int32, (8, dk_all), 0)
    a_s[...] = jnp.zeros_like(a_s)
    for t in range(c):
        n = 8 * (t // 8 + 1)
        for i in seqs:
            bt = b_s[i, t:t + 1, :]
            qt = q_s[i, t:t + 1, :]
            w = jnp.exp(bt - bs[i][:n]) * ks[i][:n] * qt
            last = jnp.where(sub <= t % 8, w[n - 8:], 0.0)
            w = last if n == 8 else jnp.concatenate([w[:n - 8], last], axis=0)
            rows = _dot_nt(head_sel, w)
            for h in range(N_HEADS):
                a_s[i, h, t:t + 1, 0:n] = rows[h:h + 1, :]
    sts = [st_s[i] for i in seqs]
    inter = [_dot_nt(qs[i] * jnp.exp(bs[i]), sts[i]) for i in seqs]
    outs = [jnp.concatenate([_dot(a_s[i, h], vs[i][:, _hs(h)]) for h in range(N_HEADS)], axis=-1)
            + inter[i] for i in seqs]
    b_ends = [bs[i][c - 1:c, :] for i in seqs]
    blk_r = lax.broadcasted_iota(jnp.int32, sts[0].shape, 0) // HEAD_V
    blk_c = lax.broadcasted_iota(jnp.int32, sts[0].shape, 1) // GLA_DK
    upds = [jnp.where(blk_r == blk_c, _dot_tn(vs[i], ks[i] * jnp.exp(b_ends[i] - bs[i])), 0.0)
            for i in seqs]
    ng = ng_ref[...]
    for i in seqs:
        st_s[i] = sts[i] * jnp.exp(b_ends[i]) + upds[i]
        r = r_ref[i]
        for h in range(N_HEADS):
            o_ref[i, :, _hs(h)] = _head_norm_gate(outs[i][:, _hs(h)], ng[:, _hs(h)],
                                                  _silu(r[:, _hs(h)]))

    @pl.when(j == pl.num_programs(1) - 1)
    def _():
        st_out[...] = st_s[...]


def _gla_call(p, w2_pad, gb_row, ng, s0, c, nb):
    b, l, _ = p.shape
    nb = math.gcd(nb, b)
    nc = l // c
    dk_all = N_HEADS * GLA_DK
    st0 = jnp.zeros((b, N_HEADS, HEAD_V, N_HEADS, GLA_DK), F32)
    st0 = st0.at[:, jnp.arange(N_HEADS), :, jnp.arange(N_HEADS), :].set(
        jnp.transpose(s0, (1, 0, 3, 2)))
    st0 = st0.reshape(b, W_GROUP, dk_all)
    blk = lambda cb: pl.BlockSpec((nb, c, W_GROUP), lambda i, j, cb=cb: (i, j, cb))
    blk128 = lambda cb: pl.BlockSpec((nb, c, LANES), lambda i, j, cb=cb: (i, j, cb))
    o, st_new = pl.pallas_call(
        functools.partial(_gla_kernel, c=c, nb=nb),
        grid=(b // nb, nc),
        in_specs=[blk128(PB128_LQ), blk128(PB128_LK), blk(PB_LV), blk(PB_LR), blk128(PB128_GATES),
                  pl.BlockSpec((LANES, dk_all), lambda i, j: (0, 0)),
                  pl.BlockSpec((1, dk_all), lambda i, j: (0, 0)),
                  pl.BlockSpec((1, W_GROUP), lambda i, j: (0, 0)),
                  pl.BlockSpec((nb, W_GROUP, dk_all), lambda i, j: (i, 0, 0))],
        out_specs=[pl.BlockSpec((nb, c, W_GROUP), lambda i, j: (i, j, 0)),
                   pl.BlockSpec((nb, W_GROUP, dk_all), lambda i, j: (i, 0, 0))],
        out_shape=[jax.ShapeDtypeStruct((b, l, W_GROUP), F32),
                   jax.ShapeDtypeStruct((b, W_GROUP, dk_all), F32)],
        scratch_shapes=[pltpu.VMEM((nb, W_GROUP, dk_all), F32), pltpu.VMEM((nb, c, dk_all), F32),
                        pltpu.VMEM((nb, c, dk_all), F32), pltpu.VMEM((nb, N_HEADS, c, c), F32)],
        compiler_params=pltpu.CompilerParams(dimension_semantics=("parallel", "arbitrary"),
                                             vmem_limit_bytes=VMEM_LIMIT),
        name="gla",
    )(p, p, p, p, p, w2_pad, gb_row, ng, st0)
    st5 = st_new.reshape(b, N_HEADS, HEAD_V, N_HEADS, GLA_DK)
    s_new = jnp.transpose(st5[:, jnp.arange(N_HEADS), :, jnp.arange(N_HEADS), :], (1, 0, 3, 2))
    return o, s_new


def _outffn_kernel(x_ref, a_ref, b_ref, c_ref, d_ref, mod_ref, g2_ref, gf_ref, wo_ref, w1_ref, w2_ref,
                   o_ref, *, ff_chunk, final):
    mod = mod_ref[0]
    mixed = jnp.concatenate([a_ref[0], b_ref[0], c_ref[0], d_ref[0]], axis=-1).astype(BF16)
    x = x_ref[0] + mod[2:3] * jnp.dot(mixed, wo_ref[...], preferred_element_type=F32)
    h = _modulated_norm(x, g2_ref[...], mod[4:5], mod[3:4]).astype(BF16)
    d_ff = w1_ref.shape[1]
    acc = jnp.zeros(x.shape, F32)
    for f0 in range(0, d_ff, ff_chunk):
        a = jnp.maximum(jnp.dot(h, w1_ref[:, f0:f0 + ff_chunk], preferred_element_type=F32), 0.0)
        acc = acc + jnp.dot((a * a).astype(BF16), w2_ref[f0:f0 + ff_chunk, :],
                            preferred_element_type=F32)
    x = x + mod[5:6] * acc
    if final:
        x = x * lax.rsqrt(jnp.mean(x * x, axis=-1, keepdims=True) + EPS) * gf_ref[...]
    o_ref[0] = x


def _outffn_call(x, mixers, mod, g2, gf, wo, w1, w2, tm, final):
    b, l, d = x.shape
    d_ff = w1.shape[1]
    tok = lambda w: pl.BlockSpec((1, tm, w), lambda i, j: (i, j, 0))
    const = lambda shape: pl.BlockSpec(shape, lambda i, j: (0,) * len(shape),
                                       pipeline_mode=pl.Buffered(1))
    return pl.pallas_call(
        functools.partial(_outffn_kernel, ff_chunk=1024, final=final),
        grid=(b, l // tm),
        in_specs=[tok(d), tok(W_GROUP), tok(W_GROUP), tok(W_GROUP), tok(W_GROUP),
                  pl.BlockSpec((1, 6, d), lambda i, j: (i, 0, 0)),
                  pl.BlockSpec((1, d), lambda i, j: (0, 0)),
                  pl.BlockSpec((1, d), lambda i, j: (0, 0)),
                  const((d, d)), const((d, d_ff)), const((d_ff, d))],
        out_specs=tok(d),
        out_shape=jax.ShapeDtypeStruct((b, l, d), F32),
        compiler_params=pltpu.CompilerParams(dimension_semantics=("parallel", "parallel"),
                                             vmem_limit_bytes=VMEM_LIMIT),
        name="outproj_ffn",
    )(x, *mixers, mod, g2, gf, wo, w1, w2)


def _permute_w_in(w_in):
    d = w_in.shape[0]
    sizes = (256, 256, 256, 256, 256, 256, 4, 4, 256, 768, 4, 4, 256, 128, 128, 256, 16, 256)
    offs = [0]
    for s in sizes:
        offs.append(offs[-1] + s)
    (sq, sk, sv, mq, mk, mv, mi, mf, mo, gqkv, gb, ga, gz, lq, lk, lv, lg, lr) = [
        w_in[:, offs[i]:offs[i + 1]] for i in range(len(sizes))]
    pad = jnp.zeros((d, LANES - (4 * N_HEADS + GLA_RANK)), w_in.dtype)
    return jnp.concatenate([sk, sv, sq, mq, mk, mv, mo, gqkv, gz, lv, lr, lq, lk,
                            mi, mf, gb, ga, lg, pad], axis=1)


def _row128(pieces):
    row = jnp.zeros((LANES,), F32)
    for off, vec in pieces:
        row = row.at[off:off + vec.shape[0]].set(vec)
    return row.reshape(1, LANES)


class _Tiles(NamedTuple):
    tm: int
    sb_tq: int
    sb_tk: int
    mlstm_chunk: int
    chunk: int


def _tiles(l):
    return _Tiles(tm=min(512, l), sb_tq=min(256, l), sb_tk=max(min(256, l), LANES),
                  mlstm_chunk=min(128, l), chunk=min(64, l))


def _layer(x, mod, lw, states, tiles, final, final_g):
    tm, sb_tq, sb_tk, mlstm_chunk, chunk = tiles
    (n1, n2, w_in_p, gate_bias, ml_ng, conv_w, alog_row, gdn_ng, w2_pad, gla_gb, gla_ng,
     w_out, w_ff1, w_ff2) = lw
    (sb_k_past, sb_v_past, ml_c, ml_n, ml_m, gdn_s, gdn_buf, gla_s) = states
    b, l, _ = x.shape
    k_new, v_new, p = _inproj_call(x, mod, n1, w_in_p, tm)
    if sb_k_past is None:
        k_all, v_all, q_offset = k_new, v_new, 0
    else:
        past = sb_k_past.shape[1]
        lk = -(-(past + l) // sb_tk) * sb_tk
        padz = jnp.zeros((b, lk - past - l, W_GROUP), F32)
        k_all = jnp.concatenate([sb_k_past.reshape(b, past, W_GROUP), k_new, padz], axis=1)
        v_all = jnp.concatenate([sb_v_past.reshape(b, past, W_GROUP), v_new, padz], axis=1)
        q_offset = past
    o_sb = _sb_call(p, PB_SQ, k_all, v_all, l, sb_tq, sb_tk, q_offset)
    o_ml, ml_c, ml_n, ml_m = _mlstm_call(p, gate_bias, ml_ng, ml_c, ml_n, ml_m, mlstm_chunk,
                                         MLSTM_SEQS_PER_STEP)
    o_gdn, gdn_s, gdn_buf = _gdn_call(p, gate_bias, alog_row, conv_w, gdn_ng, gdn_s, gdn_buf, chunk,
                                      GDN_SEQS_PER_STEP)
    o_gla, gla_s = _gla_call(p, w2_pad, gla_gb, gla_ng, gla_s, chunk, GLA_SEQS_PER_STEP)
    x = _outffn_call(x, (o_sb, o_ml, o_gdn, o_gla), mod, n2, final_g, w_out, w_ff1, w_ff2, tm, final)
    hk = lambda a: a.reshape(b, l, N_HEADS, HEAD_V)
    return x, (hk(k_new), hk(v_new), ml_c, ml_n, ml_m, gdn_s, gdn_buf, gla_s)


def kernel(x_prompt, x_sample, cache_sb_k, cache_sb_v, state_mlstm_C, state_mlstm_n, state_mlstm_m, state_gdn_S, state_gdn_conv, state_gla_S, c_prompt, c_sample, norm1_g, norm2_g, w_ada, b_ada, w_in, mlstm_i_bias, mlstm_f_bias, mlstm_norm_g, gdn_conv_w, gdn_a_log, gdn_dt_bias, gdn_norm_g, gla_w_gate2, gla_gate_bias, gla_norm_g, w_out, w_ff1, w_ff2, final_g):
    depth = w_in.shape[0]
    bp, lp, d = x_prompt.shape
    bs, ls, _ = x_sample.shape
    dk_all = N_HEADS * GLA_DK
    mod_all = _ada_call(jnp.concatenate([c_prompt, c_sample], axis=0), w_ada, b_ada)
    mod_all = mod_all.reshape(depth, bp + bs, 6, d)
    final_row = final_g.reshape(1, d)
    xp, xs = x_prompt, x_sample
    p_list, s_list = [], []
    for l in range(depth):
        gate_bias = _row128([(G_MI, mlstm_i_bias[l]), (G_MF, mlstm_f_bias[l]), (G_GA, gdn_dt_bias[l])])
        alog_row = _row128([(G_GA, gdn_a_log[l])])
        w2_pad = jnp.zeros((LANES, dk_all), F32).at[G_LG:G_LG + GLA_RANK, :].set(gla_w_gate2[l])
        lw = (norm1_g[l].reshape(1, d), norm2_g[l].reshape(1, d), _permute_w_in(w_in[l]).astype(BF16),
              gate_bias, mlstm_norm_g[l].reshape(1, W_GROUP), gdn_conv_w[l], alog_row,
              gdn_norm_g[l].reshape(1, W_GROUP), w2_pad.astype(BF16),
              gla_gate_bias[l].reshape(1, dk_all), gla_norm_g[l].reshape(1, W_GROUP),
              w_out[l].astype(BF16), w_ff1[l].astype(BF16), w_ff2[l].astype(BF16))
        fresh = (None, None, jnp.zeros((bp,) + state_mlstm_C.shape[2:], F32),
                 jnp.zeros((bp,) + state_mlstm_n.shape[2:], F32),
                 jnp.zeros((bp,) + state_mlstm_m.shape[2:], F32),
                 jnp.zeros((bp,) + state_gdn_S.shape[2:], F32),
                 jnp.zeros((bp,) + state_gdn_conv.shape[2:], F32),
                 jnp.zeros((bp,) + state_gla_S.shape[2:], F32))
        past = (cache_sb_k[l], cache_sb_v[l], state_mlstm_C[l], state_mlstm_n[l], state_mlstm_m[l],
                state_gdn_S[l], state_gdn_conv[l], state_gla_S[l])
        final = l == depth - 1
        xp, st_p = _layer(xp, mod_all[l, :bp], lw, fresh, _tiles(lp), final, final_row)
        xs, st_s = _layer(xs, mod_all[l, bp:], lw, past, _tiles(ls), final, final_row)
        p_list.append(st_p)
        s_list.append(st_s)
    stacked_p = [jnp.stack([st[i] for st in p_list]) for i in range(8)]
    stacked_s = [jnp.stack([st[i] for st in s_list]) for i in range(8)]
    return (xp, xs, *stacked_p, *stacked_s)
```

```python
import functools
import math
from typing import NamedTuple

import jax
import jax.numpy as jnp
from jax import lax
from jax.experimental import pallas as pl
from jax.experimental.pallas import tpu as pltpu

F32 = jnp.float32
BF16 = jnp.bfloat16
EPS = 1e-6
N_HEADS = 4
HEAD_V = 64
GLA_DK = 32
CONV_W = 4
GLA_TAU = 16.0
W_GROUP = N_HEADS * HEAD_V
LANES = 128
VMEM_LIMIT = 56 * 1024 * 1024
MLSTM_SEQS_PER_STEP = 1
GDN_SEQS_PER_STEP = 4
GLA_SEQS_PER_STEP = 4

PB_SQ, PB_MQ, PB_MK, PB_MV, PB_MO, PB_GQ, PB_GK, PB_GV, PB_GZ, PB_LV, PB_LR = range(11)
PB128_LQ, PB128_LK, PB128_GATES = 22, 23, 24
P_WIDTH = 25 * LANES
G_MI, G_MF, G_GB, G_GA, G_LG = 0, 4, 8, 12, 16
GLA_RANK = 16


def _dot(a, b):
    return jnp.dot(a.astype(BF16), b.astype(BF16), preferred_element_type=F32)


def _dot_nt(a, b):
    return lax.dot_general(a.astype(BF16), b.astype(BF16), (((1,), (1,)), ((), ())),
                           preferred_element_type=F32)


def _dot_tn(a, b):
    return lax.dot_general(a.astype(BF16), b.astype(BF16), (((0,), (0,)), ((), ())),
                           preferred_element_type=F32)


def _split3(x):
    x1 = x.astype(BF16)
    r1 = x - x1.astype(F32)
    x2 = r1.astype(BF16)
    x3 = (r1 - x2.astype(F32)).astype(BF16)
    return x1, x2, x3


def _split2(x):
    x1 = x.astype(BF16)
    x2 = (x - x1.astype(F32)).astype(BF16)
    return x1, x2


def _dot01_left(m01, x):
    return sum(jnp.dot(m01, p, preferred_element_type=F32) for p in _split3(x))


def _dot01_right(xs, m01):
    n = xs[0].shape[0]
    stacked = jnp.concatenate([part for x in xs for part in _split3(x)], axis=0)
    y = jnp.dot(stacked, m01, preferred_element_type=F32)
    return [y[3 * i * n:(3 * i + 1) * n] + (y[(3 * i + 1) * n:(3 * i + 2) * n]
                                             + y[(3 * i + 2) * n:(3 * i + 3) * n])
            for i in range(len(xs))]


def _transpose_exact(x):
    eye = (lax.broadcasted_iota(jnp.int32, (LANES, LANES), 0)
           == lax.broadcasted_iota(jnp.int32, (LANES, LANES), 1)).astype(BF16)
    return sum(lax.dot_general(eye, p, (((1,), (1,)), ((), ())), preferred_element_type=F32)
               for p in _split3(x))


def _dot_split(a, b):
    a1, a2 = _split2(a)
    b1, b2 = _split2(b)
    d = lambda u, v: jnp.dot(u, v, preferred_element_type=F32)
    return d(a1, b1) + (d(a1, b2) + d(a2, b1))


def _log_sigmoid(x):
    return jnp.minimum(x, 0.0) - jnp.log(1.0 + jnp.exp(-jnp.abs(x)))


def _sigmoid(x):
    return 1.0 / (1.0 + jnp.exp(-x))


def _silu(x):
    return x * _sigmoid(x)


def _softplus(x):
    return jnp.maximum(x, 0.0) + jnp.log(1.0 + jnp.exp(-jnp.abs(x)))


def _tri(c, strict=False):
    r = lax.broadcasted_iota(jnp.int32, (c, c), 0)
    s = lax.broadcasted_iota(jnp.int32, (c, c), 1)
    return (r > s) if strict else (r >= s)


def _head_norm_gate(o, g_row, gate):
    y = o * lax.rsqrt(jnp.mean(o * o, axis=-1, keepdims=True) + EPS)
    return y * g_row * gate


def _hs(h, w=HEAD_V):
    return slice(h * w, (h + 1) * w)


def _ada_kernel(c_ref, w_ref, b_ref, o_ref):
    o_ref[0] = _dot(_silu(c_ref[...]), w_ref[0]) + b_ref[0]


def _ada_call(c_all, w_ada, b_ada, tn=512):
    depth, d, n = w_ada.shape
    rows = c_all.shape[0]
    return pl.pallas_call(
        _ada_kernel,
        grid=(depth, n // tn),
        in_specs=[pl.BlockSpec((rows, d), lambda l, j: (0, 0)),
                  pl.BlockSpec((1, d, tn), lambda l, j: (l, 0, j)),
                  pl.BlockSpec((1, 1, tn), lambda l, j: (l, 0, j))],
        out_specs=pl.BlockSpec((1, rows, tn), lambda l, j: (l, 0, j)),
        out_shape=jax.ShapeDtypeStruct((depth, rows, n), F32),
        compiler_params=pltpu.CompilerParams(dimension_semantics=("parallel", "parallel"),
                                             vmem_limit_bytes=VMEM_LIMIT),
        name="ada_mod",
    )(c_all, w_ada, b_ada.reshape(depth, 1, n))


def _modulated_norm(x, g_row, scale_row, shift_row):
    y = x * lax.rsqrt(jnp.mean(x * x, axis=-1, keepdims=True) + EPS) * g_row
    return y * (1.0 + scale_row) + shift_row


def _inproj_kernel(x_ref, mod_ref, g_ref, w_ref, cw_ref, cb0_ref, k_ref, v_ref, p_ref, cb_out, xw_s,
                   *, col_chunk):
    j = pl.program_id(1)
    tm = x_ref.shape[1]
    conv_lo, conv_hi = PB_GQ * W_GROUP, (PB_GV + 1) * W_GROUP

    @pl.when(j == 0)
    def _():
        xw_s[0:8, :] = cb0_ref[0]

    mod = mod_ref[0]
    h = _modulated_norm(x_ref[0], g_ref[...], mod[1:2], mod[0:1]).astype(BF16)
    k_ref[0] = jnp.dot(h, w_ref[:, 0:W_GROUP], preferred_element_type=F32)
    v_ref[0] = jnp.dot(h, w_ref[:, W_GROUP:2 * W_GROUP], preferred_element_type=F32)
    base = 2 * W_GROUP
    bounds = sorted(set(range(0, conv_lo, col_chunk)) | {conv_lo, conv_hi}
                    | set(range(conv_hi, P_WIDTH, col_chunk)) | {P_WIDTH})
    for n0, n1 in zip(bounds[:-1], bounds[1:]):
        if n0 == conv_lo:
            continue
        p_ref[0, :, n0:n1] = jnp.dot(h, w_ref[:, base + n0:base + n1], preferred_element_type=F32)
    raw = jnp.dot(h, w_ref[:, base + conv_lo:base + conv_hi], preferred_element_type=F32)
    xw_s[8:8 + tm, :] = raw
    cw = cw_ref[...]
    y = raw * cw[CONV_W - 1:CONV_W, :]
    for d in range(1, CONV_W):
        y = y + xw_s[8 - d:8 - d + tm, :] * cw[CONV_W - 1 - d:CONV_W - d, :]
    p_ref[0, :, conv_lo:conv_hi] = _silu(y)
    xw_s[0:8, :] = raw[tm - 8:tm, :]

    @pl.when(j == pl.num_programs(1) - 1)
    def _():
        cb_out[0] = raw[tm - 8:tm, :]


def _inproj_call(x, mod, g, w_perm, conv_w, conv0, tm):
    b, l, d = x.shape
    n_all = w_perm.shape[1]
    conv_dim = conv_w.shape[1]
    conv0_pad = jnp.concatenate([jnp.zeros((b, 8 - (CONV_W - 1), conv_dim), F32), conv0], axis=1)
    k, v, p, conv_new = pl.pallas_call(
        functools.partial(_inproj_kernel, col_chunk=512),
        grid=(b, l // tm),
        in_specs=[pl.BlockSpec((1, tm, d), lambda i, j: (i, j, 0)),
                  pl.BlockSpec((1, 6, d), lambda i, j: (i, 0, 0)),
                  pl.BlockSpec((1, d), lambda i, j: (0, 0)),
                  pl.BlockSpec((d, n_all), lambda i, j: (0, 0), pipeline_mode=pl.Buffered(1)),
                  pl.BlockSpec((CONV_W, conv_dim), lambda i, j: (0, 0)),
                  pl.BlockSpec((1, 8, conv_dim), lambda i, j: (i, 0, 0))],
        out_specs=[pl.BlockSpec((1, tm, W_GROUP), lambda i, j: (i, j, 0)),
                   pl.BlockSpec((1, tm, W_GROUP), lambda i, j: (i, j, 0)),
                   pl.BlockSpec((1, tm, P_WIDTH), lambda i, j: (i, j, 0)),
                   pl.BlockSpec((1, 8, conv_dim), lambda i, j: (i, 0, 0))],
        out_shape=[jax.ShapeDtypeStruct((b, l, W_GROUP), F32),
                   jax.ShapeDtypeStruct((b, l, W_GROUP), F32),
                   jax.ShapeDtypeStruct((b, l, P_WIDTH), F32),
                   jax.ShapeDtypeStruct((b, 8, conv_dim), F32)],
        scratch_shapes=[pltpu.VMEM((tm + 8, conv_dim), F32)],
        compiler_params=pltpu.CompilerParams(dimension_semantics=("parallel", "arbitrary"),
                                             vmem_limit_bytes=VMEM_LIMIT),
        name="norm_inproj",
    )(x, mod, g, w_perm, conv_w, conv0_pad)
    return k, v, p, conv_new[:, 8 - (CONV_W - 1):, :]


SB_DEAD_LOG = -104.0


def _sb_kernel(q_ref, k_ref, v_ref, o_ref, *, tq, tk, q_offset):
    i = pl.program_id(1)
    q0 = q_offset + i * tq
    j_top = (q0 + tq - 1) // tk
    n_masked = max(tq // tk, 1)
    rj = lax.broadcasted_iota(jnp.int32, (tk, tk + LANES), 0)
    cs = lax.broadcasted_iota(jnp.int32, (tk, tk + LANES), 1)
    suffix_and_total = ((rj > cs) | (cs >= tk)).astype(BF16)
    q_all = q_ref[0] * (HEAD_V ** -0.5)
    qs = [q_all[:, _hs(h)].astype(BF16) for h in range(N_HEADS)]

    def block(j, accs, runs, causal):
        s0 = pl.multiple_of(j * tk, tk)
        heads = range(N_HEADS)
        kbs = [k_ref[0, pl.ds(s0, tk), _hs(h)].astype(BF16) for h in heads]
        vbs = [v_ref[0, pl.ds(s0, tk), _hs(h)].astype(BF16) for h in heads]
        zs = [lax.dot_general(qs[h], kbs[h], (((1,), (1,)), ((), ())), preferred_element_type=F32)
              for h in heads]
        lg1ms = [_log_sigmoid(-z) for z in zs]
        if causal is not None:
            lg1ms = [jnp.where(causal, x, 0.0) for x in lg1ms]
        splits = [_split2(x) for x in lg1ms]
        sts = [jnp.dot(hi, suffix_and_total, preferred_element_type=F32)
               + jnp.dot(lo, suffix_and_total, preferred_element_type=F32) for hi, lo in splits]
        wide = lambda r: r if tk == LANES else jnp.concatenate([r] * (tk // LANES), axis=1)
        probs = [jnp.exp((zs[h] + lg1ms[h]) + (sts[h][:, :tk] + wide(runs[h]))) for h in heads]
        if causal is not None:
            probs = [jnp.where(causal, a, 0.0) for a in probs]
        new_accs = [accs[h] + jnp.dot(probs[h].astype(BF16), vbs[h], preferred_element_type=F32)
                    for h in heads]
        new_runs = [runs[h] + sts[h][:, tk:] for h in heads]
        return tuple(new_accs), tuple(new_runs)

    def live_of(runs):
        return jnp.max(jnp.maximum(jnp.maximum(runs[0], runs[1]), jnp.maximum(runs[2], runs[3])))

    q_pos = q0 + lax.broadcasted_iota(jnp.int32, (tq, tk), 0)
    k_off = lax.broadcasted_iota(jnp.int32, (tq, tk), 1)
    zeros = lambda w: tuple(jnp.zeros((tq, w), F32) for _ in range(N_HEADS))
    accs, runs = zeros(HEAD_V), zeros(LANES)
    for m in range(n_masked):
        accs, runs = block(j_top - m, accs, runs, (j_top - m) * tk + k_off < q_pos)

    def cond(carry):
        j, live, _, _ = carry
        return (j >= 0) & (live > SB_DEAD_LOG)

    def body(carry):
        j, _, accs, runs = carry
        accs, runs = block(j, accs, runs, None)
        return j - 1, live_of(runs), accs, runs

    _, _, accs, _ = lax.while_loop(cond, body, (j_top - n_masked, live_of(runs), accs, runs))
    for h in range(N_HEADS):
        o_ref[0, :, _hs(h)] = accs[h]


def _sb_call(q_src, q_block, k_all, v_all, l, tq, tk, q_offset):
    assert (tq % tk == 0 or tk % tq == 0) and q_offset % max(tq, tk) == 0 and tk % LANES == 0
    b = q_src.shape[0]
    lk = k_all.shape[1]
    return pl.pallas_call(
        functools.partial(_sb_kernel, tq=tq, tk=tk, q_offset=q_offset),
        grid=(b, l // tq),
        in_specs=[pl.BlockSpec((1, tq, W_GROUP), lambda i, j: (i, j, q_block)),
                  pl.BlockSpec((1, lk, W_GROUP), lambda i, j: (i, 0, 0)),
                  pl.BlockSpec((1, lk, W_GROUP), lambda i, j: (i, 0, 0))],
        out_specs=pl.BlockSpec((1, tq, W_GROUP), lambda i, j: (i, j, 0)),
        out_shape=jax.ShapeDtypeStruct((b, l, W_GROUP), F32),
        compiler_params=pltpu.CompilerParams(dimension_semantics=("parallel", "parallel"),
                                             vmem_limit_bytes=VMEM_LIMIT),
        name="sb_attention",
    )(q_src, k_all, v_all)


def _mlstm_kernel(q_ref, k_ref, v_ref, og_ref, g_ref, bias_ref, ng_ref, c0_ref, n0_ref, m0_ref,
                  h_ref, c_out, n_out, m_out, c_s, n_s, m_s, *, c, nb):
    j = pl.program_id(1)

    @pl.when(j == 0)
    def _():
        c_s[...] = c0_ref[...]
        n_s[...] = n0_ref[...]
        m_s[...] = m0_ref[...]

    seqs = range(nb)
    lane = lax.broadcasted_iota(jnp.int32, (c, LANES), 1)
    is_f = (lane >= G_MF) & (lane < G_MF + N_HEADS)
    incl = _tri(c)
    incl_b = incl.astype(BF16)
    gates = [g_ref[i] + bias_ref[...] for i in seqs]
    xs = [jnp.where(is_f, _log_sigmoid(g), g) for g in gates]
    csums = [_dot01_left(incl_b, x) for x in xs]
    x_ts = [_transpose_exact(x) for x in xs]
    csum_ts = [_transpose_exact(cs) for cs in csums]
    ng = ng_ref[...]
    units = [(i, h) for i in seqs for h in range(N_HEADS)]
    heads = range(len(units))
    bcs = [csums[i][:, G_MF + h:G_MF + h + 1] for i, h in units]
    brs = [csum_ts[i][G_MF + h:G_MF + h + 1, :] for i, h in units]
    ics = [xs[i][:, G_MI + h:G_MI + h + 1] for i, h in units]
    irs = [x_ts[i][G_MI + h:G_MI + h + 1, :] for i, h in units]
    m_prevs = [m_s[i, 0:1, h:h + 1] for i, h in units]
    qs = [q_ref[i, :, _hs(h)] for i, h in units]
    ks = [k_ref[i, :, _hs(h)] * (HEAD_V ** -0.5) for i, h in units]
    vs = [v_ref[i, :, _hs(h)] for i, h in units]
    c_hs = [c_s[i, h] for i, h in units]
    n_hs = [n_s[i, h:h + 1, :] for i, h in units]
    qk = [_dot_nt(qs[h], ks[h]) for h in heads]
    q_c = [_dot(qs[h], c_hs[h]) for h in heads]
    log_ds = [jnp.where(incl, bcs[h] - brs[h] + irs[h], -jnp.inf) for h in heads]
    inters = [bcs[h] + m_prevs[h] for h in heads]
    m_ts = [jnp.maximum(inters[h], jnp.max(log_ds[h], axis=-1, keepdims=True)) for h in heads]
    scs = [qk[h] * jnp.exp(log_ds[h] - m_ts[h]) for h in heads]
    sis = [jnp.exp(inters[h] - m_ts[h]) for h in heads]
    nums = [_dot(scs[h], vs[h]) + sis[h] * q_c[h] for h in heads]
    dens = [jnp.sum(scs[h], axis=-1, keepdims=True)
            + sis[h] * jnp.sum(qs[h] * n_hs[h], axis=-1, keepdims=True) for h in heads]
    hcs = [nums[h] / jnp.maximum(jnp.abs(dens[h]), jnp.exp(-m_ts[h])) for h in heads]
    b_ends = [bcs[h][c - 1:c, :] for h in heads]
    m_news = [jnp.maximum(b_ends[h] + m_prevs[h],
                          jnp.max(b_ends[h] - brs[h] + irs[h], axis=-1, keepdims=True)) for h in heads]
    kws = [ks[h] * jnp.exp(b_ends[h] - bcs[h] + ics[h] - m_news[h]) for h in heads]
    decs = [jnp.exp(b_ends[h] + m_prevs[h] - m_news[h]) for h in heads]
    c_upd = [_dot_tn(kws[h], vs[h]) for h in heads]
    for u, (i, h) in enumerate(units):
        c_s[i, h] = decs[u] * c_hs[u] + c_upd[u]
        n_s[i, h:h + 1, :] = decs[u] * n_hs[u] + jnp.sum(kws[u], axis=0, keepdims=True)
        m_s[i, 0:1, h:h + 1] = m_news[u]
        h_ref[i, :, _hs(h)] = _head_norm_gate(hcs[u], ng[:, _hs(h)], _sigmoid(og_ref[i, :, _hs(h)]))

    @pl.when(j == pl.num_programs(1) - 1)
    def _():
        c_out[...] = c_s[...]
        n_out[...] = n_s[...]
        m_out[...] = m_s[...]


def _mlstm_call(p, bias_row, ng, c0, n0, m0, c, nb):
    b, l, _ = p.shape
    nb = math.gcd(nb, b)
    nc = l // c
    blk = lambda cb: pl.BlockSpec((nb, c, W_GROUP), lambda i, j, cb=cb: (i, j, cb))
    full = lambda a: pl.BlockSpec((nb,) + a.shape[1:], lambda i, j: (i,) + (0,) * (a.ndim - 1))
    m0 = m0.reshape(b, 1, N_HEADS)
    out_shape = [jax.ShapeDtypeStruct((b, l, W_GROUP), F32),
                 jax.ShapeDtypeStruct(c0.shape, F32),
                 jax.ShapeDtypeStruct(n0.shape, F32),
                 jax.ShapeDtypeStruct(m0.shape, F32)]
    h, c_new, n_new, m_new = pl.pallas_call(
        functools.partial(_mlstm_kernel, c=c, nb=nb),
        grid=(b // nb, nc),
        in_specs=[blk(PB_MQ), blk(PB_MK), blk(PB_MV), blk(PB_MO),
                  pl.BlockSpec((nb, c, LANES), lambda i, j: (i, j, PB128_GATES)),
                  pl.BlockSpec((1, LANES), lambda i, j: (0, 0)),
                  pl.BlockSpec((1, W_GROUP), lambda i, j: (0, 0)),
                  full(c0), full(n0), full(m0)],
        out_specs=[pl.BlockSpec((nb, c, W_GROUP), lambda i, j: (i, j, 0)),
                   full(c0), full(n0), full(m0)],
        out_shape=out_shape,
        scratch_shapes=[pltpu.VMEM((nb,) + c0.shape[1:], F32), pltpu.VMEM((nb,) + n0.shape[1:], F32),
                        pltpu.VMEM((nb, 1, N_HEADS), F32)],
        compiler_params=pltpu.CompilerParams(dimension_semantics=("parallel", "arbitrary"),
                                             vmem_limit_bytes=VMEM_LIMIT),
        name="mlstm",
    )(p, p, p, p, p, bias_row, ng, c0, n0, m0)
    return h, c_new, n_new, m_new.reshape(b, N_HEADS)


def _same_head(rows, lanes, row_group, lane_group):
    rh = lax.broadcasted_iota(jnp.int32, (rows, lanes), 0) >> (row_group.bit_length() - 1)
    lh = lax.broadcasted_iota(jnp.int32, (rows, lanes), 1) >> (lane_group.bit_length() - 1)
    return rh == lh


def _head_blocks(x, same_head):
    tiled = jnp.concatenate([x] * N_HEADS, axis=0)
    return jnp.where(same_head, tiled, jnp.zeros_like(tiled))


def _heads_dot_split(l, r, same_head):
    return _heads_dot_parts(_split2(l), _split2(r), same_head)


def _heads_dot_parts(l_parts, r_parts, same_head):
    l1, l2 = l_parts
    rb1, rb2 = _head_blocks(r_parts[0], same_head), _head_blocks(r_parts[1], same_head)
    m = l1.shape[0]
    y = jnp.dot(jnp.concatenate([l1, l2], axis=0), rb1, preferred_element_type=F32)
    return y[:m] + (y[m:] + jnp.dot(l1, rb2, preferred_element_type=F32))


def _unit_lower_inverses(lows, eye4, same_head, c):
    ts = [eye4 - a for a in lows]
    a_parts = [_split2(a) for a in lows]
    ps = [_heads_dot_parts(ap, ap, same_head) for ap in a_parts]
    for _ in range(c.bit_length() - 3):
        t_parts = [_split2(t) for t in ts]
        p_parts = [_split2(p) for p in ps]
        stacked = [tuple(jnp.concatenate([tp[n], pp[n]], axis=0) for n in range(2))
                   for tp, pp in zip(t_parts, p_parts)]
        ys = [_heads_dot_parts(sp, pp, same_head) for sp, pp in zip(stacked, p_parts)]
        ts = [t + y[:c] for t, y in zip(ts, ys)]
        ps = [y[c:] for y in ys]
    return [t + _heads_dot_split(t, p, same_head) for t, p in zip(ts, ps)]


def _gdn_kernel(q_ref, k_ref, v_ref, z_ref, g_ref, bias_ref, alog_ref, ng_ref, s0_ref,
                o_ref, s_out, s_s, *, c, nb):
    j = pl.program_id(1)
    seqs = range(nb)

    @pl.when(j == 0)
    def _():
        s_s[...] = s0_ref[...]

    hs = N_HEADS * c
    lane = lax.broadcasted_iota(jnp.int32, (c, LANES), 1)
    is_a = (lane >= G_GA) & (lane < G_GA + N_HEADS)
    incl_b = _tri(c).astype(BF16)
    t_idx = lax.broadcasted_iota(jnp.int32, (c, hs), 0)
    s_idx = lax.broadcasted_iota(jnp.int32, (c, hs), 1) & (c - 1)
    incl4, strict4, eye4 = t_idx >= s_idx, t_idx > s_idx, (t_idx == s_idx).astype(F32)
    same_hs = _same_head(hs, hs, c, c)
    same_hv = _same_head(hs, W_GROUP, c, HEAD_V)
    same_vv = _same_head(W_GROUP, W_GROUP, HEAD_V, HEAD_V)
    group_sum = same_vv.astype(BF16)

    def pick(lanes, group):
        g = lax.broadcasted_iota(jnp.int32, (LANES, 2 * lanes), 0)
        l = lax.broadcasted_iota(jnp.int32, (LANES, 2 * lanes), 1)
        h = (l & (lanes - 1)) >> (group.bit_length() - 1)
        return (g == jnp.where(l < lanes, G_GA, G_GB) + h).astype(BF16)

    pick_hs = pick(hs, c)
    row_sel = (lax.broadcasted_iota(jnp.int32, (8, LANES), 1)
               == G_GA + lax.broadcasted_iota(jnp.int32, (8, LANES), 0)).astype(BF16)
    gates = [g_ref[i] + bias_ref[...] for i in seqs]
    xgs = [jnp.where(is_a, -jnp.exp(alog_ref[...]) * _softplus(g), _sigmoid(g)) for g in gates]
    csums = [jnp.where(is_a, _dot01_left(incl_b, xg), xg) for xg in xgs]
    cols_hs = _dot01_right(csums, pick_hs)
    bc_hs, beta_hs = [x[:, :hs] for x in cols_hs], [x[:, hs:] for x in cols_hs]
    if c == HEAD_V:
        bc_hv, beta_hv = bc_hs, beta_hs
    else:
        cols_hv = _dot01_right(csums, pick(W_GROUP, HEAD_V))
        bc_hv, beta_hv = [x[:, :W_GROUP] for x in cols_hv], [x[:, W_GROUP:] for x in cols_hv]
    b_rows = [sum(lax.dot_general(row_sel, part, (((1,), (1,)), ((), ())), preferred_element_type=F32)
                  for part in _split3(cs)) for cs in csums]
    br_hs = [jnp.concatenate([br[h:h + 1, :] for h in range(N_HEADS)], axis=1) for br in b_rows]
    decays = [jnp.exp(jnp.where(incl4, bc_hs[i] - br_hs[i], 0.0)) for i in seqs]
    qk_raw = [jnp.concatenate([q_ref[i], k_ref[i]], axis=0) for i in seqs]
    norms = _dot01_right([x * x for x in qk_raw], group_sum)
    qk_n = [qk_raw[i] * lax.rsqrt(norms[i] + EPS) for i in seqs]
    qs = [x[:c] * (HEAD_V ** -0.5) for x in qk_n]
    ks = [x[c:] for x in qk_n]
    vs = [v_ref[i] for i in seqs]
    kq = [jnp.concatenate([ks[i], qs[i]], axis=0).astype(BF16) for i in seqs]
    states = [s_s[i] for i in seqs]
    scores = [lax.dot_general(kq[i], _head_blocks(ks[i].astype(BF16), same_hv),
                              (((1,), (1,)), ((), ())), preferred_element_type=F32) for i in seqs]
    on_state = [jnp.dot(kq[i], states[i].astype(BF16), preferred_element_type=F32) for i in seqs]
    lows = [jnp.where(strict4, beta_hs[i] * decays[i] * scores[i][:c], 0.0) for i in seqs]
    invs = _unit_lower_inverses(lows, eye4, same_hs, c)
    ebs = [jnp.exp(bc_hv[i]) for i in seqs]
    rhss = [beta_hv[i] * (vs[i] - ebs[i] * on_state[i][:c]) for i in seqs]
    us = [_heads_dot_split(invs[i], rhss[i], same_hv) for i in seqs]
    qkm = [jnp.where(incl4, decays[i] * scores[i][c:], 0.0).astype(BF16) for i in seqs]
    outs = [ebs[i] * on_state[i][c:]
            + jnp.dot(qkm[i], _head_blocks(us[i].astype(BF16), same_hv), preferred_element_type=F32)
            for i in seqs]
    b_ends = [bc_hv[i][c - 1:c, :] for i in seqs]
    upds = [jnp.where(same_vv, _dot_tn(ks[i] * jnp.exp(b_ends[i] - bc_hv[i]), us[i]), 0.0) for i in seqs]
    ng = ng_ref[...]
    sum_sq = _dot01_right([o * o for o in outs], group_sum)
    for i in seqs:
        s_s[i] = jnp.exp(b_ends[i]) * states[i] + upds[i]
        o_ref[i] = (outs[i] * lax.rsqrt(sum_sq[i] * (1.0 / HEAD_V) + EPS) * ng * _silu(z_ref[i]))

    @pl.when(j == pl.num_programs(1) - 1)
    def _():
        s_out[...] = s_s[...]


def _gdn_call(p, bias_row, alog_row, ng, s0, c, nb):
    b, l, _ = p.shape
    nb = math.gcd(nb, b)
    nc = l // c
    blk = lambda cb: pl.BlockSpec((nb, c, W_GROUP), lambda i, j, cb=cb: (i, j, cb))
    full = lambda a: pl.BlockSpec((nb,) + a.shape[1:], lambda i, j: (i,) + (0,) * (a.ndim - 1))
    heads = jnp.arange(N_HEADS)
    s_bd = jnp.zeros((b, N_HEADS, HEAD_V, N_HEADS, HEAD_V), F32)
    s_bd = s_bd.at[:, heads, :, heads, :].set(jnp.transpose(s0, (1, 0, 2, 3)))
    s0 = s_bd.reshape(b, W_GROUP, W_GROUP)
    o, s_new = pl.pallas_call(
        functools.partial(_gdn_kernel, c=c, nb=nb),
        grid=(b // nb, nc),
        in_specs=[blk(PB_GQ), blk(PB_GK), blk(PB_GV), blk(PB_GZ),
                  pl.BlockSpec((nb, c, LANES), lambda i, j: (i, j, PB128_GATES)),
                  pl.BlockSpec((1, LANES), lambda i, j: (0, 0)),
                  pl.BlockSpec((1, LANES), lambda i, j: (0, 0)),
                  pl.BlockSpec((1, W_GROUP), lambda i, j: (0, 0)),
                  full(s0)],
        out_specs=[pl.BlockSpec((nb, c, W_GROUP), lambda i, j: (i, j, 0)), full(s0)],
        out_shape=[jax.ShapeDtypeStruct((b, l, W_GROUP), F32), jax.ShapeDtypeStruct(s0.shape, F32)],
        scratch_shapes=[pltpu.VMEM((nb,) + s0.shape[1:], F32)],
        compiler_params=pltpu.CompilerParams(dimension_semantics=("parallel", "arbitrary"),
                                             vmem_limit_bytes=VMEM_LIMIT),
        name="gdn",
    )(p, p, p, p, p, bias_row, alog_row, ng, s0)
    s_new = s_new.reshape(b, N_HEADS, HEAD_V, N_HEADS, HEAD_V)
    s_new = jnp.transpose(s_new[:, heads, :, heads, :], (1, 0, 2, 3))
    return o, s_new


def _gla_kernel(q_ref, k_ref, v_ref, r_ref, g_ref, w2_ref, gb_ref, ng_ref, st0_ref,
                o_ref, st_out, st_s, b_s, q_s, a_s, *, c, nb):
    j = pl.program_id(1)
    dk_all = N_HEADS * GLA_DK
    seqs = range(nb)

    @pl.when(j == 0)
    def _():
        st_s[...] = st0_ref[...]

    incl = _tri(c).astype(BF16)
    log_as = [_log_sigmoid(_dot(g_ref[i], w2_ref[...]) + gb_ref[...]) / GLA_TAU for i in seqs]
    bs = [_dot01_left(incl, la) for la in log_as]
    qs = [q_ref[i] * (GLA_DK ** -0.5) for i in seqs]
    ks = [k_ref[i] for i in seqs]
    vs = [v_ref[i] for i in seqs]
    for i in seqs:
        b_s[i] = bs[i]
        q_s[i] = qs[i]
    hr = lax.broadcasted_iota(jnp.int32, (8, dk_all), 0)
    hc = lax.broadcasted_iota(jnp.int32, (8, dk_all), 1) // GLA_DK
    head_sel = (hr == hc).astype(BF16)
    sub = lax.broadcasted_iota(jnp.int32, (8, dk_all), 0)
    a_s[...] = jnp.zeros_like(a_s)
    for t in range(c):
        n = 8 * (t // 8 + 1)
        for i in seqs:
            bt = b_s[i, t:t + 1, :]
            qt = q_s[i, t:t + 1, :]
            w = jnp.exp(bt - bs[i][:n]) * ks[i][:n] * qt
            last = jnp.where(sub <= t % 8, w[n - 8:], 0.0)
            w = last if n == 8 else jnp.concatenate([w[:n - 8], last], axis=0)
            rows = _dot_nt(head_sel, w)
            for h in range(N_HEADS):
                a_s[i, h, t:t + 1, 0:n] = rows[h:h + 1, :]
    sts = [st_s[i] for i in seqs]
    inter = [_dot_nt(qs[i] * jnp.exp(bs[i]), sts[i]) for i in seqs]
    outs = [jnp.concatenate([_dot(a_s[i, h], vs[i][:, _hs(h)]) for h in range(N_HEADS)], axis=-1)
            + inter[i] for i in seqs]
    b_ends = [bs[i][c - 1:c, :] for i in seqs]
    blk_r = lax.broadcasted_iota(jnp.int32, sts[0].shape, 0) // HEAD_V
    blk_c = lax.broadcasted_iota(jnp.int32, sts[0].shape, 1) // GLA_DK
    upds = [jnp.where(blk_r == blk_c, _dot_tn(vs[i], ks[i] * jnp.exp(b_ends[i] - bs[i])), 0.0)
            for i in seqs]
    ng = ng_ref[...]
    for i in seqs:
        st_s[i] = sts[i] * jnp.exp(b_ends[i]) + upds[i]
        r = r_ref[i]
        for h in range(N_HEADS):
            o_ref[i, :, _hs(h)] = _head_norm_gate(outs[i][:, _hs(h)], ng[:, _hs(h)],
                                                  _silu(r[:, _hs(h)]))

    @pl.when(j == pl.num_programs(1) - 1)
    def _():
        st_out[...] = st_s[...]


def _gla_call(p, w2_pad, gb_row, ng, s0, c, nb):
    b, l, _ = p.shape
    nb = math.gcd(nb, b)
    nc = l // c
    dk_all = N_HEADS * GLA_DK
    st0 = jnp.zeros((b, N_HEADS, HEAD_V, N_HEADS, GLA_DK), F32)
    st0 = st0.at[:, jnp.arange(N_HEADS), :, jnp.arange(N_HEADS), :].set(
        jnp.transpose(s0, (1, 0, 3, 2)))
    st0 = st0.reshape(b, W_GROUP, dk_all)
    blk = lambda cb: pl.BlockSpec((nb, c, W_GROUP), lambda i, j, cb=cb: (i, j, cb))
    blk128 = lambda cb: pl.BlockSpec((nb, c, LANES), lambda i, j, cb=cb: (i, j, cb))
    o, st_new = pl.pallas_call(
        functools.partial(_gla_kernel, c=c, nb=nb),
        grid=(b // nb, nc),
        in_specs=[blk128(PB128_LQ), blk128(PB128_LK), blk(PB_LV), blk(PB_LR), blk128(PB128_GATES),
                  pl.BlockSpec((LANES, dk_all), lambda i, j: (0, 0)),
                  pl.BlockSpec((1, dk_all), lambda i, j: (0, 0)),
                  pl.BlockSpec((1, W_GROUP), lambda i, j: (0, 0)),
                  pl.BlockSpec((nb, W_GROUP, dk_all), lambda i, j: (i, 0, 0))],
        out_specs=[pl.BlockSpec((nb, c, W_GROUP), lambda i, j: (i, j, 0)),
                   pl.BlockSpec((nb, W_GROUP, dk_all), lambda i, j: (i, 0, 0))],
        out_shape=[jax.ShapeDtypeStruct((b, l, W_GROUP), F32),
                   jax.ShapeDtypeStruct((b, W_GROUP, dk_all), F32)],
        scratch_shapes=[pltpu.VMEM((nb, W_GROUP, dk_all), F32), pltpu.VMEM((nb, c, dk_all), F32),
                        pltpu.VMEM((nb, c, dk_all), F32), pltpu.VMEM((nb, N_HEADS, c, c), F32)],
        compiler_params=pltpu.CompilerParams(dimension_semantics=("parallel", "arbitrary"),
                                             vmem_limit_bytes=VMEM_LIMIT),
        name="gla",
    )(p, p, p, p, p, w2_pad, gb_row, ng, st0)
    st5 = st_new.reshape(b, N_HEADS, HEAD_V, N_HEADS, GLA_DK)
    s_new = jnp.transpose(st5[:, jnp.arange(N_HEADS), :, jnp.arange(N_HEADS), :], (1, 0, 3, 2))
    return o, s_new


def _outffn_kernel(x_ref, a_ref, b_ref, c_ref, d_ref, mod_ref, g2_ref, gf_ref, wo_ref, w1_ref, w2_ref,
                   o_ref, *, ff_chunk, final):
    mod = mod_ref[0]
    mixed = jnp.concatenate([a_ref[0], b_ref[0], c_ref[0], d_ref[0]], axis=-1).astype(BF16)
    x = x_ref[0] + mod[2:3] * jnp.dot(mixed, wo_ref[...], preferred_element_type=F32)
    h = _modulated_norm(x, g2_ref[...], mod[4:5], mod[3:4]).astype(BF16)
    d_ff = w1_ref.shape[1]
    acc = jnp.zeros(x.shape, F32)
    for f0 in range(0, d_ff, ff_chunk):
        a = jnp.maximum(jnp.dot(h, w1_ref[:, f0:f0 + ff_chunk], preferred_element_type=F32), 0.0)
        acc = acc + jnp.dot((a * a).astype(BF16), w2_ref[f0:f0 + ff_chunk, :],
                            preferred_element_type=F32)
    x = x + mod[5:6] * acc
    if final:
        x = x * lax.rsqrt(jnp.mean(x * x, axis=-1, keepdims=True) + EPS) * gf_ref[...]
    o_ref[0] = x


def _outffn_call(x, mixers, mod, g2, gf, wo, w1, w2, tm, final):
    b, l, d = x.shape
    d_ff = w1.shape[1]
    tok = lambda w: pl.BlockSpec((1, tm, w), lambda i, j: (i, j, 0))
    const = lambda shape: pl.BlockSpec(shape, lambda i, j: (0,) * len(shape),
                                       pipeline_mode=pl.Buffered(1))
    return pl.pallas_call(
        functools.partial(_outffn_kernel, ff_chunk=1024, final=final),
        grid=(b, l // tm),
        in_specs=[tok(d), tok(W_GROUP), tok(W_GROUP), tok(W_GROUP), tok(W_GROUP),
                  pl.BlockSpec((1, 6, d), lambda i, j: (i, 0, 0)),
                  pl.BlockSpec((1, d), lambda i, j: (0, 0)),
                  pl.BlockSpec((1, d), lambda i, j: (0, 0)),
                  const((d, d)), const((d, d_ff)), const((d_ff, d))],
        out_specs=tok(d),
        out_shape=jax.ShapeDtypeStruct((b, l, d), F32),
        compiler_params=pltpu.CompilerParams(dimension_semantics=("parallel", "parallel"),
                                             vmem_limit_bytes=VMEM_LIMIT),
        name="outproj_ffn",
    )(x, *mixers, mod, g2, gf, wo, w1, w2)


def _permute_w_in(w_in):
    d = w_in.shape[0]
    sizes = (256, 256, 256, 256, 256, 256, 4, 4, 256, 768, 4, 4, 256, 128, 128, 256, 16, 256)
    offs = [0]
    for s in sizes:
        offs.append(offs[-1] + s)
    (sq, sk, sv, mq, mk, mv, mi, mf, mo, gqkv, gb, ga, gz, lq, lk, lv, lg, lr) = [
        w_in[:, offs[i]:offs[i + 1]] for i in range(len(sizes))]
    pad = jnp.zeros((d, LANES - (4 * N_HEADS + GLA_RANK)), w_in.dtype)
    return jnp.concatenate([sk, sv, sq, mq, mk, mv, mo, gqkv, gz, lv, lr, lq, lk,
                            mi, mf, gb, ga, lg, pad], axis=1)


def _row128(pieces):
    row = jnp.zeros((LANES,), F32)
    for off, vec in pieces:
        row = row.at[off:off + vec.shape[0]].set(vec)
    return row.reshape(1, LANES)


class _Tiles(NamedTuple):
    tm: int
    sb_tq: int
    sb_tk: int
    mlstm_chunk: int
    chunk: int


def _tiles(l):
    return _Tiles(tm=min(512, l), sb_tq=min(256, l), sb_tk=max(min(256, l), LANES),
                  mlstm_chunk=min(128, l), chunk=min(64, l))


def _layer(x, mod, lw, states, tiles, final, final_g):
    tm, sb_tq, sb_tk, mlstm_chunk, chunk = tiles
    (n1, n2, w_in_p, gate_bias, ml_ng, conv_w, alog_row, gdn_ng, w2_pad, gla_gb, gla_ng,
     w_out, w_ff1, w_ff2) = lw
    (sb_k_past, sb_v_past, ml_c, ml_n, ml_m, gdn_s, gdn_buf, gla_s) = states
    b, l, _ = x.shape
    k_new, v_new, p, gdn_buf = _inproj_call(x, mod, n1, w_in_p, conv_w, gdn_buf, tm)
    if sb_k_past is None:
        k_all, v_all, q_offset = k_new, v_new, 0
    else:
        past = sb_k_past.shape[1]
        lk = -(-(past + l) // sb_tk) * sb_tk
        padz = jnp.zeros((b, lk - past - l, W_GROUP), F32)
        k_all = jnp.concatenate([sb_k_past.reshape(b, past, W_GROUP), k_new, padz], axis=1)
        v_all = jnp.concatenate([sb_v_past.reshape(b, past, W_GROUP), v_new, padz], axis=1)
        q_offset = past
    o_sb = _sb_call(p, PB_SQ, k_all, v_all, l, sb_tq, sb_tk, q_offset)
    o_ml, ml_c, ml_n, ml_m = _mlstm_call(p, gate_bias, ml_ng, ml_c, ml_n, ml_m, mlstm_chunk,
                                         MLSTM_SEQS_PER_STEP)
    o_gdn, gdn_s = _gdn_call(p, gate_bias, alog_row, gdn_ng, gdn_s, chunk, GDN_SEQS_PER_STEP)
    o_gla, gla_s = _gla_call(p, w2_pad, gla_gb, gla_ng, gla_s, chunk, GLA_SEQS_PER_STEP)
    x = _outffn_call(x, (o_sb, o_ml, o_gdn, o_gla), mod, n2, final_g, w_out, w_ff1, w_ff2, tm, final)
    hk = lambda a: a.reshape(b, l, N_HEADS, HEAD_V)
    return x, (hk(k_new), hk(v_new), ml_c, ml_n, ml_m, gdn_s, gdn_buf, gla_s)


def kernel(x_prompt, x_sample, cache_sb_k, cache_sb_v, state_mlstm_C, state_mlstm_n, state_mlstm_m, state_gdn_S, state_gdn_conv, state_gla_S, c_prompt, c_sample, norm1_g, norm2_g, w_ada, b_ada, w_in, mlstm_i_bias, mlstm_f_bias, mlstm_norm_g, gdn_conv_w, gdn_a_log, gdn_dt_bias, gdn_norm_g, gla_w_gate2, gla_gate_bias, gla_norm_g, w_out, w_ff1, w_ff2, final_g):
    depth = w_in.shape[0]
    bp, lp, d = x_prompt.shape
    bs, ls, _ = x_sample.shape
    dk_all = N_HEADS * GLA_DK
    mod_all = _ada_call(jnp.concatenate([c_prompt, c_sample], axis=0), w_ada, b_ada)
    mod_all = mod_all.reshape(depth, bp + bs, 6, d)
    final_row = final_g.reshape(1, d)
    xp, xs = x_prompt, x_sample
    p_list, s_list = [], []
    for l in range(depth):
        gate_bias = _row128([(G_MI, mlstm_i_bias[l]), (G_MF, mlstm_f_bias[l]), (G_GA, gdn_dt_bias[l])])
        alog_row = _row128([(G_GA, gdn_a_log[l])])
        w2_pad = jnp.zeros((LANES, dk_all), F32).at[G_LG:G_LG + GLA_RANK, :].set(gla_w_gate2[l])
        lw = (norm1_g[l].reshape(1, d), norm2_g[l].reshape(1, d), _permute_w_in(w_in[l]).astype(BF16),
              gate_bias, mlstm_norm_g[l].reshape(1, W_GROUP), gdn_conv_w[l], alog_row,
              gdn_norm_g[l].reshape(1, W_GROUP), w2_pad.astype(BF16),
              gla_gate_bias[l].reshape(1, dk_all), gla_norm_g[l].reshape(1, W_GROUP),
              w_out[l].astype(BF16), w_ff1[l].astype(BF16), w_ff2[l].astype(BF16))
        fresh = (None, None, jnp.zeros((bp,) + state_mlstm_C.shape[2:], F32),
                 jnp.zeros((bp,) + state_mlstm_n.shape[2:], F32),
                 jnp.zeros((bp,) + state_mlstm_m.shape[2:], F32),
                 jnp.zeros((bp,) + state_gdn_S.shape[2:], F32),
                 jnp.zeros((bp,) + state_gdn_conv.shape[2:], F32),
                 jnp.zeros((bp,) + state_gla_S.shape[2:], F32))
        past = (cache_sb_k[l], cache_sb_v[l], state_mlstm_C[l], state_mlstm_n[l], state_mlstm_m[l],
                state_gdn_S[l], state_gdn_conv[l], state_gla_S[l])
        final = l == depth - 1
        xp, st_p = _layer(xp, mod_all[l, :bp], lw, fresh, _tiles(lp), final, final_row)
        xs, st_s = _layer(xs, mod_all[l, bp:], lw, past, _tiles(ls), final, final_row)
        p_list.append(st_p)
        s_list.append(st_s)
    stacked_p = [jnp.stack([st[i] for st in p_list]) for i in range(8)]
    stacked_s = [jnp.stack([st[i] for st in s_list]) for i in range(8)]
    return (xp, xs, *stacked_p, *stacked_s)
```

```python
import functools
import math
from typing import NamedTuple

import jax
import jax.numpy as jnp
from jax import lax
from jax.experimental import pallas as pl
from jax.experimental.pallas import tpu as pltpu

F32 = jnp.float32
BF16 = jnp.bfloat16
EPS = 1e-6
N_HEADS = 4
HEAD_V = 64
GLA_DK = 32
CONV_W = 4
GLA_TAU = 16.0
W_GROUP = N_HEADS * HEAD_V
LANES = 128
VMEM_LIMIT = 56 * 1024 * 1024
MLSTM_SEQS_PER_STEP = 4
GDN_SEQS_PER_STEP = 4
GLA_SEQS_PER_STEP = 4

PB_SQ, PB_MQ, PB_MK, PB_MV, PB_MO, PB_GQ, PB_GK, PB_GV, PB_GZ, PB_LV, PB_LR = range(11)
PB128_LQ, PB128_LK, PB128_GATES = 22, 23, 24
P_WIDTH = 25 * LANES
G_MI, G_MF, G_GB, G_GA, G_LG = 0, 4, 8, 12, 16
GLA_RANK = 16


def _dot(a, b):
    return jnp.dot(a.astype(BF16), b.astype(BF16), preferred_element_type=F32)


def _dot_nt(a, b):
    return lax.dot_general(a.astype(BF16), b.astype(BF16), (((1,), (1,)), ((), ())),
                           preferred_element_type=F32)


def _dot_tn(a, b):
    return lax.dot_general(a.astype(BF16), b.astype(BF16), (((0,), (0,)), ((), ())),
                           preferred_element_type=F32)


def _split3(x):
    x1 = x.astype(BF16)
    r1 = x - x1.astype(F32)
    x2 = r1.astype(BF16)
    x3 = (r1 - x2.astype(F32)).astype(BF16)
    return x1, x2, x3


def _split2(x):
    x1 = x.astype(BF16)
    x2 = (x - x1.astype(F32)).astype(BF16)
    return x1, x2


def _dot01_left(m01, x):
    return sum(jnp.dot(m01, p, preferred_element_type=F32) for p in _split3(x))


def _dot01_right(xs, m01):
    n = xs[0].shape[0]
    stacked = jnp.concatenate([part for x in xs for part in _split3(x)], axis=0)
    y = jnp.dot(stacked, m01, preferred_element_type=F32)
    return [y[3 * i * n:(3 * i + 1) * n] + (y[(3 * i + 1) * n:(3 * i + 2) * n]
                                             + y[(3 * i + 2) * n:(3 * i + 3) * n])
            for i in range(len(xs))]


def _transpose_exact(x):
    eye = (lax.broadcasted_iota(jnp.int32, (LANES, LANES), 0)
           == lax.broadcasted_iota(jnp.int32, (LANES, LANES), 1)).astype(BF16)
    return sum(lax.dot_general(eye, p, (((1,), (1,)), ((), ())), preferred_element_type=F32)
               for p in _split3(x))


def _dot_split(a, b):
    a1, a2 = _split2(a)
    b1, b2 = _split2(b)
    d = lambda u, v: jnp.dot(u, v, preferred_element_type=F32)
    return d(a1, b1) + (d(a1, b2) + d(a2, b1))


def _log_sigmoid(x):
    return jnp.minimum(x, 0.0) - jnp.log(1.0 + jnp.exp(-jnp.abs(x)))


def _sigmoid(x):
    return 1.0 / (1.0 + jnp.exp(-x))


def _silu(x):
    return x * _sigmoid(x)


def _softplus(x):
    return jnp.maximum(x, 0.0) + jnp.log(1.0 + jnp.exp(-jnp.abs(x)))


def _tri(c, strict=False):
    r = lax.broadcasted_iota(jnp.int32, (c, c), 0)
    s = lax.broadcasted_iota(jnp.int32, (c, c), 1)
    return (r > s) if strict else (r >= s)


def _head_norm_gate(o, g_row, gate):
    y = o * lax.rsqrt(jnp.mean(o * o, axis=-1, keepdims=True) + EPS)
    return y * g_row * gate


def _hs(h, w=HEAD_V):
    return slice(h * w, (h + 1) * w)


def _ada_kernel(c_ref, w_ref, b_ref, o_ref):
    o_ref[0] = _dot(_silu(c_ref[...]), w_ref[0]) + b_ref[0]


def _ada_call(c_all, w_ada, b_ada, tn=512):
    depth, d, n = w_ada.shape
    rows = c_all.shape[0]
    return pl.pallas_call(
        _ada_kernel,
        grid=(depth, n // tn),
        in_specs=[pl.BlockSpec((rows, d), lambda l, j: (0, 0)),
                  pl.BlockSpec((1, d, tn), lambda l, j: (l, 0, j)),
                  pl.BlockSpec((1, 1, tn), lambda l, j: (l, 0, j))],
        out_specs=pl.BlockSpec((1, rows, tn), lambda l, j: (l, 0, j)),
        out_shape=jax.ShapeDtypeStruct((depth, rows, n), F32),
        compiler_params=pltpu.CompilerParams(dimension_semantics=("parallel", "parallel"),
                                             vmem_limit_bytes=VMEM_LIMIT),
        name="ada_mod",
    )(c_all, w_ada, b_ada.reshape(depth, 1, n))


def _modulated_norm(x, g_row, scale_row, shift_row):
    y = x * lax.rsqrt(jnp.mean(x * x, axis=-1, keepdims=True) + EPS) * g_row
    return y * (1.0 + scale_row) + shift_row


def _inproj_kernel(x_ref, mod_ref, g_ref, w_ref, cw_ref, cb0_ref, k_ref, v_ref, p_ref, cb_out, xw_s,
                   *, col_chunk):
    j = pl.program_id(1)
    tm = x_ref.shape[1]
    conv_lo, conv_hi = PB_GQ * W_GROUP, (PB_GV + 1) * W_GROUP

    @pl.when(j == 0)
    def _():
        xw_s[0:8, :] = cb0_ref[0]

    mod = mod_ref[0]
    h = _modulated_norm(x_ref[0], g_ref[...], mod[1:2], mod[0:1]).astype(BF16)
    k_ref[0] = jnp.dot(h, w_ref[:, 0:W_GROUP], preferred_element_type=F32)
    v_ref[0] = jnp.dot(h, w_ref[:, W_GROUP:2 * W_GROUP], preferred_element_type=F32)
    base = 2 * W_GROUP
    bounds = sorted(set(range(0, conv_lo, col_chunk)) | {conv_lo, conv_hi}
                    | set(range(conv_hi, P_WIDTH, col_chunk)) | {P_WIDTH})
    for n0, n1 in zip(bounds[:-1], bounds[1:]):
        if n0 == conv_lo:
            continue
        p_ref[0, :, n0:n1] = jnp.dot(h, w_ref[:, base + n0:base + n1], preferred_element_type=F32)
    raw = jnp.dot(h, w_ref[:, base + conv_lo:base + conv_hi], preferred_element_type=F32)
    xw_s[8:8 + tm, :] = raw
    cw = cw_ref[...]
    y = raw * cw[CONV_W - 1:CONV_W, :]
    for d in range(1, CONV_W):
        y = y + xw_s[8 - d:8 - d + tm, :] * cw[CONV_W - 1 - d:CONV_W - d, :]
    p_ref[0, :, conv_lo:conv_hi] = _silu(y)
    xw_s[0:8, :] = raw[tm - 8:tm, :]

    @pl.when(j == pl.num_programs(1) - 1)
    def _():
        cb_out[0] = raw[tm - 8:tm, :]


def _inproj_call(x, mod, g, w_perm, conv_w, conv0, tm):
    b, l, d = x.shape
    n_all = w_perm.shape[1]
    conv_dim = conv_w.shape[1]
    conv0_pad = jnp.concatenate([jnp.zeros((b, 8 - (CONV_W - 1), conv_dim), F32), conv0], axis=1)
    k, v, p, conv_new = pl.pallas_call(
        functools.partial(_inproj_kernel, col_chunk=512),
        grid=(b, l // tm),
        in_specs=[pl.BlockSpec((1, tm, d), lambda i, j: (i, j, 0)),
                  pl.BlockSpec((1, 6, d), lambda i, j: (i, 0, 0)),
                  pl.BlockSpec((1, d), lambda i, j: (0, 0)),
                  pl.BlockSpec((d, n_all), lambda i, j: (0, 0), pipeline_mode=pl.Buffered(1)),
                  pl.BlockSpec((CONV_W, conv_dim), lambda i, j: (0, 0)),
                  pl.BlockSpec((1, 8, conv_dim), lambda i, j: (i, 0, 0))],
        out_specs=[pl.BlockSpec((1, tm, W_GROUP), lambda i, j: (i, j, 0)),
                   pl.BlockSpec((1, tm, W_GROUP), lambda i, j: (i, j, 0)),
                   pl.BlockSpec((1, tm, P_WIDTH), lambda i, j: (i, j, 0)),
                   pl.BlockSpec((1, 8, conv_dim), lambda i, j: (i, 0, 0))],
        out_shape=[jax.ShapeDtypeStruct((b, l, W_GROUP), F32),
                   jax.ShapeDtypeStruct((b, l, W_GROUP), F32),
                   jax.ShapeDtypeStruct((b, l, P_WIDTH), F32),
                   jax.ShapeDtypeStruct((b, 8, conv_dim), F32)],
        scratch_shapes=[pltpu.VMEM((tm + 8, conv_dim), F32)],
        compiler_params=pltpu.CompilerParams(dimension_semantics=("parallel", "arbitrary"),
                                             vmem_limit_bytes=VMEM_LIMIT),
        name="norm_inproj",
    )(x, mod, g, w_perm, conv_w, conv0_pad)
    return k, v, p, conv_new[:, 8 - (CONV_W - 1):, :]


SB_DEAD_LOG = -104.0


def _sb_kernel(q_ref, k_ref, v_ref, o_ref, *, tq, tk, q_offset):
    i = pl.program_id(1)
    q0 = q_offset + i * tq
    j_top = (q0 + tq - 1) // tk
    n_masked = max(tq // tk, 1)
    rj = lax.broadcasted_iota(jnp.int32, (tk, tk + LANES), 0)
    cs = lax.broadcasted_iota(jnp.int32, (tk, tk + LANES), 1)
    suffix_and_total = ((rj > cs) | (cs >= tk)).astype(BF16)
    q_all = q_ref[0] * (HEAD_V ** -0.5)
    qs = [q_all[:, _hs(h)].astype(BF16) for h in range(N_HEADS)]

    def block(j, accs, runs, causal):
        s0 = pl.multiple_of(j * tk, tk)
        heads = range(N_HEADS)
        kbs = [k_ref[0, pl.ds(s0, tk), _hs(h)].astype(BF16) for h in heads]
        vbs = [v_ref[0, pl.ds(s0, tk), _hs(h)].astype(BF16) for h in heads]
        zs = [lax.dot_general(qs[h], kbs[h], (((1,), (1,)), ((), ())), preferred_element_type=F32)
              for h in heads]
        lg1ms = [_log_sigmoid(-z) for z in zs]
        if causal is not None:
            lg1ms = [jnp.where(causal, x, 0.0) for x in lg1ms]
        splits = [_split2(x) for x in lg1ms]
        sts = [jnp.dot(hi, suffix_and_total, preferred_element_type=F32)
               + jnp.dot(lo, suffix_and_total, preferred_element_type=F32) for hi, lo in splits]
        wide = lambda r: r if tk == LANES else jnp.concatenate([r] * (tk // LANES), axis=1)
        probs = [jnp.exp((zs[h] + lg1ms[h]) + (sts[h][:, :tk] + wide(runs[h]))) for h in heads]
        if causal is not None:
            probs = [jnp.where(causal, a, 0.0) for a in probs]
        new_accs = [accs[h] + jnp.dot(probs[h].astype(BF16), vbs[h], preferred_element_type=F32)
                    for h in heads]
        new_runs = [runs[h] + sts[h][:, tk:] for h in heads]
        return tuple(new_accs), tuple(new_runs)

    def live_of(runs):
        return jnp.max(jnp.maximum(jnp.maximum(runs[0], runs[1]), jnp.maximum(runs[2], runs[3])))

    q_pos = q0 + lax.broadcasted_iota(jnp.int32, (tq, tk), 0)
    k_off = lax.broadcasted_iota(jnp.int32, (tq, tk), 1)
    zeros = lambda w: tuple(jnp.zeros((tq, w), F32) for _ in range(N_HEADS))
    accs, runs = zeros(HEAD_V), zeros(LANES)
    for m in range(n_masked):
        accs, runs = block(j_top - m, accs, runs, (j_top - m) * tk + k_off < q_pos)

    def cond(carry):
        j, live, _, _ = carry
        return (j >= 0) & (live > SB_DEAD_LOG)

    def body(carry):
        j, _, accs, runs = carry
        accs, runs = block(j, accs, runs, None)
        return j - 1, live_of(runs), accs, runs

    _, _, accs, _ = lax.while_loop(cond, body, (j_top - n_masked, live_of(runs), accs, runs))
    for h in range(N_HEADS):
        o_ref[0, :, _hs(h)] = accs[h]


def _sb_call(q_src, q_block, k_all, v_all, l, tq, tk, q_offset):
    assert (tq % tk == 0 or tk % tq == 0) and q_offset % max(tq, tk) == 0 and tk % LANES == 0
    b = q_src.shape[0]
    lk = k_all.shape[1]
    return pl.pallas_call(
        functools.partial(_sb_kernel, tq=tq, tk=tk, q_offset=q_offset),
        grid=(b, l // tq),
        in_specs=[pl.BlockSpec((1, tq, W_GROUP), lambda i, j: (i, j, q_block)),
                  pl.BlockSpec((1, lk, W_GROUP), lambda i, j: (i, 0, 0)),
                  pl.BlockSpec((1, lk, W_GROUP), lambda i, j: (i, 0, 0))],
        out_specs=pl.BlockSpec((1, tq, W_GROUP), lambda i, j: (i, j, 0)),
        out_shape=jax.ShapeDtypeStruct((b, l, W_GROUP), F32),
        compiler_params=pltpu.CompilerParams(dimension_semantics=("parallel", "parallel"),
                                             vmem_limit_bytes=VMEM_LIMIT),
        name="sb_attention",
    )(q_src, k_all, v_all)


def _cummax_rows(x):
    r = x.shape[0]
    row = lax.broadcasted_iota(jnp.int32, x.shape, 0)
    shift = 1
    while shift < r:
        x = jnp.maximum(x, jnp.where(row >= shift, pltpu.roll(x, shift, 0), -jnp.inf))
        shift *= 2
    return x


def _mlstm_kernel(q_ref, k_ref, v_ref, og_ref, g_ref, bias_ref, ng_ref, cn0_ref, m0_ref,
                  h_ref, cn_out, m_out, cn_s, m_s, *, c, nb):
    j = pl.program_id(1)

    @pl.when(j == 0)
    def _():
        cn_s[...] = cn0_ref[...]
        m_s[...] = m0_ref[...]

    seqs = range(nb)
    hs = N_HEADS * c
    lane = lax.broadcasted_iota(jnp.int32, (c, LANES), 1)
    is_f = (lane >= G_MF) & (lane < G_MF + N_HEADS)
    head_lane = lane < N_HEADS
    incl_b = _tri(c).astype(BF16)
    t_idx = lax.broadcasted_iota(jnp.int32, (c, hs), 0)
    s_idx = lax.broadcasted_iota(jnp.int32, (c, hs), 1) & (c - 1)
    incl4 = t_idx >= s_idx
    same_hv = _same_head(hs, W_GROUP, c, HEAD_V)
    same_vv = _same_head(W_GROUP, W_GROUP, HEAD_V, HEAD_V)
    group_sum = same_vv.astype(BF16)

    def expand(lanes, group):
        g = lax.broadcasted_iota(jnp.int32, (LANES, lanes), 0)
        h = lax.broadcasted_iota(jnp.int32, (LANES, lanes), 1) >> (group.bit_length() - 1)
        return (g == h).astype(BF16)

    gi = lax.broadcasted_iota(jnp.int32, (LANES, 2 * LANES), 0)
    li = lax.broadcasted_iota(jnp.int32, (LANES, 2 * LANES), 1)
    to_heads = (jnp.where((li < N_HEADS) & (gi == G_MI + li), 1.0, 0.0)
                - jnp.where((li < N_HEADS) & (gi == G_MF + li), 1.0, 0.0)
                + jnp.where((li >= LANES) & (li < LANES + N_HEADS) & (gi == G_MF + li - LANES), 1.0, 0.0)
                ).astype(BF16)
    row_sel = (lax.broadcasted_iota(jnp.int32, (8, LANES), 1)
               == lax.broadcasted_iota(jnp.int32, (8, LANES), 0)).astype(BF16)
    gates = [g_ref[i] + bias_ref[...] for i in seqs]
    merged = [jnp.where(is_f, _dot01_left(incl_b, jnp.where(is_f, _log_sigmoid(g), 0.0)), g) for g in gates]
    gb = _dot01_right(merged, to_heads)
    g_c = [jnp.where(head_lane, x[:, :LANES], -jnp.inf) for x in gb]
    b_c = [x[:, LANES:] for x in gb]
    m_prev = [m_s[i] for i in seqs]
    r_c = [jnp.maximum(m_prev[i], _cummax_rows(g_c[i])) for i in seqs]
    r_end = [r[c - 1:c, :] for r in r_c]
    si_c = [jnp.exp(m_prev[i] - r_c[i]) for i in seqs]
    emt_c = [jnp.exp(-(b_c[i] + r_c[i])) for i in seqs]
    w_c = [jnp.exp(g_c[i] - r_end[i]) for i in seqs]
    dec_c = [jnp.broadcast_to(jnp.exp(m_prev[i] - r_end[i]), (c, LANES)) for i in seqs]
    zero_pad = lambda x: jnp.where(head_lane, x, 0.0)
    r_hs = _dot01_right([zero_pad(r) for r in r_c], expand(hs, c))
    cols = _dot01_right([zero_pad(x) for i in seqs for x in (si_c[i], emt_c[i], w_c[i], dec_c[i])],
                        expand(W_GROUP, HEAD_V))
    si_hv, emt_hv, w_hv, dec_hv = (cols[n::4] for n in range(4))
    g_rows = [sum(lax.dot_general(row_sel, part, (((1,), (1,)), ((), ())), preferred_element_type=F32)
                  for part in _split3(zero_pad(x))) for x in g_c]
    g_row_hs = [jnp.concatenate([gr[h:h + 1, :] for h in range(N_HEADS)], axis=1) for gr in g_rows]
    d_hs = [jnp.where(incl4, jnp.exp(g_row_hs[i] - r_hs[i]), 0.0) for i in seqs]
    qs = [q_ref[i].astype(BF16) for i in seqs]
    ks = [k_ref[i] * (HEAD_V ** -0.5) for i in seqs]
    vs = [v_ref[i] for i in seqs]
    cns = [cn_s[i] for i in seqs]
    qk = [lax.dot_general(qs[i], _head_blocks(ks[i].astype(BF16), same_hv), (((1,), (1,)), ((), ())),
                          preferred_element_type=F32) for i in seqs]
    q_cn = [jnp.dot(qs[i], cns[i].astype(BF16), preferred_element_type=F32) for i in seqs]
    scs = [(qk[i] * d_hs[i]).astype(BF16) for i in seqs]
    ones_blocks = same_hv.astype(BF16)
    v_ones = [jnp.concatenate([_head_blocks(vs[i].astype(BF16), same_hv), ones_blocks], axis=1)
              for i in seqs]
    sc_v = [jnp.dot(scs[i], v_ones[i], preferred_element_type=F32) for i in seqs]
    nd = [sc_v[i] + jnp.concatenate([si_hv[i], si_hv[i]], axis=1) * q_cn[i] for i in seqs]
    hcs = [nd[i][:, :W_GROUP] / jnp.maximum(jnp.abs(nd[i][:, W_GROUP:]), emt_hv[i]) for i in seqs]
    kws = [ks[i] * w_hv[i] for i in seqs]
    upds = [_dot_tn(kws[i], jnp.concatenate([vs[i], jnp.ones_like(vs[i])], axis=1)) for i in seqs]
    same_cn = jnp.concatenate([same_vv, same_vv], axis=1)
    sum_sq = _dot01_right([x * x for x in hcs], group_sum)
    ng = ng_ref[...]
    for i in seqs:
        dec_row = dec_hv[i][0:1, :]
        cn_s[i] = jnp.concatenate([dec_row, dec_row], axis=1) * cns[i] + jnp.where(same_cn, upds[i], 0.0)
        m_s[i] = jnp.where(head_lane[0:1, :], b_c[i][c - 1:c, :] + r_end[i], 0.0)
        h_ref[i] = (hcs[i] * lax.rsqrt(sum_sq[i] * (1.0 / HEAD_V) + EPS) * ng * _sigmoid(og_ref[i]))

    @pl.when(j == pl.num_programs(1) - 1)
    def _():
        cn_out[...] = cn_s[...]
        m_out[...] = m_s[...]


def _mlstm_call(p, bias_row, ng, c0, n0, m0, c, nb):
    b, l, _ = p.shape
    nb = math.gcd(nb, b)
    nc = l // c
    blk = lambda cb: pl.BlockSpec((nb, c, W_GROUP), lambda i, j, cb=cb: (i, j, cb))
    full = lambda a: pl.BlockSpec((nb,) + a.shape[1:], lambda i, j: (i,) + (0,) * (a.ndim - 1))
    heads = jnp.arange(N_HEADS)
    blocks = jnp.stack([c0, jnp.broadcast_to(n0[..., None], c0.shape)], axis=0)
    bd = jnp.zeros((2, b, N_HEADS, HEAD_V, N_HEADS, HEAD_V), F32)
    bd = bd.at[:, :, heads, :, heads, :].set(jnp.transpose(blocks, (2, 0, 1, 3, 4)))
    bd = bd.reshape(2, b, W_GROUP, W_GROUP)
    cn0 = jnp.concatenate([bd[0], bd[1]], axis=-1)
    m0 = jnp.zeros((b, 1, LANES), F32).at[:, 0, :N_HEADS].set(m0)
    h, cn_new, m_new = pl.pallas_call(
        functools.partial(_mlstm_kernel, c=c, nb=nb),
        grid=(b // nb, nc),
        in_specs=[blk(PB_MQ), blk(PB_MK), blk(PB_MV), blk(PB_MO),
                  pl.BlockSpec((nb, c, LANES), lambda i, j: (i, j, PB128_GATES)),
                  pl.BlockSpec((1, LANES), lambda i, j: (0, 0)),
                  pl.BlockSpec((1, W_GROUP), lambda i, j: (0, 0)),
                  full(cn0), full(m0)],
        out_specs=[pl.BlockSpec((nb, c, W_GROUP), lambda i, j: (i, j, 0)), full(cn0), full(m0)],
        out_shape=[jax.ShapeDtypeStruct((b, l, W_GROUP), F32), jax.ShapeDtypeStruct(cn0.shape, F32),
                   jax.ShapeDtypeStruct(m0.shape, F32)],
        scratch_shapes=[pltpu.VMEM((nb,) + cn0.shape[1:], F32), pltpu.VMEM((nb, 1, LANES), F32)],
        compiler_params=pltpu.CompilerParams(dimension_semantics=("parallel", "arbitrary"),
                                             vmem_limit_bytes=VMEM_LIMIT),
        name="mlstm",
    )(p, p, p, p, p, bias_row, ng, cn0, m0)
    cn5 = cn_new.reshape(b, N_HEADS, HEAD_V, 2, N_HEADS, HEAD_V)
    diag = cn5[:, heads, :, :, heads, :]
    c_new = jnp.transpose(diag[:, :, :, 0, :], (1, 0, 2, 3))
    n_new = jnp.transpose(diag[:, :, :, 1, 0], (1, 0, 2))
    return h, c_new, n_new, m_new[:, 0, :N_HEADS]


def _same_head(rows, lanes, row_group, lane_group):
    rh = lax.broadcasted_iota(jnp.int32, (rows, lanes), 0) >> (row_group.bit_length() - 1)
    lh = lax.broadcasted_iota(jnp.int32, (rows, lanes), 1) >> (lane_group.bit_length() - 1)
    return rh == lh


def _head_blocks(x, same_head):
    tiled = jnp.concatenate([x] * N_HEADS, axis=0)
    return jnp.where(same_head, tiled, jnp.zeros_like(tiled))


def _heads_dot_split(l, r, same_head):
    return _heads_dot_parts(_split2(l), _split2(r), same_head)


def _heads_dot_parts(l_parts, r_parts, same_head):
    l1, l2 = l_parts
    rb1, rb2 = _head_blocks(r_parts[0], same_head), _head_blocks(r_parts[1], same_head)
    m = l1.shape[0]
    y = jnp.dot(jnp.concatenate([l1, l2], axis=0), rb1, preferred_element_type=F32)
    return y[:m] + (y[m:] + jnp.dot(l1, rb2, preferred_element_type=F32))


def _unit_lower_inverses(lows, eye4, same_head, c):
    ts = [eye4 - a for a in lows]
    a_parts = [_split2(a) for a in lows]
    ps = [_heads_dot_parts(ap, ap, same_head) for ap in a_parts]
    for _ in range(c.bit_length() - 3):
        t_parts = [_split2(t) for t in ts]
        p_parts = [_split2(p) for p in ps]
        stacked = [tuple(jnp.concatenate([tp[n], pp[n]], axis=0) for n in range(2))
                   for tp, pp in zip(t_parts, p_parts)]
        ys = [_heads_dot_parts(sp, pp, same_head) for sp, pp in zip(stacked, p_parts)]
        ts = [t + y[:c] for t, y in zip(ts, ys)]
        ps = [y[c:] for y in ys]
    return [t + _heads_dot_split(t, p, same_head) for t, p in zip(ts, ps)]


def _gdn_kernel(q_ref, k_ref, v_ref, z_ref, g_ref, bias_ref, alog_ref, ng_ref, s0_ref,
                o_ref, s_out, s_s, *, c, nb):
    j = pl.program_id(1)
    seqs = range(nb)

    @pl.when(j == 0)
    def _():
        s_s[...] = s0_ref[...]

    hs = N_HEADS * c
    lane = lax.broadcasted_iota(jnp.int32, (c, LANES), 1)
    is_a = (lane >= G_GA) & (lane < G_GA + N_HEADS)
    incl_b = _tri(c).astype(BF16)
    t_idx = lax.broadcasted_iota(jnp.int32, (c, hs), 0)
    s_idx = lax.broadcasted_iota(jnp.int32, (c, hs), 1) & (c - 1)
    incl4, strict4, eye4 = t_idx >= s_idx, t_idx > s_idx, (t_idx == s_idx).astype(F32)
    same_hs = _same_head(hs, hs, c, c)
    same_hv = _same_head(hs, W_GROUP, c, HEAD_V)
    same_vv = _same_head(W_GROUP, W_GROUP, HEAD_V, HEAD_V)
    group_sum = same_vv.astype(BF16)

    def pick(lanes, group):
        g = lax.broadcasted_iota(jnp.int32, (LANES, 2 * lanes), 0)
        l = lax.broadcasted_iota(jnp.int32, (LANES, 2 * lanes), 1)
        h = (l & (lanes - 1)) >> (group.bit_length() - 1)
        return (g == jnp.where(l < lanes, G_GA, G_GB) + h).astype(BF16)

    pick_hs = pick(hs, c)
    row_sel = (lax.broadcasted_iota(jnp.int32, (8, LANES), 1)
               == G_GA + lax.broadcasted_iota(jnp.int32, (8, LANES), 0)).astype(BF16)
    gates = [g_ref[i] + bias_ref[...] for i in seqs]
    xgs = [jnp.where(is_a, -jnp.exp(alog_ref[...]) * _softplus(g), _sigmoid(g)) for g in gates]
    csums = [jnp.where(is_a, _dot01_left(incl_b, xg), xg) for xg in xgs]
    cols_hs = _dot01_right(csums, pick_hs)
    bc_hs, beta_hs = [x[:, :hs] for x in cols_hs], [x[:, hs:] for x in cols_hs]
    if c == HEAD_V:
        bc_hv, beta_hv = bc_hs, beta_hs
    else:
        cols_hv = _dot01_right(csums, pick(W_GROUP, HEAD_V))
        bc_hv, beta_hv = [x[:, :W_GROUP] for x in cols_hv], [x[:, W_GROUP:] for x in cols_hv]
    b_rows = [sum(lax.dot_general(row_sel, part, (((1,), (1,)), ((), ())), preferred_element_type=F32)
                  for part in _split3(cs)) for cs in csums]
    br_hs = [jnp.concatenate([br[h:h + 1, :] for h in range(N_HEADS)], axis=1) for br in b_rows]
    decays = [jnp.exp(jnp.where(incl4, bc_hs[i] - br_hs[i], 0.0)) for i in seqs]
    qk_raw = [jnp.concatenate([q_ref[i], k_ref[i]], axis=0) for i in seqs]
    norms = _dot01_right([x * x for x in qk_raw], group_sum)
    qk_n = [qk_raw[i] * lax.rsqrt(norms[i] + EPS) for i in seqs]
    qs = [x[:c] * (HEAD_V ** -0.5) for x in qk_n]
    ks = [x[c:] for x in qk_n]
    vs = [v_ref[i] for i in seqs]
    kq = [jnp.concatenate([ks[i], qs[i]], axis=0).astype(BF16) for i in seqs]
    states = [s_s[i] for i in seqs]
    scores = [lax.dot_general(kq[i], _head_blocks(ks[i].astype(BF16), same_hv),
                              (((1,), (1,)), ((), ())), preferred_element_type=F32) for i in seqs]
    on_state = [jnp.dot(kq[i], states[i].astype(BF16), preferred_element_type=F32) for i in seqs]
    lows = [jnp.where(strict4, beta_hs[i] * decays[i] * scores[i][:c], 0.0) for i in seqs]
    invs = _unit_lower_inverses(lows, eye4, same_hs, c)
    ebs = [jnp.exp(bc_hv[i]) for i in seqs]
    rhss = [beta_hv[i] * (vs[i] - ebs[i] * on_state[i][:c]) for i in seqs]
    us = [_heads_dot_split(invs[i], rhss[i], same_hv) for i in seqs]
    qkm = [jnp.where(incl4, decays[i] * scores[i][c:], 0.0).astype(BF16) for i in seqs]
    outs = [ebs[i] * on_state[i][c:]
            + jnp.dot(qkm[i], _head_blocks(us[i].astype(BF16), same_hv), preferred_element_type=F32)
            for i in seqs]
    b_ends = [bc_hv[i][c - 1:c, :] for i in seqs]
    upds = [jnp.where(same_vv, _dot_tn(ks[i] * jnp.exp(b_ends[i] - bc_hv[i]), us[i]), 0.0) for i in seqs]
    ng = ng_ref[...]
    sum_sq = _dot01_right([o * o for o in outs], group_sum)
    for i in seqs:
        s_s[i] = jnp.exp(b_ends[i]) * states[i] + upds[i]
        o_ref[i] = (outs[i] * lax.rsqrt(sum_sq[i] * (1.0 / HEAD_V) + EPS) * ng * _silu(z_ref[i]))

    @pl.when(j == pl.num_programs(1) - 1)
    def _():
        s_out[...] = s_s[...]


def _gdn_call(p, bias_row, alog_row, ng, s0, c, nb):
    b, l, _ = p.shape
    nb = math.gcd(nb, b)
    nc = l // c
    blk = lambda cb: pl.BlockSpec((nb, c, W_GROUP), lambda i, j, cb=cb: (i, j, cb))
    full = lambda a: pl.BlockSpec((nb,) + a.shape[1:], lambda i, j: (i,) + (0,) * (a.ndim - 1))
    heads = jnp.arange(N_HEADS)
    s_bd = jnp.zeros((b, N_HEADS, HEAD_V, N_HEADS, HEAD_V), F32)
    s_bd = s_bd.at[:, heads, :, heads, :].set(jnp.transpose(s0, (1, 0, 2, 3)))
    s0 = s_bd.reshape(b, W_GROUP, W_GROUP)
    o, s_new = pl.pallas_call(
        functools.partial(_gdn_kernel, c=c, nb=nb),
        grid=(b // nb, nc),
        in_specs=[blk(PB_GQ), blk(PB_GK), blk(PB_GV), blk(PB_GZ),
                  pl.BlockSpec((nb, c, LANES), lambda i, j: (i, j, PB128_GATES)),
                  pl.BlockSpec((1, LANES), lambda i, j: (0, 0)),
                  pl.BlockSpec((1, LANES), lambda i, j: (0, 0)),
                  pl.BlockSpec((1, W_GROUP), lambda i, j: (0, 0)),
                  full(s0)],
        out_specs=[pl.BlockSpec((nb, c, W_GROUP), lambda i, j: (i, j, 0)), full(s0)],
        out_shape=[jax.ShapeDtypeStruct((b, l, W_GROUP), F32), jax.ShapeDtypeStruct(s0.shape, F32)],
        scratch_shapes=[pltpu.VMEM((nb,) + s0.shape[1:], F32)],
        compiler_params=pltpu.CompilerParams(dimension_semantics=("parallel", "arbitrary"),
                                             vmem_limit_bytes=VMEM_LIMIT),
        name="gdn",
    )(p, p, p, p, p, bias_row, alog_row, ng, s0)
    s_new = s_new.reshape(b, N_HEADS, HEAD_V, N_HEADS, HEAD_V)
    s_new = jnp.transpose(s_new[:, heads, :, heads, :], (1, 0, 2, 3))
    return o, s_new


def _gla_kernel(q_ref, k_ref, v_ref, r_ref, g_ref, w2_ref, gb_ref, ng_ref, st0_ref,
                o_ref, st_out, st_s, b_s, q_s, a_s, *, c, nb):
    j = pl.program_id(1)
    dk_all = N_HEADS * GLA_DK
    seqs = range(nb)

    @pl.when(j == 0)
    def _():
        st_s[...] = st0_ref[...]

    incl = _tri(c).astype(BF16)
    log_as = [_log_sigmoid(_dot(g_ref[i], w2_ref[...]) + gb_ref[...]) / GLA_TAU for i in seqs]
    bs = [_dot01_left(incl, la) for la in log_as]
    qs = [q_ref[i] * (GLA_DK ** -0.5) for i in seqs]
    ks = [k_ref[i] for i in seqs]
    vs = [v_ref[i] for i in seqs]
    for i in seqs:
        b_s[i] = bs[i]
        q_s[i] = qs[i]
    hr = lax.broadcasted_iota(jnp.int32, (8, dk_all), 0)
    hc = lax.broadcasted_iota(jnp.int32, (8, dk_all), 1) // GLA_DK
    head_sel = (hr == hc).astype(BF16)
    sub = lax.broadcasted_iota(jnp.int32, (8, dk_all), 0)
    a_s[...] = jnp.zeros_like(a_s)
    for t in range(c):
        n = 8 * (t // 8 + 1)
        for i in seqs:
            bt = b_s[i, t:t + 1, :]
            qt = q_s[i, t:t + 1, :]
            w = jnp.exp(bt - bs[i][:n]) * ks[i][:n] * qt
            last = jnp.where(sub <= t % 8, w[n - 8:], 0.0)
            w = last if n == 8 else jnp.concatenate([w[:n - 8], last], axis=0)
            rows = _dot_nt(head_sel, w)
            for h in range(N_HEADS):
                a_s[i, h, t:t + 1, 0:n] = rows[h:h + 1, :]
    sts = [st_s[i] for i in seqs]
    inter = [_dot_nt(qs[i] * jnp.exp(bs[i]), sts[i]) for i in seqs]
    outs = [jnp.concatenate([_dot(a_s[i, h], vs[i][:, _hs(h)]) for h in range(N_HEADS)], axis=-1)
            + inter[i] for i in seqs]
    b_ends = [bs[i][c - 1:c, :] for i in seqs]
    blk_r = lax.broadcasted_iota(jnp.int32, sts[0].shape, 0) // HEAD_V
    blk_c = lax.broadcasted_iota(jnp.int32, sts[0].shape, 1) // GLA_DK
    upds = [jnp.where(blk_r == blk_c, _dot_tn(vs[i], ks[i] * jnp.exp(b_ends[i] - bs[i])), 0.0)
            for i in seqs]
    ng = ng_ref[...]
    for i in seqs:
        st_s[i] = sts[i] * jnp.exp(b_ends[i]) + upds[i]
        r = r_ref[i]
        for h in range(N_HEADS):
            o_ref[i, :, _hs(h)] = _head_norm_gate(outs[i][:, _hs(h)], ng[:, _hs(h)],
                                                  _silu(r[:, _hs(h)]))

    @pl.when(j == pl.num_programs(1) - 1)
    def _():
        st_out[...] = st_s[...]


def _gla_call(p, w2_pad, gb_row, ng, s0, c, nb):
    b, l, _ = p.shape
    nb = math.gcd(nb, b)
    nc = l // c
    dk_all = N_HEADS * GLA_DK
    st0 = jnp.zeros((b, N_HEADS, HEAD_V, N_HEADS, GLA_DK), F32)
    st0 = st0.at[:, jnp.arange(N_HEADS), :, jnp.arange(N_HEADS), :].set(
        jnp.transpose(s0, (1, 0, 3, 2)))
    st0 = st0.reshape(b, W_GROUP, dk_all)
    blk = lambda cb: pl.BlockSpec((nb, c, W_GROUP), lambda i, j, cb=cb: (i, j, cb))
    blk128 = lambda cb: pl.BlockSpec((nb, c, LANES), lambda i, j, cb=cb: (i, j, cb))
    o, st_new = pl.pallas_call(
        functools.partial(_gla_kernel, c=c, nb=nb),
        grid=(b // nb, nc),
        in_specs=[blk128(PB128_LQ), blk128(PB128_LK), blk(PB_LV), blk(PB_LR), blk128(PB128_GATES),
                  pl.BlockSpec((LANES, dk_all), lambda i, j: (0, 0)),
                  pl.BlockSpec((1, dk_all), lambda i, j: (0, 0)),
                  pl.BlockSpec((1, W_GROUP), lambda i, j: (0, 0)),
                  pl.BlockSpec((nb, W_GROUP, dk_all), lambda i, j: (i, 0, 0))],
        out_specs=[pl.BlockSpec((nb, c, W_GROUP), lambda i, j: (i, j, 0)),
                   pl.BlockSpec((nb, W_GROUP, dk_all), lambda i, j: (i, 0, 0))],
        out_shape=[jax.ShapeDtypeStruct((b, l, W_GROUP), F32),
                   jax.ShapeDtypeStruct((b, W_GROUP, dk_all), F32)],
        scratch_shapes=[pltpu.VMEM((nb, W_GROUP, dk_all), F32), pltpu.VMEM((nb, c, dk_all), F32),
                        pltpu.VMEM((nb, c, dk_all), F32), pltpu.VMEM((nb, N_HEADS, c, c), F32)],
        compiler_params=pltpu.CompilerParams(dimension_semantics=("parallel", "arbitrary"),
                                             vmem_limit_bytes=VMEM_LIMIT),
        name="gla",
    )(p, p, p, p, p, w2_pad, gb_row, ng, st0)
    st5 = st_new.reshape(b, N_HEADS, HEAD_V, N_HEADS, GLA_DK)
    s_new = jnp.transpose(st5[:, jnp.arange(N_HEADS), :, jnp.arange(N_HEADS), :], (1, 0, 3, 2))
    return o, s_new


def _outffn_kernel(x_ref, a_ref, b_ref, c_ref, d_ref, mod_ref, g2_ref, gf_ref, wo_ref, w1_ref, w2_ref,
                   o_ref, *, ff_chunk, final):
    mod = mod_ref[0]
    mixed = jnp.concatenate([a_ref[0], b_ref[0], c_ref[0], d_ref[0]], axis=-1).astype(BF16)
    x = x_ref[0] + mod[2:3] * jnp.dot(mixed, wo_ref[...], preferred_element_type=F32)
    h = _modulated_norm(x, g2_ref[...], mod[4:5], mod[3:4]).astype(BF16)
    d_ff = w1_ref.shape[1]
    acc = jnp.zeros(x.shape, F32)
    for f0 in range(0, d_ff, ff_chunk):
        a = jnp.maximum(jnp.dot(h, w1_ref[:, f0:f0 + ff_chunk], preferred_element_type=F32), 0.0)
        acc = acc + jnp.dot((a * a).astype(BF16), w2_ref[f0:f0 + ff_chunk, :],
                            preferred_element_type=F32)
    x = x + mod[5:6] * acc
    if final:
        x = x * lax.rsqrt(jnp.mean(x * x, axis=-1, keepdims=True) + EPS) * gf_ref[...]
    o_ref[0] = x


def _outffn_call(x, mixers, mod, g2, gf, wo, w1, w2, tm, final):
    b, l, d = x.shape
    d_ff = w1.shape[1]
    tok = lambda w: pl.BlockSpec((1, tm, w), lambda i, j: (i, j, 0))
    const = lambda shape: pl.BlockSpec(shape, lambda i, j: (0,) * len(shape),
                                       pipeline_mode=pl.Buffered(1))
    return pl.pallas_call(
        functools.partial(_outffn_kernel, ff_chunk=1024, final=final),
        grid=(b, l // tm),
        in_specs=[tok(d), tok(W_GROUP), tok(W_GROUP), tok(W_GROUP), tok(W_GROUP),
                  pl.BlockSpec((1, 6, d), lambda i, j: (i, 0, 0)),
                  pl.BlockSpec((1, d), lambda i, j: (0, 0)),
                  pl.BlockSpec((1, d), lambda i, j: (0, 0)),
                  const((d, d)), const((d, d_ff)), const((d_ff, d))],
        out_specs=tok(d),
        out_shape=jax.ShapeDtypeStruct((b, l, d), F32),
        compiler_params=pltpu.CompilerParams(dimension_semantics=("parallel", "parallel"),
                                             vmem_limit_bytes=VMEM_LIMIT),
        name="outproj_ffn",
    )(x, *mixers, mod, g2, gf, wo, w1, w2)


def _permute_w_in(w_in):
    d = w_in.shape[0]
    sizes = (256, 256, 256, 256, 256, 256, 4, 4, 256, 768, 4, 4, 256, 128, 128, 256, 16, 256)
    offs = [0]
    for s in sizes:
        offs.append(offs[-1] + s)
    (sq, sk, sv, mq, mk, mv, mi, mf, mo, gqkv, gb, ga, gz, lq, lk, lv, lg, lr) = [
        w_in[:, offs[i]:offs[i + 1]] for i in range(len(sizes))]
    pad = jnp.zeros((d, LANES - (4 * N_HEADS + GLA_RANK)), w_in.dtype)
    return jnp.concatenate([sk, sv, sq, mq, mk, mv, mo, gqkv, gz, lv, lr, lq, lk,
                            mi, mf, gb, ga, lg, pad], axis=1)


def _row128(pieces):
    row = jnp.zeros((LANES,), F32)
    for off, vec in pieces:
        row = row.at[off:off + vec.shape[0]].set(vec)
    return row.reshape(1, LANES)


class _Tiles(NamedTuple):
    tm: int
    sb_tq: int
    sb_tk: int
    mlstm_chunk: int
    chunk: int


def _tiles(l):
    return _Tiles(tm=min(512, l), sb_tq=min(256, l), sb_tk=max(min(256, l), LANES),
                  mlstm_chunk=min(64, l), chunk=min(64, l))


def _layer(x, mod, lw, states, tiles, final, final_g):
    tm, sb_tq, sb_tk, mlstm_chunk, chunk = tiles
    (n1, n2, w_in_p, gate_bias, ml_ng, conv_w, alog_row, gdn_ng, w2_pad, gla_gb, gla_ng,
     w_out, w_ff1, w_ff2) = lw
    (sb_k_past, sb_v_past, ml_c, ml_n, ml_m, gdn_s, gdn_buf, gla_s) = states
    b, l, _ = x.shape
    k_new, v_new, p, gdn_buf = _inproj_call(x, mod, n1, w_in_p, conv_w, gdn_buf, tm)
    if sb_k_past is None:
        k_all, v_all, q_offset = k_new, v_new, 0
    else:
        past = sb_k_past.shape[1]
        lk = -(-(past + l) // sb_tk) * sb_tk
        padz = jnp.zeros((b, lk - past - l, W_GROUP), F32)
        k_all = jnp.concatenate([sb_k_past.reshape(b, past, W_GROUP), k_new, padz], axis=1)
        v_all = jnp.concatenate([sb_v_past.reshape(b, past, W_GROUP), v_new, padz], axis=1)
        q_offset = past
    o_sb = _sb_call(p, PB_SQ, k_all, v_all, l, sb_tq, sb_tk, q_offset)
    o_ml, ml_c, ml_n, ml_m = _mlstm_call(p, gate_bias, ml_ng, ml_c, ml_n, ml_m, mlstm_chunk,
                                         MLSTM_SEQS_PER_STEP)
    o_gdn, gdn_s = _gdn_call(p, gate_bias, alog_row, gdn_ng, gdn_s, chunk, GDN_SEQS_PER_STEP)
    o_gla, gla_s = _gla_call(p, w2_pad, gla_gb, gla_ng, gla_s, chunk, GLA_SEQS_PER_STEP)
    x = _outffn_call(x, (o_sb, o_ml, o_gdn, o_gla), mod, n2, final_g, w_out, w_ff1, w_ff2, tm, final)
    hk = lambda a: a.reshape(b, l, N_HEADS, HEAD_V)
    return x, (hk(k_new), hk(v_new), ml_c, ml_n, ml_m, gdn_s, gdn_buf, gla_s)


def kernel(x_prompt, x_sample, cache_sb_k, cache_sb_v, state_mlstm_C, state_mlstm_n, state_mlstm_m, state_gdn_S, state_gdn_conv, state_gla_S, c_prompt, c_sample, norm1_g, norm2_g, w_ada, b_ada, w_in, mlstm_i_bias, mlstm_f_bias, mlstm_norm_g, gdn_conv_w, gdn_a_log, gdn_dt_bias, gdn_norm_g, gla_w_gate2, gla_gate_bias, gla_norm_g, w_out, w_ff1, w_ff2, final_g):
    depth = w_in.shape[0]
    bp, lp, d = x_prompt.shape
    bs, ls, _ = x_sample.shape
    dk_all = N_HEADS * GLA_DK
    mod_all = _ada_call(jnp.concatenate([c_prompt, c_sample], axis=0), w_ada, b_ada)
    mod_all = mod_all.reshape(depth, bp + bs, 6, d)
    final_row = final_g.reshape(1, d)
    xp, xs = x_prompt, x_sample
    p_list, s_list = [], []
    for l in range(depth):
        gate_bias = _row128([(G_MI, mlstm_i_bias[l]), (G_MF, mlstm_f_bias[l]), (G_GA, gdn_dt_bias[l])])
        alog_row = _row128([(G_GA, gdn_a_log[l])])
        w2_pad = jnp.zeros((LANES, dk_all), F32).at[G_LG:G_LG + GLA_RANK, :].set(gla_w_gate2[l])
        lw = (norm1_g[l].reshape(1, d), norm2_g[l].reshape(1, d), _permute_w_in(w_in[l]).astype(BF16),
              gate_bias, mlstm_norm_g[l].reshape(1, W_GROUP), gdn_conv_w[l], alog_row,
              gdn_norm_g[l].reshape(1, W_GROUP), w2_pad.astype(BF16),
              gla_gate_bias[l].reshape(1, dk_all), gla_norm_g[l].reshape(1, W_GROUP),
              w_out[l].astype(BF16), w_ff1[l].astype(BF16), w_ff2[l].astype(BF16))
        fresh = (None, None, jnp.zeros((bp,) + state_mlstm_C.shape[2:], F32),
                 jnp.zeros((bp,) + state_mlstm_n.shape[2:], F32),
                 jnp.zeros((bp,) + state_mlstm_m.shape[2:], F32),
                 jnp.zeros((bp,) + state_gdn_S.shape[2:], F32),
                 jnp.zeros((bp,) + state_gdn_conv.shape[2:], F32),
                 jnp.zeros((bp,) + state_gla_S.shape[2:], F32))
        past = (cache_sb_k[l], cache_sb_v[l], state_mlstm_C[l], state_mlstm_n[l], state_mlstm_m[l],
                state_gdn_S[l], state_gdn_conv[l], state_gla_S[l])
        final = l == depth - 1
        xp, st_p = _layer(xp, mod_all[l, :bp], lw, fresh, _tiles(lp), final, final_row)
        xs, st_s = _layer(xs, mod_all[l, bp:], lw, past, _tiles(ls), final, final_row)
        p_list.append(st_p)
        s_list.append(st_s)
    stacked_p = [jnp.stack([st[i] for st in p_list]) for i in range(8)]
    stacked_s = [jnp.stack([st[i] for st in s_list]) for i in range(8)]
    return (xp, xs, *stacked_p, *stacked_s)
```

```python
import functools
import math
from typing import NamedTuple

import jax
import jax.numpy as jnp
from jax import lax
from jax.experimental import pallas as pl
from jax.experimental.pallas import tpu as pltpu

F32 = jnp.float32
BF16 = jnp.bfloat16
EPS = 1e-6
N_HEADS = 4
HEAD_V = 64
GLA_DK = 32
CONV_W = 4
GLA_TAU = 16.0
W_GROUP = N_HEADS * HEAD_V
LANES = 128
VMEM_LIMIT = 56 * 1024 * 1024
MLSTM_SEQS_PER_STEP = 4
GDN_SEQS_PER_STEP = 4
GLA_SEQS_PER_STEP = 4

PB_SQ, PB_MQ, PB_MK, PB_MV, PB_MO, PB_GQ, PB_GK, PB_GV, PB_GZ, PB_LV, PB_LR = range(11)
PB128_LQ, PB128_LK, PB128_GATES = 22, 23, 24
P_WIDTH = 25 * LANES
G_MI, G_MF, G_GB, G_GA, G_LG = 0, 4, 8, 12, 16
GLA_RANK = 16


def _dot(a, b):
    return jnp.dot(a.astype(BF16), b.astype(BF16), preferred_element_type=F32)


def _dot_tn(a, b):
    return lax.dot_general(a.astype(BF16), b.astype(BF16), (((0,), (0,)), ((), ())),
                           preferred_element_type=F32)


def _split3(x):
    x1 = x.astype(BF16)
    r1 = x - x1.astype(F32)
    x2 = r1.astype(BF16)
    x3 = (r1 - x2.astype(F32)).astype(BF16)
    return x1, x2, x3


def _split2(x):
    x1 = x.astype(BF16)
    x2 = (x - x1.astype(F32)).astype(BF16)
    return x1, x2


def _dot01_left(m01, x):
    return sum(jnp.dot(m01, p, preferred_element_type=F32) for p in _split3(x))


def _dot01_right(xs, m01):
    n = xs[0].shape[0]
    stacked = jnp.concatenate([part for x in xs for part in _split3(x)], axis=0)
    y = jnp.dot(stacked, m01, preferred_element_type=F32)
    return [y[3 * i * n:(3 * i + 1) * n] + (y[(3 * i + 1) * n:(3 * i + 2) * n]
                                             + y[(3 * i + 2) * n:(3 * i + 3) * n])
            for i in range(len(xs))]


def _log_sigmoid(x):
    return jnp.minimum(x, 0.0) - jnp.log(1.0 + jnp.exp(-jnp.abs(x)))


def _sigmoid(x):
    return 1.0 / (1.0 + jnp.exp(-x))


def _silu(x):
    return x * _sigmoid(x)


def _softplus(x):
    return jnp.maximum(x, 0.0) + jnp.log(1.0 + jnp.exp(-jnp.abs(x)))


def _tri(c):
    r = lax.broadcasted_iota(jnp.int32, (c, c), 0)
    s = lax.broadcasted_iota(jnp.int32, (c, c), 1)
    return r >= s


def _hs(h, w=HEAD_V):
    return slice(h * w, (h + 1) * w)


def _ada_kernel(c_ref, w_ref, b_ref, o_ref):
    o_ref[0] = _dot(_silu(c_ref[...]), w_ref[0]) + b_ref[0]


def _ada_call(c_all, w_ada, b_ada, tn=512):
    depth, d, n = w_ada.shape
    rows = c_all.shape[0]
    return pl.pallas_call(
        _ada_kernel,
        grid=(depth, n // tn),
        in_specs=[pl.BlockSpec((rows, d), lambda l, j: (0, 0)),
                  pl.BlockSpec((1, d, tn), lambda l, j: (l, 0, j)),
                  pl.BlockSpec((1, 1, tn), lambda l, j: (l, 0, j))],
        out_specs=pl.BlockSpec((1, rows, tn), lambda l, j: (l, 0, j)),
        out_shape=jax.ShapeDtypeStruct((depth, rows, n), F32),
        compiler_params=pltpu.CompilerParams(dimension_semantics=("parallel", "parallel"),
                                             vmem_limit_bytes=VMEM_LIMIT),
        name="ada_mod",
    )(c_all, w_ada, b_ada.reshape(depth, 1, n))


def _modulated_norm(x, g_row, scale_row, shift_row):
    y = x * lax.rsqrt(jnp.mean(x * x, axis=-1, keepdims=True) + EPS) * g_row
    return y * (1.0 + scale_row) + shift_row


def _inproj_kernel(x_ref, mod_ref, g_ref, w_ref, cw_ref, cb0_ref, k_ref, v_ref, p_ref, cb_out, xw_s,
                   *, col_chunk):
    j = pl.program_id(1)
    tm = x_ref.shape[1]
    conv_lo, conv_hi = PB_GQ * W_GROUP, (PB_GV + 1) * W_GROUP

    @pl.when(j == 0)
    def _():
        xw_s[0:8, :] = cb0_ref[0]

    mod = mod_ref[0]
    h = _modulated_norm(x_ref[0], g_ref[...], mod[1:2], mod[0:1]).astype(BF16)
    k_ref[0] = jnp.dot(h, w_ref[:, 0:W_GROUP], preferred_element_type=F32)
    v_ref[0] = jnp.dot(h, w_ref[:, W_GROUP:2 * W_GROUP], preferred_element_type=F32)
    base = 2 * W_GROUP
    bounds = sorted(set(range(0, conv_lo, col_chunk)) | {conv_lo, conv_hi}
                    | set(range(conv_hi, P_WIDTH, col_chunk)) | {P_WIDTH})
    for n0, n1 in zip(bounds[:-1], bounds[1:]):
        if n0 == conv_lo:
            continue
        p_ref[0, :, n0:n1] = jnp.dot(h, w_ref[:, base + n0:base + n1], preferred_element_type=F32)
    raw = jnp.dot(h, w_ref[:, base + conv_lo:base + conv_hi], preferred_element_type=F32)
    xw_s[8:8 + tm, :] = raw
    cw = cw_ref[...]
    y = raw * cw[CONV_W - 1:CONV_W, :]
    for d in range(1, CONV_W):
        y = y + xw_s[8 - d:8 - d + tm, :] * cw[CONV_W - 1 - d:CONV_W - d, :]
    p_ref[0, :, conv_lo:conv_hi] = _silu(y)
    xw_s[0:8, :] = raw[tm - 8:tm, :]

    @pl.when(j == pl.num_programs(1) - 1)
    def _():
        cb_out[0] = raw[tm - 8:tm, :]


def _inproj_call(x, mod, g, w_perm, conv_w, conv0, tm):
    b, l, d = x.shape
    n_all = w_perm.shape[1]
    conv_dim = conv_w.shape[1]
    conv0_pad = jnp.concatenate([jnp.zeros((b, 8 - (CONV_W - 1), conv_dim), F32), conv0], axis=1)
    k, v, p, conv_new = pl.pallas_call(
        functools.partial(_inproj_kernel, col_chunk=512),
        grid=(b, l // tm),
        in_specs=[pl.BlockSpec((1, tm, d), lambda i, j: (i, j, 0)),
                  pl.BlockSpec((1, 6, d), lambda i, j: (i, 0, 0)),
                  pl.BlockSpec((1, d), lambda i, j: (0, 0)),
                  pl.BlockSpec((d, n_all), lambda i, j: (0, 0), pipeline_mode=pl.Buffered(1)),
                  pl.BlockSpec((CONV_W, conv_dim), lambda i, j: (0, 0)),
                  pl.BlockSpec((1, 8, conv_dim), lambda i, j: (i, 0, 0))],
        out_specs=[pl.BlockSpec((1, tm, W_GROUP), lambda i, j: (i, j, 0)),
                   pl.BlockSpec((1, tm, W_GROUP), lambda i, j: (i, j, 0)),
                   pl.BlockSpec((1, tm, P_WIDTH), lambda i, j: (i, j, 0)),
                   pl.BlockSpec((1, 8, conv_dim), lambda i, j: (i, 0, 0))],
        out_shape=[jax.ShapeDtypeStruct((b, l, W_GROUP), F32),
                   jax.ShapeDtypeStruct((b, l, W_GROUP), F32),
                   jax.ShapeDtypeStruct((b, l, P_WIDTH), F32),
                   jax.ShapeDtypeStruct((b, 8, conv_dim), F32)],
        scratch_shapes=[pltpu.VMEM((tm + 8, conv_dim), F32)],
        compiler_params=pltpu.CompilerParams(dimension_semantics=("parallel", "arbitrary"),
                                             vmem_limit_bytes=VMEM_LIMIT),
        name="norm_inproj",
    )(x, mod, g, w_perm, conv_w, conv0_pad)
    return k, v, p, conv_new[:, 8 - (CONV_W - 1):, :]


SB_DEAD_LOG = -104.0


def _sb_kernel(q_ref, k_ref, v_ref, o_ref, *, tq, tk, q_offset):
    i = pl.program_id(1)
    q0 = q_offset + i * tq
    j_top = (q0 + tq - 1) // tk
    n_masked = max(tq // tk, 1)
    rj = lax.broadcasted_iota(jnp.int32, (tk, tk + LANES), 0)
    cs = lax.broadcasted_iota(jnp.int32, (tk, tk + LANES), 1)
    suffix_and_total = ((rj > cs) | (cs >= tk)).astype(BF16)
    q_all = q_ref[0] * (HEAD_V ** -0.5)
    qs = [q_all[:, _hs(h)].astype(BF16) for h in range(N_HEADS)]

    def block(j, accs, runs, causal):
        s0 = pl.multiple_of(j * tk, tk)
        heads = range(N_HEADS)
        kbs = [k_ref[0, pl.ds(s0, tk), _hs(h)].astype(BF16) for h in heads]
        vbs = [v_ref[0, pl.ds(s0, tk), _hs(h)].astype(BF16) for h in heads]
        zs = [lax.dot_general(qs[h], kbs[h], (((1,), (1,)), ((), ())), preferred_element_type=F32)
              for h in heads]
        lg1ms = [_log_sigmoid(-z) for z in zs]
        if causal is not None:
            lg1ms = [jnp.where(causal, x, 0.0) for x in lg1ms]
        splits = [_split2(x) for x in lg1ms]
        sts = [jnp.dot(hi, suffix_and_total, preferred_element_type=F32)
               + jnp.dot(lo, suffix_and_total, preferred_element_type=F32) for hi, lo in splits]
        wide = lambda r: r if tk == LANES else jnp.concatenate([r] * (tk // LANES), axis=1)
        probs = [jnp.exp((zs[h] + lg1ms[h]) + (sts[h][:, :tk] + wide(runs[h]))) for h in heads]
        if causal is not None:
            probs = [jnp.where(causal, a, 0.0) for a in probs]
        new_accs = [accs[h] + jnp.dot(probs[h].astype(BF16), vbs[h], preferred_element_type=F32)
                    for h in heads]
        new_runs = [runs[h] + sts[h][:, tk:] for h in heads]
        return tuple(new_accs), tuple(new_runs)

    def live_of(runs):
        return jnp.max(jnp.maximum(jnp.maximum(runs[0], runs[1]), jnp.maximum(runs[2], runs[3])))

    q_pos = q0 + lax.broadcasted_iota(jnp.int32, (tq, tk), 0)
    k_off = lax.broadcasted_iota(jnp.int32, (tq, tk), 1)
    zeros = lambda w: tuple(jnp.zeros((tq, w), F32) for _ in range(N_HEADS))
    accs, runs = zeros(HEAD_V), zeros(LANES)
    for m in range(n_masked):
        accs, runs = block(j_top - m, accs, runs, (j_top - m) * tk + k_off < q_pos)

    def cond(carry):
        j, live, _, _ = carry
        return (j >= 0) & (live > SB_DEAD_LOG)

    def body(carry):
        j, _, accs, runs = carry
        accs, runs = block(j, accs, runs, None)
        return j - 1, live_of(runs), accs, runs

    _, _, accs, _ = lax.while_loop(cond, body, (j_top - n_masked, live_of(runs), accs, runs))
    for h in range(N_HEADS):
        o_ref[0, :, _hs(h)] = accs[h]


def _sb_call(q_src, q_block, k_all, v_all, l, tq, tk, q_offset):
    assert (tq % tk == 0 or tk % tq == 0) and q_offset % max(tq, tk) == 0 and tk % LANES == 0
    b = q_src.shape[0]
    lk = k_all.shape[1]
    return pl.pallas_call(
        functools.partial(_sb_kernel, tq=tq, tk=tk, q_offset=q_offset),
        grid=(b, l // tq),
        in_specs=[pl.BlockSpec((1, tq, W_GROUP), lambda i, j: (i, j, q_block)),
                  pl.BlockSpec((1, lk, W_GROUP), lambda i, j: (i, 0, 0)),
                  pl.BlockSpec((1, lk, W_GROUP), lambda i, j: (i, 0, 0))],
        out_specs=pl.BlockSpec((1, tq, W_GROUP), lambda i, j: (i, j, 0)),
        out_shape=jax.ShapeDtypeStruct((b, l, W_GROUP), F32),
        compiler_params=pltpu.CompilerParams(dimension_semantics=("parallel", "parallel"),
                                             vmem_limit_bytes=VMEM_LIMIT),
        name="sb_attention",
    )(q_src, k_all, v_all)


def _cummax_rows(x):
    r = x.shape[0]
    row = lax.broadcasted_iota(jnp.int32, x.shape, 0)
    shift = 1
    while shift < r:
        x = jnp.maximum(x, jnp.where(row >= shift, pltpu.roll(x, shift, 0), -jnp.inf))
        shift *= 2
    return x


def _mlstm_kernel(q_ref, k_ref, v_ref, og_ref, g_ref, bias_ref, ng_ref, c0_ref, n0_ref, m0_ref,
                  h_ref, c_out, n_out, m_out, cn_s, m_s, *, c, nb):
    j = pl.program_id(1)

    @pl.when(j == 0)
    def _():
        cn_s[...] = jnp.zeros_like(cn_s)
        m_s[...] = jnp.zeros_like(m_s)
        for i in range(nb):
            for h in range(N_HEADS):
                cn_s[i, _hs(h), _hs(h)] = c0_ref[i, h]
                cn_s[i, _hs(h), W_GROUP + h * HEAD_V:W_GROUP + (h + 1) * HEAD_V] = jnp.broadcast_to(
                    n0_ref[i, h], (HEAD_V, HEAD_V))
            m_s[i, :, 0:N_HEADS] = m0_ref[i]

    seqs = range(nb)
    hs = N_HEADS * c
    lane = lax.broadcasted_iota(jnp.int32, (c, LANES), 1)
    is_f = (lane >= G_MF) & (lane < G_MF + N_HEADS)
    head_lane = lane < N_HEADS
    incl_b = _tri(c).astype(BF16)
    t_idx = lax.broadcasted_iota(jnp.int32, (c, hs), 0)
    s_idx = lax.broadcasted_iota(jnp.int32, (c, hs), 1) & (c - 1)
    incl4 = t_idx >= s_idx
    same_hv = _same_head(hs, W_GROUP, c, HEAD_V)
    same_vv = _same_head(W_GROUP, W_GROUP, HEAD_V, HEAD_V)
    group_sum = same_vv.astype(BF16)

    def expand(lanes, group):
        g = lax.broadcasted_iota(jnp.int32, (LANES, lanes), 0)
        h = lax.broadcasted_iota(jnp.int32, (LANES, lanes), 1) >> (group.bit_length() - 1)
        return (g == h).astype(BF16)

    gi = lax.broadcasted_iota(jnp.int32, (LANES, 2 * LANES), 0)
    li = lax.broadcasted_iota(jnp.int32, (LANES, 2 * LANES), 1)
    to_heads = (jnp.where((li < N_HEADS) & (gi == G_MI + li), 1.0, 0.0)
                - jnp.where((li < N_HEADS) & (gi == G_MF + li), 1.0, 0.0)
                + jnp.where((li >= LANES) & (li < LANES + N_HEADS) & (gi == G_MF + li - LANES), 1.0, 0.0)
                ).astype(BF16)
    row_sel = (lax.broadcasted_iota(jnp.int32, (8, LANES), 1)
               == lax.broadcasted_iota(jnp.int32, (8, LANES), 0)).astype(BF16)
    gates = [g_ref[i] + bias_ref[...] for i in seqs]
    merged = [jnp.where(is_f, _dot01_left(incl_b, jnp.where(is_f, _log_sigmoid(g), 0.0)), g) for g in gates]
    gb = _dot01_right(merged, to_heads)
    g_c = [jnp.where(head_lane, x[:, :LANES], -jnp.inf) for x in gb]
    b_c = [x[:, LANES:] for x in gb]
    m_prev = [m_s[i] for i in seqs]
    r_c = [jnp.maximum(m_prev[i], _cummax_rows(g_c[i])) for i in seqs]
    r_end = [r[c - 1:c, :] for r in r_c]
    si_c = [jnp.exp(m_prev[i] - r_c[i]) for i in seqs]
    emt_c = [jnp.exp(-(b_c[i] + r_c[i])) for i in seqs]
    w_c = [jnp.exp(g_c[i] - r_end[i]) for i in seqs]
    dec_c = [jnp.broadcast_to(jnp.exp(m_prev[i] - r_end[i]), (c, LANES)) for i in seqs]
    zero_pad = lambda x: jnp.where(head_lane, x, 0.0)
    r_hs = _dot01_right([zero_pad(r) for r in r_c], expand(hs, c))
    cols = _dot01_right([zero_pad(x) for i in seqs for x in (si_c[i], emt_c[i], w_c[i], dec_c[i])],
                        expand(W_GROUP, HEAD_V))
    si_hv, emt_hv, w_hv, dec_hv = (cols[n::4] for n in range(4))
    g_rows = [sum(lax.dot_general(row_sel, part, (((1,), (1,)), ((), ())), preferred_element_type=F32)
                  for part in _split3(zero_pad(x))) for x in g_c]
    g_row_hs = [jnp.concatenate([gr[h:h + 1, :] for h in range(N_HEADS)], axis=1) for gr in g_rows]
    d_hs = [jnp.where(incl4, jnp.exp(g_row_hs[i] - r_hs[i]), 0.0) for i in seqs]
    qs = [q_ref[i].astype(BF16) for i in seqs]
    ks = [k_ref[i] * (HEAD_V ** -0.5) for i in seqs]
    vs = [v_ref[i] for i in seqs]
    cns = [cn_s[i] for i in seqs]
    qk = [lax.dot_general(qs[i], _head_blocks(ks[i].astype(BF16), same_hv), (((1,), (1,)), ((), ())),
                          preferred_element_type=F32) for i in seqs]
    q_cn = [jnp.dot(qs[i], cns[i].astype(BF16), preferred_element_type=F32) for i in seqs]
    scs = [(qk[i] * d_hs[i]).astype(BF16) for i in seqs]
    ones_blocks = same_hv.astype(BF16)
    v_ones = [jnp.concatenate([_head_blocks(vs[i].astype(BF16), same_hv), ones_blocks], axis=1)
              for i in seqs]
    sc_v = [jnp.dot(scs[i], v_ones[i], preferred_element_type=F32) for i in seqs]
    nd = [sc_v[i] + jnp.concatenate([si_hv[i], si_hv[i]], axis=1) * q_cn[i] for i in seqs]
    hcs = [nd[i][:, :W_GROUP] / jnp.maximum(jnp.abs(nd[i][:, W_GROUP:]), emt_hv[i]) for i in seqs]
    kws = [ks[i] * w_hv[i] for i in seqs]
    upds = [_dot_tn(kws[i], jnp.concatenate([vs[i], jnp.ones_like(vs[i])], axis=1)) for i in seqs]
    same_cn = jnp.concatenate([same_vv, same_vv], axis=1)
    sum_sq = _dot01_right([x * x for x in hcs], group_sum)
    ng = ng_ref[...]
    for i in seqs:
        dec_row = dec_hv[i][0:1, :]
        cn_s[i] = jnp.concatenate([dec_row, dec_row], axis=1) * cns[i] + jnp.where(same_cn, upds[i], 0.0)
        m_s[i] = jnp.where(head_lane[0:1, :], b_c[i][c - 1:c, :] + r_end[i], 0.0)
        h_ref[i] = (hcs[i] * lax.rsqrt(sum_sq[i] * (1.0 / HEAD_V) + EPS) * ng * _sigmoid(og_ref[i]))

    @pl.when(j == pl.num_programs(1) - 1)
    def _():
        for i in seqs:
            for h in range(N_HEADS):
                c_out[i, h] = cn_s[i, _hs(h), _hs(h)]
                n_out[i, h] = cn_s[i, _hs(h), W_GROUP + h * HEAD_V:W_GROUP + h * HEAD_V + 1]
            m_out[i] = m_s[i, :, 0:N_HEADS]


def _mlstm_call(p, bias_row, ng, c0, n0, m0, c, nb):
    b, l, _ = p.shape
    nb = math.gcd(nb, b)
    nc = l // c
    blk = lambda cb: pl.BlockSpec((nb, c, W_GROUP), lambda i, j, cb=cb: (i, j, cb))
    full = lambda a: pl.BlockSpec((nb,) + a.shape[1:], lambda i, j: (i,) + (0,) * (a.ndim - 1))
    n0 = n0[..., None]
    m0 = m0.reshape(b, 1, N_HEADS)
    h, c_new, n_new, m_new = pl.pallas_call(
        functools.partial(_mlstm_kernel, c=c, nb=nb),
        grid=(b // nb, nc),
        in_specs=[blk(PB_MQ), blk(PB_MK), blk(PB_MV), blk(PB_MO),
                  pl.BlockSpec((nb, c, LANES), lambda i, j: (i, j, PB128_GATES)),
                  pl.BlockSpec((1, LANES), lambda i, j: (0, 0)),
                  pl.BlockSpec((1, W_GROUP), lambda i, j: (0, 0)),
                  full(c0), full(n0), full(m0)],
        out_specs=[pl.BlockSpec((nb, c, W_GROUP), lambda i, j: (i, j, 0)), full(c0), full(n0), full(m0)],
        out_shape=[jax.ShapeDtypeStruct((b, l, W_GROUP), F32), jax.ShapeDtypeStruct(c0.shape, F32),
                   jax.ShapeDtypeStruct(n0.shape, F32), jax.ShapeDtypeStruct(m0.shape, F32)],
        scratch_shapes=[pltpu.VMEM((nb, W_GROUP, 2 * W_GROUP), F32), pltpu.VMEM((nb, 1, LANES), F32)],
        compiler_params=pltpu.CompilerParams(dimension_semantics=("parallel", "arbitrary"),
                                             vmem_limit_bytes=VMEM_LIMIT),
        name="mlstm",
    )(p, p, p, p, p, bias_row, ng, c0, n0, m0)
    return h, c_new, n_new[..., 0], m_new[:, 0, :]


def _same_head(rows, lanes, row_group, lane_group):
    rh = lax.broadcasted_iota(jnp.int32, (rows, lanes), 0) >> (row_group.bit_length() - 1)
    lh = lax.broadcasted_iota(jnp.int32, (rows, lanes), 1) >> (lane_group.bit_length() - 1)
    return rh == lh


def _head_blocks(x, same_head):
    tiled = jnp.concatenate([x] * N_HEADS, axis=0)
    return jnp.where(same_head, tiled, jnp.zeros_like(tiled))


def _heads_dot_split(l, r, same_head):
    return _heads_dot_parts(_split2(l), _split2(r), same_head)


def _heads_dot_parts(l_parts, r_parts, same_head):
    l1, l2 = l_parts
    rb1, rb2 = _head_blocks(r_parts[0], same_head), _head_blocks(r_parts[1], same_head)
    m = l1.shape[0]
    y = jnp.dot(jnp.concatenate([l1, l2], axis=0), rb1, preferred_element_type=F32)
    return y[:m] + (y[m:] + jnp.dot(l1, rb2, preferred_element_type=F32))


def _unit_lower_inverses(lows, eye4, same_head, c):
    ts = [eye4 - a for a in lows]
    a_parts = [_split2(a) for a in lows]
    ps = [_heads_dot_parts(ap, ap, same_head) for ap in a_parts]
    for _ in range(c.bit_length() - 3):
        t_parts = [_split2(t) for t in ts]
        p_parts = [_split2(p) for p in ps]
        stacked = [tuple(jnp.concatenate([tp[n], pp[n]], axis=0) for n in range(2))
                   for tp, pp in zip(t_parts, p_parts)]
        ys = [_heads_dot_parts(sp, pp, same_head) for sp, pp in zip(stacked, p_parts)]
        ts = [t + y[:c] for t, y in zip(ts, ys)]
        ps = [y[c:] for y in ys]
    return [t + _heads_dot_split(t, p, same_head) for t, p in zip(ts, ps)]


def _gdn_kernel(q_ref, k_ref, v_ref, z_ref, g_ref, bias_ref, alog_ref, ng_ref, s0_ref,
                o_ref, s_out, s_s, *, c, nb):
    j = pl.program_id(1)
    seqs = range(nb)

    @pl.when(j == 0)
    def _():
        s_s[...] = jnp.zeros_like(s_s)
        for i in seqs:
            for h in range(N_HEADS):
                s_s[i, _hs(h), _hs(h)] = s0_ref[i, h]

    hs = N_HEADS * c
    lane = lax.broadcasted_iota(jnp.int32, (c, LANES), 1)
    is_a = (lane >= G_GA) & (lane < G_GA + N_HEADS)
    incl_b = _tri(c).astype(BF16)
    t_idx = lax.broadcasted_iota(jnp.int32, (c, hs), 0)
    s_idx = lax.broadcasted_iota(jnp.int32, (c, hs), 1) & (c - 1)
    incl4, strict4, eye4 = t_idx >= s_idx, t_idx > s_idx, (t_idx == s_idx).astype(F32)
    same_hs = _same_head(hs, hs, c, c)
    same_hv = _same_head(hs, W_GROUP, c, HEAD_V)
    same_vv = _same_head(W_GROUP, W_GROUP, HEAD_V, HEAD_V)
    group_sum = same_vv.astype(BF16)

    def pick(lanes, group):
        g = lax.broadcasted_iota(jnp.int32, (LANES, 2 * lanes), 0)
        l = lax.broadcasted_iota(jnp.int32, (LANES, 2 * lanes), 1)
        h = (l & (lanes - 1)) >> (group.bit_length() - 1)
        return (g == jnp.where(l < lanes, G_GA, G_GB) + h).astype(BF16)

    pick_hs = pick(hs, c)
    row_sel = (lax.broadcasted_iota(jnp.int32, (8, LANES), 1)
               == G_GA + lax.broadcasted_iota(jnp.int32, (8, LANES), 0)).astype(BF16)
    gates = [g_ref[i] + bias_ref[...] for i in seqs]
    xgs = [jnp.where(is_a, -jnp.exp(alog_ref[...]) * _softplus(g), _sigmoid(g)) for g in gates]
    csums = [jnp.where(is_a, _dot01_left(incl_b, xg), xg) for xg in xgs]
    cols_hs = _dot01_right(csums, pick_hs)
    bc_hs, beta_hs = [x[:, :hs] for x in cols_hs], [x[:, hs:] for x in cols_hs]
    if c == HEAD_V:
        bc_hv, beta_hv = bc_hs, beta_hs
    else:
        cols_hv = _dot01_right(csums, pick(W_GROUP, HEAD_V))
        bc_hv, beta_hv = [x[:, :W_GROUP] for x in cols_hv], [x[:, W_GROUP:] for x in cols_hv]
    b_rows = [sum(lax.dot_general(row_sel, part, (((1,), (1,)), ((), ())), preferred_element_type=F32)
                  for part in _split3(cs)) for cs in csums]
    br_hs = [jnp.concatenate([br[h:h + 1, :] for h in range(N_HEADS)], axis=1) for br in b_rows]
    decays = [jnp.exp(jnp.where(incl4, bc_hs[i] - br_hs[i], 0.0)) for i in seqs]
    qk_raw = [jnp.concatenate([q_ref[i], k_ref[i]], axis=0) for i in seqs]
    norms = _dot01_right([x * x for x in qk_raw], group_sum)
    qk_n = [qk_raw[i] * lax.rsqrt(norms[i] + EPS) for i in seqs]
    qs = [x[:c] * (HEAD_V ** -0.5) for x in qk_n]
    ks = [x[c:] for x in qk_n]
    vs = [v_ref[i] for i in seqs]
    kq = [jnp.concatenate([ks[i], qs[i]], axis=0).astype(BF16) for i in seqs]
    states = [s_s[i] for i in seqs]
    scores = [lax.dot_general(kq[i], _head_blocks(ks[i].astype(BF16), same_hv),
                              (((1,), (1,)), ((), ())), preferred_element_type=F32) for i in seqs]
    on_state = [jnp.dot(kq[i], states[i].astype(BF16), preferred_element_type=F32) for i in seqs]
    lows = [jnp.where(strict4, beta_hs[i] * decays[i] * scores[i][:c], 0.0) for i in seqs]
    invs = _unit_lower_inverses(lows, eye4, same_hs, c)
    ebs = [jnp.exp(bc_hv[i]) for i in seqs]
    rhss = [beta_hv[i] * (vs[i] - ebs[i] * on_state[i][:c]) for i in seqs]
    us = [_heads_dot_split(invs[i], rhss[i], same_hv) for i in seqs]
    qkm = [jnp.where(incl4, decays[i] * scores[i][c:], 0.0).astype(BF16) for i in seqs]
    outs = [ebs[i] * on_state[i][c:]
            + jnp.dot(qkm[i], _head_blocks(us[i].astype(BF16), same_hv), preferred_element_type=F32)
            for i in seqs]
    b_ends = [bc_hv[i][c - 1:c, :] for i in seqs]
    upds = [jnp.where(same_vv, _dot_tn(ks[i] * jnp.exp(b_ends[i] - bc_hv[i]), us[i]), 0.0) for i in seqs]
    ng = ng_ref[...]
    sum_sq = _dot01_right([o * o for o in outs], group_sum)
    for i in seqs:
        s_s[i] = jnp.exp(b_ends[i]) * states[i] + upds[i]
        o_ref[i] = (outs[i] * lax.rsqrt(sum_sq[i] * (1.0 / HEAD_V) + EPS) * ng * _silu(z_ref[i]))

    @pl.when(j == pl.num_programs(1) - 1)
    def _():
        for i in seqs:
            for h in range(N_HEADS):
                s_out[i, h] = s_s[i, _hs(h), _hs(h)]


def _gdn_call(p, bias_row, alog_row, ng, s0, c, nb):
    b, l, _ = p.shape
    nb = math.gcd(nb, b)
    nc = l // c
    blk = lambda cb: pl.BlockSpec((nb, c, W_GROUP), lambda i, j, cb=cb: (i, j, cb))
    full = lambda a: pl.BlockSpec((nb,) + a.shape[1:], lambda i, j: (i,) + (0,) * (a.ndim - 1))
    return pl.pallas_call(
        functools.partial(_gdn_kernel, c=c, nb=nb),
        grid=(b // nb, nc),
        in_specs=[blk(PB_GQ), blk(PB_GK), blk(PB_GV), blk(PB_GZ),
                  pl.BlockSpec((nb, c, LANES), lambda i, j: (i, j, PB128_GATES)),
                  pl.BlockSpec((1, LANES), lambda i, j: (0, 0)),
                  pl.BlockSpec((1, LANES), lambda i, j: (0, 0)),
                  pl.BlockSpec((1, W_GROUP), lambda i, j: (0, 0)),
                  full(s0)],
        out_specs=[pl.BlockSpec((nb, c, W_GROUP), lambda i, j: (i, j, 0)), full(s0)],
        out_shape=[jax.ShapeDtypeStruct((b, l, W_GROUP), F32), jax.ShapeDtypeStruct(s0.shape, F32)],
        scratch_shapes=[pltpu.VMEM((nb, W_GROUP, W_GROUP), F32)],
        compiler_params=pltpu.CompilerParams(dimension_semantics=("parallel", "arbitrary"),
                                             vmem_limit_bytes=VMEM_LIMIT),
        name="gdn",
    )(p, p, p, p, p, bias_row, alog_row, ng, s0)


GLA_SUB = 8


def _gla_kernel(q_ref, k_ref, v_ref, r_ref, g_ref, w2_ref, gb_ref, ng_ref, s0_ref,
                o_ref, s_out, s_s, *, c, nb):
    j = pl.program_id(1)
    dk_all = N_HEADS * GLA_DK
    hs = N_HEADS * c
    nsb = c // GLA_SUB
    seqs = range(nb)

    @pl.when(j == 0)
    def _():
        s_s[...] = jnp.zeros_like(s_s)
        for i in seqs:
            for h in range(N_HEADS):
                s_s[i, h * GLA_DK:(h + 1) * GLA_DK, _hs(h)] = s0_ref[i, h]

    incl_b = _tri(c).astype(BF16)
    er = lax.broadcasted_iota(jnp.int32, (2 * c, c), 0)
    ec = lax.broadcasted_iota(jnp.int32, (2 * c, c), 1)
    et = er & (c - 1)
    edge_sel = (ec == jnp.where(er < c, (et & ~(GLA_SUB - 1)) - 1, et | (GLA_SUB - 1))).astype(BF16)
    t_idx = lax.broadcasted_iota(jnp.int32, (c, hs), 0)
    s_idx = lax.broadcasted_iota(jnp.int32, (c, hs), 1) & (c - 1)
    blk_dist = (t_idx >> 3) - (s_idx >> 3)
    diag_off = t_idx - s_idx
    same_hk = _same_head(hs, dk_all, c, GLA_DK)
    same_hv = _same_head(hs, W_GROUP, c, HEAD_V)
    same_kv = _same_head(dk_all, W_GROUP, GLA_DK, HEAD_V)
    head_rep = _same_head(dk_all, hs, GLA_DK, c).astype(BF16)
    group_sum = _same_head(W_GROUP, W_GROUP, HEAD_V, HEAD_V).astype(BF16)
    row8 = lax.broadcasted_iota(jnp.int32, (c, dk_all), 0) & (GLA_SUB - 1)
    last_row = lax.broadcasted_iota(jnp.int32, (c, dk_all), 0) == c - 1

    log_as = [_log_sigmoid(_dot(g_ref[i], w2_ref[...]) + gb_ref[...]) / GLA_TAU for i in seqs]
    bs = [_dot01_left(incl_b, la) for la in log_as]
    edges = [_dot01_left(edge_sel, b) for b in bs]
    e_prev = [e[:c] for e in edges]
    e_own = [e[c:] for e in edges]
    qs = [q_ref[i] * (GLA_DK ** -0.5) for i in seqs]
    ks = [k_ref[i] for i in seqs]
    vs = [v_ref[i] for i in seqs]
    q_hat = [qs[i] * jnp.exp(bs[i] - e_prev[i]) for i in seqs]
    k_hat = [ks[i] * jnp.exp(e_own[i] - bs[i]) for i in seqs]
    def q_for_distance(i, m):
        if m == 0:
            return q_hat[i]
        rows = GLA_SUB * m
        shifted = jnp.concatenate([jnp.zeros((rows, dk_all), F32), e_prev[i][:c - rows]], axis=0)
        return q_hat[i] * jnp.exp(e_prev[i] - shifted)
    a_parts = []
    for i in seqs:
        lhs = jnp.concatenate([q_for_distance(i, m) for m in range(nsb - 1)], axis=0).astype(BF16)
        prod = lax.dot_general(lhs, _head_blocks(k_hat[i].astype(BF16), same_hk),
                               (((1,), (1,)), ((), ())), preferred_element_type=F32)
        a_parts.append(sum(jnp.where(blk_dist == m + 1, prod[m * c:(m + 1) * c], 0.0)
                           for m in range(nsb - 1)))
    def rot8(x, d):
        return x if d == 0 else pltpu.roll(x, d, 0)
    for i in seqs:
        ws = [jnp.where(row8 >= d, qs[i] * rot8(ks[i], d) * jnp.exp(bs[i] - rot8(bs[i], d)), 0.0)
              for d in range(GLA_SUB)]
        prod = jnp.dot(jnp.concatenate(ws, axis=0).astype(BF16), head_rep, preferred_element_type=F32)
        a_parts[i] = a_parts[i] + sum(jnp.where(diag_off == d, prod[d * c:(d + 1) * c], 0.0)
                                      for d in range(GLA_SUB))
    states = [s_s[i] for i in seqs]
    outs = [jnp.dot(a_parts[i].astype(BF16), _head_blocks(vs[i].astype(BF16), same_hv),
                    preferred_element_type=F32)
            + _dot(qs[i] * jnp.exp(bs[i]), states[i]) for i in seqs]
    b_last = [b[c - 1:c, :] for b in bs]
    upds = [jnp.where(same_kv, _dot_tn(ks[i] * jnp.exp(b_last[i] - bs[i]), vs[i]), 0.0) for i in seqs]
    ones_cv = jnp.ones((c, W_GROUP), BF16)
    decay_rows = [sum(lax.dot_general(part, ones_cv, (((0,), (0,)), ((), ())), preferred_element_type=F32)
                      for part in _split3(jnp.where(last_row, b, 0.0))) for b in bs]
    sum_sq = _dot01_right([o * o for o in outs], group_sum)
    ng = ng_ref[...]
    for i in seqs:
        s_s[i] = states[i] * jnp.exp(decay_rows[i]) + upds[i]
        o_ref[i] = outs[i] * lax.rsqrt(sum_sq[i] * (1.0 / HEAD_V) + EPS) * ng * _silu(r_ref[i])

    @pl.when(j == pl.num_programs(1) - 1)
    def _():
        for i in seqs:
            for h in range(N_HEADS):
                s_out[i, h] = s_s[i, h * GLA_DK:(h + 1) * GLA_DK, _hs(h)]


def _gla_call(p, w2_pad, gb_row, ng, s0, c, nb):
    b, l, _ = p.shape
    nb = math.gcd(nb, b)
    nc = l // c
    dk_all = N_HEADS * GLA_DK
    blk = lambda cb: pl.BlockSpec((nb, c, W_GROUP), lambda i, j, cb=cb: (i, j, cb))
    blk128 = lambda cb: pl.BlockSpec((nb, c, LANES), lambda i, j, cb=cb: (i, j, cb))
    state = pl.BlockSpec((nb,) + s0.shape[1:], lambda i, j: (i, 0, 0, 0))
    return pl.pallas_call(
        functools.partial(_gla_kernel, c=c, nb=nb),
        grid=(b // nb, nc),
        in_specs=[blk128(PB128_LQ), blk128(PB128_LK), blk(PB_LV), blk(PB_LR), blk128(PB128_GATES),
                  pl.BlockSpec((LANES, dk_all), lambda i, j: (0, 0)),
                  pl.BlockSpec((1, dk_all), lambda i, j: (0, 0)),
                  pl.BlockSpec((1, W_GROUP), lambda i, j: (0, 0)),
                  state],
        out_specs=[pl.BlockSpec((nb, c, W_GROUP), lambda i, j: (i, j, 0)), state],
        out_shape=[jax.ShapeDtypeStruct((b, l, W_GROUP), F32), jax.ShapeDtypeStruct(s0.shape, F32)],
        scratch_shapes=[pltpu.VMEM((nb, dk_all, W_GROUP), F32)],
        compiler_params=pltpu.CompilerParams(dimension_semantics=("parallel", "arbitrary"),
                                             vmem_limit_bytes=VMEM_LIMIT),
        name="gla",
    )(p, p, p, p, p, w2_pad, gb_row, ng, s0)


def _outffn_kernel(x_ref, a_ref, b_ref, c_ref, d_ref, mod_ref, g2_ref, gf_ref, wo_ref, w1_ref, w2_ref,
                   o_ref, *, ff_chunk, final):
    mod = mod_ref[0]
    mixed = jnp.concatenate([a_ref[0], b_ref[0], c_ref[0], d_ref[0]], axis=-1).astype(BF16)
    x = x_ref[0] + mod[2:3] * jnp.dot(mixed, wo_ref[...], preferred_element_type=F32)
    h = _modulated_norm(x, g2_ref[...], mod[4:5], mod[3:4]).astype(BF16)
    d_ff = w1_ref.shape[1]
    acc = jnp.zeros(x.shape, F32)
    for f0 in range(0, d_ff, ff_chunk):
        a = jnp.maximum(jnp.dot(h, w1_ref[:, f0:f0 + ff_chunk], preferred_element_type=F32), 0.0)
        acc = acc + jnp.dot((a * a).astype(BF16), w2_ref[f0:f0 + ff_chunk, :],
                            preferred_element_type=F32)
    x = x + mod[5:6] * acc
    if final:
        x = x * lax.rsqrt(jnp.mean(x * x, axis=-1, keepdims=True) + EPS) * gf_ref[...]
    o_ref[0] = x


def _outffn_call(x, mixers, mod, g2, gf, wo, w1, w2, tm, final):
    b, l, d = x.shape
    d_ff = w1.shape[1]
    tok = lambda w: pl.BlockSpec((1, tm, w), lambda i, j: (i, j, 0))
    const = lambda shape: pl.BlockSpec(shape, lambda i, j: (0,) * len(shape),
                                       pipeline_mode=pl.Buffered(1))
    return pl.pallas_call(
        functools.partial(_outffn_kernel, ff_chunk=1024, final=final),
        grid=(b, l // tm),
        in_specs=[tok(d), tok(W_GROUP), tok(W_GROUP), tok(W_GROUP), tok(W_GROUP),
                  pl.BlockSpec((1, 6, d), lambda i, j: (i, 0, 0)),
                  pl.BlockSpec((1, d), lambda i, j: (0, 0)),
                  pl.BlockSpec((1, d), lambda i, j: (0, 0)),
                  const((d, d)), const((d, d_ff)), const((d_ff, d))],
        out_specs=tok(d),
        out_shape=jax.ShapeDtypeStruct((b, l, d), F32),
        compiler_params=pltpu.CompilerParams(dimension_semantics=("parallel", "parallel"),
                                             vmem_limit_bytes=VMEM_LIMIT),
        name="outproj_ffn",
    )(x, *mixers, mod, g2, gf, wo, w1, w2)


def _permute_w_in(w_in):
    d = w_in.shape[0]
    sizes = (256, 256, 256, 256, 256, 256, 4, 4, 256, 768, 4, 4, 256, 128, 128, 256, 16, 256)
    offs = [0]
    for s in sizes:
        offs.append(offs[-1] + s)
    (sq, sk, sv, mq, mk, mv, mi, mf, mo, gqkv, gb, ga, gz, lq, lk, lv, lg, lr) = [
        w_in[:, offs[i]:offs[i + 1]] for i in range(len(sizes))]
    pad = jnp.zeros((d, LANES - (4 * N_HEADS + GLA_RANK)), w_in.dtype)
    return jnp.concatenate([sk, sv, sq, mq, mk, mv, mo, gqkv, gz, lv, lr, lq, lk,
                            mi, mf, gb, ga, lg, pad], axis=1)


def _row128(pieces):
    row = jnp.zeros((LANES,), F32)
    for off, vec in pieces:
        row = row.at[off:off + vec.shape[0]].set(vec)
    return row.reshape(1, LANES)


class _Tiles(NamedTuple):
    tm: int
    sb_tq: int
    sb_tk: int
    mlstm_chunk: int
    chunk: int


def _tiles(l):
    return _Tiles(tm=min(512, l), sb_tq=min(256, l), sb_tk=max(min(256, l), LANES),
                  mlstm_chunk=min(64, l), chunk=min(64, l))


def _layer(x, mod, lw, states, tiles, final, final_g):
    tm, sb_tq, sb_tk, mlstm_chunk, chunk = tiles
    (n1, n2, w_in_p, gate_bias, ml_ng, conv_w, alog_row, gdn_ng, w2_pad, gla_gb, gla_ng,
     w_out, w_ff1, w_ff2) = lw
    (sb_k_past, sb_v_past, ml_c, ml_n, ml_m, gdn_s, gdn_buf, gla_s) = states
    b, l, _ = x.shape
    k_new, v_new, p, gdn_buf = _inproj_call(x, mod, n1, w_in_p, conv_w, gdn_buf, tm)
    if sb_k_past is None:
        k_all, v_all, q_offset = k_new, v_new, 0
    else:
        past = sb_k_past.shape[1]
        lk = -(-(past + l) // sb_tk) * sb_tk
        padz = jnp.zeros((b, lk - past - l, W_GROUP), F32)
        k_all = jnp.concatenate([sb_k_past.reshape(b, past, W_GROUP), k_new, padz], axis=1)
        v_all = jnp.concatenate([sb_v_past.reshape(b, past, W_GROUP), v_new, padz], axis=1)
        q_offset = past
    o_sb = _sb_call(p, PB_SQ, k_all, v_all, l, sb_tq, sb_tk, q_offset)
    o_ml, ml_c, ml_n, ml_m = _mlstm_call(p, gate_bias, ml_ng, ml_c, ml_n, ml_m, mlstm_chunk,
                                         MLSTM_SEQS_PER_STEP)
    o_gdn, gdn_s = _gdn_call(p, gate_bias, alog_row, gdn_ng, gdn_s, chunk, GDN_SEQS_PER_STEP)
    o_gla, gla_s = _gla_call(p, w2_pad, gla_gb, gla_ng, gla_s, chunk, GLA_SEQS_PER_STEP)
    x = _outffn_call(x, (o_sb, o_ml, o_gdn, o_gla), mod, n2, final_g, w_out, w_ff1, w_ff2, tm, final)
    hk = lambda a: a.reshape(b, l, N_HEADS, HEAD_V)
    return x, (hk(k_new), hk(v_new), ml_c, ml_n, ml_m, gdn_s, gdn_buf, gla_s)


def kernel(x_prompt, x_sample, cache_sb_k, cache_sb_v, state_mlstm_C, state_mlstm_n, state_mlstm_m, state_gdn_S, state_gdn_conv, state_gla_S, c_prompt, c_sample, norm1_g, norm2_g, w_ada, b_ada, w_in, mlstm_i_bias, mlstm_f_bias, mlstm_norm_g, gdn_conv_w, gdn_a_log, gdn_dt_bias, gdn_norm_g, gla_w_gate2, gla_gate_bias, gla_norm_g, w_out, w_ff1, w_ff2, final_g):
    depth = w_in.shape[0]
    bp, lp, d = x_prompt.shape
    bs, ls, _ = x_sample.shape
    dk_all = N_HEADS * GLA_DK
    mod_all = _ada_call(jnp.concatenate([c_prompt, c_sample], axis=0), w_ada, b_ada)
    mod_all = mod_all.reshape(depth, bp + bs, 6, d)
    final_row = final_g.reshape(1, d)
    xp, xs = x_prompt, x_sample
    p_list, s_list = [], []
    for l in range(depth):
        gate_bias = _row128([(G_MI, mlstm_i_bias[l]), (G_MF, mlstm_f_bias[l]), (G_GA, gdn_dt_bias[l])])
        alog_row = _row128([(G_GA, gdn_a_log[l])])
        w2_pad = jnp.zeros((LANES, dk_all), F32).at[G_LG:G_LG + GLA_RANK, :].set(gla_w_gate2[l])
        lw = (norm1_g[l].reshape(1, d), norm2_g[l].reshape(1, d), _permute_w_in(w_in[l]).astype(BF16),
              gate_bias, mlstm_norm_g[l].reshape(1, W_GROUP), gdn_conv_w[l], alog_row,
              gdn_norm_g[l].reshape(1, W_GROUP), w2_pad.astype(BF16),
              gla_gate_bias[l].reshape(1, dk_all), gla_norm_g[l].reshape(1, W_GROUP),
              w_out[l].astype(BF16), w_ff1[l].astype(BF16), w_ff2[l].astype(BF16))
        fresh = (None, None, jnp.zeros((bp,) + state_mlstm_C.shape[2:], F32),
                 jnp.zeros((bp,) + state_mlstm_n.shape[2:], F32),
                 jnp.zeros((bp,) + state_mlstm_m.shape[2:], F32),
                 jnp.zeros((bp,) + state_gdn_S.shape[2:], F32),
                 jnp.zeros((bp,) + state_gdn_conv.shape[2:], F32),
                 jnp.zeros((bp,) + state_gla_S.shape[2:], F32))
        past = (cache_sb_k[l], cache_sb_v[l], state_mlstm_C[l], state_mlstm_n[l], state_mlstm_m[l],
                state_gdn_S[l], state_gdn_conv[l], state_gla_S[l])
        final = l == depth - 1
        xp, st_p = _layer(xp, mod_all[l, :bp], lw, fresh, _tiles(lp), final, final_row)
        xs, st_s = _layer(xs, mod_all[l, bp:], lw, past, _tiles(ls), final, final_row)
        p_list.append(st_p)
        s_list.append(st_s)
    stacked_p = [jnp.stack([st[i] for st in p_list]) for i in range(8)]
    stacked_s = [jnp.stack([st[i] for st in s_list]) for i in range(8)]
    return (xp, xs, *stacked_p, *stacked_s)
```

```python
import functools
import math
from typing import NamedTuple

import jax
import jax.numpy as jnp
from jax import lax
from jax.experimental import pallas as pl
from jax.experimental.pallas import tpu as pltpu

F32 = jnp.float32
BF16 = jnp.bfloat16
EPS = 1e-6
N_HEADS = 4
HEAD_V = 64
GLA_DK = 32
CONV_W = 4
GLA_TAU = 16.0
W_GROUP = N_HEADS * HEAD_V
LANES = 128
VMEM_LIMIT = 56 * 1024 * 1024
MLSTM_SEQS_PER_STEP = 4
GDN_SEQS_PER_STEP = 4
GLA_SEQS_PER_STEP = 4

PB_SQ, PB_MQ, PB_MK, PB_MV, PB_MO, PB_GQ, PB_GK, PB_GV, PB_GZ, PB_LV, PB_LR = range(11)
PB128_LQ, PB128_LK, PB128_GATES = 22, 23, 24
P_WIDTH = 25 * LANES
G_MI, G_MF, G_GB, G_GA, G_LG = 0, 4, 8, 12, 16
GLA_RANK = 16


def _dot(a, b):
    return jnp.dot(a.astype(BF16), b.astype(BF16), preferred_element_type=F32)


def _dot_tn(a, b):
    return lax.dot_general(a.astype(BF16), b.astype(BF16), (((0,), (0,)), ((), ())),
                           preferred_element_type=F32)


def _split3(x):
    x1 = x.astype(BF16)
    r1 = x - x1.astype(F32)
    x2 = r1.astype(BF16)
    x3 = (r1 - x2.astype(F32)).astype(BF16)
    return x1, x2, x3


def _split2(x):
    x1 = x.astype(BF16)
    x2 = (x - x1.astype(F32)).astype(BF16)
    return x1, x2


def _dot01_left(m01, x):
    return sum(jnp.dot(m01, p, preferred_element_type=F32) for p in _split3(x))


def _dot01_right(xs, m01):
    n = xs[0].shape[0]
    stacked = jnp.concatenate([part for x in xs for part in _split3(x)], axis=0)
    y = jnp.dot(stacked, m01, preferred_element_type=F32)
    return [y[3 * i * n:(3 * i + 1) * n] + (y[(3 * i + 1) * n:(3 * i + 2) * n]
                                             + y[(3 * i + 2) * n:(3 * i + 3) * n])
            for i in range(len(xs))]


def _log_sigmoid(x):
    return jnp.minimum(x, 0.0) - jnp.log(1.0 + jnp.exp(-jnp.abs(x)))


def _sigmoid(x):
    return 1.0 / (1.0 + jnp.exp(-x))


def _silu(x):
    return x * _sigmoid(x)


def _softplus(x):
    return jnp.maximum(x, 0.0) + jnp.log(1.0 + jnp.exp(-jnp.abs(x)))


def _tri(c):
    r = lax.broadcasted_iota(jnp.int32, (c, c), 0)
    s = lax.broadcasted_iota(jnp.int32, (c, c), 1)
    return r >= s


def _hs(h, w=HEAD_V):
    return slice(h * w, (h + 1) * w)


def _ada_kernel(c_ref, w_ref, b_ref, o_ref):
    o_ref[0] = _dot(_silu(c_ref[...]), w_ref[0]) + b_ref[0]


def _ada_call(c_all, w_ada, b_ada, tn=512):
    depth, d, n = w_ada.shape
    rows = c_all.shape[0]
    return pl.pallas_call(
        _ada_kernel,
        grid=(depth, n // tn),
        in_specs=[pl.BlockSpec((rows, d), lambda l, j: (0, 0)),
                  pl.BlockSpec((1, d, tn), lambda l, j: (l, 0, j)),
                  pl.BlockSpec((1, 1, tn), lambda l, j: (l, 0, j))],
        out_specs=pl.BlockSpec((1, rows, tn), lambda l, j: (l, 0, j)),
        out_shape=jax.ShapeDtypeStruct((depth, rows, n), F32),
        compiler_params=pltpu.CompilerParams(dimension_semantics=("parallel", "parallel"),
                                             vmem_limit_bytes=VMEM_LIMIT),
        name="ada_mod",
    )(c_all, w_ada, b_ada.reshape(depth, 1, n))


def _modulated_norm(x, g_row, scale_row, shift_row):
    y = x * lax.rsqrt(jnp.mean(x * x, axis=-1, keepdims=True) + EPS) * g_row
    return y * (1.0 + scale_row) + shift_row


def _inproj_kernel(x_ref, mod_ref, g_ref, w_ref, cw_ref, cb0_ref, k_ref, v_ref, p_ref, cb_out, xw_s,
                   *, col_chunk):
    j = pl.program_id(1)
    tm = x_ref.shape[1]
    conv_lo, conv_hi = PB_GQ * W_GROUP, (PB_GV + 1) * W_GROUP

    @pl.when(j == 0)
    def _():
        xw_s[0:8, :] = cb0_ref[0]

    mod = mod_ref[0]
    h = _modulated_norm(x_ref[0], g_ref[...], mod[1:2], mod[0:1]).astype(BF16)
    k_ref[0] = jnp.dot(h, w_ref[:, 0:W_GROUP], preferred_element_type=F32)
    v_ref[0] = jnp.dot(h, w_ref[:, W_GROUP:2 * W_GROUP], preferred_element_type=F32)
    base = 2 * W_GROUP
    bounds = sorted(set(range(0, conv_lo, col_chunk)) | {conv_lo, conv_hi}
                    | set(range(conv_hi, P_WIDTH, col_chunk)) | {P_WIDTH})
    for n0, n1 in zip(bounds[:-1], bounds[1:]):
        if n0 == conv_lo:
            continue
        p_ref[0, :, n0:n1] = jnp.dot(h, w_ref[:, base + n0:base + n1], preferred_element_type=F32)
    raw = jnp.dot(h, w_ref[:, base + conv_lo:base + conv_hi], preferred_element_type=F32)
    xw_s[8:8 + tm, :] = raw
    cw = cw_ref[...]
    y = raw * cw[CONV_W - 1:CONV_W, :]
    for d in range(1, CONV_W):
        y = y + xw_s[8 - d:8 - d + tm, :] * cw[CONV_W - 1 - d:CONV_W - d, :]
    p_ref[0, :, conv_lo:conv_hi] = _silu(y)
    xw_s[0:8, :] = raw[tm - 8:tm, :]

    @pl.when(j == pl.num_programs(1) - 1)
    def _():
        cb_out[0] = raw[tm - 8:tm, :]


def _inproj_call(x, mod, g, w_perm, conv_w, conv0, tm):
    b, l, d = x.shape
    n_all = w_perm.shape[1]
    conv_dim = conv_w.shape[1]
    conv0_pad = jnp.concatenate([jnp.zeros((b, 8 - (CONV_W - 1), conv_dim), F32), conv0], axis=1)
    k, v, p, conv_new = pl.pallas_call(
        functools.partial(_inproj_kernel, col_chunk=512),
        grid=(b, l // tm),
        in_specs=[pl.BlockSpec((1, tm, d), lambda i, j: (i, j, 0)),
                  pl.BlockSpec((1, 6, d), lambda i, j: (i, 0, 0)),
                  pl.BlockSpec((1, d), lambda i, j: (0, 0)),
                  pl.BlockSpec((d, n_all), lambda i, j: (0, 0), pipeline_mode=pl.Buffered(1)),
                  pl.BlockSpec((CONV_W, conv_dim), lambda i, j: (0, 0)),
                  pl.BlockSpec((1, 8, conv_dim), lambda i, j: (i, 0, 0))],
        out_specs=[pl.BlockSpec((1, tm, W_GROUP), lambda i, j: (i, j, 0)),
                   pl.BlockSpec((1, tm, W_GROUP), lambda i, j: (i, j, 0)),
                   pl.BlockSpec((1, tm, P_WIDTH), lambda i, j: (i, j, 0)),
                   pl.BlockSpec((1, 8, conv_dim), lambda i, j: (i, 0, 0))],
        out_shape=[jax.ShapeDtypeStruct((b, l, W_GROUP), F32),
                   jax.ShapeDtypeStruct((b, l, W_GROUP), F32),
                   jax.ShapeDtypeStruct((b, l, P_WIDTH), F32),
                   jax.ShapeDtypeStruct((b, 8, conv_dim), F32)],
        scratch_shapes=[pltpu.VMEM((tm + 8, conv_dim), F32)],
        compiler_params=pltpu.CompilerParams(dimension_semantics=("parallel", "arbitrary"),
                                             vmem_limit_bytes=VMEM_LIMIT),
        name="norm_inproj",
    )(x, mod, g, w_perm, conv_w, conv0_pad)
    return k, v, p, conv_new[:, 8 - (CONV_W - 1):, :]


SB_DEAD_LOG = -104.0


def _sb_kernel(q_ref, k_ref, v_ref, o_ref, *, tq, tk, q_offset):
    i = pl.program_id(1)
    q0 = q_offset + i * tq
    j_top = (q0 + tq - 1) // tk
    n_masked = max(tq // tk, 1)
    suffix = (lax.broadcasted_iota(jnp.int32, (tk, tk), 0)
              >= lax.broadcasted_iota(jnp.int32, (tk, tk), 1)).astype(BF16)
    q_all = q_ref[0] * (HEAD_V ** -0.5)
    qs = [q_all[:, _hs(h)].astype(BF16) for h in range(N_HEADS)]

    def block(j, accs, runs, causal):
        s0 = pl.multiple_of(j * tk, tk)
        heads = range(N_HEADS)
        kbs = [k_ref[0, pl.ds(s0, tk), _hs(h)].astype(BF16) for h in heads]
        vbs = [v_ref[0, pl.ds(s0, tk), _hs(h)].astype(BF16) for h in heads]
        zs = [lax.dot_general(qs[h], kbs[h], (((1,), (1,)), ((), ())), preferred_element_type=F32)
              for h in heads]
        sps = [_softplus(z) for z in zs]
        if causal is not None:
            sps = [jnp.where(causal, x, 0.0) for x in sps]
        splits = [_split2(x) for x in sps]
        incs = [jnp.dot(hi, suffix, preferred_element_type=F32)
                + jnp.dot(lo, suffix, preferred_element_type=F32) for hi, lo in splits]
        probs = [jnp.exp(zs[h] - incs[h] - runs[h]) for h in heads]
        if causal is not None:
            probs = [jnp.where(causal, a, 0.0) for a in probs]
        new_accs = [accs[h] + jnp.dot(probs[h].astype(BF16), vbs[h], preferred_element_type=F32)
                    for h in heads]
        new_runs = [runs[h] + incs[h][:, 0:1] for h in heads]
        return tuple(new_accs), tuple(new_runs)

    def live_of(runs):
        return -jnp.min(jnp.minimum(jnp.minimum(runs[0], runs[1]), jnp.minimum(runs[2], runs[3])))

    q_pos = q0 + lax.broadcasted_iota(jnp.int32, (tq, tk), 0)
    k_off = lax.broadcasted_iota(jnp.int32, (tq, tk), 1)
    zeros = lambda w: tuple(jnp.zeros((tq, w), F32) for _ in range(N_HEADS))
    accs, runs = zeros(HEAD_V), zeros(1)
    for m in range(n_masked):
        accs, runs = block(j_top - m, accs, runs, (j_top - m) * tk + k_off < q_pos)

    def cond(carry):
        j, live, _, _ = carry
        return (j >= 0) & (live > SB_DEAD_LOG)

    def body(carry):
        j, _, accs, runs = carry
        accs, runs = block(j, accs, runs, None)
        return j - 1, live_of(runs), accs, runs

    _, _, accs, _ = lax.while_loop(cond, body, (j_top - n_masked, live_of(runs), accs, runs))
    for h in range(N_HEADS):
        o_ref[0, :, _hs(h)] = accs[h]


def _sb_call(q_src, q_block, k_all, v_all, l, tq, tk, q_offset):
    assert (tq % tk == 0 or tk % tq == 0) and q_offset % max(tq, tk) == 0 and tk % LANES == 0
    b = q_src.shape[0]
    lk = k_all.shape[1]
    return pl.pallas_call(
        functools.partial(_sb_kernel, tq=tq, tk=tk, q_offset=q_offset),
        grid=(b, l // tq),
        in_specs=[pl.BlockSpec((1, tq, W_GROUP), lambda i, j: (i, j, q_block)),
                  pl.BlockSpec((1, lk, W_GROUP), lambda i, j: (i, 0, 0)),
                  pl.BlockSpec((1, lk, W_GROUP), lambda i, j: (i, 0, 0))],
        out_specs=pl.BlockSpec((1, tq, W_GROUP), lambda i, j: (i, j, 0)),
        out_shape=jax.ShapeDtypeStruct((b, l, W_GROUP), F32),
        compiler_params=pltpu.CompilerParams(dimension_semantics=("parallel", "parallel"),
                                             vmem_limit_bytes=VMEM_LIMIT),
        name="sb_attention",
    )(q_src, k_all, v_all)


def _cummax_rows(x):
    r = x.shape[0]
    row = lax.broadcasted_iota(jnp.int32, x.shape, 0)
    shift = 1
    while shift < r:
        x = jnp.maximum(x, jnp.where(row >= shift, pltpu.roll(x, shift, 0), -jnp.inf))
        shift *= 2
    return x


def _mlstm_kernel(q_ref, k_ref, v_ref, og_ref, g_ref, bias_ref, ng_ref, c0_ref, n0_ref, m0_ref,
                  h_ref, c_out, n_out, m_out, cn_s, m_s, *, c, nb):
    j = pl.program_id(1)

    @pl.when(j == 0)
    def _():
        cn_s[...] = jnp.zeros_like(cn_s)
        m_s[...] = jnp.zeros_like(m_s)
        for i in range(nb):
            for h in range(N_HEADS):
                cn_s[i, _hs(h), _hs(h)] = c0_ref[i, h]
                cn_s[i, _hs(h), W_GROUP + h * HEAD_V:W_GROUP + (h + 1) * HEAD_V] = jnp.broadcast_to(
                    n0_ref[i, h], (HEAD_V, HEAD_V))
            m_s[i, :, 0:N_HEADS] = m0_ref[i]

    seqs = range(nb)
    hs = N_HEADS * c
    lane = lax.broadcasted_iota(jnp.int32, (c, LANES), 1)
    is_f = (lane >= G_MF) & (lane < G_MF + N_HEADS)
    head_lane = lane < N_HEADS
    incl_b = _tri(c).astype(BF16)
    t_idx = lax.broadcasted_iota(jnp.int32, (c, hs), 0)
    s_idx = lax.broadcasted_iota(jnp.int32, (c, hs), 1) & (c - 1)
    incl4 = t_idx >= s_idx
    same_hv = _same_head(hs, W_GROUP, c, HEAD_V)
    same_vv = _same_head(W_GROUP, W_GROUP, HEAD_V, HEAD_V)
    group_sum = same_vv.astype(BF16)

    def expand(lanes, group):
        g = lax.broadcasted_iota(jnp.int32, (LANES, lanes), 0)
        h = lax.broadcasted_iota(jnp.int32, (LANES, lanes), 1) >> (group.bit_length() - 1)
        return (g == h).astype(BF16)

    gi = lax.broadcasted_iota(jnp.int32, (LANES, 2 * LANES), 0)
    li = lax.broadcasted_iota(jnp.int32, (LANES, 2 * LANES), 1)
    to_heads = (jnp.where((li < N_HEADS) & (gi == G_MI + li), 1.0, 0.0)
                - jnp.where((li < N_HEADS) & (gi == G_MF + li), 1.0, 0.0)
                + jnp.where((li >= LANES) & (li < LANES + N_HEADS) & (gi == G_MF + li - LANES), 1.0, 0.0)
                ).astype(BF16)
    row_sel = (lax.broadcasted_iota(jnp.int32, (8, LANES), 1)
               == lax.broadcasted_iota(jnp.int32, (8, LANES), 0)).astype(BF16)
    gates = [g_ref[i] + bias_ref[...] for i in seqs]
    merged = [jnp.where(is_f, _dot01_left(incl_b, jnp.where(is_f, _log_sigmoid(g), 0.0)), g) for g in gates]
    gb = _dot01_right(merged, to_heads)
    g_c = [jnp.where(head_lane, x[:, :LANES], -jnp.inf) for x in gb]
    b_c = [x[:, LANES:] for x in gb]
    m_prev = [m_s[i] for i in seqs]
    r_c = [jnp.maximum(m_prev[i], _cummax_rows(g_c[i])) for i in seqs]
    r_end = [r[c - 1:c, :] for r in r_c]
    si_c = [jnp.exp(m_prev[i] - r_c[i]) for i in seqs]
    emt_c = [jnp.exp(-(b_c[i] + r_c[i])) for i in seqs]
    w_c = [jnp.exp(g_c[i] - r_end[i]) for i in seqs]
    dec_c = [jnp.broadcast_to(jnp.exp(m_prev[i] - r_end[i]), (c, LANES)) for i in seqs]
    zero_pad = lambda x: jnp.where(head_lane, x, 0.0)
    r_hs = _dot01_right([zero_pad(r) for r in r_c], expand(hs, c))
    cols = _dot01_right([zero_pad(x) for i in seqs for x in (si_c[i], emt_c[i], w_c[i], dec_c[i])],
                        expand(W_GROUP, HEAD_V))
    si_hv, emt_hv, w_hv, dec_hv = (cols[n::4] for n in range(4))
    g_rows = [sum(lax.dot_general(row_sel, part, (((1,), (1,)), ((), ())), preferred_element_type=F32)
                  for part in _split3(zero_pad(x))) for x in g_c]
    g_row_hs = [jnp.concatenate([gr[h:h + 1, :] for h in range(N_HEADS)], axis=1) for gr in g_rows]
    d_hs = [jnp.where(incl4, jnp.exp(g_row_hs[i] - r_hs[i]), 0.0) for i in seqs]
    qs = [q_ref[i].astype(BF16) for i in seqs]
    ks = [k_ref[i] * (HEAD_V ** -0.5) for i in seqs]
    vs = [v_ref[i] for i in seqs]
    cns = [cn_s[i] for i in seqs]
    qk = [lax.dot_general(qs[i], _head_blocks(ks[i].astype(BF16), same_hv), (((1,), (1,)), ((), ())),
                          preferred_element_type=F32) for i in seqs]
    q_cn = [jnp.dot(qs[i], cns[i].astype(BF16), preferred_element_type=F32) for i in seqs]
    scs = [(qk[i] * d_hs[i]).astype(BF16) for i in seqs]
    ones_blocks = same_hv.astype(BF16)
    v_ones = [jnp.concatenate([_head_blocks(vs[i].astype(BF16), same_hv), ones_blocks], axis=1)
              for i in seqs]
    sc_v = [jnp.dot(scs[i], v_ones[i], preferred_element_type=F32) for i in seqs]
    nd = [sc_v[i] + jnp.concatenate([si_hv[i], si_hv[i]], axis=1) * q_cn[i] for i in seqs]
    hcs = [nd[i][:, :W_GROUP] / jnp.maximum(jnp.abs(nd[i][:, W_GROUP:]), emt_hv[i]) for i in seqs]
    kws = [ks[i] * w_hv[i] for i in seqs]
    upds = [_dot_tn(kws[i], jnp.concatenate([vs[i], jnp.ones_like(vs[i])], axis=1)) for i in seqs]
    same_cn = jnp.concatenate([same_vv, same_vv], axis=1)
    sum_sq = _dot01_right([x * x for x in hcs], group_sum)
    ng = ng_ref[...]
    for i in seqs:
        dec_row = dec_hv[i][0:1, :]
        cn_s[i] = jnp.concatenate([dec_row, dec_row], axis=1) * cns[i] + jnp.where(same_cn, upds[i], 0.0)
        m_s[i] = jnp.where(head_lane[0:1, :], b_c[i][c - 1:c, :] + r_end[i], 0.0)
        h_ref[i] = (hcs[i] * lax.rsqrt(sum_sq[i] * (1.0 / HEAD_V) + EPS) * ng * _sigmoid(og_ref[i]))

    @pl.when(j == pl.num_programs(1) - 1)
    def _():
        for i in seqs:
            for h in range(N_HEADS):
                c_out[i, h] = cn_s[i, _hs(h), _hs(h)]
                n_out[i, h] = cn_s[i, _hs(h), W_GROUP + h * HEAD_V:W_GROUP + h * HEAD_V + 1]
            m_out[i] = m_s[i, :, 0:N_HEADS]


def _mlstm_call(p, bias_row, ng, c0, n0, m0, c, nb):
    b, l, _ = p.shape
    nb = math.gcd(nb, b)
    nc = l // c
    blk = lambda cb: pl.BlockSpec((nb, c, W_GROUP), lambda i, j, cb=cb: (i, j, cb))
    full = lambda a: pl.BlockSpec((nb,) + a.shape[1:], lambda i, j: (i,) + (0,) * (a.ndim - 1))
    n0 = n0[..., None]
    m0 = m0.reshape(b, 1, N_HEADS)
    h, c_new, n_new, m_new = pl.pallas_call(
        functools.partial(_mlstm_kernel, c=c, nb=nb),
        grid=(b // nb, nc),
        in_specs=[blk(PB_MQ), blk(PB_MK), blk(PB_MV), blk(PB_MO),
                  pl.BlockSpec((nb, c, LANES), lambda i, j: (i, j, PB128_GATES)),
                  pl.BlockSpec((1, LANES), lambda i, j: (0, 0)),
                  pl.BlockSpec((1, W_GROUP), lambda i, j: (0, 0)),
                  full(c0), full(n0), full(m0)],
        out_specs=[pl.BlockSpec((nb, c, W_GROUP), lambda i, j: (i, j, 0)), full(c0), full(n0), full(m0)],
        out_shape=[jax.ShapeDtypeStruct((b, l, W_GROUP), F32), jax.ShapeDtypeStruct(c0.shape, F32),
                   jax.ShapeDtypeStruct(n0.shape, F32), jax.ShapeDtypeStruct(m0.shape, F32)],
        scratch_shapes=[pltpu.VMEM((nb, W_GROUP, 2 * W_GROUP), F32), pltpu.VMEM((nb, 1, LANES), F32)],
        compiler_params=pltpu.CompilerParams(dimension_semantics=("parallel", "arbitrary"),
                                             vmem_limit_bytes=VMEM_LIMIT),
        name="mlstm",
    )(p, p, p, p, p, bias_row, ng, c0, n0, m0)
    return h, c_new, n_new[..., 0], m_new[:, 0, :]


def _same_head(rows, lanes, row_group, lane_group):
    rh = lax.broadcasted_iota(jnp.int32, (rows, lanes), 0) >> (row_group.bit_length() - 1)
    lh = lax.broadcasted_iota(jnp.int32, (rows, lanes), 1) >> (lane_group.bit_length() - 1)
    return rh == lh


def _head_blocks(x, same_head):
    tiled = jnp.concatenate([x] * N_HEADS, axis=0)
    return jnp.where(same_head, tiled, jnp.zeros_like(tiled))


def _heads_dot_split(l, r, same_head):
    return _heads_dot_parts(_split2(l), _split2(r), same_head)


def _heads_dot_parts(l_parts, r_parts, same_head):
    l1, l2 = l_parts
    rb1, rb2 = _head_blocks(r_parts[0], same_head), _head_blocks(r_parts[1], same_head)
    m = l1.shape[0]
    y = jnp.dot(jnp.concatenate([l1, l2], axis=0), rb1, preferred_element_type=F32)
    return y[:m] + (y[m:] + jnp.dot(l1, rb2, preferred_element_type=F32))


def _unit_lower_inverses(lows, eye4, same_head, c):
    ts = [eye4 - a for a in lows]
    a_parts = [_split2(a) for a in lows]
    ps = [_heads_dot_parts(ap, ap, same_head) for ap in a_parts]
    for _ in range(c.bit_length() - 3):
        t_parts = [_split2(t) for t in ts]
        p_parts = [_split2(p) for p in ps]
        stacked = [tuple(jnp.concatenate([tp[n], pp[n]], axis=0) for n in range(2))
                   for tp, pp in zip(t_parts, p_parts)]
        ys = [_heads_dot_parts(sp, pp, same_head) for sp, pp in zip(stacked, p_parts)]
        ts = [t + y[:c] for t, y in zip(ts, ys)]
        ps = [y[c:] for y in ys]
    return [t + _heads_dot_split(t, p, same_head) for t, p in zip(ts, ps)]


def _gdn_kernel(q_ref, k_ref, v_ref, z_ref, g_ref, bias_ref, alog_ref, ng_ref, s0_ref,
                o_ref, s_out, s_s, *, c, nb):
    j = pl.program_id(1)
    seqs = range(nb)

    @pl.when(j == 0)
    def _():
        s_s[...] = jnp.zeros_like(s_s)
        for i in seqs:
            for h in range(N_HEADS):
                s_s[i, _hs(h), _hs(h)] = s0_ref[i, h]

    hs = N_HEADS * c
    lane = lax.broadcasted_iota(jnp.int32, (c, LANES), 1)
    is_a = (lane >= G_GA) & (lane < G_GA + N_HEADS)
    incl_b = _tri(c).astype(BF16)
    t_idx = lax.broadcasted_iota(jnp.int32, (c, hs), 0)
    s_idx = lax.broadcasted_iota(jnp.int32, (c, hs), 1) & (c - 1)
    incl4, strict4, eye4 = t_idx >= s_idx, t_idx > s_idx, (t_idx == s_idx).astype(F32)
    same_hs = _same_head(hs, hs, c, c)
    same_hv = _same_head(hs, W_GROUP, c, HEAD_V)
    same_vv = _same_head(W_GROUP, W_GROUP, HEAD_V, HEAD_V)
    group_sum = same_vv.astype(BF16)

    def pick(lanes, group):
        g = lax.broadcasted_iota(jnp.int32, (LANES, 2 * lanes), 0)
        l = lax.broadcasted_iota(jnp.int32, (LANES, 2 * lanes), 1)
        h = (l & (lanes - 1)) >> (group.bit_length() - 1)
        return (g == jnp.where(l < lanes, G_GA, G_GB) + h).astype(BF16)

    pick_hs = pick(hs, c)
    row_sel = (lax.broadcasted_iota(jnp.int32, (8, LANES), 1)
               == G_GA + lax.broadcasted_iota(jnp.int32, (8, LANES), 0)).astype(BF16)
    gates = [g_ref[i] + bias_ref[...] for i in seqs]
    xgs = [jnp.where(is_a, -jnp.exp(alog_ref[...]) * _softplus(g), _sigmoid(g)) for g in gates]
    csums = [jnp.where(is_a, _dot01_left(incl_b, xg), xg) for xg in xgs]
    cols_hs = _dot01_right(csums, pick_hs)
    bc_hs, beta_hs = [x[:, :hs] for x in cols_hs], [x[:, hs:] for x in cols_hs]
    if c == HEAD_V:
        bc_hv, beta_hv = bc_hs, beta_hs
    else:
        cols_hv = _dot01_right(csums, pick(W_GROUP, HEAD_V))
        bc_hv, beta_hv = [x[:, :W_GROUP] for x in cols_hv], [x[:, W_GROUP:] for x in cols_hv]
    b_rows = [sum(lax.dot_general(row_sel, part, (((1,), (1,)), ((), ())), preferred_element_type=F32)
                  for part in _split3(cs)) for cs in csums]
    br_hs = [jnp.concatenate([br[h:h + 1, :] for h in range(N_HEADS)], axis=1) for br in b_rows]
    decays = [jnp.exp(jnp.where(incl4, bc_hs[i] - br_hs[i], 0.0)) for i in seqs]
    qk_raw = [jnp.concatenate([q_ref[i], k_ref[i]], axis=0) for i in seqs]
    norms = _dot01_right([x * x for x in qk_raw], group_sum)
    qk_n = [qk_raw[i] * lax.rsqrt(norms[i] + EPS) for i in seqs]
    qs = [x[:c] * (HEAD_V ** -0.5) for x in qk_n]
    ks = [x[c:] for x in qk_n]
    vs = [v_ref[i] for i in seqs]
    kq = [jnp.concatenate([ks[i], qs[i]], axis=0).astype(BF16) for i in seqs]
    states = [s_s[i] for i in seqs]
    scores = [lax.dot_general(kq[i], _head_blocks(ks[i].astype(BF16), same_hv),
                              (((1,), (1,)), ((), ())), preferred_element_type=F32) for i in seqs]
    on_state = [jnp.dot(kq[i], states[i].astype(BF16), preferred_element_type=F32) for i in seqs]
    lows = [jnp.where(strict4, beta_hs[i] * decays[i] * scores[i][:c], 0.0) for i in seqs]
    invs = _unit_lower_inverses(lows, eye4, same_hs, c)
    ebs = [jnp.exp(bc_hv[i]) for i in seqs]
    rhss = [beta_hv[i] * (vs[i] - ebs[i] * on_state[i][:c]) for i in seqs]
    us = [_heads_dot_split(invs[i], rhss[i], same_hv) for i in seqs]
    qkm = [jnp.where(incl4, decays[i] * scores[i][c:], 0.0).astype(BF16) for i in seqs]
    outs = [ebs[i] * on_state[i][c:]
            + jnp.dot(qkm[i], _head_blocks(us[i].astype(BF16), same_hv), preferred_element_type=F32)
            for i in seqs]
    b_ends = [bc_hv[i][c - 1:c, :] for i in seqs]
    upds = [jnp.where(same_vv, _dot_tn(ks[i] * jnp.exp(b_ends[i] - bc_hv[i]), us[i]), 0.0) for i in seqs]
    ng = ng_ref[...]
    sum_sq = _dot01_right([o * o for o in outs], group_sum)
    for i in seqs:
        s_s[i] = jnp.exp(b_ends[i]) * states[i] + upds[i]
        o_ref[i] = (outs[i] * lax.rsqrt(sum_sq[i] * (1.0 / HEAD_V) + EPS) * ng * _silu(z_ref[i]))

    @pl.when(j == pl.num_programs(1) - 1)
    def _():
        for i in seqs:
            for h in range(N_HEADS):
                s_out[i, h] = s_s[i, _hs(h), _hs(h)]


def _gdn_call(p, bias_row, alog_row, ng, s0, c, nb):
    b, l, _ = p.shape
    nb = math.gcd(nb, b)
    nc = l // c
    blk = lambda cb: pl.BlockSpec((nb, c, W_GROUP), lambda i, j, cb=cb: (i, j, cb))
    full = lambda a: pl.BlockSpec((nb,) + a.shape[1:], lambda i, j: (i,) + (0,) * (a.ndim - 1))
    return pl.pallas_call(
        functools.partial(_gdn_kernel, c=c, nb=nb),
        grid=(b // nb, nc),
        in_specs=[blk(PB_GQ), blk(PB_GK), blk(PB_GV), blk(PB_GZ),
                  pl.BlockSpec((nb, c, LANES), lambda i, j: (i, j, PB128_GATES)),
                  pl.BlockSpec((1, LANES), lambda i, j: (0, 0)),
                  pl.BlockSpec((1, LANES), lambda i, j: (0, 0)),
                  pl.BlockSpec((1, W_GROUP), lambda i, j: (0, 0)),
                  full(s0)],
        out_specs=[pl.BlockSpec((nb, c, W_GROUP), lambda i, j: (i, j, 0)), full(s0)],
        out_shape=[jax.ShapeDtypeStruct((b, l, W_GROUP), F32), jax.ShapeDtypeStruct(s0.shape, F32)],
        scratch_shapes=[pltpu.VMEM((nb, W_GROUP, W_GROUP), F32)],
        compiler_params=pltpu.CompilerParams(dimension_semantics=("parallel", "arbitrary"),
                                             vmem_limit_bytes=VMEM_LIMIT),
        name="gdn",
    )(p, p, p, p, p, bias_row, alog_row, ng, s0)


GLA_SUB = 8


def _gla_kernel(q_ref, k_ref, v_ref, r_ref, g_ref, w2_ref, gb_ref, ng_ref, s0_ref,
                o_ref, s_out, s_s, *, c, nb):
    j = pl.program_id(1)
    dk_all = N_HEADS * GLA_DK
    hs = N_HEADS * c
    nsb = c // GLA_SUB
    seqs = range(nb)

    @pl.when(j == 0)
    def _():
        s_s[...] = jnp.zeros_like(s_s)
        for i in seqs:
            for h in range(N_HEADS):
                s_s[i, h * GLA_DK:(h + 1) * GLA_DK, _hs(h)] = s0_ref[i, h]

    incl_b = _tri(c).astype(BF16)
    er = lax.broadcasted_iota(jnp.int32, (2 * c, c), 0)
    ec = lax.broadcasted_iota(jnp.int32, (2 * c, c), 1)
    et = er & (c - 1)
    edge_sel = (ec == jnp.where(er < c, (et & ~(GLA_SUB - 1)) - 1, et | (GLA_SUB - 1))).astype(BF16)
    t_idx = lax.broadcasted_iota(jnp.int32, (c, hs), 0)
    s_idx = lax.broadcasted_iota(jnp.int32, (c, hs), 1) & (c - 1)
    blk_dist = (t_idx >> 3) - (s_idx >> 3)
    diag_off = t_idx - s_idx
    same_hk = _same_head(hs, dk_all, c, GLA_DK)
    same_hv = _same_head(hs, W_GROUP, c, HEAD_V)
    same_kv = _same_head(dk_all, W_GROUP, GLA_DK, HEAD_V)
    head_rep = _same_head(dk_all, hs, GLA_DK, c).astype(BF16)
    group_sum = _same_head(W_GROUP, W_GROUP, HEAD_V, HEAD_V).astype(BF16)
    row8 = lax.broadcasted_iota(jnp.int32, (c, dk_all), 0) & (GLA_SUB - 1)
    last_row = lax.broadcasted_iota(jnp.int32, (c, dk_all), 0) == c - 1

    log_as = [_log_sigmoid(_dot(g_ref[i], w2_ref[...]) + gb_ref[...]) / GLA_TAU for i in seqs]
    bs = [_dot01_left(incl_b, la) for la in log_as]
    edges = [_dot01_left(edge_sel, b) for b in bs]
    e_prev = [e[:c] for e in edges]
    e_own = [e[c:] for e in edges]
    qs = [q_ref[i] * (GLA_DK ** -0.5) for i in seqs]
    ks = [k_ref[i] for i in seqs]
    vs = [v_ref[i] for i in seqs]
    q_hat = [qs[i] * jnp.exp(bs[i] - e_prev[i]) for i in seqs]
    k_hat = [ks[i] * jnp.exp(e_own[i] - bs[i]) for i in seqs]
    def q_for_distance(i, m):
        if m == 0:
            return q_hat[i]
        rows = GLA_SUB * m
        shifted = jnp.concatenate([jnp.zeros((rows, dk_all), F32), e_prev[i][:c - rows]], axis=0)
        return q_hat[i] * jnp.exp(e_prev[i] - shifted)
    a_parts = []
    for i in seqs:
        lhs = jnp.concatenate([q_for_distance(i, m) for m in range(nsb - 1)], axis=0).astype(BF16)
        prod = lax.dot_general(lhs, _head_blocks(k_hat[i].astype(BF16), same_hk),
                               (((1,), (1,)), ((), ())), preferred_element_type=F32)
        a_parts.append(sum(jnp.where(blk_dist == m + 1, prod[m * c:(m + 1) * c], 0.0)
                           for m in range(nsb - 1)))
    def rot8(x, d):
        return x if d == 0 else pltpu.roll(x, d, 0)
    for i in seqs:
        ws = [jnp.where(row8 >= d, qs[i] * rot8(ks[i], d) * jnp.exp(bs[i] - rot8(bs[i], d)), 0.0)
              for d in range(GLA_SUB)]
        prod = jnp.dot(jnp.concatenate(ws, axis=0).astype(BF16), head_rep, preferred_element_type=F32)
        a_parts[i] = a_parts[i] + sum(jnp.where(diag_off == d, prod[d * c:(d + 1) * c], 0.0)
                                      for d in range(GLA_SUB))
    states = [s_s[i] for i in seqs]
    outs = [jnp.dot(a_parts[i].astype(BF16), _head_blocks(vs[i].astype(BF16), same_hv),
                    preferred_element_type=F32)
            + _dot(qs[i] * jnp.exp(bs[i]), states[i]) for i in seqs]
    b_last = [b[c - 1:c, :] for b in bs]
    upds = [jnp.where(same_kv, _dot_tn(ks[i] * jnp.exp(b_last[i] - bs[i]), vs[i]), 0.0) for i in seqs]
    ones_cv = jnp.ones((c, W_GROUP), BF16)
    decay_rows = [sum(lax.dot_general(part, ones_cv, (((0,), (0,)), ((), ())), preferred_element_type=F32)
                      for part in _split3(jnp.where(last_row, b, 0.0))) for b in bs]
    sum_sq = _dot01_right([o * o for o in outs], group_sum)
    ng = ng_ref[...]
    for i in seqs:
        s_s[i] = states[i] * jnp.exp(decay_rows[i]) + upds[i]
        o_ref[i] = outs[i] * lax.rsqrt(sum_sq[i] * (1.0 / HEAD_V) + EPS) * ng * _silu(r_ref[i])

    @pl.when(j == pl.num_programs(1) - 1)
    def _():
        for i in seqs:
            for h in range(N_HEADS):
                s_out[i, h] = s_s[i, h * GLA_DK:(h + 1) * GLA_DK, _hs(h)]


def _gla_call(p, w2_pad, gb_row, ng, s0, c, nb):
    b, l, _ = p.shape
    nb = math.gcd(nb, b)
    nc = l // c
    dk_all = N_HEADS * GLA_DK
    blk = lambda cb: pl.BlockSpec((nb, c, W_GROUP), lambda i, j, cb=cb: (i, j, cb))
    blk128 = lambda cb: pl.BlockSpec((nb, c, LANES), lambda i, j, cb=cb: (i, j, cb))
    state = pl.BlockSpec((nb,) + s0.shape[1:], lambda i, j: (i, 0, 0, 0))
    return pl.pallas_call(
        functools.partial(_gla_kernel, c=c, nb=nb),
        grid=(b // nb, nc),
        in_specs=[blk128(PB128_LQ), blk128(PB128_LK), blk(PB_LV), blk(PB_LR), blk128(PB128_GATES),
                  pl.BlockSpec((LANES, dk_all), lambda i, j: (0, 0)),
                  pl.BlockSpec((1, dk_all), lambda i, j: (0, 0)),
                  pl.BlockSpec((1, W_GROUP), lambda i, j: (0, 0)),
                  state],
        out_specs=[pl.BlockSpec((nb, c, W_GROUP), lambda i, j: (i, j, 0)), state],
        out_shape=[jax.ShapeDtypeStruct((b, l, W_GROUP), F32), jax.ShapeDtypeStruct(s0.shape, F32)],
        scratch_shapes=[pltpu.VMEM((nb, dk_all, W_GROUP), F32)],
        compiler_params=pltpu.CompilerParams(dimension_semantics=("parallel", "arbitrary"),
                                             vmem_limit_bytes=VMEM_LIMIT),
        name="gla",
    )(p, p, p, p, p, w2_pad, gb_row, ng, s0)


def _outffn_kernel(x_ref, a_ref, b_ref, c_ref, d_ref, mod_ref, g2_ref, gf_ref, wo_ref, w1_ref, w2_ref,
                   o_ref, *, ff_chunk, final):
    mod = mod_ref[0]
    mixed = jnp.concatenate([a_ref[0], b_ref[0], c_ref[0], d_ref[0]], axis=-1).astype(BF16)
    x = x_ref[0] + mod[2:3] * jnp.dot(mixed, wo_ref[...], preferred_element_type=F32)
    h = _modulated_norm(x, g2_ref[...], mod[4:5], mod[3:4]).astype(BF16)
    d_ff = w1_ref.shape[1]
    acc = jnp.zeros(x.shape, F32)
    for f0 in range(0, d_ff, ff_chunk):
        a = jnp.maximum(jnp.dot(h, w1_ref[:, f0:f0 + ff_chunk], preferred_element_type=F32), 0.0)
        acc = acc + jnp.dot((a * a).astype(BF16), w2_ref[f0:f0 + ff_chunk, :],
                            preferred_element_type=F32)
    x = x + mod[5:6] * acc
    if final:
        x = x * lax.rsqrt(jnp.mean(x * x, axis=-1, keepdims=True) + EPS) * gf_ref[...]
    o_ref[0] = x


def _outffn_call(x, mixers, mod, g2, gf, wo, w1, w2, tm, final):
    b, l, d = x.shape
    d_ff = w1.shape[1]
    tok = lambda w: pl.BlockSpec((1, tm, w), lambda i, j: (i, j, 0))
    const = lambda shape: pl.BlockSpec(shape, lambda i, j: (0,) * len(shape),
                                       pipeline_mode=pl.Buffered(1))
    return pl.pallas_call(
        functools.partial(_outffn_kernel, ff_chunk=1024, final=final),
        grid=(b, l // tm),
        in_specs=[tok(d), tok(W_GROUP), tok(W_GROUP), tok(W_GROUP), tok(W_GROUP),
                  pl.BlockSpec((1, 6, d), lambda i, j: (i, 0, 0)),
                  pl.BlockSpec((1, d), lambda i, j: (0, 0)),
                  pl.BlockSpec((1, d), lambda i, j: (0, 0)),
                  const((d, d)), const((d, d_ff)), const((d_ff, d))],
        out_specs=tok(d),
        out_shape=jax.ShapeDtypeStruct((b, l, d), F32),
        compiler_params=pltpu.CompilerParams(dimension_semantics=("parallel", "parallel"),
                                             vmem_limit_bytes=VMEM_LIMIT),
        name="outproj_ffn",
    )(x, *mixers, mod, g2, gf, wo, w1, w2)


def _permute_w_in(w_in):
    d = w_in.shape[0]
    sizes = (256, 256, 256, 256, 256, 256, 4, 4, 256, 768, 4, 4, 256, 128, 128, 256, 16, 256)
    offs = [0]
    for s in sizes:
        offs.append(offs[-1] + s)
    (sq, sk, sv, mq, mk, mv, mi, mf, mo, gqkv, gb, ga, gz, lq, lk, lv, lg, lr) = [
        w_in[:, offs[i]:offs[i + 1]] for i in range(len(sizes))]
    pad = jnp.zeros((d, LANES - (4 * N_HEADS + GLA_RANK)), w_in.dtype)
    return jnp.concatenate([sk, sv, sq, mq, mk, mv, mo, gqkv, gz, lv, lr, lq, lk,
                            mi, mf, gb, ga, lg, pad], axis=1)


def _row128(pieces):
    row = jnp.zeros((LANES,), F32)
    for off, vec in pieces:
        row = row.at[off:off + vec.shape[0]].set(vec)
    return row.reshape(1, LANES)


class _Tiles(NamedTuple):
    tm: int
    sb_tq: int
    sb_tk: int
    mlstm_chunk: int
    chunk: int


def _tiles(l):
    return _Tiles(tm=min(512, l), sb_tq=min(256, l), sb_tk=max(min(256, l), LANES),
                  mlstm_chunk=min(64, l), chunk=min(64, l))


def _layer(x, mod, lw, states, tiles, final, final_g):
    tm, sb_tq, sb_tk, mlstm_chunk, chunk = tiles
    (n1, n2, w_in_p, gate_bias, ml_ng, conv_w, alog_row, gdn_ng, w2_pad, gla_gb, gla_ng,
     w_out, w_ff1, w_ff2) = lw
    (sb_k_past, sb_v_past, ml_c, ml_n, ml_m, gdn_s, gdn_buf, gla_s) = states
    b, l, _ = x.shape
    k_new, v_new, p, gdn_buf = _inproj_call(x, mod, n1, w_in_p, conv_w, gdn_buf, tm)
    if sb_k_past is None:
        k_all, v_all, q_offset = k_new, v_new, 0
    else:
        past = sb_k_past.shape[1]
        lk = -(-(past + l) // sb_tk) * sb_tk
        padz = jnp.zeros((b, lk - past - l, W_GROUP), F32)
        k_all = jnp.concatenate([sb_k_past.reshape(b, past, W_GROUP), k_new, padz], axis=1)
        v_all = jnp.concatenate([sb_v_past.reshape(b, past, W_GROUP), v_new, padz], axis=1)
        q_offset = past
    o_sb = _sb_call(p, PB_SQ, k_all, v_all, l, sb_tq, sb_tk, q_offset)
    o_ml, ml_c, ml_n, ml_m = _mlstm_call(p, gate_bias, ml_ng, ml_c, ml_n, ml_m, mlstm_chunk,
                                         MLSTM_SEQS_PER_STEP)
    o_gdn, gdn_s = _gdn_call(p, gate_bias, alog_row, gdn_ng, gdn_s, chunk, GDN_SEQS_PER_STEP)
    o_gla, gla_s = _gla_call(p, w2_pad, gla_gb, gla_ng, gla_s, chunk, GLA_SEQS_PER_STEP)
    x = _outffn_call(x, (o_sb, o_ml, o_gdn, o_gla), mod, n2, final_g, w_out, w_ff1, w_ff2, tm, final)
    hk = lambda a: a.reshape(b, l, N_HEADS, HEAD_V)
    return x, (hk(k_new), hk(v_new), ml_c, ml_n, ml_m, gdn_s, gdn_buf, gla_s)


def kernel(x_prompt, x_sample, cache_sb_k, cache_sb_v, state_mlstm_C, state_mlstm_n, state_mlstm_m, state_gdn_S, state_gdn_conv, state_gla_S, c_prompt, c_sample, norm1_g, norm2_g, w_ada, b_ada, w_in, mlstm_i_bias, mlstm_f_bias, mlstm_norm_g, gdn_conv_w, gdn_a_log, gdn_dt_bias, gdn_norm_g, gla_w_gate2, gla_gate_bias, gla_norm_g, w_out, w_ff1, w_ff2, final_g):
    depth = w_in.shape[0]
    bp, lp, d = x_prompt.shape
    bs, ls, _ = x_sample.shape
    dk_all = N_HEADS * GLA_DK
    mod_all = _ada_call(jnp.concatenate([c_prompt, c_sample], axis=0), w_ada, b_ada)
    mod_all = mod_all.reshape(depth, bp + bs, 6, d)
    final_row = final_g.reshape(1, d)
    xp, xs = x_prompt, x_sample
    p_list, s_list = [], []
    for l in range(depth):
        gate_bias = _row128([(G_MI, mlstm_i_bias[l]), (G_MF, mlstm_f_bias[l]), (G_GA, gdn_dt_bias[l])])
        alog_row = _row128([(G_GA, gdn_a_log[l])])
        w2_pad = jnp.zeros((LANES, dk_all), F32).at[G_LG:G_LG + GLA_RANK, :].set(gla_w_gate2[l])
        lw = (norm1_g[l].reshape(1, d), norm2_g[l].reshape(1, d), _permute_w_in(w_in[l]).astype(BF16),
              gate_bias, mlstm_norm_g[l].reshape(1, W_GROUP), gdn_conv_w[l], alog_row,
              gdn_norm_g[l].reshape(1, W_GROUP), w2_pad.astype(BF16),
              gla_gate_bias[l].reshape(1, dk_all), gla_norm_g[l].reshape(1, W_GROUP),
              w_out[l].astype(BF16), w_ff1[l].astype(BF16), w_ff2[l].astype(BF16))
        fresh = (None, None, jnp.zeros((bp,) + state_mlstm_C.shape[2:], F32),
                 jnp.zeros((bp,) + state_mlstm_n.shape[2:], F32),
                 jnp.zeros((bp,) + state_mlstm_m.shape[2:], F32),
                 jnp.zeros((bp,) + state_gdn_S.shape[2:], F32),
                 jnp.zeros((bp,) + state_gdn_conv.shape[2:], F32),
                 jnp.zeros((bp,) + state_gla_S.shape[2:], F32))
        past = (cache_sb_k[l], cache_sb_v[l], state_mlstm_C[l], state_mlstm_n[l], state_mlstm_m[l],
                state_gdn_S[l], state_gdn_conv[l], state_gla_S[l])
        final = l == depth - 1
        xp, st_p = _layer(xp, mod_all[l, :bp], lw, fresh, _tiles(lp), final, final_row)
        xs, st_s = _layer(xs, mod_all[l, bp:], lw, past, _tiles(ls), final, final_row)
        p_list.append(st_p)
        s_list.append(st_s)
    stacked_p = [jnp.stack([st[i] for st in p_list]) for i in range(8)]
    stacked_s = [jnp.stack([st[i] for st in s_list]) for i in range(8)]
    return (xp, xs, *stacked_p, *stacked_s)
```

```python
import functools
import math
from typing import NamedTuple

import jax
import jax.numpy as jnp
from jax import lax
from jax.experimental import pallas as pl
from jax.experimental.pallas import tpu as pltpu

F32 = jnp.float32
BF16 = jnp.bfloat16
EPS = 1e-6
N_HEADS = 4
HEAD_V = 64
GLA_DK = 32
CONV_W = 4
GLA_TAU = 16.0
W_GROUP = N_HEADS * HEAD_V
LANES = 128
VMEM_LIMIT = 56 * 1024 * 1024
MLSTM_SEQS_PER_STEP = 4
GDN_SEQS_PER_STEP = 8
GLA_SEQS_PER_STEP = 8

PB_SQ, PB_MQ, PB_MK, PB_MV, PB_MO, PB_GQ, PB_GK, PB_GV, PB_GZ, PB_LV, PB_LR = range(11)
PB128_LQ, PB128_LK, PB128_GATES = 22, 23, 24
P_WIDTH = 25 * LANES
G_MI, G_MF, G_GB, G_GA, G_LG = 0, 4, 8, 12, 16
GLA_RANK = 16


def _dot(a, b):
    return jnp.dot(a.astype(BF16), b.astype(BF16), preferred_element_type=F32)


def _dot_tn(a, b):
    return lax.dot_general(a.astype(BF16), b.astype(BF16), (((0,), (0,)), ((), ())),
                           preferred_element_type=F32)


def _split3(x):
    x1 = x.astype(BF16)
    r1 = x - x1.astype(F32)
    x2 = r1.astype(BF16)
    x3 = (r1 - x2.astype(F32)).astype(BF16)
    return x1, x2, x3


def _split2(x):
    x1 = x.astype(BF16)
    x2 = (x - x1.astype(F32)).astype(BF16)
    return x1, x2


def _dot01_left(m01, x):
    return sum(jnp.dot(m01, p, preferred_element_type=F32) for p in _split3(x))


def _dot01_right(xs, m01):
    n = xs[0].shape[0]
    stacked = jnp.concatenate([part for x in xs for part in _split3(x)], axis=0)
    y = jnp.dot(stacked, m01, preferred_element_type=F32)
    return [y[3 * i * n:(3 * i + 1) * n] + (y[(3 * i + 1) * n:(3 * i + 2) * n]
                                             + y[(3 * i + 2) * n:(3 * i + 3) * n])
            for i in range(len(xs))]


def _group_sums(xs, group_sum):
    n = xs[0].shape[0]
    y = jnp.dot(jnp.concatenate([x.astype(BF16) for x in xs], axis=0), group_sum,
                preferred_element_type=F32)
    return [y[i * n:(i + 1) * n] for i in range(len(xs))]


def _log_sigmoid(x):
    return jnp.minimum(x, 0.0) - jnp.log(1.0 + jnp.exp(-jnp.abs(x)))


def _sigmoid(x):
    return 1.0 / (1.0 + jnp.exp(-x))


def _silu(x):
    return x * _sigmoid(x)


def _softplus(x):
    return jnp.maximum(x, 0.0) + jnp.log(1.0 + jnp.exp(-jnp.abs(x)))


def _tri(c):
    r = lax.broadcasted_iota(jnp.int32, (c, c), 0)
    s = lax.broadcasted_iota(jnp.int32, (c, c), 1)
    return r >= s


def _hs(h, w=HEAD_V):
    return slice(h * w, (h + 1) * w)


def _ada_kernel(c_ref, w_ref, b_ref, o_ref):
    o_ref[0] = _dot(_silu(c_ref[...]), w_ref[0]) + b_ref[0]


def _ada_call(c_all, w_ada, b_ada, tn=512):
    depth, d, n = w_ada.shape
    rows = c_all.shape[0]
    return pl.pallas_call(
        _ada_kernel,
        grid=(depth, n // tn),
        in_specs=[pl.BlockSpec((rows, d), lambda l, j: (0, 0)),
                  pl.BlockSpec((1, d, tn), lambda l, j: (l, 0, j)),
                  pl.BlockSpec((1, 1, tn), lambda l, j: (l, 0, j))],
        out_specs=pl.BlockSpec((1, rows, tn), lambda l, j: (l, 0, j)),
        out_shape=jax.ShapeDtypeStruct((depth, rows, n), F32),
        compiler_params=pltpu.CompilerParams(dimension_semantics=("parallel", "parallel"),
                                             vmem_limit_bytes=VMEM_LIMIT),
        name="ada_mod",
    )(c_all, w_ada, b_ada.reshape(depth, 1, n))


def _modulated_norm(x, g_row, scale_row, shift_row):
    y = x * lax.rsqrt(jnp.mean(x * x, axis=-1, keepdims=True) + EPS) * g_row
    return y * (1.0 + scale_row) + shift_row


def _inproj_kernel(x_ref, mod_ref, g_ref, w_ref, cw_ref, cb0_ref, k_ref, v_ref, p_ref, cb_out, xw_s,
                   *, col_chunk):
    j = pl.program_id(1)
    tm = x_ref.shape[1]
    conv_lo, conv_hi = PB_GQ * W_GROUP, (PB_GV + 1) * W_GROUP

    @pl.when(j == 0)
    def _():
        xw_s[0:8, :] = cb0_ref[0]

    mod = mod_ref[0]
    h = _modulated_norm(x_ref[0], g_ref[...], mod[1:2], mod[0:1]).astype(BF16)
    k_ref[0] = jnp.dot(h, w_ref[:, 0:W_GROUP], preferred_element_type=F32)
    v_ref[0] = jnp.dot(h, w_ref[:, W_GROUP:2 * W_GROUP], preferred_element_type=F32)
    base = 2 * W_GROUP
    bounds = sorted(set(range(0, conv_lo, col_chunk)) | {conv_lo, conv_hi}
                    | set(range(conv_hi, P_WIDTH, col_chunk)) | {P_WIDTH})
    for n0, n1 in zip(bounds[:-1], bounds[1:]):
        if n0 == conv_lo:
            continue
        p_ref[0, :, n0:n1] = jnp.dot(h, w_ref[:, base + n0:base + n1], preferred_element_type=F32)
    raw = jnp.dot(h, w_ref[:, base + conv_lo:base + conv_hi], preferred_element_type=F32)
    xw_s[8:8 + tm, :] = raw
    cw = cw_ref[...]
    y = raw * cw[CONV_W - 1:CONV_W, :]
    for d in range(1, CONV_W):
        y = y + xw_s[8 - d:8 - d + tm, :] * cw[CONV_W - 1 - d:CONV_W - d, :]
    p_ref[0, :, conv_lo:conv_hi] = _silu(y)
    xw_s[0:8, :] = raw[tm - 8:tm, :]

    @pl.when(j == pl.num_programs(1) - 1)
    def _():
        cb_out[0] = raw[tm - 8:tm, :]


def _inproj_call(x, mod, g, w_perm, conv_w, conv0, tm):
    b, l, d = x.shape
    n_all = w_perm.shape[1]
    conv_dim = conv_w.shape[1]
    conv0_pad = jnp.concatenate([jnp.zeros((b, 8 - (CONV_W - 1), conv_dim), F32), conv0], axis=1)
    k, v, p, conv_new = pl.pallas_call(
        functools.partial(_inproj_kernel, col_chunk=512),
        grid=(b, l // tm),
        in_specs=[pl.BlockSpec((1, tm, d), lambda i, j: (i, j, 0)),
                  pl.BlockSpec((1, 6, d), lambda i, j: (i, 0, 0)),
                  pl.BlockSpec((1, d), lambda i, j: (0, 0)),
                  pl.BlockSpec((d, n_all), lambda i, j: (0, 0), pipeline_mode=pl.Buffered(1)),
                  pl.BlockSpec((CONV_W, conv_dim), lambda i, j: (0, 0)),
                  pl.BlockSpec((1, 8, conv_dim), lambda i, j: (i, 0, 0))],
        out_specs=[pl.BlockSpec((1, tm, W_GROUP), lambda i, j: (i, j, 0)),
                   pl.BlockSpec((1, tm, W_GROUP), lambda i, j: (i, j, 0)),
                   pl.BlockSpec((1, tm, P_WIDTH), lambda i, j: (i, j, 0)),
                   pl.BlockSpec((1, 8, conv_dim), lambda i, j: (i, 0, 0))],
        out_shape=[jax.ShapeDtypeStruct((b, l, W_GROUP), F32),
                   jax.ShapeDtypeStruct((b, l, W_GROUP), F32),
                   jax.ShapeDtypeStruct((b, l, P_WIDTH), F32),
                   jax.ShapeDtypeStruct((b, 8, conv_dim), F32)],
        scratch_shapes=[pltpu.VMEM((tm + 8, conv_dim), F32)],
        compiler_params=pltpu.CompilerParams(dimension_semantics=("parallel", "arbitrary"),
                                             vmem_limit_bytes=VMEM_LIMIT),
        name="norm_inproj",
    )(x, mod, g, w_perm, conv_w, conv0_pad)
    return k, v, p, conv_new[:, 8 - (CONV_W - 1):, :]


SB_DEAD_LOG = -104.0


def _sb_kernel(q_ref, k_ref, v_ref, o_ref, *, tq, tk, q_offset):
    i = pl.program_id(1)
    q0 = q_offset + i * tq
    j_top = (q0 + tq - 1) // tk
    n_masked = max(tq // tk, 1)
    suffix = (lax.broadcasted_iota(jnp.int32, (tk, tk), 0)
              >= lax.broadcasted_iota(jnp.int32, (tk, tk), 1)).astype(BF16)
    q_all = q_ref[0] * (HEAD_V ** -0.5)
    qs = [q_all[:, _hs(h)].astype(BF16) for h in range(N_HEADS)]

    def block(j, accs, runs, causal):
        s0 = pl.multiple_of(j * tk, tk)
        heads = range(N_HEADS)
        kbs = [k_ref[0, pl.ds(s0, tk), _hs(h)].astype(BF16) for h in heads]
        vbs = [v_ref[0, pl.ds(s0, tk), _hs(h)].astype(BF16) for h in heads]
        zs = [lax.dot_general(qs[h], kbs[h], (((1,), (1,)), ((), ())), preferred_element_type=F32)
              for h in heads]
        sps = [_softplus(z) for z in zs]
        if causal is not None:
            sps = [jnp.where(causal, x, 0.0) for x in sps]
        splits = [_split2(x) for x in sps]
        incs = [jnp.dot(hi, suffix, preferred_element_type=F32)
                + jnp.dot(lo, suffix, preferred_element_type=F32) for hi, lo in splits]
        probs = [jnp.exp(zs[h] - incs[h] - runs[h]) for h in heads]
        if causal is not None:
            probs = [jnp.where(causal, a, 0.0) for a in probs]
        new_accs = [accs[h] + jnp.dot(probs[h].astype(BF16), vbs[h], preferred_element_type=F32)
                    for h in heads]
        new_runs = [runs[h] + incs[h][:, 0:1] for h in heads]
        return tuple(new_accs), tuple(new_runs)

    def live_of(runs):
        return -jnp.min(jnp.minimum(jnp.minimum(runs[0], runs[1]), jnp.minimum(runs[2], runs[3])))

    q_pos = q0 + lax.broadcasted_iota(jnp.int32, (tq, tk), 0)
    k_off = lax.broadcasted_iota(jnp.int32, (tq, tk), 1)
    zeros = lambda w: tuple(jnp.zeros((tq, w), F32) for _ in range(N_HEADS))
    accs, runs = zeros(HEAD_V), zeros(1)
    for m in range(n_masked):
        accs, runs = block(j_top - m, accs, runs, (j_top - m) * tk + k_off < q_pos)

    def cond(carry):
        j, live, _, _ = carry
        return (j >= 0) & (live > SB_DEAD_LOG)

    def body(carry):
        j, _, accs, runs = carry
        accs, runs = block(j, accs, runs, None)
        return j - 1, live_of(runs), accs, runs

    _, _, accs, _ = lax.while_loop(cond, body, (j_top - n_masked, live_of(runs), accs, runs))
    for h in range(N_HEADS):
        o_ref[0, :, _hs(h)] = accs[h]


def _sb_call(q_src, q_block, k_all, v_all, l, tq, tk, q_offset):
    assert (tq % tk == 0 or tk % tq == 0) and q_offset % max(tq, tk) == 0 and tk % LANES == 0
    b = q_src.shape[0]
    lk = k_all.shape[1]
    return pl.pallas_call(
        functools.partial(_sb_kernel, tq=tq, tk=tk, q_offset=q_offset),
        grid=(b, l // tq),
        in_specs=[pl.BlockSpec((1, tq, W_GROUP), lambda i, j: (i, j, q_block)),
                  pl.BlockSpec((1, lk, W_GROUP), lambda i, j: (i, 0, 0)),
                  pl.BlockSpec((1, lk, W_GROUP), lambda i, j: (i, 0, 0))],
        out_specs=pl.BlockSpec((1, tq, W_GROUP), lambda i, j: (i, j, 0)),
        out_shape=jax.ShapeDtypeStruct((b, l, W_GROUP), F32),
        compiler_params=pltpu.CompilerParams(dimension_semantics=("parallel", "parallel"),
                                             vmem_limit_bytes=VMEM_LIMIT),
        name="sb_attention",
    )(q_src, k_all, v_all)


def _cummax_rows(x):
    r = x.shape[0]
    row = lax.broadcasted_iota(jnp.int32, x.shape, 0)
    shift = 1
    while shift < r:
        x = jnp.maximum(x, jnp.where(row >= shift, pltpu.roll(x, shift, 0), -jnp.inf))
        shift *= 2
    return x


def _mlstm_kernel(q_ref, k_ref, v_ref, og_ref, g_ref, bias_ref, ng_ref, c0_ref, n0_ref, m0_ref,
                  h_ref, c_out, n_out, m_out, cn_s, m_s, *, c, nb):
    j = pl.program_id(1)

    @pl.when(j == 0)
    def _():
        cn_s[...] = jnp.zeros_like(cn_s)
        m_s[...] = jnp.zeros_like(m_s)
        for i in range(nb):
            for h in range(N_HEADS):
                cn_s[i, _hs(h), _hs(h)] = c0_ref[i, h]
                cn_s[i, _hs(h), W_GROUP + h * HEAD_V:W_GROUP + (h + 1) * HEAD_V] = jnp.broadcast_to(
                    n0_ref[i, h], (HEAD_V, HEAD_V))
            m_s[i, :, 0:N_HEADS] = m0_ref[i]

    seqs = range(nb)
    hs = N_HEADS * c
    lane = lax.broadcasted_iota(jnp.int32, (c, LANES), 1)
    is_f = (lane >= G_MF) & (lane < G_MF + N_HEADS)
    head_lane = lane < N_HEADS
    incl_b = _tri(c).astype(BF16)
    t_idx = lax.broadcasted_iota(jnp.int32, (c, hs), 0)
    s_idx = lax.broadcasted_iota(jnp.int32, (c, hs), 1) & (c - 1)
    incl4 = t_idx >= s_idx
    same_hv = _same_head(hs, W_GROUP, c, HEAD_V)
    same_vv = _same_head(W_GROUP, W_GROUP, HEAD_V, HEAD_V)
    group_sum = same_vv.astype(BF16)

    def expand(lanes, group):
        g = lax.broadcasted_iota(jnp.int32, (LANES, lanes), 0)
        h = lax.broadcasted_iota(jnp.int32, (LANES, lanes), 1) >> (group.bit_length() - 1)
        return (g == h).astype(BF16)

    gi = lax.broadcasted_iota(jnp.int32, (LANES, 2 * LANES), 0)
    li = lax.broadcasted_iota(jnp.int32, (LANES, 2 * LANES), 1)
    to_heads = (jnp.where((li < N_HEADS) & (gi == G_MI + li), 1.0, 0.0)
                - jnp.where((li < N_HEADS) & (gi == G_MF + li), 1.0, 0.0)
                + jnp.where((li >= LANES) & (li < LANES + N_HEADS) & (gi == G_MF + li - LANES), 1.0, 0.0)
                ).astype(BF16)
    row_sel = (lax.broadcasted_iota(jnp.int32, (8, LANES), 1)
               == lax.broadcasted_iota(jnp.int32, (8, LANES), 0)).astype(BF16)
    gates = [g_ref[i] + bias_ref[...] for i in seqs]
    merged = [jnp.where(is_f, _dot01_left(incl_b, jnp.where(is_f, _log_sigmoid(g), 0.0)), g) for g in gates]
    gb = _dot01_right(merged, to_heads)
    g_c = [jnp.where(head_lane, x[:, :LANES], -jnp.inf) for x in gb]
    b_c = [x[:, LANES:] for x in gb]
    m_prev = [m_s[i] for i in seqs]
    r_c = [jnp.maximum(m_prev[i], _cummax_rows(g_c[i])) for i in seqs]
    r_end = [r[c - 1:c, :] for r in r_c]
    si_c = [jnp.exp(m_prev[i] - r_c[i]) for i in seqs]
    emt_c = [jnp.exp(-(b_c[i] + r_c[i])) for i in seqs]
    w_c = [jnp.exp(g_c[i] - r_end[i]) for i in seqs]
    dec_c = [jnp.broadcast_to(jnp.exp(m_prev[i] - r_end[i]), (c, LANES)) for i in seqs]
    zero_pad = lambda x: jnp.where(head_lane, x, 0.0)
    r_hs = _dot01_right([zero_pad(r) for r in r_c], expand(hs, c))
    cols = _dot01_right([zero_pad(x) for i in seqs for x in (si_c[i], emt_c[i], w_c[i], dec_c[i])],
                        expand(W_GROUP, HEAD_V))
    si_hv, emt_hv, w_hv, dec_hv = (cols[n::4] for n in range(4))
    g_rows = [sum(lax.dot_general(row_sel, part, (((1,), (1,)), ((), ())), preferred_element_type=F32)
                  for part in _split3(zero_pad(x))) for x in g_c]
    g_row_hs = [jnp.concatenate([gr[h:h + 1, :] for h in range(N_HEADS)], axis=1) for gr in g_rows]
    d_hs = [jnp.where(incl4, jnp.exp(g_row_hs[i] - r_hs[i]), 0.0) for i in seqs]
    qs = [q_ref[i].astype(BF16) for i in seqs]
    ks = [k_ref[i] * (HEAD_V ** -0.5) for i in seqs]
    vs = [v_ref[i] for i in seqs]
    cns = [cn_s[i] for i in seqs]
    qk = [lax.dot_general(qs[i], _head_blocks(ks[i].astype(BF16), same_hv), (((1,), (1,)), ((), ())),
                          preferred_element_type=F32) for i in seqs]
    q_cn = [jnp.dot(qs[i], cns[i].astype(BF16), preferred_element_type=F32) for i in seqs]
    scs = [(qk[i] * d_hs[i]).astype(BF16) for i in seqs]
    ones_blocks = same_hv.astype(BF16)
    v_ones = [jnp.concatenate([_head_blocks(vs[i].astype(BF16), same_hv), ones_blocks], axis=1)
              for i in seqs]
    sc_v = [jnp.dot(scs[i], v_ones[i], preferred_element_type=F32) for i in seqs]
    nd = [sc_v[i] + jnp.concatenate([si_hv[i], si_hv[i]], axis=1) * q_cn[i] for i in seqs]
    hcs = [nd[i][:, :W_GROUP] / jnp.maximum(jnp.abs(nd[i][:, W_GROUP:]), emt_hv[i]) for i in seqs]
    kws = [ks[i] * w_hv[i] for i in seqs]
    upds = [_dot_tn(kws[i], jnp.concatenate([vs[i], jnp.ones_like(vs[i])], axis=1)) for i in seqs]
    same_cn = jnp.concatenate([same_vv, same_vv], axis=1)
    sum_sq = _group_sums([x * x for x in hcs], group_sum)
    ng = ng_ref[...]
    for i in seqs:
        dec_row = dec_hv[i][0:1, :]
        cn_s[i] = jnp.concatenate([dec_row, dec_row], axis=1) * cns[i] + jnp.where(same_cn, upds[i], 0.0)
        m_s[i] = jnp.where(head_lane[0:1, :], b_c[i][c - 1:c, :] + r_end[i], 0.0)
        h_ref[i] = (hcs[i] * lax.rsqrt(sum_sq[i] * (1.0 / HEAD_V) + EPS) * ng * _sigmoid(og_ref[i]))

    @pl.when(j == pl.num_programs(1) - 1)
    def _():
        for i in seqs:
            for h in range(N_HEADS):
                c_out[i, h] = cn_s[i, _hs(h), _hs(h)]
                n_out[i, h] = cn_s[i, _hs(h), W_GROUP + h * HEAD_V:W_GROUP + h * HEAD_V + 1]
            m_out[i] = m_s[i, :, 0:N_HEADS]


def _mlstm_call(p, bias_row, ng, c0, n0, m0, c, nb):
    b, l, _ = p.shape
    nb = math.gcd(nb, b)
    nc = l // c
    blk = lambda cb: pl.BlockSpec((nb, c, W_GROUP), lambda i, j, cb=cb: (i, j, cb))
    full = lambda a: pl.BlockSpec((nb,) + a.shape[1:], lambda i, j: (i,) + (0,) * (a.ndim - 1))
    n0 = n0[..., None]
    m0 = m0.reshape(b, 1, N_HEADS)
    h, c_new, n_new, m_new = pl.pallas_call(
        functools.partial(_mlstm_kernel, c=c, nb=nb),
        grid=(b // nb, nc),
        in_specs=[blk(PB_MQ), blk(PB_MK), blk(PB_MV), blk(PB_MO),
                  pl.BlockSpec((nb, c, LANES), lambda i, j: (i, j, PB128_GATES)),
                  pl.BlockSpec((1, LANES), lambda i, j: (0, 0)),
                  pl.BlockSpec((1, W_GROUP), lambda i, j: (0, 0)),
                  full(c0), full(n0), full(m0)],
        out_specs=[pl.BlockSpec((nb, c, W_GROUP), lambda i, j: (i, j, 0)), full(c0), full(n0), full(m0)],
        out_shape=[jax.ShapeDtypeStruct((b, l, W_GROUP), F32), jax.ShapeDtypeStruct(c0.shape, F32),
                   jax.ShapeDtypeStruct(n0.shape, F32), jax.ShapeDtypeStruct(m0.shape, F32)],
        scratch_shapes=[pltpu.VMEM((nb, W_GROUP, 2 * W_GROUP), F32), pltpu.VMEM((nb, 1, LANES), F32)],
        compiler_params=pltpu.CompilerParams(dimension_semantics=("parallel", "arbitrary"),
                                             vmem_limit_bytes=VMEM_LIMIT),
        name="mlstm",
    )(p, p, p, p, p, bias_row, ng, c0, n0, m0)
    return h, c_new, n_new[..., 0], m_new[:, 0, :]


def _same_head(rows, lanes, row_group, lane_group):
    rh = lax.broadcasted_iota(jnp.int32, (rows, lanes), 0) >> (row_group.bit_length() - 1)
    lh = lax.broadcasted_iota(jnp.int32, (rows, lanes), 1) >> (lane_group.bit_length() - 1)
    return rh == lh


def _head_blocks(x, same_head):
    tiled = jnp.concatenate([x] * N_HEADS, axis=0)
    return jnp.where(same_head, tiled, jnp.zeros_like(tiled))


def _heads_dot_split(l, r, same_head):
    return _heads_dot_parts(_split2(l), _split2(r), same_head)


def _heads_dot_parts(l_parts, r_parts, same_head):
    l1, l2 = l_parts
    rb1, rb2 = _head_blocks(r_parts[0], same_head), _head_blocks(r_parts[1], same_head)
    m = l1.shape[0]
    y = jnp.dot(jnp.concatenate([l1, l2], axis=0), rb1, preferred_element_type=F32)
    return y[:m] + (y[m:] + jnp.dot(l1, rb2, preferred_element_type=F32))


def _unit_lower_inverses(lows, eye4, same_head, c):
    ts = [eye4 - a for a in lows]
    a_parts = [_split2(a) for a in lows]
    ps = [_heads_dot_parts(ap, ap, same_head) for ap in a_parts]
    for _ in range(c.bit_length() - 3):
        t_parts = [_split2(t) for t in ts]
        p_parts = [_split2(p) for p in ps]
        stacked = [tuple(jnp.concatenate([tp[n], pp[n]], axis=0) for n in range(2))
                   for tp, pp in zip(t_parts, p_parts)]
        ys = [_heads_dot_parts(sp, pp, same_head) for sp, pp in zip(stacked, p_parts)]
        ts = [t + y[:c] for t, y in zip(ts, ys)]
        ps = [y[c:] for y in ys]
    return [t + _heads_dot_split(t, p, same_head) for t, p in zip(ts, ps)]


def _gdn_kernel(q_ref, k_ref, v_ref, z_ref, g_ref, bias_ref, alog_ref, ng_ref, s0_ref,
                o_ref, s_out, s_s, *, c, nb):
    j = pl.program_id(1)
    seqs = range(nb)

    @pl.when(j == 0)
    def _():
        s_s[...] = jnp.zeros_like(s_s)
        for i in seqs:
            for h in range(N_HEADS):
                s_s[i, _hs(h), _hs(h)] = s0_ref[i, h]

    hs = N_HEADS * c
    lane = lax.broadcasted_iota(jnp.int32, (c, LANES), 1)
    is_a = (lane >= G_GA) & (lane < G_GA + N_HEADS)
    incl_b = _tri(c).astype(BF16)
    t_idx = lax.broadcasted_iota(jnp.int32, (c, hs), 0)
    s_idx = lax.broadcasted_iota(jnp.int32, (c, hs), 1) & (c - 1)
    incl4, strict4, eye4 = t_idx >= s_idx, t_idx > s_idx, (t_idx == s_idx).astype(F32)
    same_hs = _same_head(hs, hs, c, c)
    same_hv = _same_head(hs, W_GROUP, c, HEAD_V)
    same_vv = _same_head(W_GROUP, W_GROUP, HEAD_V, HEAD_V)
    group_sum = same_vv.astype(BF16)

    def pick(lanes, group):
        g = lax.broadcasted_iota(jnp.int32, (LANES, 2 * lanes), 0)
        l = lax.broadcasted_iota(jnp.int32, (LANES, 2 * lanes), 1)
        h = (l & (lanes - 1)) >> (group.bit_length() - 1)
        return (g == jnp.where(l < lanes, G_GA, G_GB) + h).astype(BF16)

    pick_hs = pick(hs, c)
    row_sel = (lax.broadcasted_iota(jnp.int32, (8, LANES), 1)
               == G_GA + lax.broadcasted_iota(jnp.int32, (8, LANES), 0)).astype(BF16)
    gates = [g_ref[i] + bias_ref[...] for i in seqs]
    xgs = [jnp.where(is_a, -jnp.exp(alog_ref[...]) * _softplus(g), _sigmoid(g)) for g in gates]
    csums = [jnp.where(is_a, _dot01_left(incl_b, xg), xg) for xg in xgs]
    cols_hs = _dot01_right(csums, pick_hs)
    bc_hs, beta_hs = [x[:, :hs] for x in cols_hs], [x[:, hs:] for x in cols_hs]
    if c == HEAD_V:
        bc_hv, beta_hv = bc_hs, beta_hs
    else:
        cols_hv = _dot01_right(csums, pick(W_GROUP, HEAD_V))
        bc_hv, beta_hv = [x[:, :W_GROUP] for x in cols_hv], [x[:, W_GROUP:] for x in cols_hv]
    b_rows = [sum(lax.dot_general(row_sel, part, (((1,), (1,)), ((), ())), preferred_element_type=F32)
                  for part in _split3(cs)) for cs in csums]
    br_hs = [jnp.concatenate([br[h:h + 1, :] for h in range(N_HEADS)], axis=1) for br in b_rows]
    decays = [jnp.exp(jnp.where(incl4, bc_hs[i] - br_hs[i], 0.0)) for i in seqs]
    qk_raw = [jnp.concatenate([q_ref[i], k_ref[i]], axis=0) for i in seqs]
    norms = _group_sums([x * x for x in qk_raw], group_sum)
    qk_n = [qk_raw[i] * lax.rsqrt(norms[i] + EPS) for i in seqs]
    qs = [x[:c] * (HEAD_V ** -0.5) for x in qk_n]
    ks = [x[c:] for x in qk_n]
    vs = [v_ref[i] for i in seqs]
    kq = [jnp.concatenate([ks[i], qs[i]], axis=0).astype(BF16) for i in seqs]
    states = [s_s[i] for i in seqs]
    scores = [lax.dot_general(kq[i], _head_blocks(ks[i].astype(BF16), same_hv),
                              (((1,), (1,)), ((), ())), preferred_element_type=F32) for i in seqs]
    on_state = [jnp.dot(kq[i], states[i].astype(BF16), preferred_element_type=F32) for i in seqs]
    lows = [jnp.where(strict4, beta_hs[i] * decays[i] * scores[i][:c], 0.0) for i in seqs]
    invs = _unit_lower_inverses(lows, eye4, same_hs, c)
    ebs = [jnp.exp(bc_hv[i]) for i in seqs]
    rhss = [beta_hv[i] * (vs[i] - ebs[i] * on_state[i][:c]) for i in seqs]
    us = [_heads_dot_split(invs[i], rhss[i], same_hv) for i in seqs]
    qkm = [jnp.where(incl4, decays[i] * scores[i][c:], 0.0).astype(BF16) for i in seqs]
    outs = [ebs[i] * on_state[i][c:]
            + jnp.dot(qkm[i], _head_blocks(us[i].astype(BF16), same_hv), preferred_element_type=F32)
            for i in seqs]
    b_ends = [bc_hv[i][c - 1:c, :] for i in seqs]
    upds = [jnp.where(same_vv, _dot_tn(ks[i] * jnp.exp(b_ends[i] - bc_hv[i]), us[i]), 0.0) for i in seqs]
    ng = ng_ref[...]
    sum_sq = _group_sums([o * o for o in outs], group_sum)
    for i in seqs:
        s_s[i] = jnp.exp(b_ends[i]) * states[i] + upds[i]
        o_ref[i] = (outs[i] * lax.rsqrt(sum_sq[i] * (1.0 / HEAD_V) + EPS) * ng * _silu(z_ref[i]))

    @pl.when(j == pl.num_programs(1) - 1)
    def _():
        for i in seqs:
            for h in range(N_HEADS):
                s_out[i, h] = s_s[i, _hs(h), _hs(h)]


def _gdn_call(p, bias_row, alog_row, ng, s0, c, nb):
    b, l, _ = p.shape
    nb = math.gcd(nb, b)
    nc = l // c
    blk = lambda cb: pl.BlockSpec((nb, c, W_GROUP), lambda i, j, cb=cb: (i, j, cb))
    full = lambda a: pl.BlockSpec((nb,) + a.shape[1:], lambda i, j: (i,) + (0,) * (a.ndim - 1))
    return pl.pallas_call(
        functools.partial(_gdn_kernel, c=c, nb=nb),
        grid=(b // nb, nc),
        in_specs=[blk(PB_GQ), blk(PB_GK), blk(PB_GV), blk(PB_GZ),
                  pl.BlockSpec((nb, c, LANES), lambda i, j: (i, j, PB128_GATES)),
                  pl.BlockSpec((1, LANES), lambda i, j: (0, 0)),
                  pl.BlockSpec((1, LANES), lambda i, j: (0, 0)),
                  pl.BlockSpec((1, W_GROUP), lambda i, j: (0, 0)),
                  full(s0)],
        out_specs=[pl.BlockSpec((nb, c, W_GROUP), lambda i, j: (i, j, 0)), full(s0)],
        out_shape=[jax.ShapeDtypeStruct((b, l, W_GROUP), F32), jax.ShapeDtypeStruct(s0.shape, F32)],
        scratch_shapes=[pltpu.VMEM((nb, W_GROUP, W_GROUP), F32)],
        compiler_params=pltpu.CompilerParams(dimension_semantics=("parallel", "arbitrary"),
                                             vmem_limit_bytes=VMEM_LIMIT),
        name="gdn",
    )(p, p, p, p, p, bias_row, alog_row, ng, s0)


GLA_SUB = 8


def _gla_kernel(q_ref, k_ref, v_ref, r_ref, g_ref, w2_ref, gb_ref, ng_ref, s0_ref,
                o_ref, s_out, s_s, *, c, nb):
    j = pl.program_id(1)
    dk_all = N_HEADS * GLA_DK
    hs = N_HEADS * c
    nsb = c // GLA_SUB
    seqs = range(nb)

    @pl.when(j == 0)
    def _():
        s_s[...] = jnp.zeros_like(s_s)
        for i in seqs:
            for h in range(N_HEADS):
                s_s[i, h * GLA_DK:(h + 1) * GLA_DK, _hs(h)] = s0_ref[i, h]

    incl_b = _tri(c).astype(BF16)
    er = lax.broadcasted_iota(jnp.int32, (2 * c, c), 0)
    ec = lax.broadcasted_iota(jnp.int32, (2 * c, c), 1)
    et = er & (c - 1)
    edge_sel = (ec == jnp.where(er < c, (et & ~(GLA_SUB - 1)) - 1, et | (GLA_SUB - 1))).astype(BF16)
    t_idx = lax.broadcasted_iota(jnp.int32, (c, hs), 0)
    s_idx = lax.broadcasted_iota(jnp.int32, (c, hs), 1) & (c - 1)
    blk_dist = (t_idx >> 3) - (s_idx >> 3)
    diag_off = t_idx - s_idx
    same_hk = _same_head(hs, dk_all, c, GLA_DK)
    same_hv = _same_head(hs, W_GROUP, c, HEAD_V)
    same_kv = _same_head(dk_all, W_GROUP, GLA_DK, HEAD_V)
    head_rep = _same_head(dk_all, hs, GLA_DK, c).astype(BF16)
    group_sum = _same_head(W_GROUP, W_GROUP, HEAD_V, HEAD_V).astype(BF16)
    row8 = lax.broadcasted_iota(jnp.int32, (c, dk_all), 0) & (GLA_SUB - 1)
    last_row = lax.broadcasted_iota(jnp.int32, (c, dk_all), 0) == c - 1

    log_as = [_log_sigmoid(_dot(g_ref[i], w2_ref[...]) + gb_ref[...]) / GLA_TAU for i in seqs]
    bs = [_dot01_left(incl_b, la) for la in log_as]
    edges = [_dot01_left(edge_sel, b) for b in bs]
    e_prev = [e[:c] for e in edges]
    e_own = [e[c:] for e in edges]
    qs = [q_ref[i] * (GLA_DK ** -0.5) for i in seqs]
    ks = [k_ref[i] for i in seqs]
    vs = [v_ref[i] for i in seqs]
    q_hat = [qs[i] * jnp.exp(bs[i] - e_prev[i]) for i in seqs]
    k_hat = [ks[i] * jnp.exp(e_own[i] - bs[i]) for i in seqs]
    def q_for_distance(i, m):
        if m == 0:
            return q_hat[i]
        rows = GLA_SUB * m
        shifted = jnp.concatenate([jnp.zeros((rows, dk_all), F32), e_prev[i][:c - rows]], axis=0)
        return q_hat[i] * jnp.exp(e_prev[i] - shifted)
    a_parts = []
    for i in seqs:
        lhs = jnp.concatenate([q_for_distance(i, m) for m in range(nsb - 1)], axis=0).astype(BF16)
        prod = lax.dot_general(lhs, _head_blocks(k_hat[i].astype(BF16), same_hk),
                               (((1,), (1,)), ((), ())), preferred_element_type=F32)
        a_parts.append(sum(jnp.where(blk_dist == m + 1, prod[m * c:(m + 1) * c], 0.0)
                           for m in range(nsb - 1)))
    def rot8(x, d):
        return x if d == 0 else pltpu.roll(x, d, 0)
    for i in seqs:
        ws = [jnp.where(row8 >= d, qs[i] * rot8(ks[i], d) * jnp.exp(bs[i] - rot8(bs[i], d)), 0.0)
              for d in range(GLA_SUB)]
        prod = jnp.dot(jnp.concatenate(ws, axis=0).astype(BF16), head_rep, preferred_element_type=F32)
        a_parts[i] = a_parts[i] + sum(jnp.where(diag_off == d, prod[d * c:(d + 1) * c], 0.0)
                                      for d in range(GLA_SUB))
    states = [s_s[i] for i in seqs]
    outs = [jnp.dot(a_parts[i].astype(BF16), _head_blocks(vs[i].astype(BF16), same_hv),
                    preferred_element_type=F32)
            + _dot(qs[i] * jnp.exp(bs[i]), states[i]) for i in seqs]
    b_last = [b[c - 1:c, :] for b in bs]
    upds = [jnp.where(same_kv, _dot_tn(ks[i] * jnp.exp(b_last[i] - bs[i]), vs[i]), 0.0) for i in seqs]
    ones_cv = jnp.ones((c, W_GROUP), BF16)
    decay_rows = [sum(lax.dot_general(part, ones_cv, (((0,), (0,)), ((), ())), preferred_element_type=F32)
                      for part in _split3(jnp.where(last_row, b, 0.0))) for b in bs]
    sum_sq = _group_sums([o * o for o in outs], group_sum)
    ng = ng_ref[...]
    for i in seqs:
        s_s[i] = states[i] * jnp.exp(decay_rows[i]) + upds[i]
        o_ref[i] = outs[i] * lax.rsqrt(sum_sq[i] * (1.0 / HEAD_V) + EPS) * ng * _silu(r_ref[i])

    @pl.when(j == pl.num_programs(1) - 1)
    def _():
        for i in seqs:
            for h in range(N_HEADS):
                s_out[i, h] = s_s[i, h * GLA_DK:(h + 1) * GLA_DK, _hs(h)]


def _gla_call(p, w2_pad, gb_row, ng, s0, c, nb):
    b, l, _ = p.shape
    nb = math.gcd(nb, b)
    nc = l // c
    dk_all = N_HEADS * GLA_DK
    blk = lambda cb: pl.BlockSpec((nb, c, W_GROUP), lambda i, j, cb=cb: (i, j, cb))
    blk128 = lambda cb: pl.BlockSpec((nb, c, LANES), lambda i, j, cb=cb: (i, j, cb))
    state = pl.BlockSpec((nb,) + s0.shape[1:], lambda i, j: (i, 0, 0, 0))
    return pl.pallas_call(
        functools.partial(_gla_kernel, c=c, nb=nb),
        grid=(b // nb, nc),
        in_specs=[blk128(PB128_LQ), blk128(PB128_LK), blk(PB_LV), blk(PB_LR), blk128(PB128_GATES),
                  pl.BlockSpec((LANES, dk_all), lambda i, j: (0, 0)),
                  pl.BlockSpec((1, dk_all), lambda i, j: (0, 0)),
                  pl.BlockSpec((1, W_GROUP), lambda i, j: (0, 0)),
                  state],
        out_specs=[pl.BlockSpec((nb, c, W_GROUP), lambda i, j: (i, j, 0)), state],
        out_shape=[jax.ShapeDtypeStruct((b, l, W_GROUP), F32), jax.ShapeDtypeStruct(s0.shape, F32)],
        scratch_shapes=[pltpu.VMEM((nb, dk_all, W_GROUP), F32)],
        compiler_params=pltpu.CompilerParams(dimension_semantics=("parallel", "arbitrary"),
                                             vmem_limit_bytes=VMEM_LIMIT),
        name="gla",
    )(p, p, p, p, p, w2_pad, gb_row, ng, s0)


def _outffn_kernel(x_ref, a_ref, b_ref, c_ref, d_ref, mod_ref, g2_ref, gf_ref, wo_ref, w1_ref, w2_ref,
                   o_ref, *, ff_chunk, final):
    mod = mod_ref[0]
    mixed = jnp.concatenate([a_ref[0], b_ref[0], c_ref[0], d_ref[0]], axis=-1).astype(BF16)
    x = x_ref[0] + mod[2:3] * jnp.dot(mixed, wo_ref[...], preferred_element_type=F32)
    h = _modulated_norm(x, g2_ref[...], mod[4:5], mod[3:4]).astype(BF16)
    d_ff = w1_ref.shape[1]
    acc = jnp.zeros(x.shape, F32)
    for f0 in range(0, d_ff, ff_chunk):
        a = jnp.maximum(jnp.dot(h, w1_ref[:, f0:f0 + ff_chunk], preferred_element_type=F32), 0.0)
        acc = acc + jnp.dot((a * a).astype(BF16), w2_ref[f0:f0 + ff_chunk, :],
                            preferred_element_type=F32)
    x = x + mod[5:6] * acc
    if final:
        x = x * lax.rsqrt(jnp.mean(x * x, axis=-1, keepdims=True) + EPS) * gf_ref[...]
    o_ref[0] = x


def _outffn_call(x, mixers, mod, g2, gf, wo, w1, w2, tm, final):
    b, l, d = x.shape
    d_ff = w1.shape[1]
    tok = lambda w: pl.BlockSpec((1, tm, w), lambda i, j: (i, j, 0))
    const = lambda shape: pl.BlockSpec(shape, lambda i, j: (0,) * len(shape),
                                       pipeline_mode=pl.Buffered(1))
    return pl.pallas_call(
        functools.partial(_outffn_kernel, ff_chunk=1024, final=final),
        grid=(b, l // tm),
        in_specs=[tok(d), tok(W_GROUP), tok(W_GROUP), tok(W_GROUP), tok(W_GROUP),
                  pl.BlockSpec((1, 6, d), lambda i, j: (i, 0, 0)),
                  pl.BlockSpec((1, d), lambda i, j: (0, 0)),
                  pl.BlockSpec((1, d), lambda i, j: (0, 0)),
                  const((d, d)), const((d, d_ff)), const((d_ff, d))],
        out_specs=tok(d),
        out_shape=jax.ShapeDtypeStruct((b, l, d), F32),
        compiler_params=pltpu.CompilerParams(dimension_semantics=("parallel", "parallel"),
                                             vmem_limit_bytes=VMEM_LIMIT),
        name="outproj_ffn",
    )(x, *mixers, mod, g2, gf, wo, w1, w2)


def _permute_w_in(w_in):
    d = w_in.shape[0]
    sizes = (256, 256, 256, 256, 256, 256, 4, 4, 256, 768, 4, 4, 256, 128, 128, 256, 16, 256)
    offs = [0]
    for s in sizes:
        offs.append(offs[-1] + s)
    (sq, sk, sv, mq, mk, mv, mi, mf, mo, gqkv, gb, ga, gz, lq, lk, lv, lg, lr) = [
        w_in[:, offs[i]:offs[i + 1]] for i in range(len(sizes))]
    pad = jnp.zeros((d, LANES - (4 * N_HEADS + GLA_RANK)), w_in.dtype)
    return jnp.concatenate([sk, sv, sq, mq, mk, mv, mo, gqkv, gz, lv, lr, lq, lk,
                            mi, mf, gb, ga, lg, pad], axis=1)


def _row128(pieces):
    row = jnp.zeros((LANES,), F32)
    for off, vec in pieces:
        row = row.at[off:off + vec.shape[0]].set(vec)
    return row.reshape(1, LANES)


class _Tiles(NamedTuple):
    tm: int
    sb_tq: int
    sb_tk: int
    mlstm_chunk: int
    chunk: int


def _tiles(l):
    return _Tiles(tm=min(512, l), sb_tq=min(256, l), sb_tk=max(min(256, l), LANES),
                  mlstm_chunk=min(64, l), chunk=min(64, l))


def _layer(x, mod, lw, states, tiles, final, final_g):
    tm, sb_tq, sb_tk, mlstm_chunk, chunk = tiles
    (n1, n2, w_in_p, gate_bias, ml_ng, conv_w, alog_row, gdn_ng, w2_pad, gla_gb, gla_ng,
     w_out, w_ff1, w_ff2) = lw
    (sb_k_past, sb_v_past, ml_c, ml_n, ml_m, gdn_s, gdn_buf, gla_s) = states
    b, l, _ = x.shape
    k_new, v_new, p, gdn_buf = _inproj_call(x, mod, n1, w_in_p, conv_w, gdn_buf, tm)
    if sb_k_past is None:
        k_all, v_all, q_offset = k_new, v_new, 0
    else:
        past = sb_k_past.shape[1]
        lk = -(-(past + l) // sb_tk) * sb_tk
        padz = jnp.zeros((b, lk - past - l, W_GROUP), F32)
        k_all = jnp.concatenate([sb_k_past.reshape(b, past, W_GROUP), k_new, padz], axis=1)
        v_all = jnp.concatenate([sb_v_past.reshape(b, past, W_GROUP), v_new, padz], axis=1)
        q_offset = past
    o_sb = _sb_call(p, PB_SQ, k_all, v_all, l, sb_tq, sb_tk, q_offset)
    o_ml, ml_c, ml_n, ml_m = _mlstm_call(p, gate_bias, ml_ng, ml_c, ml_n, ml_m, mlstm_chunk,
                                         MLSTM_SEQS_PER_STEP)
    o_gdn, gdn_s = _gdn_call(p, gate_bias, alog_row, gdn_ng, gdn_s, chunk, GDN_SEQS_PER_STEP)
    o_gla, gla_s = _gla_call(p, w2_pad, gla_gb, gla_ng, gla_s, chunk, GLA_SEQS_PER_STEP)
    x = _outffn_call(x, (o_sb, o_ml, o_gdn, o_gla), mod, n2, final_g, w_out, w_ff1, w_ff2, tm, final)
    hk = lambda a: a.reshape(b, l, N_HEADS, HEAD_V)
    return x, (hk(k_new), hk(v_new), ml_c, ml_n, ml_m, gdn_s, gdn_buf, gla_s)


def kernel(x_prompt, x_sample, cache_sb_k, cache_sb_v, state_mlstm_C, state_mlstm_n, state_mlstm_m, state_gdn_S, state_gdn_conv, state_gla_S, c_prompt, c_sample, norm1_g, norm2_g, w_ada, b_ada, w_in, mlstm_i_bias, mlstm_f_bias, mlstm_norm_g, gdn_conv_w, gdn_a_log, gdn_dt_bias, gdn_norm_g, gla_w_gate2, gla_gate_bias, gla_norm_g, w_out, w_ff1, w_ff2, final_g):
    depth = w_in.shape[0]
    bp, lp, d = x_prompt.shape
    bs, ls, _ = x_sample.shape
    dk_all = N_HEADS * GLA_DK
    mod_all = _ada_call(jnp.concatenate([c_prompt, c_sample], axis=0), w_ada, b_ada)
    mod_all = mod_all.reshape(depth, bp + bs, 6, d)
    final_row = final_g.reshape(1, d)
    xp, xs = x_prompt, x_sample
    p_list, s_list = [], []
    for l in range(depth):
        gate_bias = _row128([(G_MI, mlstm_i_bias[l]), (G_MF, mlstm_f_bias[l]), (G_GA, gdn_dt_bias[l])])
        alog_row = _row128([(G_GA, gdn_a_log[l])])
        w2_pad = jnp.zeros((LANES, dk_all), F32).at[G_LG:G_LG + GLA_RANK, :].set(gla_w_gate2[l])
        lw = (norm1_g[l].reshape(1, d), norm2_g[l].reshape(1, d), _permute_w_in(w_in[l]).astype(BF16),
              gate_bias, mlstm_norm_g[l].reshape(1, W_GROUP), gdn_conv_w[l], alog_row,
              gdn_norm_g[l].reshape(1, W_GROUP), w2_pad.astype(BF16),
              gla_gate_bias[l].reshape(1, dk_all), gla_norm_g[l].reshape(1, W_GROUP),
              w_out[l].astype(BF16), w_ff1[l].astype(BF16), w_ff2[l].astype(BF16))
        fresh = (None, None, jnp.zeros((bp,) + state_mlstm_C.shape[2:], F32),
                 jnp.zeros((bp,) + state_mlstm_n.shape[2:], F32),
                 jnp.zeros((bp,) + state_mlstm_m.shape[2:], F32),
                 jnp.zeros((bp,) + state_gdn_S.shape[2:], F32),
                 jnp.zeros((bp,) + state_gdn_conv.shape[2:], F32),
                 jnp.zeros((bp,) + state_gla_S.shape[2:], F32))
        past = (cache_sb_k[l], cache_sb_v[l], state_mlstm_C[l], state_mlstm_n[l], state_mlstm_m[l],
                state_gdn_S[l], state_gdn_conv[l], state_gla_S[l])
        final = l == depth - 1
        xp, st_p = _layer(xp, mod_all[l, :bp], lw, fresh, _tiles(lp), final, final_row)
        xs, st_s = _layer(xs, mod_all[l, bp:], lw, past, _tiles(ls), final, final_row)
        p_list.append(st_p)
        s_list.append(st_s)
    stacked_p = [jnp.stack([st[i] for st in p_list]) for i in range(8)]
    stacked_s = [jnp.stack([st[i] for st in s_list]) for i in range(8)]
    return (xp, xs, *stacked_p, *stacked_s)
```

```python
import functools
import math
from typing import NamedTuple

import jax
import jax.numpy as jnp
from jax import lax
from jax.experimental import pallas as pl
from jax.experimental.pallas import tpu as pltpu

F32 = jnp.float32
BF16 = jnp.bfloat16
EPS = 1e-6
N_HEADS = 4
HEAD_V = 64
GLA_DK = 32
CONV_W = 4
GLA_TAU = 16.0
W_GROUP = N_HEADS * HEAD_V
LANES = 128
VMEM_LIMIT = 56 * 1024 * 1024
MIXER_SEQS_PER_STEP = 8

PB_SQ, PB_MQ, PB_MK, PB_MV, PB_MO, PB_GQ, PB_GK, PB_GV, PB_GZ, PB_LV, PB_LR = range(11)
PB128_LQ, PB128_LK, PB128_GATES = 22, 23, 24
P_WIDTH = 25 * LANES
G_MI, G_MF, G_GB, G_GA, G_LG = 0, 4, 8, 12, 16
GLA_RANK = 16


def _dot(a, b):
    return jnp.dot(a.astype(BF16), b.astype(BF16), preferred_element_type=F32)


def _dot_tn(a, b):
    return lax.dot_general(a.astype(BF16), b.astype(BF16), (((0,), (0,)), ((), ())),
                           preferred_element_type=F32)


def _split3(x):
    x1 = x.astype(BF16)
    r1 = x - x1.astype(F32)
    x2 = r1.astype(BF16)
    x3 = (r1 - x2.astype(F32)).astype(BF16)
    return x1, x2, x3


def _split2(x):
    x1 = x.astype(BF16)
    x2 = (x - x1.astype(F32)).astype(BF16)
    return x1, x2


def _dot01_left(m01, x):
    return sum(jnp.dot(m01, p, preferred_element_type=F32) for p in _split3(x))


def _dot01_right(xs, m01):
    n = xs[0].shape[0]
    stacked = jnp.concatenate([part for x in xs for part in _split3(x)], axis=0)
    y = jnp.dot(stacked, m01, preferred_element_type=F32)
    return [y[3 * i * n:(3 * i + 1) * n] + (y[(3 * i + 1) * n:(3 * i + 2) * n]
                                             + y[(3 * i + 2) * n:(3 * i + 3) * n])
            for i in range(len(xs))]


def _group_sums(xs, group_sum):
    n = xs[0].shape[0]
    y = jnp.dot(jnp.concatenate([x.astype(BF16) for x in xs], axis=0), group_sum,
                preferred_element_type=F32)
    return [y[i * n:(i + 1) * n] for i in range(len(xs))]


def _log_sigmoid(x):
    return jnp.minimum(x, 0.0) - jnp.log(1.0 + jnp.exp(-jnp.abs(x)))


def _sigmoid(x):
    return 1.0 / (1.0 + jnp.exp(-x))


def _silu(x):
    return x * _sigmoid(x)


def _softplus(x):
    return jnp.maximum(x, 0.0) + jnp.log(1.0 + jnp.exp(-jnp.abs(x)))


def _tri(c):
    r = lax.broadcasted_iota(jnp.int32, (c, c), 0)
    s = lax.broadcasted_iota(jnp.int32, (c, c), 1)
    return r >= s


def _hs(h, w=HEAD_V):
    return slice(h * w, (h + 1) * w)


def _ada_kernel(c_ref, w_ref, b_ref, o_ref):
    o_ref[0] = _dot(_silu(c_ref[...]), w_ref[0]) + b_ref[0]


def _ada_call(c_all, w_ada, b_ada, tn=512):
    depth, d, n = w_ada.shape
    rows = c_all.shape[0]
    return pl.pallas_call(
        _ada_kernel,
        grid=(depth, n // tn),
        in_specs=[pl.BlockSpec((rows, d), lambda l, j: (0, 0)),
                  pl.BlockSpec((1, d, tn), lambda l, j: (l, 0, j)),
                  pl.BlockSpec((1, 1, tn), lambda l, j: (l, 0, j))],
        out_specs=pl.BlockSpec((1, rows, tn), lambda l, j: (l, 0, j)),
        out_shape=jax.ShapeDtypeStruct((depth, rows, n), F32),
        compiler_params=pltpu.CompilerParams(dimension_semantics=("parallel", "parallel"),
                                             vmem_limit_bytes=VMEM_LIMIT),
        name="ada_mod",
    )(c_all, w_ada, b_ada.reshape(depth, 1, n))


def _modulated_norm(x, g_row, scale_row, shift_row):
    y = x * lax.rsqrt(jnp.mean(x * x, axis=-1, keepdims=True) + EPS) * g_row
    return y * (1.0 + scale_row) + shift_row


def _inproj_kernel(x_ref, mod_ref, g_ref, w_ref, cw_ref, cb0_ref, k_ref, v_ref, p_ref, cb_out, xw_s,
                   *, col_chunk):
    j = pl.program_id(1)
    tm = x_ref.shape[1]
    conv_lo, conv_hi = PB_GQ * W_GROUP, (PB_GV + 1) * W_GROUP

    @pl.when(j == 0)
    def _():
        xw_s[0:8, :] = cb0_ref[0]

    mod = mod_ref[0]
    h = _modulated_norm(x_ref[0], g_ref[...], mod[1:2], mod[0:1]).astype(BF16)
    k_ref[0] = jnp.dot(h, w_ref[:, 0:W_GROUP], preferred_element_type=F32)
    v_ref[0] = jnp.dot(h, w_ref[:, W_GROUP:2 * W_GROUP], preferred_element_type=F32)
    base = 2 * W_GROUP
    bounds = sorted(set(range(0, conv_lo, col_chunk)) | {conv_lo, conv_hi}
                    | set(range(conv_hi, P_WIDTH, col_chunk)) | {P_WIDTH})
    for n0, n1 in zip(bounds[:-1], bounds[1:]):
        if n0 == conv_lo:
            continue
        p_ref[0, :, n0:n1] = jnp.dot(h, w_ref[:, base + n0:base + n1], preferred_element_type=F32)
    raw = jnp.dot(h, w_ref[:, base + conv_lo:base + conv_hi], preferred_element_type=F32)
    xw_s[8:8 + tm, :] = raw
    cw = cw_ref[...]
    y = raw * cw[CONV_W - 1:CONV_W, :]
    for d in range(1, CONV_W):
        y = y + xw_s[8 - d:8 - d + tm, :] * cw[CONV_W - 1 - d:CONV_W - d, :]
    p_ref[0, :, conv_lo:conv_hi] = _silu(y)
    xw_s[0:8, :] = raw[tm - 8:tm, :]

    @pl.when(j == pl.num_programs(1) - 1)
    def _():
        cb_out[0] = raw[tm - 8:tm, :]


def _inproj_call(x, mod, g, w_perm, conv_w, conv0, tm):
    b, l, d = x.shape
    n_all = w_perm.shape[1]
    conv_dim = conv_w.shape[1]
    conv0_pad = jnp.concatenate([jnp.zeros((b, 8 - (CONV_W - 1), conv_dim), F32), conv0], axis=1)
    k, v, p, conv_new = pl.pallas_call(
        functools.partial(_inproj_kernel, col_chunk=512),
        grid=(b, l // tm),
        in_specs=[pl.BlockSpec((1, tm, d), lambda i, j: (i, j, 0)),
                  pl.BlockSpec((1, 6, d), lambda i, j: (i, 0, 0)),
                  pl.BlockSpec((1, d), lambda i, j: (0, 0)),
                  pl.BlockSpec((d, n_all), lambda i, j: (0, 0), pipeline_mode=pl.Buffered(1)),
                  pl.BlockSpec((CONV_W, conv_dim), lambda i, j: (0, 0)),
                  pl.BlockSpec((1, 8, conv_dim), lambda i, j: (i, 0, 0))],
        out_specs=[pl.BlockSpec((1, tm, W_GROUP), lambda i, j: (i, j, 0)),
                   pl.BlockSpec((1, tm, W_GROUP), lambda i, j: (i, j, 0)),
                   pl.BlockSpec((1, tm, P_WIDTH), lambda i, j: (i, j, 0)),
                   pl.BlockSpec((1, 8, conv_dim), lambda i, j: (i, 0, 0))],
        out_shape=[jax.ShapeDtypeStruct((b, l, W_GROUP), F32),
                   jax.ShapeDtypeStruct((b, l, W_GROUP), F32),
                   jax.ShapeDtypeStruct((b, l, P_WIDTH), F32),
                   jax.ShapeDtypeStruct((b, 8, conv_dim), F32)],
        scratch_shapes=[pltpu.VMEM((tm + 8, conv_dim), F32)],
        compiler_params=pltpu.CompilerParams(dimension_semantics=("parallel", "arbitrary"),
                                             vmem_limit_bytes=VMEM_LIMIT),
        name="norm_inproj",
    )(x, mod, g, w_perm, conv_w, conv0_pad)
    return k, v, p, conv_new[:, 8 - (CONV_W - 1):, :]


SB_DEAD_LOG = -104.0


def _sb_kernel(q_ref, k_ref, v_ref, o_ref, *, tq, tk, q_offset):
    i = pl.program_id(1)
    q0 = q_offset + i * tq
    j_top = (q0 + tq - 1) // tk
    n_masked = max(tq // tk, 1)
    suffix = (lax.broadcasted_iota(jnp.int32, (tk, tk), 0)
              >= lax.broadcasted_iota(jnp.int32, (tk, tk), 1)).astype(BF16)
    q_all = q_ref[0] * (HEAD_V ** -0.5)
    qs = [q_all[:, _hs(h)].astype(BF16) for h in range(N_HEADS)]

    def block(j, accs, runs, causal):
        s0 = pl.multiple_of(j * tk, tk)
        heads = range(N_HEADS)
        kbs = [k_ref[0, pl.ds(s0, tk), _hs(h)].astype(BF16) for h in heads]
        vbs = [v_ref[0, pl.ds(s0, tk), _hs(h)].astype(BF16) for h in heads]
        zs = [lax.dot_general(qs[h], kbs[h], (((1,), (1,)), ((), ())), preferred_element_type=F32)
              for h in heads]
        sps = [_softplus(z) for z in zs]
        if causal is not None:
            sps = [jnp.where(causal, x, 0.0) for x in sps]
        splits = [_split2(x) for x in sps]
        incs = [jnp.dot(hi, suffix, preferred_element_type=F32)
                + jnp.dot(lo, suffix, preferred_element_type=F32) for hi, lo in splits]
        probs = [jnp.exp(zs[h] - incs[h] - runs[h]) for h in heads]
        if causal is not None:
            probs = [jnp.where(causal, a, 0.0) for a in probs]
        new_accs = [accs[h] + jnp.dot(probs[h].astype(BF16), vbs[h], preferred_element_type=F32)
                    for h in heads]
        new_runs = [runs[h] + incs[h][:, 0:1] for h in heads]
        return tuple(new_accs), tuple(new_runs)

    def live_of(runs):
        return -jnp.min(jnp.minimum(jnp.minimum(runs[0], runs[1]), jnp.minimum(runs[2], runs[3])))

    q_pos = q0 + lax.broadcasted_iota(jnp.int32, (tq, tk), 0)
    k_off = lax.broadcasted_iota(jnp.int32, (tq, tk), 1)
    zeros = lambda w: tuple(jnp.zeros((tq, w), F32) for _ in range(N_HEADS))
    accs, runs = zeros(HEAD_V), zeros(1)
    for m in range(n_masked):
        accs, runs = block(j_top - m, accs, runs, (j_top - m) * tk + k_off < q_pos)

    def cond(carry):
        j, live, _, _ = carry
        return (j >= 0) & (live > SB_DEAD_LOG)

    def body(carry):
        j, _, accs, runs = carry
        accs, runs = block(j, accs, runs, None)
        return j - 1, live_of(runs), accs, runs

    _, _, accs, _ = lax.while_loop(cond, body, (j_top - n_masked, live_of(runs), accs, runs))
    for h in range(N_HEADS):
        o_ref[0, :, _hs(h)] = accs[h]


def _sb_call(q_src, q_block, k_all, v_all, l, tq, tk, q_offset):
    assert (tq % tk == 0 or tk % tq == 0) and q_offset % max(tq, tk) == 0 and tk % LANES == 0
    b = q_src.shape[0]
    lk = k_all.shape[1]
    return pl.pallas_call(
        functools.partial(_sb_kernel, tq=tq, tk=tk, q_offset=q_offset),
        grid=(b, l // tq),
        in_specs=[pl.BlockSpec((1, tq, W_GROUP), lambda i, j: (i, j, q_block)),
                  pl.BlockSpec((1, lk, W_GROUP), lambda i, j: (i, 0, 0)),
                  pl.BlockSpec((1, lk, W_GROUP), lambda i, j: (i, 0, 0))],
        out_specs=pl.BlockSpec((1, tq, W_GROUP), lambda i, j: (i, j, 0)),
        out_shape=jax.ShapeDtypeStruct((b, l, W_GROUP), F32),
        compiler_params=pltpu.CompilerParams(dimension_semantics=("parallel", "parallel"),
                                             vmem_limit_bytes=VMEM_LIMIT),
        name="sb_attention",
    )(q_src, k_all, v_all)


def _cummax_rows(x):
    r = x.shape[0]
    row = lax.broadcasted_iota(jnp.int32, x.shape, 0)
    shift = 1
    while shift < r:
        x = jnp.maximum(x, jnp.where(row >= shift, pltpu.roll(x, shift, 0), -jnp.inf))
        shift *= 2
    return x


def _mlstm_steps(q_ref, k_ref, v_ref, og_ref, g_ref, bias_ref, ng_ref, c0_ref, n0_ref, m0_ref,
                 h_ref, c_out, n_out, m_out, cn_s, m_s, *, c, nb):
    j = pl.program_id(1)

    @pl.when(j == 0)
    def _():
        cn_s[...] = jnp.zeros_like(cn_s)
        m_s[...] = jnp.zeros_like(m_s)
        for i in range(nb):
            for h in range(N_HEADS):
                cn_s[i, _hs(h), _hs(h)] = c0_ref[i, h]
                cn_s[i, _hs(h), W_GROUP + h * HEAD_V:W_GROUP + (h + 1) * HEAD_V] = jnp.broadcast_to(
                    n0_ref[i, h], (HEAD_V, HEAD_V))
            m_s[i, :, 0:N_HEADS] = m0_ref[i]

    yield 'stage'

    seqs = range(nb)
    hs = N_HEADS * c
    lane = lax.broadcasted_iota(jnp.int32, (c, LANES), 1)
    is_f = (lane >= G_MF) & (lane < G_MF + N_HEADS)
    head_lane = lane < N_HEADS
    incl_b = _tri(c).astype(BF16)
    t_idx = lax.broadcasted_iota(jnp.int32, (c, hs), 0)
    s_idx = lax.broadcasted_iota(jnp.int32, (c, hs), 1) & (c - 1)
    incl4 = t_idx >= s_idx
    same_hv = _same_head(hs, W_GROUP, c, HEAD_V)
    same_vv = _same_head(W_GROUP, W_GROUP, HEAD_V, HEAD_V)
    group_sum = same_vv.astype(BF16)

    def expand(lanes, group):
        g = lax.broadcasted_iota(jnp.int32, (LANES, lanes), 0)
        h = lax.broadcasted_iota(jnp.int32, (LANES, lanes), 1) >> (group.bit_length() - 1)
        return (g == h).astype(BF16)

    gi = lax.broadcasted_iota(jnp.int32, (LANES, 2 * LANES), 0)
    li = lax.broadcasted_iota(jnp.int32, (LANES, 2 * LANES), 1)
    to_heads = (jnp.where((li < N_HEADS) & (gi == G_MI + li), 1.0, 0.0)
                - jnp.where((li < N_HEADS) & (gi == G_MF + li), 1.0, 0.0)
                + jnp.where((li >= LANES) & (li < LANES + N_HEADS) & (gi == G_MF + li - LANES), 1.0, 0.0)
                ).astype(BF16)
    row_sel = (lax.broadcasted_iota(jnp.int32, (8, LANES), 1)
               == lax.broadcasted_iota(jnp.int32, (8, LANES), 0)).astype(BF16)
    gates = [g_ref[i] + bias_ref[...] for i in seqs]
    merged = [jnp.where(is_f, _dot01_left(incl_b, jnp.where(is_f, _log_sigmoid(g), 0.0)), g) for g in gates]
    gb = _dot01_right(merged, to_heads)
    yield 'stage'
    g_c = [jnp.where(head_lane, x[:, :LANES], -jnp.inf) for x in gb]
    b_c = [x[:, LANES:] for x in gb]
    m_prev = [m_s[i] for i in seqs]
    r_c = [jnp.maximum(m_prev[i], _cummax_rows(g_c[i])) for i in seqs]
    r_end = [r[c - 1:c, :] for r in r_c]
    si_c = [jnp.exp(m_prev[i] - r_c[i]) for i in seqs]
    emt_c = [jnp.exp(-(b_c[i] + r_c[i])) for i in seqs]
    w_c = [jnp.exp(g_c[i] - r_end[i]) for i in seqs]
    dec_c = [jnp.broadcast_to(jnp.exp(m_prev[i] - r_end[i]), (c, LANES)) for i in seqs]
    yield 'stage'
    zero_pad = lambda x: jnp.where(head_lane, x, 0.0)
    r_hs = _dot01_right([zero_pad(r) for r in r_c], expand(hs, c))
    cols = _dot01_right([zero_pad(x) for i in seqs for x in (si_c[i], emt_c[i], w_c[i], dec_c[i])],
                        expand(W_GROUP, HEAD_V))
    si_hv, emt_hv, w_hv, dec_hv = (cols[n::4] for n in range(4))
    yield 'stage'
    g_rows = [sum(lax.dot_general(row_sel, part, (((1,), (1,)), ((), ())), preferred_element_type=F32)
                  for part in _split3(zero_pad(x))) for x in g_c]
    g_row_hs = [jnp.concatenate([gr[h:h + 1, :] for h in range(N_HEADS)], axis=1) for gr in g_rows]
    d_hs = [jnp.where(incl4, jnp.exp(g_row_hs[i] - r_hs[i]), 0.0) for i in seqs]
    yield 'stage'
    qs = [q_ref[i].astype(BF16) for i in seqs]
    ks = [k_ref[i] * (HEAD_V ** -0.5) for i in seqs]
    vs = [v_ref[i] for i in seqs]
    cns = [cn_s[i] for i in seqs]
    qk = [lax.dot_general(qs[i], _head_blocks(ks[i].astype(BF16), same_hv), (((1,), (1,)), ((), ())),
                          preferred_element_type=F32) for i in seqs]
    q_cn = [jnp.dot(qs[i], cns[i].astype(BF16), preferred_element_type=F32) for i in seqs]
    yield 'stage'
    scs = [(qk[i] * d_hs[i]).astype(BF16) for i in seqs]
    ones_blocks = same_hv.astype(BF16)
    v_ones = [jnp.concatenate([_head_blocks(vs[i].astype(BF16), same_hv), ones_blocks], axis=1)
              for i in seqs]
    sc_v = [jnp.dot(scs[i], v_ones[i], preferred_element_type=F32) for i in seqs]
    yield 'stage'
    nd = [sc_v[i] + jnp.concatenate([si_hv[i], si_hv[i]], axis=1) * q_cn[i] for i in seqs]
    hcs = [nd[i][:, :W_GROUP] / jnp.maximum(jnp.abs(nd[i][:, W_GROUP:]), emt_hv[i]) for i in seqs]
    kws = [ks[i] * w_hv[i] for i in seqs]
    upds = [_dot_tn(kws[i], jnp.concatenate([vs[i], jnp.ones_like(vs[i])], axis=1)) for i in seqs]
    same_cn = jnp.concatenate([same_vv, same_vv], axis=1)
    sum_sq = _group_sums([x * x for x in hcs], group_sum)
    yield 'stage'
    ng = ng_ref[...]
    for i in seqs:
        dec_row = dec_hv[i][0:1, :]
        cn_s[i] = jnp.concatenate([dec_row, dec_row], axis=1) * cns[i] + jnp.where(same_cn, upds[i], 0.0)
        m_s[i] = jnp.where(head_lane[0:1, :], b_c[i][c - 1:c, :] + r_end[i], 0.0)
        h_ref[i] = (hcs[i] * lax.rsqrt(sum_sq[i] * (1.0 / HEAD_V) + EPS) * ng * _sigmoid(og_ref[i]))
    yield 'final'

    @pl.when(j == pl.num_programs(1) - 1)
    def _():
        for i in seqs:
            for h in range(N_HEADS):
                c_out[i, h] = cn_s[i, _hs(h), _hs(h)]
                n_out[i, h] = cn_s[i, _hs(h), W_GROUP + h * HEAD_V:W_GROUP + h * HEAD_V + 1]
            m_out[i] = m_s[i, :, 0:N_HEADS]


def _same_head(rows, lanes, row_group, lane_group):
    rh = lax.broadcasted_iota(jnp.int32, (rows, lanes), 0) >> (row_group.bit_length() - 1)
    lh = lax.broadcasted_iota(jnp.int32, (rows, lanes), 1) >> (lane_group.bit_length() - 1)
    return rh == lh


def _head_blocks(x, same_head):
    tiled = jnp.concatenate([x] * N_HEADS, axis=0)
    return jnp.where(same_head, tiled, jnp.zeros_like(tiled))


def _heads_dot_split(l, r, same_head):
    return _heads_dot_parts(_split2(l), _split2(r), same_head)


def _heads_dot_parts(l_parts, r_parts, same_head):
    l1, l2 = l_parts
    rb1, rb2 = _head_blocks(r_parts[0], same_head), _head_blocks(r_parts[1], same_head)
    m = l1.shape[0]
    y = jnp.dot(jnp.concatenate([l1, l2], axis=0), rb1, preferred_element_type=F32)
    return y[:m] + (y[m:] + jnp.dot(l1, rb2, preferred_element_type=F32))


def _unit_lower_inverses(lows, eye4, same_head, c):
    ts = [eye4 - a for a in lows]
    a_parts = [_split2(a) for a in lows]
    ps = [_heads_dot_parts(ap, ap, same_head) for ap in a_parts]
    yield 'stage'
    for _ in range(c.bit_length() - 3):
        t_parts = [_split2(t) for t in ts]
        p_parts = [_split2(p) for p in ps]
        stacked = [tuple(jnp.concatenate([tp[n], pp[n]], axis=0) for n in range(2))
                   for tp, pp in zip(t_parts, p_parts)]
        ys = [_heads_dot_parts(sp, pp, same_head) for sp, pp in zip(stacked, p_parts)]
        ts = [t + y[:c] for t, y in zip(ts, ys)]
        ps = [y[c:] for y in ys]
        yield 'stage'
    return [t + _heads_dot_split(t, p, same_head) for t, p in zip(ts, ps)]


def _gdn_steps(q_ref, k_ref, v_ref, z_ref, g_ref, bias_ref, alog_ref, ng_ref, s0_ref,
               o_ref, s_out, s_s, *, c, nb):
    j = pl.program_id(1)
    seqs = range(nb)

    @pl.when(j == 0)
    def _():
        s_s[...] = jnp.zeros_like(s_s)
        for i in seqs:
            for h in range(N_HEADS):
                s_s[i, _hs(h), _hs(h)] = s0_ref[i, h]

    yield 'stage'

    hs = N_HEADS * c
    lane = lax.broadcasted_iota(jnp.int32, (c, LANES), 1)
    is_a = (lane >= G_GA) & (lane < G_GA + N_HEADS)
    incl_b = _tri(c).astype(BF16)
    t_idx = lax.broadcasted_iota(jnp.int32, (c, hs), 0)
    s_idx = lax.broadcasted_iota(jnp.int32, (c, hs), 1) & (c - 1)
    incl4, strict4, eye4 = t_idx >= s_idx, t_idx > s_idx, (t_idx == s_idx).astype(F32)
    same_hs = _same_head(hs, hs, c, c)
    same_hv = _same_head(hs, W_GROUP, c, HEAD_V)
    same_vv = _same_head(W_GROUP, W_GROUP, HEAD_V, HEAD_V)
    group_sum = same_vv.astype(BF16)

    def pick(lanes, group):
        g = lax.broadcasted_iota(jnp.int32, (LANES, 2 * lanes), 0)
        l = lax.broadcasted_iota(jnp.int32, (LANES, 2 * lanes), 1)
        h = (l & (lanes - 1)) >> (group.bit_length() - 1)
        return (g == jnp.where(l < lanes, G_GA, G_GB) + h).astype(BF16)

    pick_hs = pick(hs, c)
    row_sel = (lax.broadcasted_iota(jnp.int32, (8, LANES), 1)
               == G_GA + lax.broadcasted_iota(jnp.int32, (8, LANES), 0)).astype(BF16)
    gates = [g_ref[i] + bias_ref[...] for i in seqs]
    xgs = [jnp.where(is_a, -jnp.exp(alog_ref[...]) * _softplus(g), _sigmoid(g)) for g in gates]
    csums = [jnp.where(is_a, _dot01_left(incl_b, xg), xg) for xg in xgs]
    cols_hs = _dot01_right(csums, pick_hs)
    yield 'stage'
    bc_hs, beta_hs = [x[:, :hs] for x in cols_hs], [x[:, hs:] for x in cols_hs]
    if c == HEAD_V:
        bc_hv, beta_hv = bc_hs, beta_hs
    else:
        cols_hv = _dot01_right(csums, pick(W_GROUP, HEAD_V))
        bc_hv, beta_hv = [x[:, :W_GROUP] for x in cols_hv], [x[:, W_GROUP:] for x in cols_hv]
    b_rows = [sum(lax.dot_general(row_sel, part, (((1,), (1,)), ((), ())), preferred_element_type=F32)
                  for part in _split3(cs)) for cs in csums]
    br_hs = [jnp.concatenate([br[h:h + 1, :] for h in range(N_HEADS)], axis=1) for br in b_rows]
    decays = [jnp.exp(jnp.where(incl4, bc_hs[i] - br_hs[i], 0.0)) for i in seqs]
    yield 'stage'
    qk_raw = [jnp.concatenate([q_ref[i], k_ref[i]], axis=0) for i in seqs]
    norms = _group_sums([x * x for x in qk_raw], group_sum)
    qk_n = [qk_raw[i] * lax.rsqrt(norms[i] + EPS) for i in seqs]
    qs = [x[:c] * (HEAD_V ** -0.5) for x in qk_n]
    ks = [x[c:] for x in qk_n]
    vs = [v_ref[i] for i in seqs]
    kq = [jnp.concatenate([ks[i], qs[i]], axis=0).astype(BF16) for i in seqs]
    yield 'stage'
    states = [s_s[i] for i in seqs]
    scores = [lax.dot_general(kq[i], _head_blocks(ks[i].astype(BF16), same_hv),
                              (((1,), (1,)), ((), ())), preferred_element_type=F32) for i in seqs]
    on_state = [jnp.dot(kq[i], states[i].astype(BF16), preferred_element_type=F32) for i in seqs]
    yield 'stage'
    lows = [jnp.where(strict4, beta_hs[i] * decays[i] * scores[i][:c], 0.0) for i in seqs]
    invs = yield from _unit_lower_inverses(lows, eye4, same_hs, c)
    ebs = [jnp.exp(bc_hv[i]) for i in seqs]
    rhss = [beta_hv[i] * (vs[i] - ebs[i] * on_state[i][:c]) for i in seqs]
    yield 'stage'
    us = [_heads_dot_split(invs[i], rhss[i], same_hv) for i in seqs]
    yield 'stage'
    qkm = [jnp.where(incl4, decays[i] * scores[i][c:], 0.0).astype(BF16) for i in seqs]
    outs = [ebs[i] * on_state[i][c:]
            + jnp.dot(qkm[i], _head_blocks(us[i].astype(BF16), same_hv), preferred_element_type=F32)
            for i in seqs]
    b_ends = [bc_hv[i][c - 1:c, :] for i in seqs]
    upds = [jnp.where(same_vv, _dot_tn(ks[i] * jnp.exp(b_ends[i] - bc_hv[i]), us[i]), 0.0) for i in seqs]
    yield 'stage'
    ng = ng_ref[...]
    sum_sq = _group_sums([o * o for o in outs], group_sum)
    for i in seqs:
        s_s[i] = jnp.exp(b_ends[i]) * states[i] + upds[i]
        o_ref[i] = (outs[i] * lax.rsqrt(sum_sq[i] * (1.0 / HEAD_V) + EPS) * ng * _silu(z_ref[i]))
    yield 'final'

    @pl.when(j == pl.num_programs(1) - 1)
    def _():
        for i in seqs:
            for h in range(N_HEADS):
                s_out[i, h] = s_s[i, _hs(h), _hs(h)]


GLA_SUB = 8


def _gla_steps(q_ref, k_ref, v_ref, r_ref, g_ref, w2_ref, gb_ref, ng_ref, s0_ref,
               o_ref, s_out, s_s, *, c, nb):
    j = pl.program_id(1)
    dk_all = N_HEADS * GLA_DK
    hs = N_HEADS * c
    nsb = c // GLA_SUB
    seqs = range(nb)

    @pl.when(j == 0)
    def _():
        s_s[...] = jnp.zeros_like(s_s)
        for i in seqs:
            for h in range(N_HEADS):
                s_s[i, h * GLA_DK:(h + 1) * GLA_DK, _hs(h)] = s0_ref[i, h]

    yield 'stage'

    incl_b = _tri(c).astype(BF16)
    er = lax.broadcasted_iota(jnp.int32, (2 * c, c), 0)
    ec = lax.broadcasted_iota(jnp.int32, (2 * c, c), 1)
    et = er & (c - 1)
    edge_sel = (ec == jnp.where(er < c, (et & ~(GLA_SUB - 1)) - 1, et | (GLA_SUB - 1))).astype(BF16)
    t_idx = lax.broadcasted_iota(jnp.int32, (c, hs), 0)
    s_idx = lax.broadcasted_iota(jnp.int32, (c, hs), 1) & (c - 1)
    blk_dist = (t_idx >> 3) - (s_idx >> 3)
    diag_off = t_idx - s_idx
    same_hk = _same_head(hs, dk_all, c, GLA_DK)
    same_hv = _same_head(hs, W_GROUP, c, HEAD_V)
    same_kv = _same_head(dk_all, W_GROUP, GLA_DK, HEAD_V)
    head_rep = _same_head(dk_all, hs, GLA_DK, c).astype(BF16)
    group_sum = _same_head(W_GROUP, W_GROUP, HEAD_V, HEAD_V).astype(BF16)
    row8 = lax.broadcasted_iota(jnp.int32, (c, dk_all), 0) & (GLA_SUB - 1)
    last_row = lax.broadcasted_iota(jnp.int32, (c, dk_all), 0) == c - 1

    log_as = [_log_sigmoid(_dot(g_ref[i], w2_ref[...]) + gb_ref[...]) / GLA_TAU for i in seqs]
    bs = [_dot01_left(incl_b, la) for la in log_as]
    edges = [_dot01_left(edge_sel, b) for b in bs]
    yield 'stage'
    e_prev = [e[:c] for e in edges]
    e_own = [e[c:] for e in edges]
    qs = [q_ref[i] * (GLA_DK ** -0.5) for i in seqs]
    ks = [k_ref[i] for i in seqs]
    vs = [v_ref[i] for i in seqs]
    q_hat = [qs[i] * jnp.exp(bs[i] - e_prev[i]) for i in seqs]
    k_hat = [ks[i] * jnp.exp(e_own[i] - bs[i]) for i in seqs]
    yield 'stage'

    def q_for_distance(i, m):
        if m == 0:
            return q_hat[i]
        rows = GLA_SUB * m
        shifted = jnp.concatenate([jnp.zeros((rows, dk_all), F32), e_prev[i][:c - rows]], axis=0)
        return q_hat[i] * jnp.exp(e_prev[i] - shifted)

    a_parts = []
    for i in seqs:
        lhs = jnp.concatenate([q_for_distance(i, m) for m in range(nsb - 1)], axis=0).astype(BF16)
        prod = lax.dot_general(lhs, _head_blocks(k_hat[i].astype(BF16), same_hk),
                               (((1,), (1,)), ((), ())), preferred_element_type=F32)
        a_parts.append(sum(jnp.where(blk_dist == m + 1, prod[m * c:(m + 1) * c], 0.0)
                           for m in range(nsb - 1)))
        yield 'stage'

    def rot8(x, d):
        return x if d == 0 else pltpu.roll(x, d, 0)

    for i in seqs:
        ws = [jnp.where(row8 >= d, qs[i] * rot8(ks[i], d) * jnp.exp(bs[i] - rot8(bs[i], d)), 0.0)
              for d in range(GLA_SUB)]
        prod = jnp.dot(jnp.concatenate(ws, axis=0).astype(BF16), head_rep, preferred_element_type=F32)
        a_parts[i] = a_parts[i] + sum(jnp.where(diag_off == d, prod[d * c:(d + 1) * c], 0.0)
                                      for d in range(GLA_SUB))
        yield 'stage'
    states = [s_s[i] for i in seqs]
    outs = [jnp.dot(a_parts[i].astype(BF16), _head_blocks(vs[i].astype(BF16), same_hv),
                    preferred_element_type=F32)
            + _dot(qs[i] * jnp.exp(bs[i]), states[i]) for i in seqs]
    yield 'stage'
    b_last = [b[c - 1:c, :] for b in bs]
    upds = [jnp.where(same_kv, _dot_tn(ks[i] * jnp.exp(b_last[i] - bs[i]), vs[i]), 0.0) for i in seqs]
    ones_cv = jnp.ones((c, W_GROUP), BF16)
    decay_rows = [sum(lax.dot_general(part, ones_cv, (((0,), (0,)), ((), ())), preferred_element_type=F32)
                      for part in _split3(jnp.where(last_row, b, 0.0))) for b in bs]
    yield 'stage'
    sum_sq = _group_sums([o * o for o in outs], group_sum)
    ng = ng_ref[...]
    for i in seqs:
        s_s[i] = states[i] * jnp.exp(decay_rows[i]) + upds[i]
        o_ref[i] = outs[i] * lax.rsqrt(sum_sq[i] * (1.0 / HEAD_V) + EPS) * ng * _silu(r_ref[i])
    yield 'final'

    @pl.when(j == pl.num_programs(1) - 1)
    def _():
        for i in seqs:
            for h in range(N_HEADS):
                s_out[i, h] = s_s[i, h * GLA_DK:(h + 1) * GLA_DK, _hs(h)]


def _interleave(steps):
    active, parked = list(steps), []
    while active:
        for g in list(active):
            if next(g) == 'final':
                active.remove(g)
                parked.append(g)
    for g in parked:
        for _ in g:
            pass


def _mixers_kernel(mq, mk, mv, mo, gq, gk, gv, gz, lq, lk, lv, lr, g_ref, bias_ref, alog_ref,
                   ml_ng, gdn_ng, w2_ref, gb_ref, gla_ng, c0, n0, m0, gs0, ls0,
                   ml_h, c_out, n_out, m_out, gdn_o, gs_out, gla_o, ls_out,
                   cn_s, m_s, gs_s, ls_s, *, c, nb):
    _interleave([
        _mlstm_steps(mq, mk, mv, mo, g_ref, bias_ref, ml_ng, c0, n0, m0, ml_h, c_out, n_out, m_out,
                     cn_s, m_s, c=c, nb=nb),
        _gdn_steps(gq, gk, gv, gz, g_ref, bias_ref, alog_ref, gdn_ng, gs0, gdn_o, gs_out, gs_s,
                   c=c, nb=nb),
        _gla_steps(lq, lk, lv, lr, g_ref, w2_ref, gb_ref, gla_ng, ls0, gla_o, ls_out, ls_s,
                   c=c, nb=nb)])


def _mixers_call(p, gate_bias, alog_row, ml_ng, gdn_ng, w2_pad, gla_gb, gla_ng,
                 ml_c, ml_n, ml_m, gdn_s, gla_s, c, nb):
    b, l, _ = p.shape
    nb = math.gcd(nb, b)
    nc = l // c
    dk_all = N_HEADS * GLA_DK
    blk = lambda cb: pl.BlockSpec((nb, c, W_GROUP), lambda i, j, cb=cb: (i, j, cb))
    blk128 = lambda cb: pl.BlockSpec((nb, c, LANES), lambda i, j, cb=cb: (i, j, cb))
    row = lambda w: pl.BlockSpec((1, w), lambda i, j: (0, 0))
    full = lambda a: pl.BlockSpec((nb,) + a.shape[1:], lambda i, j: (i,) + (0,) * (a.ndim - 1))
    tok = pl.BlockSpec((nb, c, W_GROUP), lambda i, j: (i, j, 0))
    ml_n = ml_n[..., None]
    ml_m = ml_m.reshape(b, 1, N_HEADS)
    states = (ml_c, ml_n, ml_m, gdn_s, gla_s)
    sds = lambda a: jax.ShapeDtypeStruct(a.shape, F32)
    act = jax.ShapeDtypeStruct((b, l, W_GROUP), F32)
    o_ml, ml_c, ml_n, ml_m, o_gdn, gdn_s, o_gla, gla_s = pl.pallas_call(
        functools.partial(_mixers_kernel, c=c, nb=nb),
        grid=(b // nb, nc),
        in_specs=[blk(PB_MQ), blk(PB_MK), blk(PB_MV), blk(PB_MO),
                  blk(PB_GQ), blk(PB_GK), blk(PB_GV), blk(PB_GZ),
                  blk128(PB128_LQ), blk128(PB128_LK), blk(PB_LV), blk(PB_LR),
                  blk128(PB128_GATES), row(LANES), row(LANES), row(W_GROUP), row(W_GROUP),
                  pl.BlockSpec((LANES, dk_all), lambda i, j: (0, 0)), row(dk_all), row(W_GROUP)]
                 + [full(a) for a in states],
        out_specs=[tok, full(ml_c), full(ml_n), full(ml_m), tok, full(gdn_s), tok, full(gla_s)],
        out_shape=[act, sds(ml_c), sds(ml_n), sds(ml_m), act, sds(gdn_s), act, sds(gla_s)],
        scratch_shapes=[pltpu.VMEM((nb, W_GROUP, 2 * W_GROUP), F32), pltpu.VMEM((nb, 1, LANES), F32),
                        pltpu.VMEM((nb, W_GROUP, W_GROUP), F32), pltpu.VMEM((nb, dk_all, W_GROUP), F32)],
        compiler_params=pltpu.CompilerParams(dimension_semantics=("parallel", "arbitrary"),
                                             vmem_limit_bytes=VMEM_LIMIT),
        name="mixers",
    )(*([p] * 13), gate_bias, alog_row, ml_ng, gdn_ng, w2_pad, gla_gb, gla_ng, *states)
    return o_ml, ml_c, ml_n[..., 0], ml_m[:, 0, :], o_gdn, gdn_s, o_gla, gla_s


def _outffn_kernel(x_ref, a_ref, b_ref, c_ref, d_ref, mod_ref, g2_ref, gf_ref, wo_ref, w1_ref, w2_ref,
                   o_ref, *, ff_chunk, final):
    mod = mod_ref[0]
    mixed = jnp.concatenate([a_ref[0], b_ref[0], c_ref[0], d_ref[0]], axis=-1).astype(BF16)
    x = x_ref[0] + mod[2:3] * jnp.dot(mixed, wo_ref[...], preferred_element_type=F32)
    h = _modulated_norm(x, g2_ref[...], mod[4:5], mod[3:4]).astype(BF16)
    d_ff = w1_ref.shape[1]
    acc = jnp.zeros(x.shape, F32)
    for f0 in range(0, d_ff, ff_chunk):
        a = jnp.maximum(jnp.dot(h, w1_ref[:, f0:f0 + ff_chunk], preferred_element_type=F32), 0.0)
        acc = acc + jnp.dot((a * a).astype(BF16), w2_ref[f0:f0 + ff_chunk, :],
                            preferred_element_type=F32)
    x = x + mod[5:6] * acc
    if final:
        x = x * lax.rsqrt(jnp.mean(x * x, axis=-1, keepdims=True) + EPS) * gf_ref[...]
    o_ref[0] = x


def _outffn_call(x, mixers, mod, g2, gf, wo, w1, w2, tm, final):
    b, l, d = x.shape
    d_ff = w1.shape[1]
    tok = lambda w: pl.BlockSpec((1, tm, w), lambda i, j: (i, j, 0))
    const = lambda shape: pl.BlockSpec(shape, lambda i, j: (0,) * len(shape),
                                       pipeline_mode=pl.Buffered(1))
    return pl.pallas_call(
        functools.partial(_outffn_kernel, ff_chunk=1024, final=final),
        grid=(b, l // tm),
        in_specs=[tok(d), tok(W_GROUP), tok(W_GROUP), tok(W_GROUP), tok(W_GROUP),
                  pl.BlockSpec((1, 6, d), lambda i, j: (i, 0, 0)),
                  pl.BlockSpec((1, d), lambda i, j: (0, 0)),
                  pl.BlockSpec((1, d), lambda i, j: (0, 0)),
                  const((d, d)), const((d, d_ff)), const((d_ff, d))],
        out_specs=tok(d),
        out_shape=jax.ShapeDtypeStruct((b, l, d), F32),
        compiler_params=pltpu.CompilerParams(dimension_semantics=("parallel", "parallel"),
                                             vmem_limit_bytes=VMEM_LIMIT),
        name="outproj_ffn",
    )(x, *mixers, mod, g2, gf, wo, w1, w2)


def _permute_w_in(w_in):
    d = w_in.shape[0]
    sizes = (256, 256, 256, 256, 256, 256, 4, 4, 256, 768, 4, 4, 256, 128, 128, 256, 16, 256)
    offs = [0]
    for s in sizes:
        offs.append(offs[-1] + s)
    (sq, sk, sv, mq, mk, mv, mi, mf, mo, gqkv, gb, ga, gz, lq, lk, lv, lg, lr) = [
        w_in[:, offs[i]:offs[i + 1]] for i in range(len(sizes))]
    pad = jnp.zeros((d, LANES - (4 * N_HEADS + GLA_RANK)), w_in.dtype)
    return jnp.concatenate([sk, sv, sq, mq, mk, mv, mo, gqkv, gz, lv, lr, lq, lk,
                            mi, mf, gb, ga, lg, pad], axis=1)


def _row128(pieces):
    row = jnp.zeros((LANES,), F32)
    for off, vec in pieces:
        row = row.at[off:off + vec.shape[0]].set(vec)
    return row.reshape(1, LANES)


class _Tiles(NamedTuple):
    tm: int
    sb_tq: int
    sb_tk: int
    chunk: int


def _tiles(l):
    return _Tiles(tm=min(512, l), sb_tq=min(256, l), sb_tk=max(min(256, l), LANES),
                  chunk=min(64, l))


def _layer(x, mod, lw, states, tiles, final, final_g):
    tm, sb_tq, sb_tk, chunk = tiles
    (n1, n2, w_in_p, gate_bias, ml_ng, conv_w, alog_row, gdn_ng, w2_pad, gla_gb, gla_ng,
     w_out, w_ff1, w_ff2) = lw
    (sb_k_past, sb_v_past, ml_c, ml_n, ml_m, gdn_s, gdn_buf, gla_s) = states
    b, l, _ = x.shape
    k_new, v_new, p, gdn_buf = _inproj_call(x, mod, n1, w_in_p, conv_w, gdn_buf, tm)
    if sb_k_past is None:
        k_all, v_all, q_offset = k_new, v_new, 0
    else:
        past = sb_k_past.shape[1]
        lk = -(-(past + l) // sb_tk) * sb_tk
        padz = jnp.zeros((b, lk - past - l, W_GROUP), F32)
        k_all = jnp.concatenate([sb_k_past.reshape(b, past, W_GROUP), k_new, padz], axis=1)
        v_all = jnp.concatenate([sb_v_past.reshape(b, past, W_GROUP), v_new, padz], axis=1)
        q_offset = past
    o_sb = _sb_call(p, PB_SQ, k_all, v_all, l, sb_tq, sb_tk, q_offset)
    o_ml, ml_c, ml_n, ml_m, o_gdn, gdn_s, o_gla, gla_s = _mixers_call(
        p, gate_bias, alog_row, ml_ng, gdn_ng, w2_pad, gla_gb, gla_ng, ml_c, ml_n, ml_m, gdn_s, gla_s,
        chunk, MIXER_SEQS_PER_STEP)
    x = _outffn_call(x, (o_sb, o_ml, o_gdn, o_gla), mod, n2, final_g, w_out, w_ff1, w_ff2, tm, final)
    hk = lambda a: a.reshape(b, l, N_HEADS, HEAD_V)
    return x, (hk(k_new), hk(v_new), ml_c, ml_n, ml_m, gdn_s, gdn_buf, gla_s)


def kernel(x_prompt, x_sample, cache_sb_k, cache_sb_v, state_mlstm_C, state_mlstm_n, state_mlstm_m, state_gdn_S, state_gdn_conv, state_gla_S, c_prompt, c_sample, norm1_g, norm2_g, w_ada, b_ada, w_in, mlstm_i_bias, mlstm_f_bias, mlstm_norm_g, gdn_conv_w, gdn_a_log, gdn_dt_bias, gdn_norm_g, gla_w_gate2, gla_gate_bias, gla_norm_g, w_out, w_ff1, w_ff2, final_g):
    depth = w_in.shape[0]
    bp, lp, d = x_prompt.shape
    bs, ls, _ = x_sample.shape
    dk_all = N_HEADS * GLA_DK
    mod_all = _ada_call(jnp.concatenate([c_prompt, c_sample], axis=0), w_ada, b_ada)
    mod_all = mod_all.reshape(depth, bp + bs, 6, d)
    final_row = final_g.reshape(1, d)
    xp, xs = x_prompt, x_sample
    p_list, s_list = [], []
    for l in range(depth):
        gate_bias = _row128([(G_MI, mlstm_i_bias[l]), (G_MF, mlstm_f_bias[l]), (G_GA, gdn_dt_bias[l])])
        alog_row = _row128([(G_GA, gdn_a_log[l])])
        w2_pad = jnp.zeros((LANES, dk_all), F32).at[G_LG:G_LG + GLA_RANK, :].set(gla_w_gate2[l])
        lw = (norm1_g[l].reshape(1, d), norm2_g[l].reshape(1, d), _permute_w_in(w_in[l]).astype(BF16),
              gate_bias, mlstm_norm_g[l].reshape(1, W_GROUP), gdn_conv_w[l], alog_row,
              gdn_norm_g[l].reshape(1, W_GROUP), w2_pad.astype(BF16),
              gla_gate_bias[l].reshape(1, dk_all), gla_norm_g[l].reshape(1, W_GROUP),
              w_out[l].astype(BF16), w_ff1[l].astype(BF16), w_ff2[l].astype(BF16))
        fresh = (None, None, jnp.zeros((bp,) + state_mlstm_C.shape[2:], F32),
                 jnp.zeros((bp,) + state_mlstm_n.shape[2:], F32),
                 jnp.zeros((bp,) + state_mlstm_m.shape[2:], F32),
                 jnp.zeros((bp,) + state_gdn_S.shape[2:], F32),
                 jnp.zeros((bp,) + state_gdn_conv.shape[2:], F32),
                 jnp.zeros((bp,) + state_gla_S.shape[2:], F32))
        past = (cache_sb_k[l], cache_sb_v[l], state_mlstm_C[l], state_mlstm_n[l], state_mlstm_m[l],
                state_gdn_S[l], state_gdn_conv[l], state_gla_S[l])
        final = l == depth - 1
        xp, st_p = _layer(xp, mod_all[l, :bp], lw, fresh, _tiles(lp), final, final_row)
        xs, st_s = _layer(xs, mod_all[l, bp:], lw, past, _tiles(ls), final, final_row)
        p_list.append(st_p)
        s_list.append(st_s)
    stacked_p = [jnp.stack([st[i] for st in p_list]) for i in range(8)]
    stacked_s = [jnp.stack([st[i] for st in s_list]) for i in range(8)]
    return (xp, xs, *stacked_p, *stacked_s)
```

```python
import functools
import math
from typing import NamedTuple

import jax
import jax.numpy as jnp
from jax import lax
from jax.experimental import pallas as pl
from jax.experimental.pallas import tpu as pltpu

F32 = jnp.float32
BF16 = jnp.bfloat16
EPS = 1e-6
N_HEADS = 4
HEAD_V = 64
GLA_DK = 32
CONV_W = 4
GLA_TAU = 16.0
W_GROUP = N_HEADS * HEAD_V
LANES = 128
VMEM_LIMIT = 56 * 1024 * 1024
MIXER_SEQS_PER_STEP = 8

PB_SQ, PB_MQ, PB_MK, PB_MV, PB_MO, PB_GQ, PB_GK, PB_GV, PB_GZ, PB_LV, PB_LR = range(11)
PB128_LQ, PB128_LK, PB128_GATES = 22, 23, 24
P_WIDTH = 25 * LANES
G_MI, G_MF, G_GB, G_GA, G_LG = 0, 4, 8, 12, 16
GLA_RANK = 16


def _dot(a, b):
    return jnp.dot(a.astype(BF16), b.astype(BF16), preferred_element_type=F32)


def _dot_tn(a, b):
    return lax.dot_general(a.astype(BF16), b.astype(BF16), (((0,), (0,)), ((), ())),
                           preferred_element_type=F32)


def _split3(x):
    x1 = x.astype(BF16)
    r1 = x - x1.astype(F32)
    x2 = r1.astype(BF16)
    x3 = (r1 - x2.astype(F32)).astype(BF16)
    return x1, x2, x3


def _split2(x):
    x1 = x.astype(BF16)
    x2 = (x - x1.astype(F32)).astype(BF16)
    return x1, x2


def _dot01_left(m01, x):
    return sum(jnp.dot(m01, p, preferred_element_type=F32) for p in _split3(x))


def _dot01_right(xs, m01):
    n = xs[0].shape[0]
    stacked = jnp.concatenate([part for x in xs for part in _split3(x)], axis=0)
    y = jnp.dot(stacked, m01, preferred_element_type=F32)
    return [y[3 * i * n:(3 * i + 1) * n] + (y[(3 * i + 1) * n:(3 * i + 2) * n]
                                             + y[(3 * i + 2) * n:(3 * i + 3) * n])
            for i in range(len(xs))]


def _group_sums(xs, group_sum):
    n = xs[0].shape[0]
    y = jnp.dot(jnp.concatenate([x.astype(BF16) for x in xs], axis=0), group_sum,
                preferred_element_type=F32)
    return [y[i * n:(i + 1) * n] for i in range(len(xs))]


def _log_sigmoid(x):
    return jnp.minimum(x, 0.0) - jnp.log(1.0 + jnp.exp(-jnp.abs(x)))


def _sigmoid(x):
    return 1.0 / (1.0 + jnp.exp(-x))


def _silu(x):
    return x * _sigmoid(x)


def _softplus(x):
    return jnp.maximum(x, 0.0) + jnp.log(1.0 + jnp.exp(-jnp.abs(x)))


def _tri(c):
    r = lax.broadcasted_iota(jnp.int32, (c, c), 0)
    s = lax.broadcasted_iota(jnp.int32, (c, c), 1)
    return r >= s


def _hs(h, w=HEAD_V):
    return slice(h * w, (h + 1) * w)


def _ada_kernel(c_ref, w_ref, b_ref, o_ref):
    o_ref[0] = _dot(_silu(c_ref[...]), w_ref[0]) + b_ref[0]


def _ada_call(c_all, w_ada, b_ada, tn=512):
    depth, d, n = w_ada.shape
    rows = c_all.shape[0]
    return pl.pallas_call(
        _ada_kernel,
        grid=(depth, n // tn),
        in_specs=[pl.BlockSpec((rows, d), lambda l, j: (0, 0)),
                  pl.BlockSpec((1, d, tn), lambda l, j: (l, 0, j)),
                  pl.BlockSpec((1, 1, tn), lambda l, j: (l, 0, j))],
        out_specs=pl.BlockSpec((1, rows, tn), lambda l, j: (l, 0, j)),
        out_shape=jax.ShapeDtypeStruct((depth, rows, n), F32),
        compiler_params=pltpu.CompilerParams(dimension_semantics=("parallel", "parallel"),
                                             vmem_limit_bytes=VMEM_LIMIT),
        name="ada_mod",
    )(c_all, w_ada, b_ada.reshape(depth, 1, n))


def _modulated_norm(x, g_row, scale_row, shift_row):
    y = x * lax.rsqrt(jnp.mean(x * x, axis=-1, keepdims=True) + EPS) * g_row
    return y * (1.0 + scale_row) + shift_row


def _inproj_kernel(x_ref, mod_ref, g_ref, w_ref, cw_ref, cb0_ref, k_ref, v_ref, p_ref, cb_out, xw_s,
                   *, col_chunk):
    j = pl.program_id(1)
    ns, tm = x_ref.shape[0], x_ref.shape[1]
    conv_lo, conv_hi = PB_GQ * W_GROUP, (PB_GV + 1) * W_GROUP

    @pl.when(j == 0)
    def _():
        xw_s[:, 0:8, :] = cb0_ref[...]

    mod = mod_ref[...]
    h = _modulated_norm(x_ref[...], g_ref[...], mod[:, 1:2, :], mod[:, 0:1, :])
    h = h.reshape(ns * tm, h.shape[-1]).astype(BF16)
    per_seq = lambda y: y.reshape(ns, tm, y.shape[-1])
    k_ref[...] = per_seq(jnp.dot(h, w_ref[:, 0:W_GROUP], preferred_element_type=F32))
    v_ref[...] = per_seq(jnp.dot(h, w_ref[:, W_GROUP:2 * W_GROUP], preferred_element_type=F32))
    base = 2 * W_GROUP
    bounds = sorted(set(range(0, conv_lo, col_chunk)) | {conv_lo, conv_hi}
                    | set(range(conv_hi, P_WIDTH, col_chunk)) | {P_WIDTH})
    for n0, n1 in zip(bounds[:-1], bounds[1:]):
        if n0 == conv_lo:
            continue
        p_ref[:, :, n0:n1] = per_seq(jnp.dot(h, w_ref[:, base + n0:base + n1],
                                             preferred_element_type=F32))
    raw = per_seq(jnp.dot(h, w_ref[:, base + conv_lo:base + conv_hi], preferred_element_type=F32))
    xw_s[:, 8:8 + tm, :] = raw
    cw = cw_ref[...]
    y = raw * cw[CONV_W - 1:CONV_W, :]
    for d in range(1, CONV_W):
        y = y + xw_s[:, 8 - d:8 - d + tm, :] * cw[CONV_W - 1 - d:CONV_W - d, :]
    p_ref[:, :, conv_lo:conv_hi] = _silu(y)
    xw_s[:, 0:8, :] = raw[:, tm - 8:tm, :]

    @pl.when(j == pl.num_programs(1) - 1)
    def _():
        cb_out[...] = raw[:, tm - 8:tm, :]


def _inproj_call(x, mod, g, w_perm, conv_w, conv0, tm, ns):
    b, l, d = x.shape
    n_all = w_perm.shape[1]
    conv_dim = conv_w.shape[1]
    conv0_pad = jnp.concatenate([jnp.zeros((b, 8 - (CONV_W - 1), conv_dim), F32), conv0], axis=1)
    k, v, p, conv_new = pl.pallas_call(
        functools.partial(_inproj_kernel, col_chunk=512),
        grid=(b // ns, l // tm),
        in_specs=[pl.BlockSpec((ns, tm, d), lambda i, j: (i, j, 0)),
                  pl.BlockSpec((ns, 6, d), lambda i, j: (i, 0, 0)),
                  pl.BlockSpec((1, d), lambda i, j: (0, 0)),
                  pl.BlockSpec((d, n_all), lambda i, j: (0, 0), pipeline_mode=pl.Buffered(1)),
                  pl.BlockSpec((CONV_W, conv_dim), lambda i, j: (0, 0)),
                  pl.BlockSpec((ns, 8, conv_dim), lambda i, j: (i, 0, 0))],
        out_specs=[pl.BlockSpec((ns, tm, W_GROUP), lambda i, j: (i, j, 0)),
                   pl.BlockSpec((ns, tm, W_GROUP), lambda i, j: (i, j, 0)),
                   pl.BlockSpec((ns, tm, P_WIDTH), lambda i, j: (i, j, 0)),
                   pl.BlockSpec((ns, 8, conv_dim), lambda i, j: (i, 0, 0))],
        out_shape=[jax.ShapeDtypeStruct((b, l, W_GROUP), F32),
                   jax.ShapeDtypeStruct((b, l, W_GROUP), F32),
                   jax.ShapeDtypeStruct((b, l, P_WIDTH), F32),
                   jax.ShapeDtypeStruct((b, 8, conv_dim), F32)],
        scratch_shapes=[pltpu.VMEM((ns, tm + 8, conv_dim), F32)],
        compiler_params=pltpu.CompilerParams(dimension_semantics=("parallel", "arbitrary"),
                                             vmem_limit_bytes=VMEM_LIMIT),
        name="norm_inproj",
    )(x, mod, g, w_perm, conv_w, conv0_pad)
    return k, v, p, conv_new[:, 8 - (CONV_W - 1):, :]


SB_DEAD_LOG = -104.0
SB_BLOCKS_PER_STEP = 2


def _sb_kernel(q_ref, k_ref, v_ref, o_ref, *, tq, tk, q_offset, n_q):
    for qb in range(n_q):
        _sb_query_block(q_ref, k_ref, v_ref, o_ref, pl.program_id(1) * n_q + qb,
                        slice(qb * tq, (qb + 1) * tq), tq=tq, tk=tk, q_offset=q_offset)


def _sb_query_block(q_ref, k_ref, v_ref, o_ref, i, rows, *, tq, tk, q_offset):
    q0 = q_offset + i * tq
    j_top = (q0 + tq - 1) // tk
    n_masked = max(tq // tk, 1)
    suffix = (lax.broadcasted_iota(jnp.int32, (tk, tk), 0)
              >= lax.broadcasted_iota(jnp.int32, (tk, tk), 1)).astype(BF16)
    q_all = q_ref[0, rows, :] * (HEAD_V ** -0.5)
    qs = [q_all[:, _hs(h)].astype(BF16) for h in range(N_HEADS)]

    def block(j, accs, runs, causal):
        s0 = pl.multiple_of(j * tk, tk)
        heads = range(N_HEADS)
        kbs = [k_ref[0, pl.ds(s0, tk), _hs(h)].astype(BF16) for h in heads]
        vbs = [v_ref[0, pl.ds(s0, tk), _hs(h)].astype(BF16) for h in heads]
        zs = [lax.dot_general(qs[h], kbs[h], (((1,), (1,)), ((), ())), preferred_element_type=F32)
              for h in heads]
        sps = [_softplus(z) for z in zs]
        if causal is not None:
            sps = [jnp.where(causal, x, 0.0) for x in sps]
        splits = [_split2(x) for x in sps]
        incs = [jnp.dot(hi, suffix, preferred_element_type=F32)
                + jnp.dot(lo, suffix, preferred_element_type=F32) for hi, lo in splits]
        probs = [jnp.exp(zs[h] - incs[h] - runs[h]) for h in heads]
        if causal is not None:
            probs = [jnp.where(causal, a, 0.0) for a in probs]
        new_accs = [accs[h] + jnp.dot(probs[h].astype(BF16), vbs[h], preferred_element_type=F32)
                    for h in heads]
        new_runs = [runs[h] + incs[h][:, 0:1] for h in heads]
        return tuple(new_accs), tuple(new_runs)

    def live_of(runs):
        return -jnp.min(jnp.minimum(jnp.minimum(runs[0], runs[1]), jnp.minimum(runs[2], runs[3])))

    q_pos = q0 + lax.broadcasted_iota(jnp.int32, (tq, tk), 0)
    k_off = lax.broadcasted_iota(jnp.int32, (tq, tk), 1)
    zeros = lambda w: tuple(jnp.zeros((tq, w), F32) for _ in range(N_HEADS))
    accs, runs = zeros(HEAD_V), zeros(1)
    for m in range(n_masked):
        accs, runs = block(j_top - m, accs, runs, (j_top - m) * tk + k_off < q_pos)

    def cond(carry):
        j, live, _, _ = carry
        return (j >= 0) & (live > SB_DEAD_LOG)

    def body(carry):
        j, _, accs, runs = carry
        accs, runs = block(j, accs, runs, None)
        return j - 1, live_of(runs), accs, runs

    _, _, accs, _ = lax.while_loop(cond, body, (j_top - n_masked, live_of(runs), accs, runs))
    for h in range(N_HEADS):
        o_ref[0, rows, _hs(h)] = accs[h]


def _sb_call(q_src, q_block, k_all, v_all, l, tq, tk, q_offset):
    assert (tq % tk == 0 or tk % tq == 0) and q_offset % max(tq, tk) == 0 and tk % LANES == 0
    b = q_src.shape[0]
    lk = k_all.shape[1]
    n_q = math.gcd(l // tq, SB_BLOCKS_PER_STEP)
    rows = n_q * tq
    return pl.pallas_call(
        functools.partial(_sb_kernel, tq=tq, tk=tk, q_offset=q_offset, n_q=n_q),
        grid=(b, l // rows),
        in_specs=[pl.BlockSpec((1, rows, W_GROUP), lambda i, j: (i, j, q_block)),
                  pl.BlockSpec((1, lk, W_GROUP), lambda i, j: (i, 0, 0)),
                  pl.BlockSpec((1, lk, W_GROUP), lambda i, j: (i, 0, 0))],
        out_specs=pl.BlockSpec((1, rows, W_GROUP), lambda i, j: (i, j, 0)),
        out_shape=jax.ShapeDtypeStruct((b, l, W_GROUP), F32),
        compiler_params=pltpu.CompilerParams(dimension_semantics=("parallel", "parallel"),
                                             vmem_limit_bytes=VMEM_LIMIT),
        name="sb_attention",
    )(q_src, k_all, v_all)


def _cummax_rows(x):
    r = x.shape[0]
    row = lax.broadcasted_iota(jnp.int32, x.shape, 0)
    shift = 1
    while shift < r:
        x = jnp.maximum(x, jnp.where(row >= shift, pltpu.roll(x, shift, 0), -jnp.inf))
        shift *= 2
    return x


def _mlstm_steps(q_ref, k_ref, v_ref, og_ref, g_ref, bias_ref, ng_ref, c0_ref, n0_ref, m0_ref,
                 h_ref, c_out, n_out, m_out, cn_s, m_s, *, c, nb):
    j = pl.program_id(1)

    @pl.when(j == 0)
    def _():
        cn_s[...] = jnp.zeros_like(cn_s)
        m_s[...] = jnp.zeros_like(m_s)
        for i in range(nb):
            for h in range(N_HEADS):
                cn_s[i, _hs(h), _hs(h)] = c0_ref[i, h]
                cn_s[i, _hs(h), W_GROUP + h * HEAD_V:W_GROUP + (h + 1) * HEAD_V] = jnp.broadcast_to(
                    n0_ref[i, h], (HEAD_V, HEAD_V))
            m_s[i, :, 0:N_HEADS] = m0_ref[i]

    yield 'stage'

    seqs = range(nb)
    hs = N_HEADS * c
    lane = lax.broadcasted_iota(jnp.int32, (c, LANES), 1)
    is_f = (lane >= G_MF) & (lane < G_MF + N_HEADS)
    head_lane = lane < N_HEADS
    incl_b = _tri(c).astype(BF16)
    t_idx = lax.broadcasted_iota(jnp.int32, (c, hs), 0)
    s_idx = lax.broadcasted_iota(jnp.int32, (c, hs), 1) & (c - 1)
    incl4 = t_idx >= s_idx
    same_hv = _same_head(hs, W_GROUP, c, HEAD_V)
    same_vv = _same_head(W_GROUP, W_GROUP, HEAD_V, HEAD_V)
    group_sum = same_vv.astype(BF16)

    def expand(lanes, group):
        g = lax.broadcasted_iota(jnp.int32, (LANES, lanes), 0)
        h = lax.broadcasted_iota(jnp.int32, (LANES, lanes), 1) >> (group.bit_length() - 1)
        return (g == h).astype(BF16)

    gi = lax.broadcasted_iota(jnp.int32, (LANES, 2 * LANES), 0)
    li = lax.broadcasted_iota(jnp.int32, (LANES, 2 * LANES), 1)
    to_heads = (jnp.where((li < N_HEADS) & (gi == G_MI + li), 1.0, 0.0)
                - jnp.where((li < N_HEADS) & (gi == G_MF + li), 1.0, 0.0)
                + jnp.where((li >= LANES) & (li < LANES + N_HEADS) & (gi == G_MF + li - LANES), 1.0, 0.0)
                ).astype(BF16)
    row_sel = (lax.broadcasted_iota(jnp.int32, (8, LANES), 1)
               == lax.broadcasted_iota(jnp.int32, (8, LANES), 0)).astype(BF16)
    gates = [g_ref[i] + bias_ref[...] for i in seqs]
    merged = [jnp.where(is_f, _dot01_left(incl_b, jnp.where(is_f, _log_sigmoid(g), 0.0)), g) for g in gates]
    gb = _dot01_right(merged, to_heads)
    yield 'stage'
    g_c = [jnp.where(head_lane, x[:, :LANES], -jnp.inf) for x in gb]
    b_c = [x[:, LANES:] for x in gb]
    m_prev = [m_s[i] for i in seqs]
    r_c = [jnp.maximum(m_prev[i], _cummax_rows(g_c[i])) for i in seqs]
    r_end = [r[c - 1:c, :] for r in r_c]
    si_c = [jnp.exp(m_prev[i] - r_c[i]) for i in seqs]
    emt_c = [jnp.exp(-(b_c[i] + r_c[i])) for i in seqs]
    w_c = [jnp.exp(g_c[i] - r_end[i]) for i in seqs]
    dec_c = [jnp.broadcast_to(jnp.exp(m_prev[i] - r_end[i]), (c, LANES)) for i in seqs]
    yield 'stage'
    zero_pad = lambda x: jnp.where(head_lane, x, 0.0)
    r_hs = _dot01_right([zero_pad(r) for r in r_c], expand(hs, c))
    cols = _dot01_right([zero_pad(x) for i in seqs for x in (si_c[i], emt_c[i], w_c[i], dec_c[i])],
                        expand(W_GROUP, HEAD_V))
    si_hv, emt_hv, w_hv, dec_hv = (cols[n::4] for n in range(4))
    yield 'stage'
    g_rows = [sum(lax.dot_general(row_sel, part, (((1,), (1,)), ((), ())), preferred_element_type=F32)
                  for part in _split3(zero_pad(x))) for x in g_c]
    g_row_hs = [jnp.concatenate([gr[h:h + 1, :] for h in range(N_HEADS)], axis=1) for gr in g_rows]
    d_hs = [jnp.where(incl4, jnp.exp(g_row_hs[i] - r_hs[i]), 0.0) for i in seqs]
    yield 'stage'
    qs = [q_ref[i].astype(BF16) for i in seqs]
    ks = [k_ref[i] * (HEAD_V ** -0.5) for i in seqs]
    vs = [v_ref[i] for i in seqs]
    cns = [cn_s[i] for i in seqs]
    qk = [lax.dot_general(qs[i], _head_blocks(ks[i].astype(BF16), same_hv), (((1,), (1,)), ((), ())),
                          preferred_element_type=F32) for i in seqs]
    q_cn = [jnp.dot(qs[i], cns[i].astype(BF16), preferred_element_type=F32) for i in seqs]
    yield 'stage'
    scs = [(qk[i] * d_hs[i]).astype(BF16) for i in seqs]
    ones_blocks = same_hv.astype(BF16)
    v_ones = [jnp.concatenate([_head_blocks(vs[i].astype(BF16), same_hv), ones_blocks], axis=1)
              for i in seqs]
    sc_v = [jnp.dot(scs[i], v_ones[i], preferred_element_type=F32) for i in seqs]
    yield 'stage'
    nd = [sc_v[i] + jnp.concatenate([si_hv[i], si_hv[i]], axis=1) * q_cn[i] for i in seqs]
    hcs = [nd[i][:, :W_GROUP] / jnp.maximum(jnp.abs(nd[i][:, W_GROUP:]), emt_hv[i]) for i in seqs]
    kws = [ks[i] * w_hv[i] for i in seqs]
    upds = [_dot_tn(kws[i], jnp.concatenate([vs[i], jnp.ones_like(vs[i])], axis=1)) for i in seqs]
    same_cn = jnp.concatenate([same_vv, same_vv], axis=1)
    sum_sq = _group_sums([x * x for x in hcs], group_sum)
    yield 'stage'
    ng = ng_ref[...]
    for i in seqs:
        dec_row = dec_hv[i][0:1, :]
        cn_s[i] = jnp.concatenate([dec_row, dec_row], axis=1) * cns[i] + jnp.where(same_cn, upds[i], 0.0)
        m_s[i] = jnp.where(head_lane[0:1, :], b_c[i][c - 1:c, :] + r_end[i], 0.0)
        h_ref[i] = (hcs[i] * lax.rsqrt(sum_sq[i] * (1.0 / HEAD_V) + EPS) * ng * _sigmoid(og_ref[i]))
    yield 'final'

    @pl.when(j == pl.num_programs(1) - 1)
    def _():
        for i in seqs:
            for h in range(N_HEADS):
                c_out[i, h] = cn_s[i, _hs(h), _hs(h)]
                n_out[i, h] = cn_s[i, _hs(h), W_GROUP + h * HEAD_V:W_GROUP + h * HEAD_V + 1]
            m_out[i] = m_s[i, :, 0:N_HEADS]


def _same_head(rows, lanes, row_group, lane_group):
    rh = lax.broadcasted_iota(jnp.int32, (rows, lanes), 0) >> (row_group.bit_length() - 1)
    lh = lax.broadcasted_iota(jnp.int32, (rows, lanes), 1) >> (lane_group.bit_length() - 1)
    return rh == lh


def _head_blocks(x, same_head):
    tiled = jnp.concatenate([x] * N_HEADS, axis=0)
    return jnp.where(same_head, tiled, jnp.zeros_like(tiled))


def _heads_dot_split(l, r, same_head):
    return _heads_dot_parts(_split2(l), _split2(r), same_head)


def _heads_dot_parts(l_parts, r_parts, same_head):
    l1, l2 = l_parts
    rb1, rb2 = _head_blocks(r_parts[0], same_head), _head_blocks(r_parts[1], same_head)
    m = l1.shape[0]
    y = jnp.dot(jnp.concatenate([l1, l2], axis=0), rb1, preferred_element_type=F32)
    return y[:m] + (y[m:] + jnp.dot(l1, rb2, preferred_element_type=F32))


def _unit_lower_inverses(lows, eye4, same_head, c):
    ts = [eye4 - a for a in lows]
    a_parts = [_split2(a) for a in lows]
    ps = [_heads_dot_parts(ap, ap, same_head) for ap in a_parts]
    yield 'stage'
    for _ in range(c.bit_length() - 3):
        t_parts = [_split2(t) for t in ts]
        p_parts = [_split2(p) for p in ps]
        stacked = [tuple(jnp.concatenate([tp[n], pp[n]], axis=0) for n in range(2))
                   for tp, pp in zip(t_parts, p_parts)]
        ys = [_heads_dot_parts(sp, pp, same_head) for sp, pp in zip(stacked, p_parts)]
        ts = [t + y[:c] for t, y in zip(ts, ys)]
        ps = [y[c:] for y in ys]
        yield 'stage'
    return [t + _heads_dot_split(t, p, same_head) for t, p in zip(ts, ps)]


def _gdn_steps(q_ref, k_ref, v_ref, z_ref, g_ref, bias_ref, alog_ref, ng_ref, s0_ref,
               o_ref, s_out, s_s, *, c, nb):
    j = pl.program_id(1)
    seqs = range(nb)

    @pl.when(j == 0)
    def _():
        s_s[...] = jnp.zeros_like(s_s)
        for i in seqs:
            for h in range(N_HEADS):
                s_s[i, _hs(h), _hs(h)] = s0_ref[i, h]

    yield 'stage'

    hs = N_HEADS * c
    lane = lax.broadcasted_iota(jnp.int32, (c, LANES), 1)
    is_a = (lane >= G_GA) & (lane < G_GA + N_HEADS)
    incl_b = _tri(c).astype(BF16)
    t_idx = lax.broadcasted_iota(jnp.int32, (c, hs), 0)
    s_idx = lax.broadcasted_iota(jnp.int32, (c, hs), 1) & (c - 1)
    incl4, strict4, eye4 = t_idx >= s_idx, t_idx > s_idx, (t_idx == s_idx).astype(F32)
    same_hs = _same_head(hs, hs, c, c)
    same_hv = _same_head(hs, W_GROUP, c, HEAD_V)
    same_vv = _same_head(W_GROUP, W_GROUP, HEAD_V, HEAD_V)
    group_sum = same_vv.astype(BF16)

    def pick(lanes, group):
        g = lax.broadcasted_iota(jnp.int32, (LANES, 2 * lanes), 0)
        l = lax.broadcasted_iota(jnp.int32, (LANES, 2 * lanes), 1)
        h = (l & (lanes - 1)) >> (group.bit_length() - 1)
        return (g == jnp.where(l < lanes, G_GA, G_GB) + h).astype(BF16)

    pick_hs = pick(hs, c)
    row_sel = (lax.broadcasted_iota(jnp.int32, (8, LANES), 1)
               == G_GA + lax.broadcasted_iota(jnp.int32, (8, LANES), 0)).astype(BF16)
    gates = [g_ref[i] + bias_ref[...] for i in seqs]
    xgs = [jnp.where(is_a, -jnp.exp(alog_ref[...]) * _softplus(g), _sigmoid(g)) for g in gates]
    csums = [jnp.where(is_a, _dot01_left(incl_b, xg), xg) for xg in xgs]
    cols_hs = _dot01_right(csums, pick_hs)
    yield 'stage'
    bc_hs, beta_hs = [x[:, :hs] for x in cols_hs], [x[:, hs:] for x in cols_hs]
    if c == HEAD_V:
        bc_hv, beta_hv = bc_hs, beta_hs
    else:
        cols_hv = _dot01_right(csums, pick(W_GROUP, HEAD_V))
        bc_hv, beta_hv = [x[:, :W_GROUP] for x in cols_hv], [x[:, W_GROUP:] for x in cols_hv]
    b_rows = [sum(lax.dot_general(row_sel, part, (((1,), (1,)), ((), ())), preferred_element_type=F32)
                  for part in _split3(cs)) for cs in csums]
    br_hs = [jnp.concatenate([br[h:h + 1, :] for h in range(N_HEADS)], axis=1) for br in b_rows]
    decays = [jnp.exp(jnp.where(incl4, bc_hs[i] - br_hs[i], 0.0)) for i in seqs]
    yield 'stage'
    qk_raw = [jnp.concatenate([q_ref[i], k_ref[i]], axis=0) for i in seqs]
    norms = _group_sums([x * x for x in qk_raw], group_sum)
    qk_n = [qk_raw[i] * lax.rsqrt(norms[i] + EPS) for i in seqs]
    qs = [x[:c] * (HEAD_V ** -0.5) for x in qk_n]
    ks = [x[c:] for x in qk_n]
    vs = [v_ref[i] for i in seqs]
    kq = [jnp.concatenate([ks[i], qs[i]], axis=0).astype(BF16) for i in seqs]
    yield 'stage'
    states = [s_s[i] for i in seqs]
    scores = [lax.dot_general(kq[i], _head_blocks(ks[i].astype(BF16), same_hv),
                              (((1,), (1,)), ((), ())), preferred_element_type=F32) for i in seqs]
    on_state = [jnp.dot(kq[i], states[i].astype(BF16), preferred_element_type=F32) for i in seqs]
    yield 'stage'
    lows = [jnp.where(strict4, beta_hs[i] * decays[i] * scores[i][:c], 0.0) for i in seqs]
    invs = yield from _unit_lower_inverses(lows, eye4, same_hs, c)
    ebs = [jnp.exp(bc_hv[i]) for i in seqs]
    rhss = [beta_hv[i] * (vs[i] - ebs[i] * on_state[i][:c]) for i in seqs]
    yield 'stage'
    us = [_heads_dot_split(invs[i], rhss[i], same_hv) for i in seqs]
    yield 'stage'
    qkm = [jnp.where(incl4, decays[i] * scores[i][c:], 0.0).astype(BF16) for i in seqs]
    outs = [ebs[i] * on_state[i][c:]
            + jnp.dot(qkm[i], _head_blocks(us[i].astype(BF16), same_hv), preferred_element_type=F32)
            for i in seqs]
    b_ends = [bc_hv[i][c - 1:c, :] for i in seqs]
    upds = [jnp.where(same_vv, _dot_tn(ks[i] * jnp.exp(b_ends[i] - bc_hv[i]), us[i]), 0.0) for i in seqs]
    yield 'stage'
    ng = ng_ref[...]
    sum_sq = _group_sums([o * o for o in outs], group_sum)
    for i in seqs:
        s_s[i] = jnp.exp(b_ends[i]) * states[i] + upds[i]
        o_ref[i] = (outs[i] * lax.rsqrt(sum_sq[i] * (1.0 / HEAD_V) + EPS) * ng * _silu(z_ref[i]))
    yield 'final'

    @pl.when(j == pl.num_programs(1) - 1)
    def _():
        for i in seqs:
            for h in range(N_HEADS):
                s_out[i, h] = s_s[i, _hs(h), _hs(h)]


GLA_SUB = 8


def _gla_steps(q_ref, k_ref, v_ref, r_ref, g_ref, w2_ref, gb_ref, ng_ref, s0_ref,
               o_ref, s_out, s_s, *, c, nb):
    j = pl.program_id(1)
    dk_all = N_HEADS * GLA_DK
    hs = N_HEADS * c
    nsb = c // GLA_SUB
    seqs = range(nb)

    @pl.when(j == 0)
    def _():
        s_s[...] = jnp.zeros_like(s_s)
        for i in seqs:
            for h in range(N_HEADS):
                s_s[i, h * GLA_DK:(h + 1) * GLA_DK, _hs(h)] = s0_ref[i, h]

    yield 'stage'

    incl_b = _tri(c).astype(BF16)
    er = lax.broadcasted_iota(jnp.int32, (2 * c, c), 0)
    ec = lax.broadcasted_iota(jnp.int32, (2 * c, c), 1)
    et = er & (c - 1)
    edge_sel = (ec == jnp.where(er < c, (et & ~(GLA_SUB - 1)) - 1, et | (GLA_SUB - 1))).astype(BF16)
    t_idx = lax.broadcasted_iota(jnp.int32, (c, hs), 0)
    s_idx = lax.broadcasted_iota(jnp.int32, (c, hs), 1) & (c - 1)
    blk_dist = (t_idx >> 3) - (s_idx >> 3)
    diag_off = t_idx - s_idx
    same_hk = _same_head(hs, dk_all, c, GLA_DK)
    same_hv = _same_head(hs, W_GROUP, c, HEAD_V)
    same_kv = _same_head(dk_all, W_GROUP, GLA_DK, HEAD_V)
    head_rep = _same_head(dk_all, hs, GLA_DK, c).astype(BF16)
    group_sum = _same_head(W_GROUP, W_GROUP, HEAD_V, HEAD_V).astype(BF16)
    row8 = lax.broadcasted_iota(jnp.int32, (c, dk_all), 0) & (GLA_SUB - 1)
    last_row = lax.broadcasted_iota(jnp.int32, (c, dk_all), 0) == c - 1

    log_as = [_log_sigmoid(_dot(g_ref[i], w2_ref[...]) + gb_ref[...]) / GLA_TAU for i in seqs]
    bs = [_dot01_left(incl_b, la) for la in log_as]
    edges = [_dot01_left(edge_sel, b) for b in bs]
    yield 'stage'
    e_prev = [e[:c] for e in edges]
    e_own = [e[c:] for e in edges]
    qs = [q_ref[i] * (GLA_DK ** -0.5) for i in seqs]
    ks = [k_ref[i] for i in seqs]
    vs = [v_ref[i] for i in seqs]
    q_hat = [qs[i] * jnp.exp(bs[i] - e_prev[i]) for i in seqs]
    k_hat = [ks[i] * jnp.exp(e_own[i] - bs[i]) for i in seqs]
    yield 'stage'

    def q_for_distance(i, m):
        if m == 0:
            return q_hat[i]
        rows = GLA_SUB * m
        shifted = jnp.concatenate([jnp.zeros((rows, dk_all), F32), e_prev[i][:c - rows]], axis=0)
        return q_hat[i] * jnp.exp(e_prev[i] - shifted)

    a_parts = []
    for i in seqs:
        lhs = jnp.concatenate([q_for_distance(i, m) for m in range(nsb - 1)], axis=0).astype(BF16)
        prod = lax.dot_general(lhs, _head_blocks(k_hat[i].astype(BF16), same_hk),
                               (((1,), (1,)), ((), ())), preferred_element_type=F32)
        a_parts.append(sum(jnp.where(blk_dist == m + 1, prod[m * c:(m + 1) * c], 0.0)
                           for m in range(nsb - 1)))
        yield 'stage'

    def rot8(x, d):
        return x if d == 0 else pltpu.roll(x, d, 0)

    for i in seqs:
        ws = [jnp.where(row8 >= d, qs[i] * rot8(ks[i], d) * jnp.exp(bs[i] - rot8(bs[i], d)), 0.0)
              for d in range(GLA_SUB)]
        prod = jnp.dot(jnp.concatenate(ws, axis=0).astype(BF16), head_rep, preferred_element_type=F32)
        a_parts[i] = a_parts[i] + sum(jnp.where(diag_off == d, prod[d * c:(d + 1) * c], 0.0)
                                      for d in range(GLA_SUB))
        yield 'stage'
    states = [s_s[i] for i in seqs]
    outs = [jnp.dot(a_parts[i].astype(BF16), _head_blocks(vs[i].astype(BF16), same_hv),
                    preferred_element_type=F32)
            + _dot(qs[i] * jnp.exp(bs[i]), states[i]) for i in seqs]
    yield 'stage'
    b_last = [b[c - 1:c, :] for b in bs]
    upds = [jnp.where(same_kv, _dot_tn(ks[i] * jnp.exp(b_last[i] - bs[i]), vs[i]), 0.0) for i in seqs]
    ones_cv = jnp.ones((c, W_GROUP), BF16)
    decay_rows = [sum(lax.dot_general(part, ones_cv, (((0,), (0,)), ((), ())), preferred_element_type=F32)
                      for part in _split3(jnp.where(last_row, b, 0.0))) for b in bs]
    yield 'stage'
    sum_sq = _group_sums([o * o for o in outs], group_sum)
    ng = ng_ref[...]
    for i in seqs:
        s_s[i] = states[i] * jnp.exp(decay_rows[i]) + upds[i]
        o_ref[i] = outs[i] * lax.rsqrt(sum_sq[i] * (1.0 / HEAD_V) + EPS) * ng * _silu(r_ref[i])
    yield 'final'

    @pl.when(j == pl.num_programs(1) - 1)
    def _():
        for i in seqs:
            for h in range(N_HEADS):
                s_out[i, h] = s_s[i, h * GLA_DK:(h + 1) * GLA_DK, _hs(h)]


def _interleave(steps):
    active, parked = list(steps), []
    while active:
        for g in list(active):
            if next(g) == 'final':
                active.remove(g)
                parked.append(g)
    for g in parked:
        for _ in g:
            pass


def _mixers_kernel(mq, mk, mv, mo, gq, gk, gv, gz, lq, lk, lv, lr, g_ref, bias_ref, alog_ref,
                   ml_ng, gdn_ng, w2_ref, gb_ref, gla_ng, c0, n0, m0, gs0, ls0,
                   ml_h, c_out, n_out, m_out, gdn_o, gs_out, gla_o, ls_out,
                   cn_s, m_s, gs_s, ls_s, *, c, nb):
    _interleave([
        _mlstm_steps(mq, mk, mv, mo, g_ref, bias_ref, ml_ng, c0, n0, m0, ml_h, c_out, n_out, m_out,
                     cn_s, m_s, c=c, nb=nb),
        _gdn_steps(gq, gk, gv, gz, g_ref, bias_ref, alog_ref, gdn_ng, gs0, gdn_o, gs_out, gs_s,
                   c=c, nb=nb),
        _gla_steps(lq, lk, lv, lr, g_ref, w2_ref, gb_ref, gla_ng, ls0, gla_o, ls_out, ls_s,
                   c=c, nb=nb)])


def _mixers_call(p, gate_bias, alog_row, ml_ng, gdn_ng, w2_pad, gla_gb, gla_ng,
                 ml_c, ml_n, ml_m, gdn_s, gla_s, c, nb):
    b, l, _ = p.shape
    nb = math.gcd(nb, b)
    nc = l // c
    dk_all = N_HEADS * GLA_DK
    blk = lambda cb: pl.BlockSpec((nb, c, W_GROUP), lambda i, j, cb=cb: (i, j, cb))
    blk128 = lambda cb: pl.BlockSpec((nb, c, LANES), lambda i, j, cb=cb: (i, j, cb))
    row = lambda w: pl.BlockSpec((1, w), lambda i, j: (0, 0))
    full = lambda a: pl.BlockSpec((nb,) + a.shape[1:], lambda i, j: (i,) + (0,) * (a.ndim - 1))
    tok = pl.BlockSpec((nb, c, W_GROUP), lambda i, j: (i, j, 0))
    ml_n = ml_n[..., None]
    ml_m = ml_m.reshape(b, 1, N_HEADS)
    states = (ml_c, ml_n, ml_m, gdn_s, gla_s)
    sds = lambda a: jax.ShapeDtypeStruct(a.shape, F32)
    act = jax.ShapeDtypeStruct((b, l, W_GROUP), F32)
    o_ml, ml_c, ml_n, ml_m, o_gdn, gdn_s, o_gla, gla_s = pl.pallas_call(
        functools.partial(_mixers_kernel, c=c, nb=nb),
        grid=(b // nb, nc),
        in_specs=[blk(PB_MQ), blk(PB_MK), blk(PB_MV), blk(PB_MO),
                  blk(PB_GQ), blk(PB_GK), blk(PB_GV), blk(PB_GZ),
                  blk128(PB128_LQ), blk128(PB128_LK), blk(PB_LV), blk(PB_LR),
                  blk128(PB128_GATES), row(LANES), row(LANES), row(W_GROUP), row(W_GROUP),
                  pl.BlockSpec((LANES, dk_all), lambda i, j: (0, 0)), row(dk_all), row(W_GROUP)]
                 + [full(a) for a in states],
        out_specs=[tok, full(ml_c), full(ml_n), full(ml_m), tok, full(gdn_s), tok, full(gla_s)],
        out_shape=[act, sds(ml_c), sds(ml_n), sds(ml_m), act, sds(gdn_s), act, sds(gla_s)],
        scratch_shapes=[pltpu.VMEM((nb, W_GROUP, 2 * W_GROUP), F32), pltpu.VMEM((nb, 1, LANES), F32),
                        pltpu.VMEM((nb, W_GROUP, W_GROUP), F32), pltpu.VMEM((nb, dk_all, W_GROUP), F32)],
        compiler_params=pltpu.CompilerParams(dimension_semantics=("parallel", "arbitrary"),
                                             vmem_limit_bytes=VMEM_LIMIT),
        name="mixers",
    )(*([p] * 13), gate_bias, alog_row, ml_ng, gdn_ng, w2_pad, gla_gb, gla_ng, *states)
    return o_ml, ml_c, ml_n[..., 0], ml_m[:, 0, :], o_gdn, gdn_s, o_gla, gla_s


def _outffn_kernel(x_ref, a_ref, b_ref, c_ref, d_ref, mod_ref, g2_ref, gf_ref, wo_ref, w1_ref, w2_ref,
                   o_ref, *, ff_chunk, final):
    ns, tm, d = x_ref.shape
    flat = lambda y: y.reshape(ns * tm, y.shape[-1])
    per_seq = lambda y: y.reshape(ns, tm, y.shape[-1])
    mod = mod_ref[...]
    mixed = flat(jnp.concatenate([a_ref[...], b_ref[...], c_ref[...], d_ref[...]], axis=-1)).astype(BF16)
    x = x_ref[...] + mod[:, 2:3, :] * per_seq(jnp.dot(mixed, wo_ref[...], preferred_element_type=F32))
    h = flat(_modulated_norm(x, g2_ref[...], mod[:, 4:5, :], mod[:, 3:4, :])).astype(BF16)
    d_ff = w1_ref.shape[1]
    acc = jnp.zeros((ns * tm, d), F32)
    for f0 in range(0, d_ff, ff_chunk):
        a = jnp.maximum(jnp.dot(h, w1_ref[:, f0:f0 + ff_chunk], preferred_element_type=F32), 0.0)
        acc = acc + jnp.dot((a * a).astype(BF16), w2_ref[f0:f0 + ff_chunk, :],
                            preferred_element_type=F32)
    x = x + mod[:, 5:6, :] * per_seq(acc)
    if final:
        x = x * lax.rsqrt(jnp.mean(x * x, axis=-1, keepdims=True) + EPS) * gf_ref[...]
    o_ref[...] = x


def _outffn_call(x, mixers, mod, g2, gf, wo, w1, w2, tm, ns, final):
    b, l, d = x.shape
    d_ff = w1.shape[1]
    tok = lambda w: pl.BlockSpec((ns, tm, w), lambda i, j: (i, j, 0))
    const = lambda shape: pl.BlockSpec(shape, lambda i, j: (0,) * len(shape),
                                       pipeline_mode=pl.Buffered(1))
    return pl.pallas_call(
        functools.partial(_outffn_kernel, ff_chunk=1024, final=final),
        grid=(b // ns, l // tm),
        in_specs=[tok(d), tok(W_GROUP), tok(W_GROUP), tok(W_GROUP), tok(W_GROUP),
                  pl.BlockSpec((ns, 6, d), lambda i, j: (i, 0, 0)),
                  pl.BlockSpec((1, d), lambda i, j: (0, 0)),
                  pl.BlockSpec((1, d), lambda i, j: (0, 0)),
                  const((d, d)), const((d, d_ff)), const((d_ff, d))],
        out_specs=tok(d),
        out_shape=jax.ShapeDtypeStruct((b, l, d), F32),
        compiler_params=pltpu.CompilerParams(dimension_semantics=("parallel", "parallel"),
                                             vmem_limit_bytes=VMEM_LIMIT),
        name="outproj_ffn",
    )(x, *mixers, mod, g2, gf, wo, w1, w2)


def _permute_w_in(w_in):
    d = w_in.shape[0]
    sizes = (256, 256, 256, 256, 256, 256, 4, 4, 256, 768, 4, 4, 256, 128, 128, 256, 16, 256)
    offs = [0]
    for s in sizes:
        offs.append(offs[-1] + s)
    (sq, sk, sv, mq, mk, mv, mi, mf, mo, gqkv, gb, ga, gz, lq, lk, lv, lg, lr) = [
        w_in[:, offs[i]:offs[i + 1]] for i in range(len(sizes))]
    pad = jnp.zeros((d, LANES - (4 * N_HEADS + GLA_RANK)), w_in.dtype)
    return jnp.concatenate([sk, sv, sq, mq, mk, mv, mo, gqkv, gz, lv, lr, lq, lk,
                            mi, mf, gb, ga, lg, pad], axis=1)


def _row128(pieces):
    row = jnp.zeros((LANES,), F32)
    for off, vec in pieces:
        row = row.at[off:off + vec.shape[0]].set(vec)
    return row.reshape(1, LANES)


DENSE_TILE_ROWS = 512


class _Tiles(NamedTuple):
    tm: int
    dense_seqs: int
    sb_tq: int
    sb_tk: int
    chunk: int


def _tiles(b, l):
    tm = min(DENSE_TILE_ROWS, l)
    return _Tiles(tm=tm, dense_seqs=math.gcd(b, DENSE_TILE_ROWS // tm), sb_tq=min(256, l),
                  sb_tk=max(min(256, l), LANES), chunk=min(64, l))


def _layer(x, mod, lw, states, tiles, final, final_g):
    tm, dense_seqs, sb_tq, sb_tk, chunk = tiles
    (n1, n2, w_in_p, gate_bias, ml_ng, conv_w, alog_row, gdn_ng, w2_pad, gla_gb, gla_ng,
     w_out, w_ff1, w_ff2) = lw
    (sb_k_past, sb_v_past, ml_c, ml_n, ml_m, gdn_s, gdn_buf, gla_s) = states
    b, l, _ = x.shape
    k_new, v_new, p, gdn_buf = _inproj_call(x, mod, n1, w_in_p, conv_w, gdn_buf, tm, dense_seqs)
    if sb_k_past is None:
        k_all, v_all, q_offset = k_new, v_new, 0
    else:
        past = sb_k_past.shape[1]
        lk = -(-(past + l) // sb_tk) * sb_tk
        padz = jnp.zeros((b, lk - past - l, W_GROUP), F32)
        k_all = jnp.concatenate([sb_k_past.reshape(b, past, W_GROUP), k_new, padz], axis=1)
        v_all = jnp.concatenate([sb_v_past.reshape(b, past, W_GROUP), v_new, padz], axis=1)
        q_offset = past
    o_sb = _sb_call(p, PB_SQ, k_all, v_all, l, sb_tq, sb_tk, q_offset)
    o_ml, ml_c, ml_n, ml_m, o_gdn, gdn_s, o_gla, gla_s = _mixers_call(
        p, gate_bias, alog_row, ml_ng, gdn_ng, w2_pad, gla_gb, gla_ng, ml_c, ml_n, ml_m, gdn_s, gla_s,
        chunk, MIXER_SEQS_PER_STEP)
    x = _outffn_call(x, (o_sb, o_ml, o_gdn, o_gla), mod, n2, final_g, w_out, w_ff1, w_ff2, tm,
                     dense_seqs, final)
    hk = lambda a: a.reshape(b, l, N_HEADS, HEAD_V)
    return x, (hk(k_new), hk(v_new), ml_c, ml_n, ml_m, gdn_s, gdn_buf, gla_s)


def kernel(x_prompt, x_sample, cache_sb_k, cache_sb_v, state_mlstm_C, state_mlstm_n, state_mlstm_m, state_gdn_S, state_gdn_conv, state_gla_S, c_prompt, c_sample, norm1_g, norm2_g, w_ada, b_ada, w_in, mlstm_i_bias, mlstm_f_bias, mlstm_norm_g, gdn_conv_w, gdn_a_log, gdn_dt_bias, gdn_norm_g, gla_w_gate2, gla_gate_bias, gla_norm_g, w_out, w_ff1, w_ff2, final_g):
    depth = w_in.shape[0]
    bp, lp, d = x_prompt.shape
    bs, ls, _ = x_sample.shape
    dk_all = N_HEADS * GLA_DK
    mod_all = _ada_call(jnp.concatenate([c_prompt, c_sample], axis=0), w_ada, b_ada)
    mod_all = mod_all.reshape(depth, bp + bs, 6, d)
    final_row = final_g.reshape(1, d)
    xp, xs = x_prompt, x_sample
    p_list, s_list = [], []
    for l in range(depth):
        gate_bias = _row128([(G_MI, mlstm_i_bias[l]), (G_MF, mlstm_f_bias[l]), (G_GA, gdn_dt_bias[l])])
        alog_row = _row128([(G_GA, gdn_a_log[l])])
        w2_pad = jnp.zeros((LANES, dk_all), F32).at[G_LG:G_LG + GLA_RANK, :].set(gla_w_gate2[l])
        lw = (norm1_g[l].reshape(1, d), norm2_g[l].reshape(1, d), _permute_w_in(w_in[l]).astype(BF16),
              gate_bias, mlstm_norm_g[l].reshape(1, W_GROUP), gdn_conv_w[l], alog_row,
              gdn_norm_g[l].reshape(1, W_GROUP), w2_pad.astype(BF16),
              gla_gate_bias[l].reshape(1, dk_all), gla_norm_g[l].reshape(1, W_GROUP),
              w_out[l].astype(BF16), w_ff1[l].astype(BF16), w_ff2[l].astype(BF16))
        fresh = (None, None, jnp.zeros((bp,) + state_mlstm_C.shape[2:], F32),
                 jnp.zeros((bp,) + state_mlstm_n.shape[2:], F32),
                 jnp.zeros((bp,) + state_mlstm_m.shape[2:], F32),
                 jnp.zeros((bp,) + state_gdn_S.shape[2:], F32),
                 jnp.zeros((bp,) + state_gdn_conv.shape[2:], F32),
                 jnp.zeros((bp,) + state_gla_S.shape[2:], F32))
        past = (cache_sb_k[l], cache_sb_v[l], state_mlstm_C[l], state_mlstm_n[l], state_mlstm_m[l],
                state_gdn_S[l], state_gdn_conv[l], state_gla_S[l])
        final = l == depth - 1
        xp, st_p = _layer(xp, mod_all[l, :bp], lw, fresh, _tiles(bp, lp), final, final_row)
        xs, st_s = _layer(xs, mod_all[l, bp:], lw, past, _tiles(bs, ls), final, final_row)
        p_list.append(st_p)
        s_list.append(st_s)
    stacked_p = [jnp.stack([st[i] for st in p_list]) for i in range(8)]
    stacked_s = [jnp.stack([st[i] for st in s_list]) for i in range(8)]
    return (xp, xs, *stacked_p, *stacked_s)
```

```python
import functools
import math
from typing import NamedTuple

import jax
import jax.numpy as jnp
from jax import lax
from jax.experimental import pallas as pl
from jax.experimental.pallas import tpu as pltpu

F32 = jnp.float32
BF16 = jnp.bfloat16
EPS = 1e-6
N_HEADS = 4
HEAD_V = 64
GLA_DK = 32
CONV_W = 4
GLA_TAU = 16.0
W_GROUP = N_HEADS * HEAD_V
LANES = 128
SUBLANES = 8
VMEM_LIMIT = 56 * 1024 * 1024
MIXER_SEQS_PER_STEP = 8

PB_SQ, PB_MQ, PB_MK, PB_MV, PB_MO, PB_GQ, PB_GK, PB_GV, PB_GZ, PB_LV, PB_LR = range(11)
PB128_LQ, PB128_LK, PB128_GATES = 22, 23, 24
P_WIDTH = 25 * LANES
G_MI, G_MF, G_GB, G_GA, G_LG = 0, 4, 8, 12, 16
GLA_RANK = 16


def _dot(a, b):
    return jnp.dot(a.astype(BF16), b.astype(BF16), preferred_element_type=F32)


def _dot_tn(a, b):
    return lax.dot_general(a.astype(BF16), b.astype(BF16), (((0,), (0,)), ((), ())),
                           preferred_element_type=F32)


def _split3(x):
    x1 = x.astype(BF16)
    r1 = x - x1.astype(F32)
    x2 = r1.astype(BF16)
    x3 = (r1 - x2.astype(F32)).astype(BF16)
    return x1, x2, x3


def _split2(x):
    x1 = x.astype(BF16)
    x2 = (x - x1.astype(F32)).astype(BF16)
    return x1, x2


def _dot01_left(m01, x):
    return sum(jnp.dot(m01, p, preferred_element_type=F32) for p in _split3(x))


def _dot01_right(xs, m01):
    n = xs[0].shape[0]
    stacked = jnp.concatenate([part for x in xs for part in _split3(x)], axis=0)
    y = jnp.dot(stacked, m01, preferred_element_type=F32)
    return [y[3 * i * n:(3 * i + 1) * n] + (y[(3 * i + 1) * n:(3 * i + 2) * n]
                                             + y[(3 * i + 2) * n:(3 * i + 3) * n])
            for i in range(len(xs))]


def _group_sums(xs, group_sum):
    n = xs[0].shape[0]
    y = jnp.dot(jnp.concatenate([x.astype(BF16) for x in xs], axis=0), group_sum,
                preferred_element_type=F32)
    return [y[i * n:(i + 1) * n] for i in range(len(xs))]


def _log_sigmoid(x):
    return jnp.minimum(x, 0.0) - jnp.log(1.0 + jnp.exp(-jnp.abs(x)))


def _sigmoid(x):
    return 1.0 / (1.0 + jnp.exp(-x))


def _silu(x):
    return x * _sigmoid(x)


def _softplus(x):
    return jnp.maximum(x, 0.0) + jnp.log(1.0 + jnp.exp(-jnp.abs(x)))


def _tri(c):
    r = lax.broadcasted_iota(jnp.int32, (c, c), 0)
    s = lax.broadcasted_iota(jnp.int32, (c, c), 1)
    return r >= s


def _hs(h, w=HEAD_V):
    return slice(h * w, (h + 1) * w)


def _ada_kernel(c_ref, w_ref, b_ref, o_ref):
    o_ref[0] = _dot(_silu(c_ref[...]), w_ref[0]) + b_ref[0]


def _ada_call(c_all, w_ada, b_ada, tn=512):
    depth, d, n = w_ada.shape
    rows = c_all.shape[0]
    return pl.pallas_call(
        _ada_kernel,
        grid=(depth, n // tn),
        in_specs=[pl.BlockSpec((rows, d), lambda l, j: (0, 0)),
                  pl.BlockSpec((1, d, tn), lambda l, j: (l, 0, j)),
                  pl.BlockSpec((1, 1, tn), lambda l, j: (l, 0, j))],
        out_specs=pl.BlockSpec((1, rows, tn), lambda l, j: (l, 0, j)),
        out_shape=jax.ShapeDtypeStruct((depth, rows, n), F32),
        compiler_params=pltpu.CompilerParams(dimension_semantics=("parallel", "parallel"),
                                             vmem_limit_bytes=VMEM_LIMIT),
        name="ada_mod",
    )(c_all, w_ada, b_ada.reshape(depth, 1, n))


def _modulated_norm(x, g_row, scale_row, shift_row):
    y = x * lax.rsqrt(jnp.mean(x * x, axis=-1, keepdims=True) + EPS) * g_row
    return y * (1.0 + scale_row) + shift_row


def _inproj_kernel(x_ref, mod_ref, g_ref, w_ref, cw_ref, cb0_ref, k_ref, v_ref, p_ref, cb_out, xw_s,
                   *, col_chunk):
    j = pl.program_id(1)
    ns, tm = x_ref.shape[0], x_ref.shape[1]
    conv_lo, conv_hi = PB_GQ * W_GROUP, (PB_GV + 1) * W_GROUP

    @pl.when(j == 0)
    def _():
        xw_s[:, 0:SUBLANES, :] = cb0_ref[...]

    mod = mod_ref[...]
    h = _modulated_norm(x_ref[...], g_ref[...], mod[:, 1:2, :], mod[:, 0:1, :])
    h = h.reshape(ns * tm, h.shape[-1]).astype(BF16)
    per_seq = lambda y: y.reshape(ns, tm, y.shape[-1])
    k_ref[...] = per_seq(jnp.dot(h, w_ref[:, 0:W_GROUP], preferred_element_type=F32))
    v_ref[...] = per_seq(jnp.dot(h, w_ref[:, W_GROUP:2 * W_GROUP], preferred_element_type=F32))
    base = 2 * W_GROUP
    bounds = sorted(set(range(0, conv_lo, col_chunk)) | {conv_lo, conv_hi}
                    | set(range(conv_hi, P_WIDTH, col_chunk)) | {P_WIDTH})
    for n0, n1 in zip(bounds[:-1], bounds[1:]):
        if n0 == conv_lo:
            continue
        p_ref[:, :, n0:n1] = per_seq(jnp.dot(h, w_ref[:, base + n0:base + n1],
                                             preferred_element_type=F32))
    raw = per_seq(jnp.dot(h, w_ref[:, base + conv_lo:base + conv_hi], preferred_element_type=F32))
    xw_s[:, SUBLANES:SUBLANES + tm, :] = raw
    cw = cw_ref[...]
    y = raw * cw[CONV_W - 1:CONV_W, :]
    for d in range(1, CONV_W):
        y = y + xw_s[:, SUBLANES - d:SUBLANES - d + tm, :] * cw[CONV_W - 1 - d:CONV_W - d, :]
    p_ref[:, :, conv_lo:conv_hi] = _silu(y)
    xw_s[:, 0:SUBLANES, :] = raw[:, tm - SUBLANES:tm, :]

    @pl.when(j == pl.num_programs(1) - 1)
    def _():
        cb_out[...] = raw[:, tm - SUBLANES:tm, :]


def _inproj_call(x, mod, g, w_perm, conv_w, conv0, tm, ns):
    b, l, d = x.shape
    n_all = w_perm.shape[1]
    conv_dim = conv_w.shape[1]
    pad_rows = SUBLANES - (CONV_W - 1)
    conv0_pad = jnp.concatenate([jnp.zeros((b, pad_rows, conv_dim), F32), conv0], axis=1)
    carry = pl.BlockSpec((ns, SUBLANES, conv_dim), lambda i, j: (i, 0, 0))
    k, v, p, conv_new = pl.pallas_call(
        functools.partial(_inproj_kernel, col_chunk=512),
        grid=(b // ns, l // tm),
        in_specs=[pl.BlockSpec((ns, tm, d), lambda i, j: (i, j, 0)),
                  pl.BlockSpec((ns, 6, d), lambda i, j: (i, 0, 0)),
                  pl.BlockSpec((1, d), lambda i, j: (0, 0)),
                  pl.BlockSpec((d, n_all), lambda i, j: (0, 0), pipeline_mode=pl.Buffered(1)),
                  pl.BlockSpec((CONV_W, conv_dim), lambda i, j: (0, 0)),
                  carry],
        out_specs=[pl.BlockSpec((ns, tm, W_GROUP), lambda i, j: (i, j, 0)),
                   pl.BlockSpec((ns, tm, W_GROUP), lambda i, j: (i, j, 0)),
                   pl.BlockSpec((ns, tm, P_WIDTH), lambda i, j: (i, j, 0)),
                   carry],
        out_shape=[jax.ShapeDtypeStruct((b, l, W_GROUP), F32),
                   jax.ShapeDtypeStruct((b, l, W_GROUP), F32),
                   jax.ShapeDtypeStruct((b, l, P_WIDTH), F32),
                   jax.ShapeDtypeStruct((b, SUBLANES, conv_dim), F32)],
        scratch_shapes=[pltpu.VMEM((ns, tm + SUBLANES, conv_dim), F32)],
        compiler_params=pltpu.CompilerParams(dimension_semantics=("parallel", "arbitrary"),
                                             vmem_limit_bytes=VMEM_LIMIT),
        name="norm_inproj",
    )(x, mod, g, w_perm, conv_w, conv0_pad)
    return k, v, p, conv_new[:, pad_rows:, :]


SB_DEAD_LOG = -104.0
SB_BLOCKS_PER_STEP = 4


def _sb_kernel(q_ref, k_ref, v_ref, o_ref, *, tq, tk, q_offset, n_q):
    for qb in range(n_q):
        _sb_query_block(q_ref, k_ref, v_ref, o_ref, pl.program_id(1) * n_q + qb,
                        slice(qb * tq, (qb + 1) * tq), tq=tq, tk=tk, q_offset=q_offset)


def _sb_query_block(q_ref, k_ref, v_ref, o_ref, i, rows, *, tq, tk, q_offset):
    q0 = q_offset + i * tq
    j_top = (q0 + tq - 1) // tk
    n_masked = max(tq // tk, 1)
    suffix = (lax.broadcasted_iota(jnp.int32, (tk, tk), 0)
              >= lax.broadcasted_iota(jnp.int32, (tk, tk), 1)).astype(BF16)
    q_all = q_ref[0, rows, :] * (HEAD_V ** -0.5)
    qs = [q_all[:, _hs(h)].astype(BF16) for h in range(N_HEADS)]

    def block(j, accs, runs, causal):
        s0 = pl.multiple_of(j * tk, tk)
        heads = range(N_HEADS)
        kbs = [k_ref[0, pl.ds(s0, tk), _hs(h)].astype(BF16) for h in heads]
        vbs = [v_ref[0, pl.ds(s0, tk), _hs(h)].astype(BF16) for h in heads]
        zs = [lax.dot_general(qs[h], kbs[h], (((1,), (1,)), ((), ())), preferred_element_type=F32)
              for h in heads]
        sps = [_softplus(z) for z in zs]
        if causal is not None:
            sps = [jnp.where(causal, x, 0.0) for x in sps]
        splits = [_split2(x) for x in sps]
        incs = [jnp.dot(hi, suffix, preferred_element_type=F32)
                + jnp.dot(lo, suffix, preferred_element_type=F32) for hi, lo in splits]
        probs = [jnp.exp(zs[h] - incs[h] - runs[h]) for h in heads]
        if causal is not None:
            probs = [jnp.where(causal, a, 0.0) for a in probs]
        new_accs = [accs[h] + jnp.dot(probs[h].astype(BF16), vbs[h], preferred_element_type=F32)
                    for h in heads]
        new_runs = [runs[h] + incs[h][:, 0:1] for h in heads]
        return tuple(new_accs), tuple(new_runs)

    def live_of(runs):
        return -jnp.min(jnp.minimum(jnp.minimum(runs[0], runs[1]), jnp.minimum(runs[2], runs[3])))

    q_pos = q0 + lax.broadcasted_iota(jnp.int32, (tq, tk), 0)
    k_off = lax.broadcasted_iota(jnp.int32, (tq, tk), 1)
    zeros = lambda w: tuple(jnp.zeros((tq, w), F32) for _ in range(N_HEADS))
    accs, runs = zeros(HEAD_V), zeros(1)
    for m in range(n_masked):
        accs, runs = block(j_top - m, accs, runs, (j_top - m) * tk + k_off < q_pos)

    def cond(carry):
        j, live, _, _ = carry
        return (j >= 0) & (live > SB_DEAD_LOG)

    def body(carry):
        j, _, accs, runs = carry
        accs, runs = block(j, accs, runs, None)
        return j - 1, live_of(runs), accs, runs

    _, _, accs, _ = lax.while_loop(cond, body, (j_top - n_masked, live_of(runs), accs, runs))
    for h in range(N_HEADS):
        o_ref[0, rows, _hs(h)] = accs[h]


def _sb_call(q_src, q_block, k_all, v_all, l, tq, tk, q_offset):
    assert (tq % tk == 0 or tk % tq == 0) and q_offset % max(tq, tk) == 0 and tk % LANES == 0
    b = q_src.shape[0]
    lk = k_all.shape[1]
    n_q = math.gcd(l // tq, SB_BLOCKS_PER_STEP)
    rows = n_q * tq
    return pl.pallas_call(
        functools.partial(_sb_kernel, tq=tq, tk=tk, q_offset=q_offset, n_q=n_q),
        grid=(b, l // rows),
        in_specs=[pl.BlockSpec((1, rows, W_GROUP), lambda i, j: (i, j, q_block)),
                  pl.BlockSpec((1, lk, W_GROUP), lambda i, j: (i, 0, 0)),
                  pl.BlockSpec((1, lk, W_GROUP), lambda i, j: (i, 0, 0))],
        out_specs=pl.BlockSpec((1, rows, W_GROUP), lambda i, j: (i, j, 0)),
        out_shape=jax.ShapeDtypeStruct((b, l, W_GROUP), F32),
        compiler_params=pltpu.CompilerParams(dimension_semantics=("parallel", "parallel"),
                                             vmem_limit_bytes=VMEM_LIMIT),
        name="sb_attention",
    )(q_src, k_all, v_all)


def _cummax_rows(x):
    r = x.shape[0]
    row = lax.broadcasted_iota(jnp.int32, x.shape, 0)
    shift = 1
    while shift < r:
        x = jnp.maximum(x, jnp.where(row >= shift, pltpu.roll(x, shift, 0), -jnp.inf))
        shift *= 2
    return x


def _mlstm_steps(q_ref, k_ref, v_ref, og_ref, g_ref, bias_ref, ng_ref, c0_ref, n0_ref, m0_ref,
                 h_ref, c_out, n_out, m_out, cn_s, m_s, *, c, nb):
    j = pl.program_id(1)

    @pl.when(j == 0)
    def _():
        cn_s[...] = jnp.zeros_like(cn_s)
        m_s[...] = jnp.zeros_like(m_s)
        for i in range(nb):
            for h in range(N_HEADS):
                cn_s[i, _hs(h), _hs(h)] = c0_ref[i, h]
                cn_s[i, _hs(h), W_GROUP + h * HEAD_V:W_GROUP + (h + 1) * HEAD_V] = jnp.broadcast_to(
                    n0_ref[i, h], (HEAD_V, HEAD_V))
            m_s[i, :, 0:N_HEADS] = m0_ref[i]

    yield 'stage'

    seqs = range(nb)
    hs = N_HEADS * c
    lane = lax.broadcasted_iota(jnp.int32, (c, LANES), 1)
    is_f = (lane >= G_MF) & (lane < G_MF + N_HEADS)
    head_lane = lane < N_HEADS
    incl_b = _tri(c).astype(BF16)
    t_idx = lax.broadcasted_iota(jnp.int32, (c, hs), 0)
    s_idx = lax.broadcasted_iota(jnp.int32, (c, hs), 1) & (c - 1)
    incl4 = t_idx >= s_idx
    same_hv = _same_head(hs, W_GROUP, c, HEAD_V)
    same_vv = _same_head(W_GROUP, W_GROUP, HEAD_V, HEAD_V)
    group_sum = same_vv.astype(BF16)

    def expand(lanes, group):
        g = lax.broadcasted_iota(jnp.int32, (LANES, lanes), 0)
        h = lax.broadcasted_iota(jnp.int32, (LANES, lanes), 1) >> (group.bit_length() - 1)
        return (g == h).astype(BF16)

    gi = lax.broadcasted_iota(jnp.int32, (LANES, 2 * LANES), 0)
    li = lax.broadcasted_iota(jnp.int32, (LANES, 2 * LANES), 1)
    to_heads = (jnp.where((li < N_HEADS) & (gi == G_MI + li), 1.0, 0.0)
                - jnp.where((li < N_HEADS) & (gi == G_MF + li), 1.0, 0.0)
                + jnp.where((li >= LANES) & (li < LANES + N_HEADS) & (gi == G_MF + li - LANES), 1.0, 0.0)
                ).astype(BF16)
    row_sel = (lax.broadcasted_iota(jnp.int32, (SUBLANES, LANES), 1)
               == lax.broadcasted_iota(jnp.int32, (SUBLANES, LANES), 0)).astype(BF16)
    gates = [g_ref[i] + bias_ref[...] for i in seqs]
    merged = [jnp.where(is_f, _dot01_left(incl_b, jnp.where(is_f, _log_sigmoid(g), 0.0)), g) for g in gates]
    gb = _dot01_right(merged, to_heads)
    yield 'stage'
    g_c = [jnp.where(head_lane, x[:, :LANES], -jnp.inf) for x in gb]
    b_c = [x[:, LANES:] for x in gb]
    m_prev = [m_s[i] for i in seqs]
    r_c = [jnp.maximum(m_prev[i], _cummax_rows(g_c[i])) for i in seqs]
    r_end = [r[c - 1:c, :] for r in r_c]
    si_c = [jnp.exp(m_prev[i] - r_c[i]) for i in seqs]
    emt_c = [jnp.exp(-(b_c[i] + r_c[i])) for i in seqs]
    w_c = [jnp.exp(g_c[i] - r_end[i]) for i in seqs]
    dec_c = [jnp.broadcast_to(jnp.exp(m_prev[i] - r_end[i]), (c, LANES)) for i in seqs]
    yield 'stage'
    zero_pad = lambda x: jnp.where(head_lane, x, 0.0)
    r_hs = _dot01_right([zero_pad(r) for r in r_c], expand(hs, c))
    cols = _dot01_right([zero_pad(x) for i in seqs for x in (si_c[i], emt_c[i], w_c[i], dec_c[i])],
                        expand(W_GROUP, HEAD_V))
    si_hv, emt_hv, w_hv, dec_hv = (cols[n::4] for n in range(4))
    yield 'stage'
    g_rows = [sum(lax.dot_general(row_sel, part, (((1,), (1,)), ((), ())), preferred_element_type=F32)
                  for part in _split3(zero_pad(x))) for x in g_c]
    g_row_hs = [jnp.concatenate([gr[h:h + 1, :] for h in range(N_HEADS)], axis=1) for gr in g_rows]
    d_hs = [jnp.where(incl4, jnp.exp(g_row_hs[i] - r_hs[i]), 0.0) for i in seqs]
    yield 'stage'
    qs = [q_ref[i].astype(BF16) for i in seqs]
    ks = [k_ref[i] * (HEAD_V ** -0.5) for i in seqs]
    vs = [v_ref[i] for i in seqs]
    cns = [cn_s[i] for i in seqs]
    qk = [lax.dot_general(qs[i], _head_blocks(ks[i].astype(BF16), same_hv), (((1,), (1,)), ((), ())),
                          preferred_element_type=F32) for i in seqs]
    q_cn = [jnp.dot(qs[i], cns[i].astype(BF16), preferred_element_type=F32) for i in seqs]
    yield 'stage'
    scs = [(qk[i] * d_hs[i]).astype(BF16) for i in seqs]
    ones_blocks = same_hv.astype(BF16)
    v_ones = [jnp.concatenate([_head_blocks(vs[i].astype(BF16), same_hv), ones_blocks], axis=1)
              for i in seqs]
    sc_v = [jnp.dot(scs[i], v_ones[i], preferred_element_type=F32) for i in seqs]
    yield 'stage'
    nd = [sc_v[i] + jnp.concatenate([si_hv[i], si_hv[i]], axis=1) * q_cn[i] for i in seqs]
    hcs = [nd[i][:, :W_GROUP] / jnp.maximum(jnp.abs(nd[i][:, W_GROUP:]), emt_hv[i]) for i in seqs]
    kws = [ks[i] * w_hv[i] for i in seqs]
    upds = [_dot_tn(kws[i], jnp.concatenate([vs[i], jnp.ones_like(vs[i])], axis=1)) for i in seqs]
    same_cn = jnp.concatenate([same_vv, same_vv], axis=1)
    sum_sq = _group_sums([x * x for x in hcs], group_sum)
    yield 'stage'
    ng = ng_ref[...]
    for i in seqs:
        dec_row = dec_hv[i][0:1, :]
        cn_s[i] = jnp.concatenate([dec_row, dec_row], axis=1) * cns[i] + jnp.where(same_cn, upds[i], 0.0)
        m_s[i] = jnp.where(head_lane[0:1, :], b_c[i][c - 1:c, :] + r_end[i], 0.0)
        h_ref[i] = (hcs[i] * lax.rsqrt(sum_sq[i] * (1.0 / HEAD_V) + EPS) * ng * _sigmoid(og_ref[i]))
    yield 'final'

    @pl.when(j == pl.num_programs(1) - 1)
    def _():
        for i in seqs:
            for h in range(N_HEADS):
                c_out[i, h] = cn_s[i, _hs(h), _hs(h)]
                n_out[i, h] = cn_s[i, _hs(h), W_GROUP + h * HEAD_V:W_GROUP + h * HEAD_V + 1]
            m_out[i] = m_s[i, :, 0:N_HEADS]


def _same_head(rows, lanes, row_group, lane_group):
    rh = lax.broadcasted_iota(jnp.int32, (rows, lanes), 0) >> (row_group.bit_length() - 1)
    lh = lax.broadcasted_iota(jnp.int32, (rows, lanes), 1) >> (lane_group.bit_length() - 1)
    return rh == lh


def _head_blocks(x, same_head):
    tiled = jnp.concatenate([x] * N_HEADS, axis=0)
    return jnp.where(same_head, tiled, jnp.zeros_like(tiled))


def _heads_dot_split(l, r, same_head):
    return _heads_dot_parts(_split2(l), _split2(r), same_head)


def _heads_dot_parts(l_parts, r_parts, same_head):
    l1, l2 = l_parts
    rb1, rb2 = _head_blocks(r_parts[0], same_head), _head_blocks(r_parts[1], same_head)
    m = l1.shape[0]
    y = jnp.dot(jnp.concatenate([l1, l2], axis=0), rb1, preferred_element_type=F32)
    return y[:m] + (y[m:] + jnp.dot(l1, rb2, preferred_element_type=F32))


def _unit_lower_inverses(lows, eye4, same_head, c):
    ts = [eye4 - a for a in lows]
    a_parts = [_split2(a) for a in lows]
    ps = [_heads_dot_parts(ap, ap, same_head) for ap in a_parts]
    yield 'stage'
    for _ in range(c.bit_length() - 3):
        t_parts = [_split2(t) for t in ts]
        p_parts = [_split2(p) for p in ps]
        stacked = [tuple(jnp.concatenate([tp[n], pp[n]], axis=0) for n in range(2))
                   for tp, pp in zip(t_parts, p_parts)]
        ys = [_heads_dot_parts(sp, pp, same_head) for sp, pp in zip(stacked, p_parts)]
        ts = [t + y[:c] for t, y in zip(ts, ys)]
        ps = [y[c:] for y in ys]
        yield 'stage'
    return [t + _heads_dot_split(t, p, same_head) for t, p in zip(ts, ps)]


def _gdn_steps(q_ref, k_ref, v_ref, z_ref, g_ref, bias_ref, alog_ref, ng_ref, s0_ref,
               o_ref, s_out, s_s, *, c, nb):
    j = pl.program_id(1)
    seqs = range(nb)

    @pl.when(j == 0)
    def _():
        s_s[...] = jnp.zeros_like(s_s)
        for i in seqs:
            for h in range(N_HEADS):
                s_s[i, _hs(h), _hs(h)] = s0_ref[i, h]

    yield 'stage'

    hs = N_HEADS * c
    lane = lax.broadcasted_iota(jnp.int32, (c, LANES), 1)
    is_a = (lane >= G_GA) & (lane < G_GA + N_HEADS)
    incl_b = _tri(c).astype(BF16)
    t_idx = lax.broadcasted_iota(jnp.int32, (c, hs), 0)
    s_idx = lax.broadcasted_iota(jnp.int32, (c, hs), 1) & (c - 1)
    incl4, strict4, eye4 = t_idx >= s_idx, t_idx > s_idx, (t_idx == s_idx).astype(F32)
    same_hs = _same_head(hs, hs, c, c)
    same_hv = _same_head(hs, W_GROUP, c, HEAD_V)
    same_vv = _same_head(W_GROUP, W_GROUP, HEAD_V, HEAD_V)
    group_sum = same_vv.astype(BF16)

    def pick(lanes, group):
        g = lax.broadcasted_iota(jnp.int32, (LANES, 2 * lanes), 0)
        l = lax.broadcasted_iota(jnp.int32, (LANES, 2 * lanes), 1)
        h = (l & (lanes - 1)) >> (group.bit_length() - 1)
        return (g == jnp.where(l < lanes, G_GA, G_GB) + h).astype(BF16)

    pick_hs = pick(hs, c)
    row_sel = (lax.broadcasted_iota(jnp.int32, (SUBLANES, LANES), 1)
               == G_GA + lax.broadcasted_iota(jnp.int32, (SUBLANES, LANES), 0)).astype(BF16)
    gates = [g_ref[i] + bias_ref[...] for i in seqs]
    xgs = [jnp.where(is_a, -jnp.exp(alog_ref[...]) * _softplus(g), _sigmoid(g)) for g in gates]
    csums = [jnp.where(is_a, _dot01_left(incl_b, xg), xg) for xg in xgs]
    cols_hs = _dot01_right(csums, pick_hs)
    yield 'stage'
    bc_hs, beta_hs = [x[:, :hs] for x in cols_hs], [x[:, hs:] for x in cols_hs]
    if c == HEAD_V:
        bc_hv, beta_hv = bc_hs, beta_hs
    else:
        cols_hv = _dot01_right(csums, pick(W_GROUP, HEAD_V))
        bc_hv, beta_hv = [x[:, :W_GROUP] for x in cols_hv], [x[:, W_GROUP:] for x in cols_hv]
    b_rows = [sum(lax.dot_general(row_sel, part, (((1,), (1,)), ((), ())), preferred_element_type=F32)
                  for part in _split3(cs)) for cs in csums]
    br_hs = [jnp.concatenate([br[h:h + 1, :] for h in range(N_HEADS)], axis=1) for br in b_rows]
    decays = [jnp.exp(jnp.where(incl4, bc_hs[i] - br_hs[i], 0.0)) for i in seqs]
    yield 'stage'
    qk_raw = [jnp.concatenate([q_ref[i], k_ref[i]], axis=0) for i in seqs]
    norms = _group_sums([x * x for x in qk_raw], group_sum)
    qk_n = [qk_raw[i] * lax.rsqrt(norms[i] + EPS) for i in seqs]
    qs = [x[:c] * (HEAD_V ** -0.5) for x in qk_n]
    ks = [x[c:] for x in qk_n]
    vs = [v_ref[i] for i in seqs]
    kq = [jnp.concatenate([ks[i], qs[i]], axis=0).astype(BF16) for i in seqs]
    yield 'stage'
    states = [s_s[i] for i in seqs]
    scores = [lax.dot_general(kq[i], _head_blocks(ks[i].astype(BF16), same_hv),
                              (((1,), (1,)), ((), ())), preferred_element_type=F32) for i in seqs]
    on_state = [jnp.dot(kq[i], states[i].astype(BF16), preferred_element_type=F32) for i in seqs]
    yield 'stage'
    lows = [jnp.where(strict4, beta_hs[i] * decays[i] * scores[i][:c], 0.0) for i in seqs]
    invs = yield from _unit_lower_inverses(lows, eye4, same_hs, c)
    ebs = [jnp.exp(bc_hv[i]) for i in seqs]
    rhss = [beta_hv[i] * (vs[i] - ebs[i] * on_state[i][:c]) for i in seqs]
    yield 'stage'
    us = [_heads_dot_split(invs[i], rhss[i], same_hv) for i in seqs]
    yield 'stage'
    qkm = [jnp.where(incl4, decays[i] * scores[i][c:], 0.0).astype(BF16) for i in seqs]
    outs = [ebs[i] * on_state[i][c:]
            + jnp.dot(qkm[i], _head_blocks(us[i].astype(BF16), same_hv), preferred_element_type=F32)
            for i in seqs]
    b_ends = [bc_hv[i][c - 1:c, :] for i in seqs]
    upds = [jnp.where(same_vv, _dot_tn(ks[i] * jnp.exp(b_ends[i] - bc_hv[i]), us[i]), 0.0) for i in seqs]
    yield 'stage'
    ng = ng_ref[...]
    sum_sq = _group_sums([o * o for o in outs], group_sum)
    for i in seqs:
        s_s[i] = jnp.exp(b_ends[i]) * states[i] + upds[i]
        o_ref[i] = (outs[i] * lax.rsqrt(sum_sq[i] * (1.0 / HEAD_V) + EPS) * ng * _silu(z_ref[i]))
    yield 'final'

    @pl.when(j == pl.num_programs(1) - 1)
    def _():
        for i in seqs:
            for h in range(N_HEADS):
                s_out[i, h] = s_s[i, _hs(h), _hs(h)]


GLA_SUB = SUBLANES


def _gla_steps(q_ref, k_ref, v_ref, r_ref, g_ref, w2_ref, gb_ref, ng_ref, s0_ref,
               o_ref, s_out, s_s, *, c, nb):
    j = pl.program_id(1)
    dk_all = N_HEADS * GLA_DK
    hs = N_HEADS * c
    nsb = c // GLA_SUB
    seqs = range(nb)

    @pl.when(j == 0)
    def _():
        s_s[...] = jnp.zeros_like(s_s)
        for i in seqs:
            for h in range(N_HEADS):
                s_s[i, h * GLA_DK:(h + 1) * GLA_DK, _hs(h)] = s0_ref[i, h]

    yield 'stage'

    incl_b = _tri(c).astype(BF16)
    er = lax.broadcasted_iota(jnp.int32, (2 * c, c), 0)
    ec = lax.broadcasted_iota(jnp.int32, (2 * c, c), 1)
    et = er & (c - 1)
    edge_sel = (ec == jnp.where(er < c, (et & ~(GLA_SUB - 1)) - 1, et | (GLA_SUB - 1))).astype(BF16)
    t_idx = lax.broadcasted_iota(jnp.int32, (c, hs), 0)
    s_idx = lax.broadcasted_iota(jnp.int32, (c, hs), 1) & (c - 1)
    sub_shift = GLA_SUB.bit_length() - 1
    blk_dist = (t_idx >> sub_shift) - (s_idx >> sub_shift)
    diag_off = t_idx - s_idx
    same_hk = _same_head(hs, dk_all, c, GLA_DK)
    same_hv = _same_head(hs, W_GROUP, c, HEAD_V)
    same_kv = _same_head(dk_all, W_GROUP, GLA_DK, HEAD_V)
    head_rep = _same_head(dk_all, hs, GLA_DK, c).astype(BF16)
    group_sum = _same_head(W_GROUP, W_GROUP, HEAD_V, HEAD_V).astype(BF16)
    row8 = lax.broadcasted_iota(jnp.int32, (c, dk_all), 0) & (GLA_SUB - 1)
    last_row = lax.broadcasted_iota(jnp.int32, (c, dk_all), 0) == c - 1

    log_as = [_log_sigmoid(_dot(g_ref[i], w2_ref[...]) + gb_ref[...]) / GLA_TAU for i in seqs]
    bs = [_dot01_left(incl_b, la) for la in log_as]
    edges = [_dot01_left(edge_sel, b) for b in bs]
    yield 'stage'
    e_prev = [e[:c] for e in edges]
    e_own = [e[c:] for e in edges]
    qs = [q_ref[i] * (GLA_DK ** -0.5) for i in seqs]
    ks = [k_ref[i] for i in seqs]
    vs = [v_ref[i] for i in seqs]
    q_hat = [qs[i] * jnp.exp(bs[i] - e_prev[i]) for i in seqs]
    k_hat = [ks[i] * jnp.exp(e_own[i] - bs[i]) for i in seqs]
    yield 'stage'

    def q_for_distance(i, m):
        if m == 0:
            return q_hat[i]
        rows = GLA_SUB * m
        shifted = jnp.concatenate([jnp.zeros((rows, dk_all), F32), e_prev[i][:c - rows]], axis=0)
        return q_hat[i] * jnp.exp(e_prev[i] - shifted)

    a_parts = []
    for i in seqs:
        lhs = jnp.concatenate([q_for_distance(i, m) for m in range(nsb - 1)], axis=0).astype(BF16)
        prod = lax.dot_general(lhs, _head_blocks(k_hat[i].astype(BF16), same_hk),
                               (((1,), (1,)), ((), ())), preferred_element_type=F32)
        a_parts.append(sum(jnp.where(blk_dist == m + 1, prod[m * c:(m + 1) * c], 0.0)
                           for m in range(nsb - 1)))
        yield 'stage'

    def rot8(x, d):
        return x if d == 0 else pltpu.roll(x, d, 0)

    for i in seqs:
        ws = [jnp.where(row8 >= d, qs[i] * rot8(ks[i], d) * jnp.exp(bs[i] - rot8(bs[i], d)), 0.0)
              for d in range(GLA_SUB)]
        prod = jnp.dot(jnp.concatenate(ws, axis=0).astype(BF16), head_rep, preferred_element_type=F32)
        a_parts[i] = a_parts[i] + sum(jnp.where(diag_off == d, prod[d * c:(d + 1) * c], 0.0)
                                      for d in range(GLA_SUB))
        yield 'stage'
    states = [s_s[i] for i in seqs]
    outs = [jnp.dot(a_parts[i].astype(BF16), _head_blocks(vs[i].astype(BF16), same_hv),
                    preferred_element_type=F32)
            + _dot(qs[i] * jnp.exp(bs[i]), states[i]) for i in seqs]
    yield 'stage'
    b_last = [b[c - 1:c, :] for b in bs]
    upds = [jnp.where(same_kv, _dot_tn(ks[i] * jnp.exp(b_last[i] - bs[i]), vs[i]), 0.0) for i in seqs]
    ones_cv = jnp.ones((c, W_GROUP), BF16)
    decay_rows = [sum(lax.dot_general(part, ones_cv, (((0,), (0,)), ((), ())), preferred_element_type=F32)
                      for part in _split3(jnp.where(last_row, b, 0.0))) for b in bs]
    yield 'stage'
    sum_sq = _group_sums([o * o for o in outs], group_sum)
    ng = ng_ref[...]
    for i in seqs:
        s_s[i] = states[i] * jnp.exp(decay_rows[i]) + upds[i]
        o_ref[i] = outs[i] * lax.rsqrt(sum_sq[i] * (1.0 / HEAD_V) + EPS) * ng * _silu(r_ref[i])
    yield 'final'

    @pl.when(j == pl.num_programs(1) - 1)
    def _():
        for i in seqs:
            for h in range(N_HEADS):
                s_out[i, h] = s_s[i, h * GLA_DK:(h + 1) * GLA_DK, _hs(h)]


def _interleave(steps):
    active, parked = list(steps), []
    while active:
        for g in list(active):
            if next(g) == 'final':
                active.remove(g)
                parked.append(g)
    for g in parked:
        for _ in g:
            pass


def _mixers_kernel(mq, mk, mv, mo, gq, gk, gv, gz, lq, lk, lv, lr, g_ref, bias_ref, alog_ref,
                   ml_ng, gdn_ng, w2_ref, gb_ref, gla_ng, c0, n0, m0, gs0, ls0,
                   ml_h, c_out, n_out, m_out, gdn_o, gs_out, gla_o, ls_out,
                   cn_s, m_s, gs_s, ls_s, *, c, nb):
    _interleave([
        _mlstm_steps(mq, mk, mv, mo, g_ref, bias_ref, ml_ng, c0, n0, m0, ml_h, c_out, n_out, m_out,
                     cn_s, m_s, c=c, nb=nb),
        _gdn_steps(gq, gk, gv, gz, g_ref, bias_ref, alog_ref, gdn_ng, gs0, gdn_o, gs_out, gs_s,
                   c=c, nb=nb),
        _gla_steps(lq, lk, lv, lr, g_ref, w2_ref, gb_ref, gla_ng, ls0, gla_o, ls_out, ls_s,
                   c=c, nb=nb)])


def _mixers_call(p, gate_bias, alog_row, ml_ng, gdn_ng, w2_pad, gla_gb, gla_ng,
                 ml_c, ml_n, ml_m, gdn_s, gla_s, c, nb):
    b, l, _ = p.shape
    nb = math.gcd(nb, b)
    nc = l // c
    dk_all = N_HEADS * GLA_DK
    blk = lambda cb: pl.BlockSpec((nb, c, W_GROUP), lambda i, j, cb=cb: (i, j, cb))
    blk128 = lambda cb: pl.BlockSpec((nb, c, LANES), lambda i, j, cb=cb: (i, j, cb))
    row = lambda w: pl.BlockSpec((1, w), lambda i, j: (0, 0))
    full = lambda a: pl.BlockSpec((nb,) + a.shape[1:], lambda i, j: (i,) + (0,) * (a.ndim - 1))
    tok = pl.BlockSpec((nb, c, W_GROUP), lambda i, j: (i, j, 0))
    ml_n = ml_n[..., None]
    ml_m = ml_m.reshape(b, 1, N_HEADS)
    states = (ml_c, ml_n, ml_m, gdn_s, gla_s)
    sds = lambda a: jax.ShapeDtypeStruct(a.shape, F32)
    act = jax.ShapeDtypeStruct((b, l, W_GROUP), F32)
    o_ml, ml_c, ml_n, ml_m, o_gdn, gdn_s, o_gla, gla_s = pl.pallas_call(
        functools.partial(_mixers_kernel, c=c, nb=nb),
        grid=(b // nb, nc),
        in_specs=[blk(PB_MQ), blk(PB_MK), blk(PB_MV), blk(PB_MO),
                  blk(PB_GQ), blk(PB_GK), blk(PB_GV), blk(PB_GZ),
                  blk128(PB128_LQ), blk128(PB128_LK), blk(PB_LV), blk(PB_LR),
                  blk128(PB128_GATES), row(LANES), row(LANES), row(W_GROUP), row(W_GROUP),
                  pl.BlockSpec((LANES, dk_all), lambda i, j: (0, 0)), row(dk_all), row(W_GROUP)]
                 + [full(a) for a in states],
        out_specs=[tok, full(ml_c), full(ml_n), full(ml_m), tok, full(gdn_s), tok, full(gla_s)],
        out_shape=[act, sds(ml_c), sds(ml_n), sds(ml_m), act, sds(gdn_s), act, sds(gla_s)],
        scratch_shapes=[pltpu.VMEM((nb, W_GROUP, 2 * W_GROUP), F32), pltpu.VMEM((nb, 1, LANES), F32),
                        pltpu.VMEM((nb, W_GROUP, W_GROUP), F32), pltpu.VMEM((nb, dk_all, W_GROUP), F32)],
        compiler_params=pltpu.CompilerParams(dimension_semantics=("parallel", "arbitrary"),
                                             vmem_limit_bytes=VMEM_LIMIT),
        name="mixers",
    )(*([p] * 13), gate_bias, alog_row, ml_ng, gdn_ng, w2_pad, gla_gb, gla_ng, *states)
    return o_ml, ml_c, ml_n[..., 0], ml_m[:, 0, :], o_gdn, gdn_s, o_gla, gla_s


def _outffn_kernel(x_ref, a_ref, b_ref, c_ref, d_ref, mod_ref, g2_ref, gf_ref, wo_ref, w1_ref, w2_ref,
                   o_ref, *, ff_chunk, final):
    ns, tm, d = x_ref.shape
    flat = lambda y: y.reshape(ns * tm, y.shape[-1])
    per_seq = lambda y: y.reshape(ns, tm, y.shape[-1])
    mod = mod_ref[...]
    mixed = flat(jnp.concatenate([a_ref[...], b_ref[...], c_ref[...], d_ref[...]], axis=-1)).astype(BF16)
    x = x_ref[...] + mod[:, 2:3, :] * per_seq(jnp.dot(mixed, wo_ref[...], preferred_element_type=F32))
    h = flat(_modulated_norm(x, g2_ref[...], mod[:, 4:5, :], mod[:, 3:4, :])).astype(BF16)
    d_ff = w1_ref.shape[1]
    acc = jnp.zeros((ns * tm, d), F32)
    for f0 in range(0, d_ff, ff_chunk):
        a = jnp.maximum(jnp.dot(h, w1_ref[:, f0:f0 + ff_chunk], preferred_element_type=F32), 0.0)
        acc = acc + jnp.dot((a * a).astype(BF16), w2_ref[f0:f0 + ff_chunk, :],
                            preferred_element_type=F32)
    x = x + mod[:, 5:6, :] * per_seq(acc)
    if final:
        x = x * lax.rsqrt(jnp.mean(x * x, axis=-1, keepdims=True) + EPS) * gf_ref[...]
    o_ref[...] = x


def _outffn_call(x, mixers, mod, g2, gf, wo, w1, w2, tm, ns, final):
    b, l, d = x.shape
    d_ff = w1.shape[1]
    tok = lambda w: pl.BlockSpec((ns, tm, w), lambda i, j: (i, j, 0))
    const = lambda shape: pl.BlockSpec(shape, lambda i, j: (0,) * len(shape),
                                       pipeline_mode=pl.Buffered(1))
    return pl.pallas_call(
        functools.partial(_outffn_kernel, ff_chunk=1024, final=final),
        grid=(b // ns, l // tm),
        in_specs=[tok(d), tok(W_GROUP), tok(W_GROUP), tok(W_GROUP), tok(W_GROUP),
                  pl.BlockSpec((ns, 6, d), lambda i, j: (i, 0, 0)),
                  pl.BlockSpec((1, d), lambda i, j: (0, 0)),
                  pl.BlockSpec((1, d), lambda i, j: (0, 0)),
                  const((d, d)), const((d, d_ff)), const((d_ff, d))],
        out_specs=tok(d),
        out_shape=jax.ShapeDtypeStruct((b, l, d), F32),
        compiler_params=pltpu.CompilerParams(dimension_semantics=("parallel", "parallel"),
                                             vmem_limit_bytes=VMEM_LIMIT),
        name="outproj_ffn",
    )(x, *mixers, mod, g2, gf, wo, w1, w2)


def _permute_w_in(w_in):
    d = w_in.shape[0]
    sizes = (256, 256, 256, 256, 256, 256, 4, 4, 256, 768, 4, 4, 256, 128, 128, 256, 16, 256)
    offs = [0]
    for s in sizes:
        offs.append(offs[-1] + s)
    (sq, sk, sv, mq, mk, mv, mi, mf, mo, gqkv, gb, ga, gz, lq, lk, lv, lg, lr) = [
        w_in[:, offs[i]:offs[i + 1]] for i in range(len(sizes))]
    pad = jnp.zeros((d, LANES - (4 * N_HEADS + GLA_RANK)), w_in.dtype)
    return jnp.concatenate([sk, sv, sq, mq, mk, mv, mo, gqkv, gz, lv, lr, lq, lk,
                            mi, mf, gb, ga, lg, pad], axis=1)


def _row128(pieces):
    row = jnp.zeros((LANES,), F32)
    for off, vec in pieces:
        row = row.at[off:off + vec.shape[0]].set(vec)
    return row.reshape(1, LANES)


DENSE_TILE_ROWS = 512


class _Tiles(NamedTuple):
    tm: int
    dense_seqs: int
    sb_tq: int
    sb_tk: int
    chunk: int


def _tiles(b, l):
    tm = min(DENSE_TILE_ROWS, l)
    return _Tiles(tm=tm, dense_seqs=math.gcd(b, DENSE_TILE_ROWS // tm), sb_tq=min(256, l),
                  sb_tk=max(min(256, l), LANES), chunk=min(64, l))


def _layer(x, mod, lw, states, tiles, final, final_g):
    tm, dense_seqs, sb_tq, sb_tk, chunk = tiles
    (n1, n2, w_in_p, gate_bias, ml_ng, conv_w, alog_row, gdn_ng, w2_pad, gla_gb, gla_ng,
     w_out, w_ff1, w_ff2) = lw
    (sb_k_past, sb_v_past, ml_c, ml_n, ml_m, gdn_s, gdn_buf, gla_s) = states
    b, l, _ = x.shape
    k_new, v_new, p, gdn_buf = _inproj_call(x, mod, n1, w_in_p, conv_w, gdn_buf, tm, dense_seqs)
    if sb_k_past is None:
        k_all, v_all, q_offset = k_new, v_new, 0
    else:
        past = sb_k_past.shape[1]
        lk = -(-(past + l) // sb_tk) * sb_tk
        padz = jnp.zeros((b, lk - past - l, W_GROUP), F32)
        k_all = jnp.concatenate([sb_k_past.reshape(b, past, W_GROUP), k_new, padz], axis=1)
        v_all = jnp.concatenate([sb_v_past.reshape(b, past, W_GROUP), v_new, padz], axis=1)
        q_offset = past
    o_sb = _sb_call(p, PB_SQ, k_all, v_all, l, sb_tq, sb_tk, q_offset)
    o_ml, ml_c, ml_n, ml_m, o_gdn, gdn_s, o_gla, gla_s = _mixers_call(
        p, gate_bias, alog_row, ml_ng, gdn_ng, w2_pad, gla_gb, gla_ng, ml_c, ml_n, ml_m, gdn_s, gla_s,
        chunk, MIXER_SEQS_PER_STEP)
    x = _outffn_call(x, (o_sb, o_ml, o_gdn, o_gla), mod, n2, final_g, w_out, w_ff1, w_ff2, tm,
                     dense_seqs, final)
    hk = lambda a: a.reshape(b, l, N_HEADS, HEAD_V)
    return x, (hk(k_new), hk(v_new), ml_c, ml_n, ml_m, gdn_s, gdn_buf, gla_s)


def kernel(x_prompt, x_sample, cache_sb_k, cache_sb_v, state_mlstm_C, state_mlstm_n, state_mlstm_m, state_gdn_S, state_gdn_conv, state_gla_S, c_prompt, c_sample, norm1_g, norm2_g, w_ada, b_ada, w_in, mlstm_i_bias, mlstm_f_bias, mlstm_norm_g, gdn_conv_w, gdn_a_log, gdn_dt_bias, gdn_norm_g, gla_w_gate2, gla_gate_bias, gla_norm_g, w_out, w_ff1, w_ff2, final_g):
    depth = w_in.shape[0]
    bp, lp, d = x_prompt.shape
    bs, ls, _ = x_sample.shape
    dk_all = N_HEADS * GLA_DK
    mod_all = _ada_call(jnp.concatenate([c_prompt, c_sample], axis=0), w_ada, b_ada)
    mod_all = mod_all.reshape(depth, bp + bs, 6, d)
    final_row = final_g.reshape(1, d)
    xp, xs = x_prompt, x_sample
    p_list, s_list = [], []
    for l in range(depth):
        gate_bias = _row128([(G_MI, mlstm_i_bias[l]), (G_MF, mlstm_f_bias[l]), (G_GA, gdn_dt_bias[l])])
        alog_row = _row128([(G_GA, gdn_a_log[l])])
        w2_pad = jnp.zeros((LANES, dk_all), F32).at[G_LG:G_LG + GLA_RANK, :].set(gla_w_gate2[l])
        lw = (norm1_g[l].reshape(1, d), norm2_g[l].reshape(1, d), _permute_w_in(w_in[l]).astype(BF16),
              gate_bias, mlstm_norm_g[l].reshape(1, W_GROUP), gdn_conv_w[l], alog_row,
              gdn_norm_g[l].reshape(1, W_GROUP), w2_pad.astype(BF16),
              gla_gate_bias[l].reshape(1, dk_all), gla_norm_g[l].reshape(1, W_GROUP),
              w_out[l].astype(BF16), w_ff1[l].astype(BF16), w_ff2[l].astype(BF16))
        fresh = (None, None, jnp.zeros((bp,) + state_mlstm_C.shape[2:], F32),
                 jnp.zeros((bp,) + state_mlstm_n.shape[2:], F32),
                 jnp.zeros((bp,) + state_mlstm_m.shape[2:], F32),
                 jnp.zeros((bp,) + state_gdn_S.shape[2:], F32),
                 jnp.zeros((bp,) + state_gdn_conv.shape[2:], F32),
                 jnp.zeros((bp,) + state_gla_S.shape[2:], F32))
        past = (cache_sb_k[l], cache_sb_v[l], state_mlstm_C[l], state_mlstm_n[l], state_mlstm_m[l],
                state_gdn_S[l], state_gdn_conv[l], state_gla_S[l])
        final = l == depth - 1
        xp, st_p = _layer(xp, mod_all[l, :bp], lw, fresh, _tiles(bp, lp), final, final_row)
        xs, st_s = _layer(xs, mod_all[l, bp:], lw, past, _tiles(bs, ls), final, final_row)
        p_list.append(st_p)
        s_list.append(st_s)
    stacked_p = [jnp.stack([st[i] for st in p_list]) for i in range(8)]
    stacked_s = [jnp.stack([st[i] for st in s_list]) for i in range(8)]
    return (xp, xs, *stacked_p, *stacked_s)
```

```python
import functools
import math
from typing import NamedTuple

import jax
import jax.numpy as jnp
from jax import lax
from jax.experimental import pallas as pl
from jax.experimental.pallas import tpu as pltpu

F32 = jnp.float32
BF16 = jnp.bfloat16
EPS = 1e-6
N_HEADS = 4
HEAD_V = 64
GLA_DK = 32
CONV_W = 4
GLA_TAU = 16.0
W_GROUP = N_HEADS * HEAD_V
LANES = 128
SUBLANES = 8
VMEM_LIMIT = 56 * 1024 * 1024
MIXER_SEQS_PER_STEP = 8

PB_SQ, PB_MQ, PB_MK, PB_MV, PB_MO, PB_GQ, PB_GK, PB_GV, PB_GZ, PB_LV, PB_LR = range(11)
PB128_LQ, PB128_LK, PB128_GATES = 22, 23, 24
P_WIDTH = 25 * LANES
G_MI, G_MF, G_GB, G_GA, G_LG = 0, 4, 8, 12, 16
GLA_RANK = 16


def _dot(a, b):
    return jnp.dot(a.astype(BF16), b.astype(BF16), preferred_element_type=F32)


def _dot_tn(a, b):
    return lax.dot_general(a.astype(BF16), b.astype(BF16), (((0,), (0,)), ((), ())),
                           preferred_element_type=F32)


def _split3(x):
    x1 = x.astype(BF16)
    r1 = x - x1.astype(F32)
    x2 = r1.astype(BF16)
    x3 = (r1 - x2.astype(F32)).astype(BF16)
    return x1, x2, x3


def _split2(x):
    x1 = x.astype(BF16)
    x2 = (x - x1.astype(F32)).astype(BF16)
    return x1, x2


def _dot01_left(m01, x):
    return sum(jnp.dot(m01, p, preferred_element_type=F32) for p in _split3(x))


def _dot01_right(xs, m01):
    n = xs[0].shape[0]
    stacked = jnp.concatenate([part for x in xs for part in _split3(x)], axis=0)
    y = jnp.dot(stacked, m01, preferred_element_type=F32)
    return [y[3 * i * n:(3 * i + 1) * n] + (y[(3 * i + 1) * n:(3 * i + 2) * n]
                                             + y[(3 * i + 2) * n:(3 * i + 3) * n])
            for i in range(len(xs))]


def _group_sums(xs, group_sum):
    n = xs[0].shape[0]
    y = jnp.dot(jnp.concatenate([x.astype(BF16) for x in xs], axis=0), group_sum,
                preferred_element_type=F32)
    return [y[i * n:(i + 1) * n] for i in range(len(xs))]


def _log_sigmoid(x):
    return jnp.minimum(x, 0.0) - jnp.log(1.0 + jnp.exp(-jnp.abs(x)))


def _sigmoid(x):
    return 1.0 / (1.0 + jnp.exp(-x))


def _silu(x):
    return x * _sigmoid(x)


def _softplus(x):
    return jnp.maximum(x, 0.0) + jnp.log(1.0 + jnp.exp(-jnp.abs(x)))


def _tri(c):
    r = lax.broadcasted_iota(jnp.int32, (c, c), 0)
    s = lax.broadcasted_iota(jnp.int32, (c, c), 1)
    return r >= s


def _hs(h, w=HEAD_V):
    return slice(h * w, (h + 1) * w)


def _ada_kernel(c_ref, w_ref, b_ref, o_ref):
    o_ref[0] = _dot(_silu(c_ref[...]), w_ref[0]) + b_ref[0]


def _ada_call(c_all, w_ada, b_ada, tn=512):
    depth, d, n = w_ada.shape
    rows = c_all.shape[0]
    return pl.pallas_call(
        _ada_kernel,
        grid=(depth, n // tn),
        in_specs=[pl.BlockSpec((rows, d), lambda l, j: (0, 0)),
                  pl.BlockSpec((1, d, tn), lambda l, j: (l, 0, j)),
                  pl.BlockSpec((1, 1, tn), lambda l, j: (l, 0, j))],
        out_specs=pl.BlockSpec((1, rows, tn), lambda l, j: (l, 0, j)),
        out_shape=jax.ShapeDtypeStruct((depth, rows, n), F32),
        compiler_params=pltpu.CompilerParams(dimension_semantics=("parallel", "parallel"),
                                             vmem_limit_bytes=VMEM_LIMIT),
        name="ada_mod",
    )(c_all, w_ada, b_ada.reshape(depth, 1, n))


def _modulated_norm(x, g_row, scale_row, shift_row):
    y = x * lax.rsqrt(jnp.mean(x * x, axis=-1, keepdims=True) + EPS) * g_row
    return y * (1.0 + scale_row) + shift_row


def _inproj_kernel(x_ref, mod_ref, g_ref, w_ref, cw_ref, cb0_ref, *refs, col_chunk):
    k_ref, v_ref, p_ref, cb_out, xw_s = refs[-5:]
    j = pl.program_id(1)
    ns, tm = x_ref.shape[0], x_ref.shape[1]
    conv_lo, conv_hi = PB_GQ * W_GROUP, (PB_GV + 1) * W_GROUP

    @pl.when(j == 0)
    def _():
        xw_s[:, 0:SUBLANES, :] = cb0_ref[...]

    mod = mod_ref[...]
    h = _modulated_norm(x_ref[...], g_ref[...], mod[:, 1:2, :], mod[:, 0:1, :])
    h = h.reshape(ns * tm, h.shape[-1]).astype(BF16)
    per_seq = lambda y: y.reshape(ns, tm, y.shape[-1])
    k_ref[0] = per_seq(jnp.dot(h, w_ref[:, 0:W_GROUP], preferred_element_type=F32))
    v_ref[0] = per_seq(jnp.dot(h, w_ref[:, W_GROUP:2 * W_GROUP], preferred_element_type=F32))
    base = 2 * W_GROUP
    bounds = sorted(set(range(0, conv_lo, col_chunk)) | {conv_lo, conv_hi}
                    | set(range(conv_hi, P_WIDTH, col_chunk)) | {P_WIDTH})
    for n0, n1 in zip(bounds[:-1], bounds[1:]):
        if n0 == conv_lo:
            continue
        p_ref[:, :, n0:n1] = per_seq(jnp.dot(h, w_ref[:, base + n0:base + n1],
                                             preferred_element_type=F32))
    raw = per_seq(jnp.dot(h, w_ref[:, base + conv_lo:base + conv_hi], preferred_element_type=F32))
    xw_s[:, SUBLANES:SUBLANES + tm, :] = raw
    cw = cw_ref[...]
    y = raw * cw[CONV_W - 1:CONV_W, :]
    for d in range(1, CONV_W):
        y = y + xw_s[:, SUBLANES - d:SUBLANES - d + tm, :] * cw[CONV_W - 1 - d:CONV_W - d, :]
    p_ref[:, :, conv_lo:conv_hi] = _silu(y)
    xw_s[:, 0:SUBLANES, :] = raw[:, tm - SUBLANES:tm, :]

    @pl.when(j == pl.num_programs(1) - 1)
    def _():
        cb_out[...] = raw[:, tm - SUBLANES:tm, :]


def _inproj_call(x, mod, g, w_perm, conv_w, conv0, tm, ns, layer, depth, kv_prev):
    b, l, d = x.shape
    n_all = w_perm.shape[1]
    conv_dim = conv_w.shape[1]
    pad_rows = SUBLANES - (CONV_W - 1)
    conv0_pad = jnp.concatenate([jnp.zeros((b, pad_rows, conv_dim), F32), conv0], axis=1)
    carry = pl.BlockSpec((ns, SUBLANES, conv_dim), lambda i, j: (i, 0, 0))
    kv_block = pl.BlockSpec((1, ns, tm, W_GROUP), lambda i, j: (layer, i, j, 0))
    kv_shape = jax.ShapeDtypeStruct((depth, b, l, W_GROUP), F32)
    in_specs = [pl.BlockSpec((ns, tm, d), lambda i, j: (i, j, 0)),
                pl.BlockSpec((ns, 6, d), lambda i, j: (i, 0, 0)),
                pl.BlockSpec((1, d), lambda i, j: (0, 0)),
                pl.BlockSpec((d, n_all), lambda i, j: (0, 0), pipeline_mode=pl.Buffered(1)),
                pl.BlockSpec((CONV_W, conv_dim), lambda i, j: (0, 0)),
                carry]
    operands = [x, mod, g, w_perm, conv_w, conv0_pad]
    aliases = {}
    if kv_prev is not None:
        aliases = {len(operands): 0, len(operands) + 1: 1}
        in_specs += [pl.BlockSpec(memory_space=pl.ANY)] * 2
        operands += list(kv_prev)
    k, v, p, conv_new = pl.pallas_call(
        functools.partial(_inproj_kernel, col_chunk=512),
        grid=(b // ns, l // tm),
        in_specs=in_specs,
        out_specs=[kv_block, kv_block,
                   pl.BlockSpec((ns, tm, P_WIDTH), lambda i, j: (i, j, 0)),
                   carry],
        out_shape=[kv_shape, kv_shape,
                   jax.ShapeDtypeStruct((b, l, P_WIDTH), F32),
                   jax.ShapeDtypeStruct((b, SUBLANES, conv_dim), F32)],
        scratch_shapes=[pltpu.VMEM((ns, tm + SUBLANES, conv_dim), F32)],
        input_output_aliases=aliases,
        compiler_params=pltpu.CompilerParams(dimension_semantics=("parallel", "arbitrary"),
                                             vmem_limit_bytes=VMEM_LIMIT),
        name="norm_inproj",
    )(*operands)
    return k, v, p, conv_new[:, pad_rows:, :]


SB_DEAD_LOG = -104.0
SB_BLOCKS_PER_STEP = 4


def _sb_kernel(q_ref, k_ref, v_ref, o_ref, *, tq, tk, q_offset, n_q):
    for qb in range(n_q):
        _sb_query_block(q_ref, k_ref, v_ref, o_ref, pl.program_id(1) * n_q + qb,
                        slice(qb * tq, (qb + 1) * tq), tq=tq, tk=tk, q_offset=q_offset)


def _sb_query_block(q_ref, k_ref, v_ref, o_ref, i, rows, *, tq, tk, q_offset):
    q0 = q_offset + i * tq
    j_top = (q0 + tq - 1) // tk
    n_masked = max(tq // tk, 1)
    suffix = (lax.broadcasted_iota(jnp.int32, (tk, tk), 0)
              >= lax.broadcasted_iota(jnp.int32, (tk, tk), 1)).astype(BF16)
    q_all = q_ref[0, rows, :] * (HEAD_V ** -0.5)
    qs = [q_all[:, _hs(h)].astype(BF16) for h in range(N_HEADS)]

    def block(j, accs, runs, causal):
        s0 = pl.multiple_of(j * tk, tk)
        heads = range(N_HEADS)
        kbs = [k_ref[0, pl.ds(s0, tk), _hs(h)].astype(BF16) for h in heads]
        vbs = [v_ref[0, pl.ds(s0, tk), _hs(h)].astype(BF16) for h in heads]
        zs = [lax.dot_general(qs[h], kbs[h], (((1,), (1,)), ((), ())), preferred_element_type=F32)
              for h in heads]
        sps = [_softplus(z) for z in zs]
        if causal is not None:
            sps = [jnp.where(causal, x, 0.0) for x in sps]
        splits = [_split2(x) for x in sps]
        incs = [jnp.dot(hi, suffix, preferred_element_type=F32)
                + jnp.dot(lo, suffix, preferred_element_type=F32) for hi, lo in splits]
        probs = [jnp.exp(zs[h] - incs[h] - runs[h]) for h in heads]
        if causal is not None:
            probs = [jnp.where(causal, a, 0.0) for a in probs]
        new_accs = [accs[h] + jnp.dot(probs[h].astype(BF16), vbs[h], preferred_element_type=F32)
                    for h in heads]
        new_runs = [runs[h] + incs[h][:, 0:1] for h in heads]
        return tuple(new_accs), tuple(new_runs)

    def live_of(runs):
        return -jnp.min(jnp.minimum(jnp.minimum(runs[0], runs[1]), jnp.minimum(runs[2], runs[3])))

    q_pos = q0 + lax.broadcasted_iota(jnp.int32, (tq, tk), 0)
    k_off = lax.broadcasted_iota(jnp.int32, (tq, tk), 1)
    zeros = lambda w: tuple(jnp.zeros((tq, w), F32) for _ in range(N_HEADS))
    accs, runs = zeros(HEAD_V), zeros(1)
    for m in range(n_masked):
        accs, runs = block(j_top - m, accs, runs, (j_top - m) * tk + k_off < q_pos)

    def cond(carry):
        j, live, _, _ = carry
        return (j >= 0) & (live > SB_DEAD_LOG)

    def body(carry):
        j, _, accs, runs = carry
        accs, runs = block(j, accs, runs, None)
        return j - 1, live_of(runs), accs, runs

    _, _, accs, _ = lax.while_loop(cond, body, (j_top - n_masked, live_of(runs), accs, runs))
    for h in range(N_HEADS):
        o_ref[0, rows, _hs(h)] = accs[h]


def _sb_call(q_src, q_block, k_all, v_all, l, tq, tk, q_offset, kv_layer=None):
    assert (tq % tk == 0 or tk % tq == 0) and q_offset % max(tq, tk) == 0 and tk % LANES == 0
    b = q_src.shape[0]
    n_q = math.gcd(l // tq, SB_BLOCKS_PER_STEP)
    rows = n_q * tq
    if kv_layer is None:
        kv_spec = pl.BlockSpec((1, k_all.shape[1], W_GROUP), lambda i, j: (i, 0, 0))
    else:
        kv_spec = pl.BlockSpec((None, 1, k_all.shape[2], W_GROUP), lambda i, j: (kv_layer, i, 0, 0))
    return pl.pallas_call(
        functools.partial(_sb_kernel, tq=tq, tk=tk, q_offset=q_offset, n_q=n_q),
        grid=(b, l // rows),
        in_specs=[pl.BlockSpec((1, rows, W_GROUP), lambda i, j: (i, j, q_block)), kv_spec, kv_spec],
        out_specs=pl.BlockSpec((1, rows, W_GROUP), lambda i, j: (i, j, 0)),
        out_shape=jax.ShapeDtypeStruct((b, l, W_GROUP), F32),
        compiler_params=pltpu.CompilerParams(dimension_semantics=("parallel", "parallel"),
                                             vmem_limit_bytes=VMEM_LIMIT),
        name="sb_attention",
    )(q_src, k_all, v_all)


def _cummax_rows(x):
    r = x.shape[0]
    row = lax.broadcasted_iota(jnp.int32, x.shape, 0)
    shift = 1
    while shift < r:
        x = jnp.maximum(x, jnp.where(row >= shift, pltpu.roll(x, shift, 0), -jnp.inf))
        shift *= 2
    return x


def _mlstm_steps(q_ref, k_ref, v_ref, og_ref, g_ref, bias_ref, ng_ref, c0_ref, n0_ref, m0_ref,
                 h_ref, c_out, n_out, m_out, cn_s, m_s, *, c, nb):
    j = pl.program_id(1)

    @pl.when(j == 0)
    def _():
        cn_s[...] = jnp.zeros_like(cn_s)
        m_s[...] = jnp.zeros_like(m_s)
        for i in range(nb):
            for h in range(N_HEADS):
                cn_s[i, _hs(h), _hs(h)] = c0_ref[i, h]
                cn_s[i, _hs(h), W_GROUP + h * HEAD_V:W_GROUP + (h + 1) * HEAD_V] = jnp.broadcast_to(
                    n0_ref[i, h], (HEAD_V, HEAD_V))
            m_s[i, :, 0:N_HEADS] = m0_ref[i]

    yield 'stage'

    seqs = range(nb)
    hs = N_HEADS * c
    lane = lax.broadcasted_iota(jnp.int32, (c, LANES), 1)
    is_f = (lane >= G_MF) & (lane < G_MF + N_HEADS)
    head_lane = lane < N_HEADS
    incl_b = _tri(c).astype(BF16)
    t_idx = lax.broadcasted_iota(jnp.int32, (c, hs), 0)
    s_idx = lax.broadcasted_iota(jnp.int32, (c, hs), 1) & (c - 1)
    incl4 = t_idx >= s_idx
    same_hv = _same_head(hs, W_GROUP, c, HEAD_V)
    same_vv = _same_head(W_GROUP, W_GROUP, HEAD_V, HEAD_V)
    group_sum = same_vv.astype(BF16)

    def expand(lanes, group):
        g = lax.broadcasted_iota(jnp.int32, (LANES, lanes), 0)
        h = lax.broadcasted_iota(jnp.int32, (LANES, lanes), 1) >> (group.bit_length() - 1)
        return (g == h).astype(BF16)

    gi = lax.broadcasted_iota(jnp.int32, (LANES, 2 * LANES), 0)
    li = lax.broadcasted_iota(jnp.int32, (LANES, 2 * LANES), 1)
    to_heads = (jnp.where((li < N_HEADS) & (gi == G_MI + li), 1.0, 0.0)
                - jnp.where((li < N_HEADS) & (gi == G_MF + li), 1.0, 0.0)
                + jnp.where((li >= LANES) & (li < LANES + N_HEADS) & (gi == G_MF + li - LANES), 1.0, 0.0)
                ).astype(BF16)
    row_sel = (lax.broadcasted_iota(jnp.int32, (SUBLANES, LANES), 1)
               == lax.broadcasted_iota(jnp.int32, (SUBLANES, LANES), 0)).astype(BF16)
    gates = [g_ref[i] + bias_ref[...] for i in seqs]
    merged = [jnp.where(is_f, _dot01_left(incl_b, jnp.where(is_f, _log_sigmoid(g), 0.0)), g) for g in gates]
    gb = _dot01_right(merged, to_heads)
    yield 'stage'
    g_c = [jnp.where(head_lane, x[:, :LANES], -jnp.inf) for x in gb]
    b_c = [x[:, LANES:] for x in gb]
    m_prev = [m_s[i] for i in seqs]
    r_c = [jnp.maximum(m_prev[i], _cummax_rows(g_c[i])) for i in seqs]
    r_end = [r[c - 1:c, :] for r in r_c]
    si_c = [jnp.exp(m_prev[i] - r_c[i]) for i in seqs]
    emt_c = [jnp.exp(-(b_c[i] + r_c[i])) for i in seqs]
    w_c = [jnp.exp(g_c[i] - r_end[i]) for i in seqs]
    dec_c = [jnp.broadcast_to(jnp.exp(m_prev[i] - r_end[i]), (c, LANES)) for i in seqs]
    yield 'stage'
    zero_pad = lambda x: jnp.where(head_lane, x, 0.0)
    r_hs = _dot01_right([zero_pad(r) for r in r_c], expand(hs, c))
    cols = _dot01_right([zero_pad(x) for i in seqs for x in (si_c[i], emt_c[i], w_c[i], dec_c[i])],
                        expand(W_GROUP, HEAD_V))
    si_hv, emt_hv, w_hv, dec_hv = (cols[n::4] for n in range(4))
    yield 'stage'
    g_rows = [sum(lax.dot_general(row_sel, part, (((1,), (1,)), ((), ())), preferred_element_type=F32)
                  for part in _split3(zero_pad(x))) for x in g_c]
    g_row_hs = [jnp.concatenate([gr[h:h + 1, :] for h in range(N_HEADS)], axis=1) for gr in g_rows]
    d_hs = [jnp.where(incl4, jnp.exp(g_row_hs[i] - r_hs[i]), 0.0) for i in seqs]
    yield 'stage'
    qs = [q_ref[i].astype(BF16) for i in seqs]
    ks = [k_ref[i] * (HEAD_V ** -0.5) for i in seqs]
    vs = [v_ref[i] for i in seqs]
    cns = [cn_s[i] for i in seqs]
    qk = [lax.dot_general(qs[i], _head_blocks(ks[i].astype(BF16), same_hv), (((1,), (1,)), ((), ())),
                          preferred_element_type=F32) for i in seqs]
    q_cn = [jnp.dot(qs[i], cns[i].astype(BF16), preferred_element_type=F32) for i in seqs]
    yield 'stage'
    scs = [(qk[i] * d_hs[i]).astype(BF16) for i in seqs]
    ones_blocks = same_hv.astype(BF16)
    v_ones = [jnp.concatenate([_head_blocks(vs[i].astype(BF16), same_hv), ones_blocks], axis=1)
              for i in seqs]
    sc_v = [jnp.dot(scs[i], v_ones[i], preferred_element_type=F32) for i in seqs]
    yield 'stage'
    nd = [sc_v[i] + jnp.concatenate([si_hv[i], si_hv[i]], axis=1) * q_cn[i] for i in seqs]
    hcs = [nd[i][:, :W_GROUP] / jnp.maximum(jnp.abs(nd[i][:, W_GROUP:]), emt_hv[i]) for i in seqs]
    kws = [ks[i] * w_hv[i] for i in seqs]
    upds = [_dot_tn(kws[i], jnp.concatenate([vs[i], jnp.ones_like(vs[i])], axis=1)) for i in seqs]
    same_cn = jnp.concatenate([same_vv, same_vv], axis=1)
    sum_sq = _group_sums([x * x for x in hcs], group_sum)
    yield 'stage'
    ng = ng_ref[...]
    for i in seqs:
        dec_row = dec_hv[i][0:1, :]
        cn_s[i] = jnp.concatenate([dec_row, dec_row], axis=1) * cns[i] + jnp.where(same_cn, upds[i], 0.0)
        m_s[i] = jnp.where(head_lane[0:1, :], b_c[i][c - 1:c, :] + r_end[i], 0.0)
        h_ref[i] = (hcs[i] * lax.rsqrt(sum_sq[i] * (1.0 / HEAD_V) + EPS) * ng * _sigmoid(og_ref[i]))
    yield 'final'

    @pl.when(j == pl.num_programs(1) - 1)
    def _():
        for i in seqs:
            for h in range(N_HEADS):
                c_out[i, h] = cn_s[i, _hs(h), _hs(h)]
                n_out[i, h] = cn_s[i, _hs(h), W_GROUP + h * HEAD_V:W_GROUP + h * HEAD_V + 1]
            m_out[i] = m_s[i, :, 0:N_HEADS]


def _same_head(rows, lanes, row_group, lane_group):
    rh = lax.broadcasted_iota(jnp.int32, (rows, lanes), 0) >> (row_group.bit_length() - 1)
    lh = lax.broadcasted_iota(jnp.int32, (rows, lanes), 1) >> (lane_group.bit_length() - 1)
    return rh == lh


def _head_blocks(x, same_head):
    tiled = jnp.concatenate([x] * N_HEADS, axis=0)
    return jnp.where(same_head, tiled, jnp.zeros_like(tiled))


def _heads_dot_split(l, r, same_head):
    return _heads_dot_parts(_split2(l), _split2(r), same_head)


def _heads_dot_parts(l_parts, r_parts, same_head):
    l1, l2 = l_parts
    rb1, rb2 = _head_blocks(r_parts[0], same_head), _head_blocks(r_parts[1], same_head)
    m = l1.shape[0]
    y = jnp.dot(jnp.concatenate([l1, l2], axis=0), rb1, preferred_element_type=F32)
    return y[:m] + (y[m:] + jnp.dot(l1, rb2, preferred_element_type=F32))


def _unit_lower_inverses(lows, eye4, same_head, c):
    ts = [eye4 - a for a in lows]
    a_parts = [_split2(a) for a in lows]
    ps = [_heads_dot_parts(ap, ap, same_head) for ap in a_parts]
    yield 'stage'
    for _ in range(c.bit_length() - 3):
        t_parts = [_split2(t) for t in ts]
        p_parts = [_split2(p) for p in ps]
        stacked = [tuple(jnp.concatenate([tp[n], pp[n]], axis=0) for n in range(2))
                   for tp, pp in zip(t_parts, p_parts)]
        ys = [_heads_dot_parts(sp, pp, same_head) for sp, pp in zip(stacked, p_parts)]
        ts = [t + y[:c] for t, y in zip(ts, ys)]
        ps = [y[c:] for y in ys]
        yield 'stage'
    return [t + _heads_dot_split(t, p, same_head) for t, p in zip(ts, ps)]


def _gdn_steps(q_ref, k_ref, v_ref, z_ref, g_ref, bias_ref, alog_ref, ng_ref, s0_ref,
               o_ref, s_out, s_s, *, c, nb):
    j = pl.program_id(1)
    seqs = range(nb)

    @pl.when(j == 0)
    def _():
        s_s[...] = jnp.zeros_like(s_s)
        for i in seqs:
            for h in range(N_HEADS):
                s_s[i, _hs(h), _hs(h)] = s0_ref[i, h]

    yield 'stage'

    hs = N_HEADS * c
    lane = lax.broadcasted_iota(jnp.int32, (c, LANES), 1)
    is_a = (lane >= G_GA) & (lane < G_GA + N_HEADS)
    incl_b = _tri(c).astype(BF16)
    t_idx = lax.broadcasted_iota(jnp.int32, (c, hs), 0)
    s_idx = lax.broadcasted_iota(jnp.int32, (c, hs), 1) & (c - 1)
    incl4, strict4, eye4 = t_idx >= s_idx, t_idx > s_idx, (t_idx == s_idx).astype(F32)
    same_hs = _same_head(hs, hs, c, c)
    same_hv = _same_head(hs, W_GROUP, c, HEAD_V)
    same_vv = _same_head(W_GROUP, W_GROUP, HEAD_V, HEAD_V)
    group_sum = same_vv.astype(BF16)

    def pick(lanes, group):
        g = lax.broadcasted_iota(jnp.int32, (LANES, 2 * lanes), 0)
        l = lax.broadcasted_iota(jnp.int32, (LANES, 2 * lanes), 1)
        h = (l & (lanes - 1)) >> (group.bit_length() - 1)
        return (g == jnp.where(l < lanes, G_GA, G_GB) + h).astype(BF16)

    pick_hs = pick(hs, c)
    row_sel = (lax.broadcasted_iota(jnp.int32, (SUBLANES, LANES), 1)
               == G_GA + lax.broadcasted_iota(jnp.int32, (SUBLANES, LANES), 0)).astype(BF16)
    gates = [g_ref[i] + bias_ref[...] for i in seqs]
    xgs = [jnp.where(is_a, -jnp.exp(alog_ref[...]) * _softplus(g), _sigmoid(g)) for g in gates]
    csums = [jnp.where(is_a, _dot01_left(incl_b, xg), xg) for xg in xgs]
    cols_hs = _dot01_right(csums, pick_hs)
    yield 'stage'
    bc_hs, beta_hs = [x[:, :hs] for x in cols_hs], [x[:, hs:] for x in cols_hs]
    if c == HEAD_V:
        bc_hv, beta_hv = bc_hs, beta_hs
    else:
        cols_hv = _dot01_right(csums, pick(W_GROUP, HEAD_V))
        bc_hv, beta_hv = [x[:, :W_GROUP] for x in cols_hv], [x[:, W_GROUP:] for x in cols_hv]
    b_rows = [sum(lax.dot_general(row_sel, part, (((1,), (1,)), ((), ())), preferred_element_type=F32)
                  for part in _split3(cs)) for cs in csums]
    br_hs = [jnp.concatenate([br[h:h + 1, :] for h in range(N_HEADS)], axis=1) for br in b_rows]
    decays = [jnp.exp(jnp.where(incl4, bc_hs[i] - br_hs[i], 0.0)) for i in seqs]
    yield 'stage'
    qk_raw = [jnp.concatenate([q_ref[i], k_ref[i]], axis=0) for i in seqs]
    norms = _group_sums([x * x for x in qk_raw], group_sum)
    qk_n = [qk_raw[i] * lax.rsqrt(norms[i] + EPS) for i in seqs]
    qs = [x[:c] * (HEAD_V ** -0.5) for x in qk_n]
    ks = [x[c:] for x in qk_n]
    vs = [v_ref[i] for i in seqs]
    kq = [jnp.concatenate([ks[i], qs[i]], axis=0).astype(BF16) for i in seqs]
    yield 'stage'
    states = [s_s[i] for i in seqs]
    scores = [lax.dot_general(kq[i], _head_blocks(ks[i].astype(BF16), same_hv),
                              (((1,), (1,)), ((), ())), preferred_element_type=F32) for i in seqs]
    on_state = [jnp.dot(kq[i], states[i].astype(BF16), preferred_element_type=F32) for i in seqs]
    yield 'stage'
    lows = [jnp.where(strict4, beta_hs[i] * decays[i] * scores[i][:c], 0.0) for i in seqs]
    invs = yield from _unit_lower_inverses(lows, eye4, same_hs, c)
    ebs = [jnp.exp(bc_hv[i]) for i in seqs]
    rhss = [beta_hv[i] * (vs[i] - ebs[i] * on_state[i][:c]) for i in seqs]
    yield 'stage'
    us = [_heads_dot_split(invs[i], rhss[i], same_hv) for i in seqs]
    yield 'stage'
    qkm = [jnp.where(incl4, decays[i] * scores[i][c:], 0.0).astype(BF16) for i in seqs]
    outs = [ebs[i] * on_state[i][c:]
            + jnp.dot(qkm[i], _head_blocks(us[i].astype(BF16), same_hv), preferred_element_type=F32)
            for i in seqs]
    b_ends = [bc_hv[i][c - 1:c, :] for i in seqs]
    upds = [jnp.where(same_vv, _dot_tn(ks[i] * jnp.exp(b_ends[i] - bc_hv[i]), us[i]), 0.0) for i in seqs]
    yield 'stage'
    ng = ng_ref[...]
    sum_sq = _group_sums([o * o for o in outs], group_sum)
    for i in seqs:
        s_s[i] = jnp.exp(b_ends[i]) * states[i] + upds[i]
        o_ref[i] = (outs[i] * lax.rsqrt(sum_sq[i] * (1.0 / HEAD_V) + EPS) * ng * _silu(z_ref[i]))
    yield 'final'

    @pl.when(j == pl.num_programs(1) - 1)
    def _():
        for i in seqs:
            for h in range(N_HEADS):
                s_out[i, h] = s_s[i, _hs(h), _hs(h)]


GLA_SUB = SUBLANES


def _gla_steps(q_ref, k_ref, v_ref, r_ref, g_ref, w2_ref, gb_ref, ng_ref, s0_ref,
               o_ref, s_out, s_s, *, c, nb):
    j = pl.program_id(1)
    dk_all = N_HEADS * GLA_DK
    hs = N_HEADS * c
    nsb = c // GLA_SUB
    seqs = range(nb)

    @pl.when(j == 0)
    def _():
        s_s[...] = jnp.zeros_like(s_s)
        for i in seqs:
            for h in range(N_HEADS):
                s_s[i, h * GLA_DK:(h + 1) * GLA_DK, _hs(h)] = s0_ref[i, h]

    yield 'stage'

    incl_b = _tri(c).astype(BF16)
    er = lax.broadcasted_iota(jnp.int32, (2 * c, c), 0)
    ec = lax.broadcasted_iota(jnp.int32, (2 * c, c), 1)
    et = er & (c - 1)
    edge_sel = (ec == jnp.where(er < c, (et & ~(GLA_SUB - 1)) - 1, et | (GLA_SUB - 1))).astype(BF16)
    t_idx = lax.broadcasted_iota(jnp.int32, (c, hs), 0)
    s_idx = lax.broadcasted_iota(jnp.int32, (c, hs), 1) & (c - 1)
    sub_shift = GLA_SUB.bit_length() - 1
    blk_dist = (t_idx >> sub_shift) - (s_idx >> sub_shift)
    diag_off = t_idx - s_idx
    same_hk = _same_head(hs, dk_all, c, GLA_DK)
    same_hv = _same_head(hs, W_GROUP, c, HEAD_V)
    same_kv = _same_head(dk_all, W_GROUP, GLA_DK, HEAD_V)
    head_rep = _same_head(dk_all, hs, GLA_DK, c).astype(BF16)
    group_sum = _same_head(W_GROUP, W_GROUP, HEAD_V, HEAD_V).astype(BF16)
    row8 = lax.broadcasted_iota(jnp.int32, (c, dk_all), 0) & (GLA_SUB - 1)
    last_row = lax.broadcasted_iota(jnp.int32, (c, dk_all), 0) == c - 1

    log_as = [_log_sigmoid(_dot(g_ref[i], w2_ref[...]) + gb_ref[...]) / GLA_TAU for i in seqs]
    bs = [_dot01_left(incl_b, la) for la in log_as]
    edges = [_dot01_left(edge_sel, b) for b in bs]
    yield 'stage'
    e_prev = [e[:c] for e in edges]
    e_own = [e[c:] for e in edges]
    qs = [q_ref[i] * (GLA_DK ** -0.5) for i in seqs]
    ks = [k_ref[i] for i in seqs]
    vs = [v_ref[i] for i in seqs]
    q_hat = [qs[i] * jnp.exp(bs[i] - e_prev[i]) for i in seqs]
    k_hat = [ks[i] * jnp.exp(e_own[i] - bs[i]) for i in seqs]
    yield 'stage'

    def q_for_distance(i, m):
        if m == 0:
            return q_hat[i]
        rows = GLA_SUB * m
        shifted = jnp.concatenate([jnp.zeros((rows, dk_all), F32), e_prev[i][:c - rows]], axis=0)
        return q_hat[i] * jnp.exp(e_prev[i] - shifted)

    a_parts = []
    for i in seqs:
        lhs = jnp.concatenate([q_for_distance(i, m) for m in range(nsb - 1)], axis=0).astype(BF16)
        prod = lax.dot_general(lhs, _head_blocks(k_hat[i].astype(BF16), same_hk),
                               (((1,), (1,)), ((), ())), preferred_element_type=F32)
        a_parts.append(sum(jnp.where(blk_dist == m + 1, prod[m * c:(m + 1) * c], 0.0)
                           for m in range(nsb - 1)))
        yield 'stage'

    def rot8(x, d):
        return x if d == 0 else pltpu.roll(x, d, 0)

    for i in seqs:
        ws = [jnp.where(row8 >= d, qs[i] * rot8(ks[i], d) * jnp.exp(bs[i] - rot8(bs[i], d)), 0.0)
              for d in range(GLA_SUB)]
        prod = jnp.dot(jnp.concatenate(ws, axis=0).astype(BF16), head_rep, preferred_element_type=F32)
        a_parts[i] = a_parts[i] + sum(jnp.where(diag_off == d, prod[d * c:(d + 1) * c], 0.0)
                                      for d in range(GLA_SUB))
        yield 'stage'
    states = [s_s[i] for i in seqs]
    outs = [jnp.dot(a_parts[i].astype(BF16), _head_blocks(vs[i].astype(BF16), same_hv),
                    preferred_element_type=F32)
            + _dot(qs[i] * jnp.exp(bs[i]), states[i]) for i in seqs]
    yield 'stage'
    b_last = [b[c - 1:c, :] for b in bs]
    upds = [jnp.where(same_kv, _dot_tn(ks[i] * jnp.exp(b_last[i] - bs[i]), vs[i]), 0.0) for i in seqs]
    ones_cv = jnp.ones((c, W_GROUP), BF16)
    decay_rows = [sum(lax.dot_general(part, ones_cv, (((0,), (0,)), ((), ())), preferred_element_type=F32)
                      for part in _split3(jnp.where(last_row, b, 0.0))) for b in bs]
    yield 'stage'
    sum_sq = _group_sums([o * o for o in outs], group_sum)
    ng = ng_ref[...]
    for i in seqs:
        s_s[i] = states[i] * jnp.exp(decay_rows[i]) + upds[i]
        o_ref[i] = outs[i] * lax.rsqrt(sum_sq[i] * (1.0 / HEAD_V) + EPS) * ng * _silu(r_ref[i])
    yield 'final'

    @pl.when(j == pl.num_programs(1) - 1)
    def _():
        for i in seqs:
            for h in range(N_HEADS):
                s_out[i, h] = s_s[i, h * GLA_DK:(h + 1) * GLA_DK, _hs(h)]


def _interleave(steps):
    active, parked = list(steps), []
    while active:
        for g in list(active):
            if next(g) == 'final':
                active.remove(g)
                parked.append(g)
    for g in parked:
        for _ in g:
            pass


def _mixers_kernel(mq, mk, mv, mo, gq, gk, gv, gz, lq, lk, lv, lr, g_ref, bias_ref, alog_ref,
                   ml_ng, gdn_ng, w2_ref, gb_ref, gla_ng, c0, n0, m0, gs0, ls0,
                   ml_h, c_out, n_out, m_out, gdn_o, gs_out, gla_o, ls_out,
                   cn_s, m_s, gs_s, ls_s, *, c, nb):
    _interleave([
        _mlstm_steps(mq, mk, mv, mo, g_ref, bias_ref, ml_ng, c0, n0, m0, ml_h, c_out, n_out, m_out,
                     cn_s, m_s, c=c, nb=nb),
        _gdn_steps(gq, gk, gv, gz, g_ref, bias_ref, alog_ref, gdn_ng, gs0, gdn_o, gs_out, gs_s,
                   c=c, nb=nb),
        _gla_steps(lq, lk, lv, lr, g_ref, w2_ref, gb_ref, gla_ng, ls0, gla_o, ls_out, ls_s,
                   c=c, nb=nb)])


def _mixers_call(p, gate_bias, alog_row, ml_ng, gdn_ng, w2_pad, gla_gb, gla_ng,
                 ml_c, ml_n, ml_m, gdn_s, gla_s, c, nb):
    b, l, _ = p.shape
    nb = math.gcd(nb, b)
    nc = l // c
    dk_all = N_HEADS * GLA_DK
    blk = lambda cb: pl.BlockSpec((nb, c, W_GROUP), lambda i, j, cb=cb: (i, j, cb))
    blk128 = lambda cb: pl.BlockSpec((nb, c, LANES), lambda i, j, cb=cb: (i, j, cb))
    row = lambda w: pl.BlockSpec((1, w), lambda i, j: (0, 0))
    full = lambda a: pl.BlockSpec((nb,) + a.shape[1:], lambda i, j: (i,) + (0,) * (a.ndim - 1))
    tok = pl.BlockSpec((nb, c, W_GROUP), lambda i, j: (i, j, 0))
    ml_n = ml_n[..., None]
    ml_m = ml_m.reshape(b, 1, N_HEADS)
    states = (ml_c, ml_n, ml_m, gdn_s, gla_s)
    sds = lambda a: jax.ShapeDtypeStruct(a.shape, F32)
    act = jax.ShapeDtypeStruct((b, l, W_GROUP), F32)
    o_ml, ml_c, ml_n, ml_m, o_gdn, gdn_s, o_gla, gla_s = pl.pallas_call(
        functools.partial(_mixers_kernel, c=c, nb=nb),
        grid=(b // nb, nc),
        in_specs=[blk(PB_MQ), blk(PB_MK), blk(PB_MV), blk(PB_MO),
                  blk(PB_GQ), blk(PB_GK), blk(PB_GV), blk(PB_GZ),
                  blk128(PB128_LQ), blk128(PB128_LK), blk(PB_LV), blk(PB_LR),
                  blk128(PB128_GATES), row(LANES), row(LANES), row(W_GROUP), row(W_GROUP),
                  pl.BlockSpec((LANES, dk_all), lambda i, j: (0, 0)), row(dk_all), row(W_GROUP)]
                 + [full(a) for a in states],
        out_specs=[tok, full(ml_c), full(ml_n), full(ml_m), tok, full(gdn_s), tok, full(gla_s)],
        out_shape=[act, sds(ml_c), sds(ml_n), sds(ml_m), act, sds(gdn_s), act, sds(gla_s)],
        scratch_shapes=[pltpu.VMEM((nb, W_GROUP, 2 * W_GROUP), F32), pltpu.VMEM((nb, 1, LANES), F32),
                        pltpu.VMEM((nb, W_GROUP, W_GROUP), F32), pltpu.VMEM((nb, dk_all, W_GROUP), F32)],
        compiler_params=pltpu.CompilerParams(dimension_semantics=("parallel", "arbitrary"),
                                             vmem_limit_bytes=VMEM_LIMIT),
        name="mixers",
    )(*([p] * 13), gate_bias, alog_row, ml_ng, gdn_ng, w2_pad, gla_gb, gla_ng, *states)
    return o_ml, ml_c, ml_n[..., 0], ml_m[:, 0, :], o_gdn, gdn_s, o_gla, gla_s


def _outffn_kernel(x_ref, a_ref, b_ref, c_ref, d_ref, mod_ref, g2_ref, gf_ref, wo_ref, w1_ref, w2_ref,
                   o_ref, *, ff_chunk, final):
    ns, tm, d = x_ref.shape
    flat = lambda y: y.reshape(ns * tm, y.shape[-1])
    per_seq = lambda y: y.reshape(ns, tm, y.shape[-1])
    mod = mod_ref[...]
    mixed = flat(jnp.concatenate([a_ref[...], b_ref[...], c_ref[...], d_ref[...]], axis=-1)).astype(BF16)
    x = x_ref[...] + mod[:, 2:3, :] * per_seq(jnp.dot(mixed, wo_ref[...], preferred_element_type=F32))
    h = flat(_modulated_norm(x, g2_ref[...], mod[:, 4:5, :], mod[:, 3:4, :])).astype(BF16)
    d_ff = w1_ref.shape[1]
    acc = jnp.zeros((ns * tm, d), F32)
    for f0 in range(0, d_ff, ff_chunk):
        a = jnp.maximum(jnp.dot(h, w1_ref[:, f0:f0 + ff_chunk], preferred_element_type=F32), 0.0)
        acc = acc + jnp.dot((a * a).astype(BF16), w2_ref[f0:f0 + ff_chunk, :],
                            preferred_element_type=F32)
    x = x + mod[:, 5:6, :] * per_seq(acc)
    if final:
        x = x * lax.rsqrt(jnp.mean(x * x, axis=-1, keepdims=True) + EPS) * gf_ref[...]
    o_ref[...] = x


def _outffn_call(x, mixers, mod, g2, gf, wo, w1, w2, tm, ns, final):
    b, l, d = x.shape
    d_ff = w1.shape[1]
    tok = lambda w: pl.BlockSpec((ns, tm, w), lambda i, j: (i, j, 0))
    const = lambda shape: pl.BlockSpec(shape, lambda i, j: (0,) * len(shape),
                                       pipeline_mode=pl.Buffered(1))
    return pl.pallas_call(
        functools.partial(_outffn_kernel, ff_chunk=1024, final=final),
        grid=(b // ns, l // tm),
        in_specs=[tok(d), tok(W_GROUP), tok(W_GROUP), tok(W_GROUP), tok(W_GROUP),
                  pl.BlockSpec((ns, 6, d), lambda i, j: (i, 0, 0)),
                  pl.BlockSpec((1, d), lambda i, j: (0, 0)),
                  pl.BlockSpec((1, d), lambda i, j: (0, 0)),
                  const((d, d)), const((d, d_ff)), const((d_ff, d))],
        out_specs=tok(d),
        out_shape=jax.ShapeDtypeStruct((b, l, d), F32),
        compiler_params=pltpu.CompilerParams(dimension_semantics=("parallel", "parallel"),
                                             vmem_limit_bytes=VMEM_LIMIT),
        name="outproj_ffn",
    )(x, *mixers, mod, g2, gf, wo, w1, w2)


def _permute_w_in(w_in):
    d = w_in.shape[0]
    sizes = (256, 256, 256, 256, 256, 256, 4, 4, 256, 768, 4, 4, 256, 128, 128, 256, 16, 256)
    offs = [0]
    for s in sizes:
        offs.append(offs[-1] + s)
    (sq, sk, sv, mq, mk, mv, mi, mf, mo, gqkv, gb, ga, gz, lq, lk, lv, lg, lr) = [
        w_in[:, offs[i]:offs[i + 1]] for i in range(len(sizes))]
    pad = jnp.zeros((d, LANES - (4 * N_HEADS + GLA_RANK)), w_in.dtype)
    return jnp.concatenate([sk, sv, sq, mq, mk, mv, mo, gqkv, gz, lv, lr, lq, lk,
                            mi, mf, gb, ga, lg, pad], axis=1)


def _row128(pieces):
    row = jnp.zeros((LANES,), F32)
    for off, vec in pieces:
        row = row.at[off:off + vec.shape[0]].set(vec)
    return row.reshape(1, LANES)


DENSE_TILE_ROWS = 512


class _Tiles(NamedTuple):
    tm: int
    dense_seqs: int
    sb_tq: int
    sb_tk: int
    chunk: int


def _tiles(b, l):
    tm = min(DENSE_TILE_ROWS, l)
    return _Tiles(tm=tm, dense_seqs=math.gcd(b, DENSE_TILE_ROWS // tm), sb_tq=min(256, l),
                  sb_tk=max(min(256, l), LANES), chunk=min(64, l))


def _layer(x, mod, lw, states, tiles, layer, depth, kv_prev, final_g):
    tm, dense_seqs, sb_tq, sb_tk, chunk = tiles
    (n1, n2, w_in_p, gate_bias, ml_ng, conv_w, alog_row, gdn_ng, w2_pad, gla_gb, gla_ng,
     w_out, w_ff1, w_ff2) = lw
    (sb_k_past, sb_v_past, ml_c, ml_n, ml_m, gdn_s, gdn_buf, gla_s) = states
    b, l, _ = x.shape
    final = layer == depth - 1
    k_buf, v_buf, p, gdn_buf = _inproj_call(x, mod, n1, w_in_p, conv_w, gdn_buf, tm, dense_seqs,
                                            layer, depth, kv_prev)
    if sb_k_past is None:
        o_sb = _sb_call(p, PB_SQ, k_buf, v_buf, l, sb_tq, sb_tk, 0, kv_layer=layer)
    else:
        past = sb_k_past.shape[1]
        lk = -(-(past + l) // sb_tk) * sb_tk
        padz = jnp.zeros((b, lk - past - l, W_GROUP), F32)
        k_all = jnp.concatenate([sb_k_past.reshape(b, past, W_GROUP), k_buf[layer], padz], axis=1)
        v_all = jnp.concatenate([sb_v_past.reshape(b, past, W_GROUP), v_buf[layer], padz], axis=1)
        o_sb = _sb_call(p, PB_SQ, k_all, v_all, l, sb_tq, sb_tk, past)
    o_ml, ml_c, ml_n, ml_m, o_gdn, gdn_s, o_gla, gla_s = _mixers_call(
        p, gate_bias, alog_row, ml_ng, gdn_ng, w2_pad, gla_gb, gla_ng, ml_c, ml_n, ml_m, gdn_s, gla_s,
        chunk, MIXER_SEQS_PER_STEP)
    x = _outffn_call(x, (o_sb, o_ml, o_gdn, o_gla), mod, n2, final_g, w_out, w_ff1, w_ff2, tm,
                     dense_seqs, final)
    return x, (k_buf, v_buf), (ml_c, ml_n, ml_m, gdn_s, gdn_buf, gla_s)


def kernel(x_prompt, x_sample, cache_sb_k, cache_sb_v, state_mlstm_C, state_mlstm_n, state_mlstm_m, state_gdn_S, state_gdn_conv, state_gla_S, c_prompt, c_sample, norm1_g, norm2_g, w_ada, b_ada, w_in, mlstm_i_bias, mlstm_f_bias, mlstm_norm_g, gdn_conv_w, gdn_a_log, gdn_dt_bias, gdn_norm_g, gla_w_gate2, gla_gate_bias, gla_norm_g, w_out, w_ff1, w_ff2, final_g):
    depth = w_in.shape[0]
    bp, lp, d = x_prompt.shape
    bs, ls, _ = x_sample.shape
    dk_all = N_HEADS * GLA_DK
    mod_all = _ada_call(jnp.concatenate([c_prompt, c_sample], axis=0), w_ada, b_ada)
    mod_all = mod_all.reshape(depth, bp + bs, 6, d)
    final_row = final_g.reshape(1, d)
    xp, xs = x_prompt, x_sample
    p_list, s_list = [], []
    kv_p = kv_s = None
    for l in range(depth):
        gate_bias = _row128([(G_MI, mlstm_i_bias[l]), (G_MF, mlstm_f_bias[l]), (G_GA, gdn_dt_bias[l])])
        alog_row = _row128([(G_GA, gdn_a_log[l])])
        w2_pad = jnp.zeros((LANES, dk_all), F32).at[G_LG:G_LG + GLA_RANK, :].set(gla_w_gate2[l])
        lw = (norm1_g[l].reshape(1, d), norm2_g[l].reshape(1, d), _permute_w_in(w_in[l]).astype(BF16),
              gate_bias, mlstm_norm_g[l].reshape(1, W_GROUP), gdn_conv_w[l], alog_row,
              gdn_norm_g[l].reshape(1, W_GROUP), w2_pad.astype(BF16),
              gla_gate_bias[l].reshape(1, dk_all), gla_norm_g[l].reshape(1, W_GROUP),
              w_out[l].astype(BF16), w_ff1[l].astype(BF16), w_ff2[l].astype(BF16))
        fresh = (None, None, jnp.zeros((bp,) + state_mlstm_C.shape[2:], F32),
                 jnp.zeros((bp,) + state_mlstm_n.shape[2:], F32),
                 jnp.zeros((bp,) + state_mlstm_m.shape[2:], F32),
                 jnp.zeros((bp,) + state_gdn_S.shape[2:], F32),
                 jnp.zeros((bp,) + state_gdn_conv.shape[2:], F32),
                 jnp.zeros((bp,) + state_gla_S.shape[2:], F32))
        past = (cache_sb_k[l], cache_sb_v[l], state_mlstm_C[l], state_mlstm_n[l], state_mlstm_m[l],
                state_gdn_S[l], state_gdn_conv[l], state_gla_S[l])
        xp, kv_p, st_p = _layer(xp, mod_all[l, :bp], lw, fresh, _tiles(bp, lp), l, depth, kv_p, final_row)
        xs, kv_s, st_s = _layer(xs, mod_all[l, bp:], lw, past, _tiles(bs, ls), l, depth, kv_s, final_row)
        p_list.append(st_p)
        s_list.append(st_s)
    per_head = lambda a: a.reshape(a.shape[:-1] + (N_HEADS, HEAD_V))
    stacked_p = [jnp.stack([st[i] for st in p_list]) for i in range(6)]
    stacked_s = [jnp.stack([st[i] for st in s_list]) for i in range(6)]
    return (xp, xs, per_head(kv_p[0]), per_head(kv_p[1]), *stacked_p,
            per_head(kv_s[0]), per_head(kv_s[1]), *stacked_s)
```

```python
import functools
import math
from typing import NamedTuple

import jax
import jax.numpy as jnp
from jax import lax
from jax.experimental import pallas as pl
from jax.experimental.pallas import tpu as pltpu

F32 = jnp.float32
BF16 = jnp.bfloat16
EPS = 1e-6
N_HEADS = 4
HEAD_V = 64
GLA_DK = 32
CONV_W = 4
GLA_TAU = 16.0
W_GROUP = N_HEADS * HEAD_V
LANES = 128
SUBLANES = 8
VMEM_LIMIT = 56 * 1024 * 1024
MIXER_SEQS_PER_STEP = 8

PB_SQ, PB_MQ, PB_MK, PB_MV, PB_MO, PB_GQ, PB_GK, PB_GV, PB_GZ, PB_LV, PB_LR = range(11)
PB128_LQ, PB128_LK, PB128_GATES = 22, 23, 24
P_WIDTH = 25 * LANES
G_MI, G_MF, G_GB, G_GA, G_LG = 0, 4, 8, 12, 16
GLA_RANK = 16


def _dot(a, b):
    return jnp.dot(a.astype(BF16), b.astype(BF16), preferred_element_type=F32)


def _dot_tn(a, b):
    return lax.dot_general(a.astype(BF16), b.astype(BF16), (((0,), (0,)), ((), ())),
                           preferred_element_type=F32)


def _split3(x):
    x1 = x.astype(BF16)
    r1 = x - x1.astype(F32)
    x2 = r1.astype(BF16)
    x3 = (r1 - x2.astype(F32)).astype(BF16)
    return x1, x2, x3


def _split2(x):
    x1 = x.astype(BF16)
    x2 = (x - x1.astype(F32)).astype(BF16)
    return x1, x2


def _dot01_left(m01, x):
    return sum(jnp.dot(m01, p, preferred_element_type=F32) for p in _split3(x))


def _dot01_right(xs, m01):
    n = xs[0].shape[0]
    stacked = jnp.concatenate([part for x in xs for part in _split3(x)], axis=0)
    y = jnp.dot(stacked, m01, preferred_element_type=F32)
    return [y[3 * i * n:(3 * i + 1) * n] + (y[(3 * i + 1) * n:(3 * i + 2) * n]
                                             + y[(3 * i + 2) * n:(3 * i + 3) * n])
            for i in range(len(xs))]


def _group_sums(xs, group_sum):
    n = xs[0].shape[0]
    y = jnp.dot(jnp.concatenate([x.astype(BF16) for x in xs], axis=0), group_sum,
                preferred_element_type=F32)
    return [y[i * n:(i + 1) * n] for i in range(len(xs))]


def _log_sigmoid(x):
    return jnp.minimum(x, 0.0) - jnp.log(1.0 + jnp.exp(-jnp.abs(x)))


def _sigmoid(x):
    return 1.0 / (1.0 + jnp.exp(-x))


def _silu(x):
    return x * _sigmoid(x)


def _softplus(x):
    return jnp.maximum(x, 0.0) + jnp.log(1.0 + jnp.exp(-jnp.abs(x)))


def _tri(c):
    r = lax.broadcasted_iota(jnp.int32, (c, c), 0)
    s = lax.broadcasted_iota(jnp.int32, (c, c), 1)
    return r >= s


def _hs(h, w=HEAD_V):
    return slice(h * w, (h + 1) * w)


def _ada_kernel(c_ref, w_ref, b_ref, o_ref):
    o_ref[0] = _dot(_silu(c_ref[...]), w_ref[0]) + b_ref[0]


def _ada_call(c_all, w_ada, b_ada, tn=512):
    depth, d, n = w_ada.shape
    rows = c_all.shape[0]
    return pl.pallas_call(
        _ada_kernel,
        grid=(depth, n // tn),
        in_specs=[pl.BlockSpec((rows, d), lambda l, j: (0, 0)),
                  pl.BlockSpec((1, d, tn), lambda l, j: (l, 0, j)),
                  pl.BlockSpec((1, 1, tn), lambda l, j: (l, 0, j))],
        out_specs=pl.BlockSpec((1, rows, tn), lambda l, j: (l, 0, j)),
        out_shape=jax.ShapeDtypeStruct((depth, rows, n), F32),
        compiler_params=pltpu.CompilerParams(dimension_semantics=("parallel", "parallel"),
                                             vmem_limit_bytes=VMEM_LIMIT),
        name="ada_mod",
    )(c_all, w_ada, b_ada.reshape(depth, 1, n))


def _modulated_norm(x, g_row, scale_row, shift_row):
    y = x * lax.rsqrt(jnp.mean(x * x, axis=-1, keepdims=True) + EPS) * g_row
    return y * (1.0 + scale_row) + shift_row


def _inproj_kernel(x_ref, mod_ref, g_ref, w_ref, cw_ref, cb0_ref, *refs, col_chunk):
    k_ref, v_ref, p_ref, cb_out, xw_s = refs[-5:]
    j = pl.program_id(1)
    ns, tm = x_ref.shape[0], x_ref.shape[1]
    conv_lo, conv_hi = PB_GQ * W_GROUP, (PB_GV + 1) * W_GROUP

    @pl.when(j == 0)
    def _():
        xw_s[:, 0:SUBLANES, :] = cb0_ref[...]

    mod = mod_ref[...]
    h = _modulated_norm(x_ref[...], g_ref[...], mod[:, 1:2, :], mod[:, 0:1, :])
    h = h.reshape(ns * tm, h.shape[-1]).astype(BF16)
    per_seq = lambda y: y.reshape(ns, tm, y.shape[-1])
    k_ref[0] = per_seq(jnp.dot(h, w_ref[:, 0:W_GROUP], preferred_element_type=F32))
    v_ref[0] = per_seq(jnp.dot(h, w_ref[:, W_GROUP:2 * W_GROUP], preferred_element_type=F32))
    base = 2 * W_GROUP
    bounds = sorted(set(range(0, conv_lo, col_chunk)) | {conv_lo, conv_hi}
                    | set(range(conv_hi, P_WIDTH, col_chunk)) | {P_WIDTH})
    for n0, n1 in zip(bounds[:-1], bounds[1:]):
        if n0 == conv_lo:
            continue
        p_ref[:, :, n0:n1] = per_seq(jnp.dot(h, w_ref[:, base + n0:base + n1],
                                             preferred_element_type=F32))
    raw = per_seq(jnp.dot(h, w_ref[:, base + conv_lo:base + conv_hi], preferred_element_type=F32))
    xw_s[:, SUBLANES:SUBLANES + tm, :] = raw
    cw = cw_ref[...]
    y = raw * cw[CONV_W - 1:CONV_W, :]
    for d in range(1, CONV_W):
        y = y + xw_s[:, SUBLANES - d:SUBLANES - d + tm, :] * cw[CONV_W - 1 - d:CONV_W - d, :]
    p_ref[:, :, conv_lo:conv_hi] = _silu(y)
    xw_s[:, 0:SUBLANES, :] = raw[:, tm - SUBLANES:tm, :]

    @pl.when(j == pl.num_programs(1) - 1)
    def _():
        cb_out[...] = raw[:, tm - SUBLANES:tm, :]


def _inproj_call(x, mod, g, w_perm, conv_w, conv0, tm, ns, layer, depth, kv_prev):
    b, l, d = x.shape
    n_all = w_perm.shape[1]
    conv_dim = conv_w.shape[1]
    pad_rows = SUBLANES - (CONV_W - 1)
    conv0_pad = jnp.concatenate([jnp.zeros((b, pad_rows, conv_dim), F32), conv0], axis=1)
    carry = pl.BlockSpec((ns, SUBLANES, conv_dim), lambda i, j: (i, 0, 0))
    kv_block = pl.BlockSpec((1, ns, tm, W_GROUP), lambda i, j: (layer, i, j, 0))
    kv_shape = jax.ShapeDtypeStruct((depth, b, l, W_GROUP), F32)
    in_specs = [pl.BlockSpec((ns, tm, d), lambda i, j: (i, j, 0)),
                pl.BlockSpec((ns, 6, d), lambda i, j: (i, 0, 0)),
                pl.BlockSpec((1, d), lambda i, j: (0, 0)),
                pl.BlockSpec((d, n_all), lambda i, j: (0, 0), pipeline_mode=pl.Buffered(1)),
                pl.BlockSpec((CONV_W, conv_dim), lambda i, j: (0, 0)),
                carry]
    operands = [x, mod, g, w_perm, conv_w, conv0_pad]
    aliases = {}
    if kv_prev is not None:
        aliases = {len(operands): 0, len(operands) + 1: 1}
        in_specs += [pl.BlockSpec(memory_space=pl.ANY)] * 2
        operands += list(kv_prev)
    k, v, p, conv_new = pl.pallas_call(
        functools.partial(_inproj_kernel, col_chunk=512),
        grid=(b // ns, l // tm),
        in_specs=in_specs,
        out_specs=[kv_block, kv_block,
                   pl.BlockSpec((ns, tm, P_WIDTH), lambda i, j: (i, j, 0)),
                   carry],
        out_shape=[kv_shape, kv_shape,
                   jax.ShapeDtypeStruct((b, l, P_WIDTH), F32),
                   jax.ShapeDtypeStruct((b, SUBLANES, conv_dim), F32)],
        scratch_shapes=[pltpu.VMEM((ns, tm + SUBLANES, conv_dim), F32)],
        input_output_aliases=aliases,
        compiler_params=pltpu.CompilerParams(dimension_semantics=("parallel", "arbitrary"),
                                             vmem_limit_bytes=VMEM_LIMIT),
        name="norm_inproj",
    )(*operands)
    return k, v, p, conv_new[:, pad_rows:, :]


SB_DEAD_LOG = -104.0
SB_BLOCKS_PER_STEP = 4


def _sb_kernel(q_ref, k_ref, v_ref, o_ref, *, tq, tk, q_offset, n_q):
    _interleave([_sb_query_block(q_ref, k_ref, v_ref, o_ref, pl.program_id(1) * n_q + qb,
                                 slice(qb * tq, (qb + 1) * tq), tq=tq, tk=tk, q_offset=q_offset)
                 for qb in range(n_q)])


def _drain(steps):
    try:
        while True:
            next(steps)
    except StopIteration as done:
        return done.value


def _sb_query_block(q_ref, k_ref, v_ref, o_ref, i, rows, *, tq, tk, q_offset):
    q0 = q_offset + i * tq
    j_top = (q0 + tq - 1) // tk
    n_masked = max(tq // tk, 1)
    def suffix_matrix(w):
        return (lax.broadcasted_iota(jnp.int32, (w, w), 0)
                >= lax.broadcasted_iota(jnp.int32, (w, w), 1)).astype(BF16)

    q_all = q_ref[0, rows, :] * (HEAD_V ** -0.5)
    qs = [q_all[:, _hs(h)].astype(BF16) for h in range(N_HEADS)]

    def block_steps(s0, width, suffix, q_rows, accs, runs, causal):
        heads = range(N_HEADS)
        kbs = [k_ref[0, pl.ds(s0, width), _hs(h)].astype(BF16) for h in heads]
        vbs = [v_ref[0, pl.ds(s0, width), _hs(h)].astype(BF16) for h in heads]
        zs = [lax.dot_general(q_rows[h], kbs[h], (((1,), (1,)), ((), ())), preferred_element_type=F32)
              for h in heads]
        yield 'stage'
        sps = [_softplus(z) for z in zs]
        if causal is not None:
            sps = [jnp.where(causal, x, 0.0) for x in sps]
        splits = [_split2(x) for x in sps]
        incs = [jnp.dot(hi, suffix, preferred_element_type=F32)
                + jnp.dot(lo, suffix, preferred_element_type=F32) for hi, lo in splits]
        yield 'stage'
        probs =[jnp.exp(zs[h] - incs[h] - runs[h]) for h in heads]
        if causal is not None:
            probs = [jnp.where(causal, a, 0.0) for a in probs]
        new_accs = [accs[h] + jnp.dot(probs[h].astype(BF16), vbs[h], preferred_element_type=F32)
                    for h in heads]
        new_runs = [runs[h] + incs[h][:, 0:1] for h in heads]
        return tuple(new_accs), tuple(new_runs)

    def live_of(runs):
        return -jnp.min(jnp.minimum(jnp.minimum(runs[0], runs[1]), jnp.minimum(runs[2], runs[3])))

    zeros = lambda r, w: tuple(jnp.zeros((r, w), F32) for _ in range(N_HEADS))
    suffix = suffix_matrix(tk)
    half = tq // 2
    if tq == tk and half % LANES == 0:
        s_diag = pl.multiple_of(j_top * tk, tk)
        tri = (lax.broadcasted_iota(jnp.int32, (half, half), 1)
               < lax.broadcasted_iota(jnp.int32, (half, half), 0))
        suffix_half = suffix_matrix(half)
        q_lo, q_hi = [q[:half] for q in qs], [q[half:] for q in qs]
        acc_lo, run_lo = yield from block_steps(s_diag, half, suffix_half, q_lo, zeros(half, HEAD_V),
                                                zeros(half, 1), tri)
        acc_hi, run_hi = yield from block_steps(s_diag + half, half, suffix_half, q_hi,
                                                zeros(half, HEAD_V), zeros(half, 1), tri)
        acc_hi, run_hi = yield from block_steps(s_diag, half, suffix_half, q_hi, acc_hi, run_hi, None)
        accs = tuple(jnp.concatenate([a, b], axis=0) for a, b in zip(acc_lo, acc_hi))
        runs = tuple(jnp.concatenate([a, b], axis=0) for a, b in zip(run_lo, run_hi))
    else:
        q_pos = q0 + lax.broadcasted_iota(jnp.int32, (tq, tk), 0)
        k_off = lax.broadcasted_iota(jnp.int32, (tq, tk), 1)
        accs, runs = zeros(tq, HEAD_V), zeros(tq, 1)
        for m in range(n_masked):
            s0 = pl.multiple_of((j_top - m) * tk, tk)
            accs, runs = yield from block_steps(s0, tk, suffix, qs, accs, runs, s0 + k_off < q_pos)

    yield 'final'

    def cond(carry):
        j, live, _, _ = carry
        return (j >= 0) & (live > SB_DEAD_LOG)

    def body(carry):
        j, _, accs, runs = carry
        accs, runs = _drain(block_steps(pl.multiple_of(j * tk, tk), tk, suffix, qs, accs, runs, None))
        return j - 1, live_of(runs), accs, runs

    _, _, accs, _ = lax.while_loop(cond, body, (j_top - n_masked, live_of(runs), accs, runs))
    for h in range(N_HEADS):
        o_ref[0, rows, _hs(h)] = accs[h]


def _sb_call(q_src, q_block, k_all, v_all, l, tq, tk, q_offset, kv_layer=None):
    assert (tq % tk == 0 or tk % tq == 0) and q_offset % max(tq, tk) == 0 and tk % LANES == 0
    b = q_src.shape[0]
    n_q = math.gcd(l // tq, SB_BLOCKS_PER_STEP)
    rows = n_q * tq
    if kv_layer is None:
        kv_spec = pl.BlockSpec((1, k_all.shape[1], W_GROUP), lambda i, j: (i, 0, 0))
    else:
        kv_spec = pl.BlockSpec((None, 1, k_all.shape[2], W_GROUP), lambda i, j: (kv_layer, i, 0, 0))
    return pl.pallas_call(
        functools.partial(_sb_kernel, tq=tq, tk=tk, q_offset=q_offset, n_q=n_q),
        grid=(b, l // rows),
        in_specs=[pl.BlockSpec((1, rows, W_GROUP), lambda i, j: (i, j, q_block)), kv_spec, kv_spec],
        out_specs=pl.BlockSpec((1, rows, W_GROUP), lambda i, j: (i, j, 0)),
        out_shape=jax.ShapeDtypeStruct((b, l, W_GROUP), F32),
        compiler_params=pltpu.CompilerParams(dimension_semantics=("parallel", "parallel"),
                                             vmem_limit_bytes=VMEM_LIMIT),
        name="sb_attention",
    )(q_src, k_all, v_all)


def _cummax_rows(x):
    r = x.shape[0]
    row = lax.broadcasted_iota(jnp.int32, x.shape, 0)
    shift = 1
    while shift < r:
        x = jnp.maximum(x, jnp.where(row >= shift, pltpu.roll(x, shift, 0), -jnp.inf))
        shift *= 2
    return x


def _mlstm_steps(q_ref, k_ref, v_ref, og_ref, g_ref, bias_ref, ng_ref, c0_ref, n0_ref, m0_ref,
                 h_ref, c_out, n_out, m_out, cn_s, m_s, *, c, nb):
    j = pl.program_id(1)

    @pl.when(j == 0)
    def _():
        cn_s[...] = jnp.zeros_like(cn_s)
        m_s[...] = jnp.zeros_like(m_s)
        for i in range(nb):
            for h in range(N_HEADS):
                cn_s[i, _hs(h), _hs(h)] = c0_ref[i, h]
                cn_s[i, _hs(h), W_GROUP + h * HEAD_V:W_GROUP + (h + 1) * HEAD_V] = jnp.broadcast_to(
                    n0_ref[i, h], (HEAD_V, HEAD_V))
            m_s[i, :, 0:N_HEADS] = m0_ref[i]

    yield 'stage'

    seqs = range(nb)
    hs = N_HEADS * c
    lane = lax.broadcasted_iota(jnp.int32, (c, LANES), 1)
    is_f = (lane >= G_MF) & (lane < G_MF + N_HEADS)
    head_lane = lane < N_HEADS
    incl_b = _tri(c).astype(BF16)
    t_idx = lax.broadcasted_iota(jnp.int32, (c, hs), 0)
    s_idx = lax.broadcasted_iota(jnp.int32, (c, hs), 1) & (c - 1)
    incl4 = t_idx >= s_idx
    same_hv = _same_head(hs, W_GROUP, c, HEAD_V)
    same_vv = _same_head(W_GROUP, W_GROUP, HEAD_V, HEAD_V)
    group_sum = same_vv.astype(BF16)

    def expand(lanes, group):
        g = lax.broadcasted_iota(jnp.int32, (LANES, lanes), 0)
        h = lax.broadcasted_iota(jnp.int32, (LANES, lanes), 1) >> (group.bit_length() - 1)
        return (g == h).astype(BF16)

    gi = lax.broadcasted_iota(jnp.int32, (LANES, 2 * LANES), 0)
    li = lax.broadcasted_iota(jnp.int32, (LANES, 2 * LANES), 1)
    to_heads = (jnp.where((li < N_HEADS) & (gi == G_MI + li), 1.0, 0.0)
                - jnp.where((li < N_HEADS) & (gi == G_MF + li), 1.0, 0.0)
                + jnp.where((li >= LANES) & (li < LANES + N_HEADS) & (gi == G_MF + li - LANES), 1.0, 0.0)
                ).astype(BF16)
    row_sel = (lax.broadcasted_iota(jnp.int32, (SUBLANES, LANES), 1)
               == lax.broadcasted_iota(jnp.int32, (SUBLANES, LANES), 0)).astype(BF16)
    gates = [g_ref[i] + bias_ref[...] for i in seqs]
    merged = [jnp.where(is_f, _dot01_left(incl_b, jnp.where(is_f, _log_sigmoid(g), 0.0)), g) for g in gates]
    gb = _dot01_right(merged, to_heads)
    yield 'stage'
    g_c = [jnp.where(head_lane, x[:, :LANES], -jnp.inf) for x in gb]
    b_c = [x[:, LANES:] for x in gb]
    m_prev = [m_s[i] for i in seqs]
    r_c = [jnp.maximum(m_prev[i], _cummax_rows(g_c[i])) for i in seqs]
    r_end = [r[c - 1:c, :] for r in r_c]
    si_c = [jnp.exp(m_prev[i] - r_c[i]) for i in seqs]
    emt_c = [jnp.exp(-(b_c[i] + r_c[i])) for i in seqs]
    w_c = [jnp.exp(g_c[i] - r_end[i]) for i in seqs]
    dec_c = [jnp.broadcast_to(jnp.exp(m_prev[i] - r_end[i]), (c, LANES)) for i in seqs]
    yield 'stage'
    zero_pad = lambda x: jnp.where(head_lane, x, 0.0)
    r_hs = _dot01_right([zero_pad(r) for r in r_c], expand(hs, c))
    cols = _dot01_right([zero_pad(x) for i in seqs for x in (si_c[i], emt_c[i], w_c[i], dec_c[i])],
                        expand(W_GROUP, HEAD_V))
    si_hv, emt_hv, w_hv, dec_hv = (cols[n::4] for n in range(4))
    yield 'stage'
    g_rows = [sum(lax.dot_general(row_sel, part, (((1,), (1,)), ((), ())), preferred_element_type=F32)
                  for part in _split3(zero_pad(x))) for x in g_c]
    g_row_hs = [jnp.concatenate([gr[h:h + 1, :] for h in range(N_HEADS)], axis=1) for gr in g_rows]
    d_hs = [jnp.where(incl4, jnp.exp(g_row_hs[i] - r_hs[i]), 0.0) for i in seqs]
    yield 'stage'
    qs = [q_ref[i].astype(BF16) for i in seqs]
    ks = [k_ref[i] * (HEAD_V ** -0.5) for i in seqs]
    vs = [v_ref[i] for i in seqs]
    cns = [cn_s[i] for i in seqs]
    qk = [lax.dot_general(qs[i], _head_blocks(ks[i].astype(BF16), same_hv), (((1,), (1,)), ((), ())),
                          preferred_element_type=F32) for i in seqs]
    q_cn = [jnp.dot(qs[i], cns[i].astype(BF16), preferred_element_type=F32) for i in seqs]
    yield 'stage'
    scs = [(qk[i] * d_hs[i]).astype(BF16) for i in seqs]
    ones_blocks = same_hv.astype(BF16)
    v_ones = [jnp.concatenate([_head_blocks(vs[i].astype(BF16), same_hv), ones_blocks], axis=1)
              for i in seqs]
    sc_v = [jnp.dot(scs[i], v_ones[i], preferred_element_type=F32) for i in seqs]
    yield 'stage'
    nd = [sc_v[i] + jnp.concatenate([si_hv[i], si_hv[i]], axis=1) * q_cn[i] for i in seqs]
    hcs = [nd[i][:, :W_GROUP] / jnp.maximum(jnp.abs(nd[i][:, W_GROUP:]), emt_hv[i]) for i in seqs]
    kws = [ks[i] * w_hv[i] for i in seqs]
    upds = [_dot_tn(kws[i], jnp.concatenate([vs[i], jnp.ones_like(vs[i])], axis=1)) for i in seqs]
    same_cn = jnp.concatenate([same_vv, same_vv], axis=1)
    sum_sq = _group_sums([x * x for x in hcs], group_sum)
    yield 'stage'
    ng = ng_ref[...]
    for i in seqs:
        dec_row = dec_hv[i][0:1, :]
        cn_s[i] = jnp.concatenate([dec_row, dec_row], axis=1) * cns[i] + jnp.where(same_cn, upds[i], 0.0)
        m_s[i] = jnp.where(head_lane[0:1, :], b_c[i][c - 1:c, :] + r_end[i], 0.0)
        h_ref[i] = (hcs[i] * lax.rsqrt(sum_sq[i] * (1.0 / HEAD_V) + EPS) * ng * _sigmoid(og_ref[i]))
    yield 'final'

    @pl.when(j == pl.num_programs(1) - 1)
    def _():
        for i in seqs:
            for h in range(N_HEADS):
                c_out[i, h] = cn_s[i, _hs(h), _hs(h)]
                n_out[i, h] = cn_s[i, _hs(h), W_GROUP + h * HEAD_V:W_GROUP + h * HEAD_V + 1]
            m_out[i] = m_s[i, :, 0:N_HEADS]


def _same_head(rows, lanes, row_group, lane_group):
    rh = lax.broadcasted_iota(jnp.int32, (rows, lanes), 0) >> (row_group.bit_length() - 1)
    lh = lax.broadcasted_iota(jnp.int32, (rows, lanes), 1) >> (lane_group.bit_length() - 1)
    return rh == lh


def _head_blocks(x, same_head):
    tiled = jnp.concatenate([x] * N_HEADS, axis=0)
    return jnp.where(same_head, tiled, jnp.zeros_like(tiled))


def _heads_dot_split(l, r, same_head):
    return _heads_dot_parts(_split2(l), _split2(r), same_head)


def _heads_dot_parts(l_parts, r_parts, same_head):
    l1, l2 = l_parts
    rb1, rb2 = _head_blocks(r_parts[0], same_head), _head_blocks(r_parts[1], same_head)
    m = l1.shape[0]
    y = jnp.dot(jnp.concatenate([l1, l2], axis=0), rb1, preferred_element_type=F32)
    return y[:m] + (y[m:] + jnp.dot(l1, rb2, preferred_element_type=F32))


def _unit_lower_inverses(lows, eye4, same_head, c):
    ts = [eye4 - a for a in lows]
    a_parts = [_split2(a) for a in lows]
    ps = [_heads_dot_parts(ap, ap, same_head) for ap in a_parts]
    yield 'stage'
    for _ in range(c.bit_length() - 3):
        t_parts = [_split2(t) for t in ts]
        p_parts = [_split2(p) for p in ps]
        stacked = [tuple(jnp.concatenate([tp[n], pp[n]], axis=0) for n in range(2))
                   for tp, pp in zip(t_parts, p_parts)]
        ys = [_heads_dot_parts(sp, pp, same_head) for sp, pp in zip(stacked, p_parts)]
        ts = [t + y[:c] for t, y in zip(ts, ys)]
        ps = [y[c:] for y in ys]
        yield 'stage'
    return [t + _heads_dot_split(t, p, same_head) for t, p in zip(ts, ps)]


def _gdn_steps(q_ref, k_ref, v_ref, z_ref, g_ref, bias_ref, alog_ref, ng_ref, s0_ref,
               o_ref, s_out, s_s, *, c, nb):
    j = pl.program_id(1)
    seqs = range(nb)

    @pl.when(j == 0)
    def _():
        s_s[...] = jnp.zeros_like(s_s)
        for i in seqs:
            for h in range(N_HEADS):
                s_s[i, _hs(h), _hs(h)] = s0_ref[i, h]

    yield 'stage'

    hs = N_HEADS * c
    lane = lax.broadcasted_iota(jnp.int32, (c, LANES), 1)
    is_a = (lane >= G_GA) & (lane < G_GA + N_HEADS)
    incl_b = _tri(c).astype(BF16)
    t_idx = lax.broadcasted_iota(jnp.int32, (c, hs), 0)
    s_idx = lax.broadcasted_iota(jnp.int32, (c, hs), 1) & (c - 1)
    incl4, strict4, eye4 = t_idx >= s_idx, t_idx > s_idx, (t_idx == s_idx).astype(F32)
    same_hs = _same_head(hs, hs, c, c)
    same_hv = _same_head(hs, W_GROUP, c, HEAD_V)
    same_vv = _same_head(W_GROUP, W_GROUP, HEAD_V, HEAD_V)
    group_sum = same_vv.astype(BF16)

    def pick(lanes, group):
        g = lax.broadcasted_iota(jnp.int32, (LANES, 2 * lanes), 0)
        l = lax.broadcasted_iota(jnp.int32, (LANES, 2 * lanes), 1)
        h = (l & (lanes - 1)) >> (group.bit_length() - 1)
        return (g == jnp.where(l < lanes, G_GA, G_GB) + h).astype(BF16)

    pick_hs = pick(hs, c)
    row_sel = (lax.broadcasted_iota(jnp.int32, (SUBLANES, LANES), 1)
               == G_GA + lax.broadcasted_iota(jnp.int32, (SUBLANES, LANES), 0)).astype(BF16)
    gates = [g_ref[i] + bias_ref[...] for i in seqs]
    xgs = [jnp.where(is_a, -jnp.exp(alog_ref[...]) * _softplus(g), _sigmoid(g)) for g in gates]
    csums = [jnp.where(is_a, _dot01_left(incl_b, xg), xg) for xg in xgs]
    cols_hs = _dot01_right(csums, pick_hs)
    yield 'stage'
    bc_hs, beta_hs = [x[:, :hs] for x in cols_hs], [x[:, hs:] for x in cols_hs]
    if c == HEAD_V:
        bc_hv, beta_hv = bc_hs, beta_hs
    else:
        cols_hv = _dot01_right(csums, pick(W_GROUP, HEAD_V))
        bc_hv, beta_hv = [x[:, :W_GROUP] for x in cols_hv], [x[:, W_GROUP:] for x in cols_hv]
    b_rows = [sum(lax.dot_general(row_sel, part, (((1,), (1,)), ((), ())), preferred_element_type=F32)
                  for part in _split3(cs)) for cs in csums]
    br_hs = [jnp.concatenate([br[h:h + 1, :] for h in range(N_HEADS)], axis=1) for br in b_rows]
    decays = [jnp.exp(jnp.where(incl4, bc_hs[i] - br_hs[i], 0.0)) for i in seqs]
    yield 'stage'
    qk_raw = [jnp.concatenate([q_ref[i], k_ref[i]], axis=0) for i in seqs]
    norms = _group_sums([x * x for x in qk_raw], group_sum)
    qk_n = [qk_raw[i] * lax.rsqrt(norms[i] + EPS) for i in seqs]
    qs = [x[:c] * (HEAD_V ** -0.5) for x in qk_n]
    ks = [x[c:] for x in qk_n]
    vs = [v_ref[i] for i in seqs]
    kq = [jnp.concatenate([ks[i], qs[i]], axis=0).astype(BF16) for i in seqs]
    yield 'stage'
    states = [s_s[i] for i in seqs]
    scores = [lax.dot_general(kq[i], _head_blocks(ks[i].astype(BF16), same_hv),
                              (((1,), (1,)), ((), ())), preferred_element_type=F32) for i in seqs]
    on_state = [jnp.dot(kq[i], states[i].astype(BF16), preferred_element_type=F32) for i in seqs]
    yield 'stage'
    lows = [jnp.where(strict4, beta_hs[i] * decays[i] * scores[i][:c], 0.0) for i in seqs]
    invs = yield from _unit_lower_inverses(lows, eye4, same_hs, c)
    ebs = [jnp.exp(bc_hv[i]) for i in seqs]
    rhss = [beta_hv[i] * (vs[i] - ebs[i] * on_state[i][:c]) for i in seqs]
    yield 'stage'
    us = [_heads_dot_split(invs[i], rhss[i], same_hv) for i in seqs]
    yield 'stage'
    qkm = [jnp.where(incl4, decays[i] * scores[i][c:], 0.0).astype(BF16) for i in seqs]
    outs = [ebs[i] * on_state[i][c:]
            + jnp.dot(qkm[i], _head_blocks(us[i].astype(BF16), same_hv), preferred_element_type=F32)
            for i in seqs]
    b_ends = [bc_hv[i][c - 1:c, :] for i in seqs]
    upds = [jnp.where(same_vv, _dot_tn(ks[i] * jnp.exp(b_ends[i] - bc_hv[i]), us[i]), 0.0) for i in seqs]
    yield 'stage'
    ng = ng_ref[...]
    sum_sq = _group_sums([o * o for o in outs], group_sum)
    for i in seqs:
        s_s[i] = jnp.exp(b_ends[i]) * states[i] + upds[i]
        o_ref[i] = (outs[i] * lax.rsqrt(sum_sq[i] * (1.0 / HEAD_V) + EPS) * ng * _silu(z_ref[i]))
    yield 'final'

    @pl.when(j == pl.num_programs(1) - 1)
    def _():
        for i in seqs:
            for h in range(N_HEADS):
                s_out[i, h] = s_s[i, _hs(h), _hs(h)]


GLA_SUB = SUBLANES


def _gla_steps(q_ref, k_ref, v_ref, r_ref, g_ref, w2_ref, gb_ref, ng_ref, s0_ref,
               o_ref, s_out, s_s, *, c, nb):
    j = pl.program_id(1)
    dk_all = N_HEADS * GLA_DK
    hs = N_HEADS * c
    nsb = c // GLA_SUB
    seqs = range(nb)

    @pl.when(j == 0)
    def _():
        s_s[...] = jnp.zeros_like(s_s)
        for i in seqs:
            for h in range(N_HEADS):
                s_s[i, h * GLA_DK:(h + 1) * GLA_DK, _hs(h)] = s0_ref[i, h]

    yield 'stage'

    incl_b = _tri(c).astype(BF16)
    er = lax.broadcasted_iota(jnp.int32, (2 * c, c), 0)
    ec = lax.broadcasted_iota(jnp.int32, (2 * c, c), 1)
    et = er & (c - 1)
    edge_sel = (ec == jnp.where(er < c, (et & ~(GLA_SUB - 1)) - 1, et | (GLA_SUB - 1))).astype(BF16)
    t_idx = lax.broadcasted_iota(jnp.int32, (c, hs), 0)
    s_idx = lax.broadcasted_iota(jnp.int32, (c, hs), 1) & (c - 1)
    sub_shift = GLA_SUB.bit_length() - 1
    blk_dist = (t_idx >> sub_shift) - (s_idx >> sub_shift)
    diag_off = t_idx - s_idx
    same_hk = _same_head(hs, dk_all, c, GLA_DK)
    same_hv = _same_head(hs, W_GROUP, c, HEAD_V)
    same_kv = _same_head(dk_all, W_GROUP, GLA_DK, HEAD_V)
    head_rep = _same_head(dk_all, hs, GLA_DK, c).astype(BF16)
    group_sum = _same_head(W_GROUP, W_GROUP, HEAD_V, HEAD_V).astype(BF16)
    row8 = lax.broadcasted_iota(jnp.int32, (c, dk_all), 0) & (GLA_SUB - 1)
    last_row = lax.broadcasted_iota(jnp.int32, (c, dk_all), 0) == c - 1

    log_as = [_log_sigmoid(_dot(g_ref[i], w2_ref[...]) + gb_ref[...]) / GLA_TAU for i in seqs]
    bs = [_dot01_left(incl_b, la) for la in log_as]
    edges = [_dot01_left(edge_sel, b) for b in bs]
    yield 'stage'
    e_prev = [e[:c] for e in edges]
    e_own = [e[c:] for e in edges]
    qs = [q_ref[i] * (GLA_DK ** -0.5) for i in seqs]
    ks = [k_ref[i] for i in seqs]
    vs = [v_ref[i] for i in seqs]
    q_hat = [qs[i] * jnp.exp(bs[i] - e_prev[i]) for i in seqs]
    k_hat = [ks[i] * jnp.exp(e_own[i] - bs[i]) for i in seqs]
    yield 'stage'

    def q_for_distance(i, m):
        if m == 0:
            return q_hat[i]
        rows = GLA_SUB * m
        shifted = jnp.concatenate([jnp.zeros((rows, dk_all), F32), e_prev[i][:c - rows]], axis=0)
        return q_hat[i] * jnp.exp(e_prev[i] - shifted)

    a_parts = []
    for i in seqs:
        lhs = jnp.concatenate([q_for_distance(i, m) for m in range(nsb - 1)], axis=0).astype(BF16)
        prod = lax.dot_general(lhs, _head_blocks(k_hat[i].astype(BF16), same_hk),
                               (((1,), (1,)), ((), ())), preferred_element_type=F32)
        a_parts.append(sum(jnp.where(blk_dist == m + 1, prod[m * c:(m + 1) * c], 0.0)
                           for m in range(nsb - 1)))
        yield 'stage'

    def rot8(x, d):
        return x if d == 0 else pltpu.roll(x, d, 0)

    for i in seqs:
        ws = [jnp.where(row8 >= d, qs[i] * rot8(ks[i], d) * jnp.exp(bs[i] - rot8(bs[i], d)), 0.0)
              for d in range(GLA_SUB)]
        prod = jnp.dot(jnp.concatenate(ws, axis=0).astype(BF16), head_rep, preferred_element_type=F32)
        a_parts[i] = a_parts[i] + sum(jnp.where(diag_off == d, prod[d * c:(d + 1) * c], 0.0)
                                      for d in range(GLA_SUB))
        yield 'stage'
    states = [s_s[i] for i in seqs]
    outs = [jnp.dot(a_parts[i].astype(BF16), _head_blocks(vs[i].astype(BF16), same_hv),
                    preferred_element_type=F32)
            + _dot(qs[i] * jnp.exp(bs[i]), states[i]) for i in seqs]
    yield 'stage'
    b_last = [b[c - 1:c, :] for b in bs]
    upds = [jnp.where(same_kv, _dot_tn(ks[i] * jnp.exp(b_last[i] - bs[i]), vs[i]), 0.0) for i in seqs]
    ones_cv = jnp.ones((c, W_GROUP), BF16)
    decay_rows = [sum(lax.dot_general(part, ones_cv, (((0,), (0,)), ((), ())), preferred_element_type=F32)
                      for part in _split3(jnp.where(last_row, b, 0.0))) for b in bs]
    yield 'stage'
    sum_sq = _group_sums([o * o for o in outs], group_sum)
    ng = ng_ref[...]
    for i in seqs:
        s_s[i] = states[i] * jnp.exp(decay_rows[i]) + upds[i]
        o_ref[i] = outs[i] * lax.rsqrt(sum_sq[i] * (1.0 / HEAD_V) + EPS) * ng * _silu(r_ref[i])
    yield 'final'

    @pl.when(j == pl.num_programs(1) - 1)
    def _():
        for i in seqs:
            for h in range(N_HEADS):
                s_out[i, h] = s_s[i, h * GLA_DK:(h + 1) * GLA_DK, _hs(h)]


def _interleave(steps):
    active, parked = list(steps), []
    while active:
        for g in list(active):
            if next(g) == 'final':
                active.remove(g)
                parked.append(g)
    for g in parked:
        for _ in g:
            pass


def _mixers_kernel(mq, mk, mv, mo, gq, gk, gv, gz, lq, lk, lv, lr, g_ref, bias_ref, alog_ref,
                   ml_ng, gdn_ng, w2_ref, gb_ref, gla_ng, c0, n0, m0, gs0, ls0,
                   ml_h, c_out, n_out, m_out, gdn_o, gs_out, gla_o, ls_out,
                   cn_s, m_s, gs_s, ls_s, *, c, nb):
    _interleave([
        _mlstm_steps(mq, mk, mv, mo, g_ref, bias_ref, ml_ng, c0, n0, m0, ml_h, c_out, n_out, m_out,
                     cn_s, m_s, c=c, nb=nb),
        _gdn_steps(gq, gk, gv, gz, g_ref, bias_ref, alog_ref, gdn_ng, gs0, gdn_o, gs_out, gs_s,
                   c=c, nb=nb),
        _gla_steps(lq, lk, lv, lr, g_ref, w2_ref, gb_ref, gla_ng, ls0, gla_o, ls_out, ls_s,
                   c=c, nb=nb)])


def _mixers_call(p, gate_bias, alog_row, ml_ng, gdn_ng, w2_pad, gla_gb, gla_ng,
                 ml_c, ml_n, ml_m, gdn_s, gla_s, c, nb):
    b, l, _ = p.shape
    nb = math.gcd(nb, b)
    nc = l // c
    dk_all = N_HEADS * GLA_DK
    blk = lambda cb: pl.BlockSpec((nb, c, W_GROUP), lambda i, j, cb=cb: (i, j, cb))
    blk128 = lambda cb: pl.BlockSpec((nb, c, LANES), lambda i, j, cb=cb: (i, j, cb))
    row = lambda w: pl.BlockSpec((1, w), lambda i, j: (0, 0))
    full = lambda a: pl.BlockSpec((nb,) + a.shape[1:], lambda i, j: (i,) + (0,) * (a.ndim - 1))
    tok = pl.BlockSpec((nb, c, W_GROUP), lambda i, j: (i, j, 0))
    ml_n = ml_n[..., None]
    ml_m = ml_m.reshape(b, 1, N_HEADS)
    states = (ml_c, ml_n, ml_m, gdn_s, gla_s)
    sds = lambda a: jax.ShapeDtypeStruct(a.shape, F32)
    act = jax.ShapeDtypeStruct((b, l, W_GROUP), F32)
    o_ml, ml_c, ml_n, ml_m, o_gdn, gdn_s, o_gla, gla_s = pl.pallas_call(
        functools.partial(_mixers_kernel, c=c, nb=nb),
        grid=(b // nb, nc),
        in_specs=[blk(PB_MQ), blk(PB_MK), blk(PB_MV), blk(PB_MO),
                  blk(PB_GQ), blk(PB_GK), blk(PB_GV), blk(PB_GZ),
                  blk128(PB128_LQ), blk128(PB128_LK), blk(PB_LV), blk(PB_LR),
                  blk128(PB128_GATES), row(LANES), row(LANES), row(W_GROUP), row(W_GROUP),
                  pl.BlockSpec((LANES, dk_all), lambda i, j: (0, 0)), row(dk_all), row(W_GROUP)]
                 + [full(a) for a in states],
        out_specs=[tok, full(ml_c), full(ml_n), full(ml_m), tok, full(gdn_s), tok, full(gla_s)],
        out_shape=[act, sds(ml_c), sds(ml_n), sds(ml_m), act, sds(gdn_s), act, sds(gla_s)],
        scratch_shapes=[pltpu.VMEM((nb, W_GROUP, 2 * W_GROUP), F32), pltpu.VMEM((nb, 1, LANES), F32),
                        pltpu.VMEM((nb, W_GROUP, W_GROUP), F32), pltpu.VMEM((nb, dk_all, W_GROUP), F32)],
        compiler_params=pltpu.CompilerParams(dimension_semantics=("parallel", "arbitrary"),
                                             vmem_limit_bytes=VMEM_LIMIT),
        name="mixers",
    )(*([p] * 13), gate_bias, alog_row, ml_ng, gdn_ng, w2_pad, gla_gb, gla_ng, *states)
    return o_ml, ml_c, ml_n[..., 0], ml_m[:, 0, :], o_gdn, gdn_s, o_gla, gla_s


def _outffn_kernel(x_ref, a_ref, b_ref, c_ref, d_ref, mod_ref, g2_ref, gf_ref, wo_ref, w1_ref, w2_ref,
                   o_ref, *, ff_chunk, final):
    ns, tm, d = x_ref.shape
    flat = lambda y: y.reshape(ns * tm, y.shape[-1])
    per_seq = lambda y: y.reshape(ns, tm, y.shape[-1])
    mod = mod_ref[...]
    mixed = flat(jnp.concatenate([a_ref[...], b_ref[...], c_ref[...], d_ref[...]], axis=-1)).astype(BF16)
    x = x_ref[...] + mod[:, 2:3, :] * per_seq(jnp.dot(mixed, wo_ref[...], preferred_element_type=F32))
    h = flat(_modulated_norm(x, g2_ref[...], mod[:, 4:5, :], mod[:, 3:4, :])).astype(BF16)
    d_ff = w1_ref.shape[1]
    acc = jnp.zeros((ns * tm, d), F32)
    for f0 in range(0, d_ff, ff_chunk):
        a = jnp.maximum(jnp.dot(h, w1_ref[:, f0:f0 + ff_chunk], preferred_element_type=F32), 0.0)
        acc = acc + jnp.dot((a * a).astype(BF16), w2_ref[f0:f0 + ff_chunk, :],
                            preferred_element_type=F32)
    x = x + mod[:, 5:6, :] * per_seq(acc)
    if final:
        x = x * lax.rsqrt(jnp.mean(x * x, axis=-1, keepdims=True) + EPS) * gf_ref[...]
    o_ref[...] = x


def _outffn_call(x, mixers, mod, g2, gf, wo, w1, w2, tm, ns, final):
    b, l, d = x.shape
    d_ff = w1.shape[1]
    tok = lambda w: pl.BlockSpec((ns, tm, w), lambda i, j: (i, j, 0))
    const = lambda shape: pl.BlockSpec(shape, lambda i, j: (0,) * len(shape),
                                       pipeline_mode=pl.Buffered(1))
    return pl.pallas_call(
        functools.partial(_outffn_kernel, ff_chunk=1024, final=final),
        grid=(b // ns, l // tm),
        in_specs=[tok(d), tok(W_GROUP), tok(W_GROUP), tok(W_GROUP), tok(W_GROUP),
                  pl.BlockSpec((ns, 6, d), lambda i, j: (i, 0, 0)),
                  pl.BlockSpec((1, d), lambda i, j: (0, 0)),
                  pl.BlockSpec((1, d), lambda i, j: (0, 0)),
                  const((d, d)), const((d, d_ff)), const((d_ff, d))],
        out_specs=tok(d),
        out_shape=jax.ShapeDtypeStruct((b, l, d), F32),
        compiler_params=pltpu.CompilerParams(dimension_semantics=("parallel", "parallel"),
                                             vmem_limit_bytes=VMEM_LIMIT),
        name="outproj_ffn",
    )(x, *mixers, mod, g2, gf, wo, w1, w2)


def _permute_w_in(w_in):
    d = w_in.shape[0]
    sizes = (256, 256, 256, 256, 256, 256, 4, 4, 256, 768, 4, 4, 256, 128, 128, 256, 16, 256)
    offs = [0]
    for s in sizes:
        offs.append(offs[-1] + s)
    (sq, sk, sv, mq, mk, mv, mi, mf, mo, gqkv, gb, ga, gz, lq, lk, lv, lg, lr) = [
        w_in[:, offs[i]:offs[i + 1]] for i in range(len(sizes))]
    pad = jnp.zeros((d, LANES - (4 * N_HEADS + GLA_RANK)), w_in.dtype)
    return jnp.concatenate([sk, sv, sq, mq, mk, mv, mo, gqkv, gz, lv, lr, lq, lk,
                            mi, mf, gb, ga, lg, pad], axis=1)


def _row128(pieces):
    row = jnp.zeros((LANES,), F32)
    for off, vec in pieces:
        row = row.at[off:off + vec.shape[0]].set(vec)
    return row.reshape(1, LANES)


DENSE_TILE_ROWS = 512


class _Tiles(NamedTuple):
    tm: int
    dense_seqs: int
    sb_tq: int
    sb_tk: int
    chunk: int


def _tiles(b, l):
    tm = min(DENSE_TILE_ROWS, l)
    return _Tiles(tm=tm, dense_seqs=math.gcd(b, DENSE_TILE_ROWS // tm), sb_tq=min(256, l),
                  sb_tk=max(min(256, l), LANES), chunk=min(64, l))


def _layer(x, mod, lw, states, tiles, layer, depth, kv_prev, final_g):
    tm, dense_seqs, sb_tq, sb_tk, chunk = tiles
    (n1, n2, w_in_p, gate_bias, ml_ng, conv_w, alog_row, gdn_ng, w2_pad, gla_gb, gla_ng,
     w_out, w_ff1, w_ff2) = lw
    (sb_k_past, sb_v_past, ml_c, ml_n, ml_m, gdn_s, gdn_buf, gla_s) = states
    b, l, _ = x.shape
    final = layer == depth - 1
    k_buf, v_buf, p, gdn_buf = _inproj_call(x, mod, n1, w_in_p, conv_w, gdn_buf, tm, dense_seqs,
                                            layer, depth, kv_prev)
    if sb_k_past is None:
        o_sb = _sb_call(p, PB_SQ, k_buf, v_buf, l, sb_tq, sb_tk, 0, kv_layer=layer)
    else:
        past = sb_k_past.shape[1]
        lk = -(-(past + l) // sb_tk) * sb_tk
        padz = jnp.zeros((b, lk - past - l, W_GROUP), F32)
        k_all = jnp.concatenate([sb_k_past.reshape(b, past, W_GROUP), k_buf[layer], padz], axis=1)
        v_all = jnp.concatenate([sb_v_past.reshape(b, past, W_GROUP), v_buf[layer], padz], axis=1)
        o_sb = _sb_call(p, PB_SQ, k_all, v_all, l, sb_tq, sb_tk, past)
    o_ml, ml_c, ml_n, ml_m, o_gdn, gdn_s, o_gla, gla_s = _mixers_call(
        p, gate_bias, alog_row, ml_ng, gdn_ng, w2_pad, gla_gb, gla_ng, ml_c, ml_n, ml_m, gdn_s, gla_s,
        chunk, MIXER_SEQS_PER_STEP)
    x = _outffn_call(x, (o_sb, o_ml, o_gdn, o_gla), mod, n2, final_g, w_out, w_ff1, w_ff2, tm,
                     dense_seqs, final)
    return x, (k_buf, v_buf), (ml_c, ml_n, ml_m, gdn_s, gdn_buf, gla_s)


def kernel(x_prompt, x_sample, cache_sb_k, cache_sb_v, state_mlstm_C, state_mlstm_n, state_mlstm_m, state_gdn_S, state_gdn_conv, state_gla_S, c_prompt, c_sample, norm1_g, norm2_g, w_ada, b_ada, w_in, mlstm_i_bias, mlstm_f_bias, mlstm_norm_g, gdn_conv_w, gdn_a_log, gdn_dt_bias, gdn_norm_g, gla_w_gate2, gla_gate_bias, gla_norm_g, w_out, w_ff1, w_ff2, final_g):
    depth = w_in.shape[0]
    bp, lp, d = x_prompt.shape
    bs, ls, _ = x_sample.shape
    dk_all = N_HEADS * GLA_DK
    mod_all = _ada_call(jnp.concatenate([c_prompt, c_sample], axis=0), w_ada, b_ada)
    mod_all = mod_all.reshape(depth, bp + bs, 6, d)
    final_row = final_g.reshape(1, d)
    xp, xs = x_prompt, x_sample
    p_list, s_list = [], []
    kv_p = kv_s = None
    for l in range(depth):
        gate_bias = _row128([(G_MI, mlstm_i_bias[l]), (G_MF, mlstm_f_bias[l]), (G_GA, gdn_dt_bias[l])])
        alog_row = _row128([(G_GA, gdn_a_log[l])])
        w2_pad = jnp.zeros((LANES, dk_all), F32).at[G_LG:G_LG + GLA_RANK, :].set(gla_w_gate2[l])
        lw = (norm1_g[l].reshape(1, d), norm2_g[l].reshape(1, d), _permute_w_in(w_in[l]).astype(BF16),
              gate_bias, mlstm_norm_g[l].reshape(1, W_GROUP), gdn_conv_w[l], alog_row,
              gdn_norm_g[l].reshape(1, W_GROUP), w2_pad.astype(BF16),
              gla_gate_bias[l].reshape(1, dk_all), gla_norm_g[l].reshape(1, W_GROUP),
              w_out[l].astype(BF16), w_ff1[l].astype(BF16), w_ff2[l].astype(BF16))
        fresh = (None, None, jnp.zeros((bp,) + state_mlstm_C.shape[2:], F32),
                 jnp.zeros((bp,) + state_mlstm_n.shape[2:], F32),
                 jnp.zeros((bp,) + state_mlstm_m.shape[2:], F32),
                 jnp.zeros((bp,) + state_gdn_S.shape[2:], F32),
                 jnp.zeros((bp,) + state_gdn_conv.shape[2:], F32),
                 jnp.zeros((bp,) + state_gla_S.shape[2:], F32))
        past = (cache_sb_k[l], cache_sb_v[l], state_mlstm_C[l], state_mlstm_n[l], state_mlstm_m[l],
                state_gdn_S[l], state_gdn_conv[l], state_gla_S[l])
        xp, kv_p, st_p = _layer(xp, mod_all[l, :bp], lw, fresh, _tiles(bp, lp), l, depth, kv_p, final_row)
        xs, kv_s, st_s = _layer(xs, mod_all[l, bp:], lw, past, _tiles(bs, ls), l, depth, kv_s, final_row)
        p_list.append(st_p)
        s_list.append(st_s)
    per_head = lambda a: a.reshape(a.shape[:-1] + (N_HEADS, HEAD_V))
    stacked_p = [jnp.stack([st[i] for st in p_list]) for i in range(6)]
    stacked_s = [jnp.stack([st[i] for st in s_list]) for i in range(6)]
    return (xp, xs, per_head(kv_p[0]), per_head(kv_p[1]), *stacked_p,
            per_head(kv_s[0]), per_head(kv_s[1]), *stacked_s)
```

```python
import functools
import math
from typing import NamedTuple

import jax
import jax.numpy as jnp
from jax import lax
from jax.experimental import pallas as pl
from jax.experimental.pallas import tpu as pltpu

F32 = jnp.float32
BF16 = jnp.bfloat16
EPS = 1e-6
N_HEADS = 4
HEAD_V = 64
GLA_DK = 32
CONV_W = 4
GLA_TAU = 16.0
W_GROUP = N_HEADS * HEAD_V
LANES = 128
SUBLANES = 8
VMEM_LIMIT = 56 * 1024 * 1024
MIXER_SEQS_PER_STEP = 8

PB_SQ, PB_MQ, PB_MK, PB_MV, PB_MO, PB_GQ, PB_GK, PB_GV, PB_GZ, PB_LV, PB_LR = range(11)
PB128_LQ, PB128_LK, PB128_GATES = 22, 23, 24
P_WIDTH = 25 * LANES
G_MI, G_MF, G_GB, G_GA, G_LG = 0, 4, 8, 12, 16
GLA_RANK = 16


def _dot(a, b):
    return jnp.dot(a.astype(BF16), b.astype(BF16), preferred_element_type=F32)


def _dot_tn(a, b):
    return lax.dot_general(a.astype(BF16), b.astype(BF16), (((0,), (0,)), ((), ())),
                           preferred_element_type=F32)


def _split3(x):
    x1 = x.astype(BF16)
    r1 = x - x1.astype(F32)
    x2 = r1.astype(BF16)
    x3 = (r1 - x2.astype(F32)).astype(BF16)
    return x1, x2, x3


def _split2(x):
    x1 = x.astype(BF16)
    x2 = (x - x1.astype(F32)).astype(BF16)
    return x1, x2


def _dot01_left(m01, x):
    return sum(jnp.dot(m01, p, preferred_element_type=F32) for p in _split3(x))


def _dot01_right(xs, m01):
    n = xs[0].shape[0]
    stacked = jnp.concatenate([part for x in xs for part in _split3(x)], axis=0)
    y = jnp.dot(stacked, m01, preferred_element_type=F32)
    return [y[3 * i * n:(3 * i + 1) * n] + (y[(3 * i + 1) * n:(3 * i + 2) * n]
                                             + y[(3 * i + 2) * n:(3 * i + 3) * n])
            for i in range(len(xs))]


def _group_sums(xs, group_sum):
    n = xs[0].shape[0]
    y = jnp.dot(jnp.concatenate([x.astype(BF16) for x in xs], axis=0), group_sum,
                preferred_element_type=F32)
    return [y[i * n:(i + 1) * n] for i in range(len(xs))]


def _log_sigmoid(x):
    return jnp.minimum(x, 0.0) - jnp.log(1.0 + jnp.exp(-jnp.abs(x)))


def _sigmoid(x):
    return 1.0 / (1.0 + jnp.exp(-x))


def _silu(x):
    return x * _sigmoid(x)


def _softplus(x):
    return jnp.maximum(x, 0.0) + jnp.log(1.0 + jnp.exp(-jnp.abs(x)))


def _tri(c):
    r = lax.broadcasted_iota(jnp.int32, (c, c), 0)
    s = lax.broadcasted_iota(jnp.int32, (c, c), 1)
    return r >= s


def _hs(h, w=HEAD_V):
    return slice(h * w, (h + 1) * w)


def _ada_kernel(c_ref, w_ref, b_ref, o_ref):
    o_ref[0] = _dot(_silu(c_ref[...]), w_ref[0]) + b_ref[0]


def _ada_call(c_all, w_ada, b_ada, tn=512):
    depth, d, n = w_ada.shape
    rows = c_all.shape[0]
    return pl.pallas_call(
        _ada_kernel,
        grid=(depth, n // tn),
        in_specs=[pl.BlockSpec((rows, d), lambda l, j: (0, 0)),
                  pl.BlockSpec((1, d, tn), lambda l, j: (l, 0, j)),
                  pl.BlockSpec((1, 1, tn), lambda l, j: (l, 0, j))],
        out_specs=pl.BlockSpec((1, rows, tn), lambda l, j: (l, 0, j)),
        out_shape=jax.ShapeDtypeStruct((depth, rows, n), F32),
        compiler_params=pltpu.CompilerParams(dimension_semantics=("parallel", "parallel"),
                                             vmem_limit_bytes=VMEM_LIMIT),
        name="ada_mod",
    )(c_all, w_ada, b_ada.reshape(depth, 1, n))


def _modulated_norm(x, g_row, scale_row, shift_row):
    y = x * lax.rsqrt(jnp.mean(x * x, axis=-1, keepdims=True) + EPS) * g_row
    return y * (1.0 + scale_row) + shift_row


def _inproj_kernel(x_ref, mod_ref, g_ref, w_ref, cw_ref, cb0_ref, *refs, col_chunk):
    k_ref, v_ref, p_ref, cb_out, xw_s = refs[-5:]
    j = pl.program_id(1)
    ns, tm = x_ref.shape[0], x_ref.shape[1]
    conv_lo, conv_hi = PB_GQ * W_GROUP, (PB_GV + 1) * W_GROUP

    @pl.when(j == 0)
    def _():
        xw_s[:, 0:SUBLANES, :] = cb0_ref[...]

    mod = mod_ref[...]
    h = _modulated_norm(x_ref[...], g_ref[...], mod[:, 1:2, :], mod[:, 0:1, :])
    h = h.reshape(ns * tm, h.shape[-1]).astype(BF16)
    per_seq = lambda y: y.reshape(ns, tm, y.shape[-1])
    k_ref[0] = per_seq(jnp.dot(h, w_ref[:, 0:W_GROUP], preferred_element_type=F32))
    v_ref[0] = per_seq(jnp.dot(h, w_ref[:, W_GROUP:2 * W_GROUP], preferred_element_type=F32))
    base = 2 * W_GROUP
    bounds = sorted(set(range(0, conv_lo, col_chunk)) | {conv_lo, conv_hi}
                    | set(range(conv_hi, P_WIDTH, col_chunk)) | {P_WIDTH})
    for n0, n1 in zip(bounds[:-1], bounds[1:]):
        if n0 == conv_lo:
            continue
        p_ref[:, :, n0:n1] = per_seq(jnp.dot(h, w_ref[:, base + n0:base + n1],
                                             preferred_element_type=F32))
    raw = per_seq(jnp.dot(h, w_ref[:, base + conv_lo:base + conv_hi], preferred_element_type=F32))
    xw_s[:, SUBLANES:SUBLANES + tm, :] = raw
    cw = cw_ref[...]
    y = raw * cw[CONV_W - 1:CONV_W, :]
    for d in range(1, CONV_W):
        y = y + xw_s[:, SUBLANES - d:SUBLANES - d + tm, :] * cw[CONV_W - 1 - d:CONV_W - d, :]
    p_ref[:, :, conv_lo:conv_hi] = _silu(y)
    xw_s[:, 0:SUBLANES, :] = raw[:, tm - SUBLANES:tm, :]

    @pl.when(j == pl.num_programs(1) - 1)
    def _():
        cb_out[...] = raw[:, tm - SUBLANES:tm, :]


def _inproj_call(x, mod, g, w_perm, conv_w, conv0, tm, ns, layer, depth, kv_prev):
    b, l, d = x.shape
    n_all = w_perm.shape[1]
    conv_dim = conv_w.shape[1]
    pad_rows = SUBLANES - (CONV_W - 1)
    conv0_pad = jnp.concatenate([jnp.zeros((b, pad_rows, conv_dim), F32), conv0], axis=1)
    carry = pl.BlockSpec((ns, SUBLANES, conv_dim), lambda i, j: (i, 0, 0))
    kv_block = pl.BlockSpec((1, ns, tm, W_GROUP), lambda i, j: (layer, i, j, 0))
    kv_shape = jax.ShapeDtypeStruct((depth, b, l, W_GROUP), F32)
    in_specs = [pl.BlockSpec((ns, tm, d), lambda i, j: (i, j, 0)),
                pl.BlockSpec((ns, 6, d), lambda i, j: (i, 0, 0)),
                pl.BlockSpec((1, d), lambda i, j: (0, 0)),
                pl.BlockSpec((d, n_all), lambda i, j: (0, 0), pipeline_mode=pl.Buffered(1)),
                pl.BlockSpec((CONV_W, conv_dim), lambda i, j: (0, 0)),
                carry]
    operands = [x, mod, g, w_perm, conv_w, conv0_pad]
    aliases = {}
    if kv_prev is not None:
        aliases = {len(operands): 0, len(operands) + 1: 1}
        in_specs += [pl.BlockSpec(memory_space=pl.ANY)] * 2
        operands += list(kv_prev)
    k, v, p, conv_new = pl.pallas_call(
        functools.partial(_inproj_kernel, col_chunk=512),
        grid=(b // ns, l // tm),
        in_specs=in_specs,
        out_specs=[kv_block, kv_block,
                   pl.BlockSpec((ns, tm, P_WIDTH), lambda i, j: (i, j, 0)),
                   carry],
        out_shape=[kv_shape, kv_shape,
                   jax.ShapeDtypeStruct((b, l, P_WIDTH), F32),
                   jax.ShapeDtypeStruct((b, SUBLANES, conv_dim), F32)],
        scratch_shapes=[pltpu.VMEM((ns, tm + SUBLANES, conv_dim), F32)],
        input_output_aliases=aliases,
        compiler_params=pltpu.CompilerParams(dimension_semantics=("parallel", "arbitrary"),
                                             vmem_limit_bytes=VMEM_LIMIT),
        name="norm_inproj",
    )(*operands)
    return k, v, p, conv_new[:, pad_rows:, :]


SB_DEAD_LOG = -104.0
SB_BLOCKS_PER_STEP = 8


def _sb_kernel(q_ref, k_ref, v_ref, o_ref, *, tq, tk, q_offset, n_q):
    _interleave([_sb_query_block(q_ref, k_ref, v_ref, o_ref, pl.program_id(1) * n_q + qb,
                                 slice(qb * tq, (qb + 1) * tq), tq=tq, tk=tk, q_offset=q_offset)
                 for qb in range(n_q)])


def _drain(steps):
    try:
        while True:
            next(steps)
    except StopIteration as done:
        return done.value


def _sb_query_block(q_ref, k_ref, v_ref, o_ref, i, rows, *, tq, tk, q_offset):
    q0 = q_offset + i * tq
    j_top = (q0 + tq - 1) // tk
    n_masked = max(tq // tk, 1)
    def suffix_matrix(w):
        return (lax.broadcasted_iota(jnp.int32, (w, w), 0)
                >= lax.broadcasted_iota(jnp.int32, (w, w), 1)).astype(BF16)

    q_all = q_ref[0, rows, :] * (HEAD_V ** -0.5)
    qs = [q_all[:, _hs(h)].astype(BF16) for h in range(N_HEADS)]

    def block_steps(s0, width, suffix, q_rows, accs, runs, causal):
        heads = range(N_HEADS)
        kbs = [k_ref[0, pl.ds(s0, width), _hs(h)].astype(BF16) for h in heads]
        vbs = [v_ref[0, pl.ds(s0, width), _hs(h)].astype(BF16) for h in heads]
        zs = [lax.dot_general(q_rows[h], kbs[h], (((1,), (1,)), ((), ())), preferred_element_type=F32)
              for h in heads]
        yield 'stage'
        sps = [_softplus(z) for z in zs]
        if causal is not None:
            sps = [jnp.where(causal, x, 0.0) for x in sps]
        splits = [_split2(x) for x in sps]
        incs = [jnp.dot(hi, suffix, preferred_element_type=F32)
                + jnp.dot(lo, suffix, preferred_element_type=F32) for hi, lo in splits]
        yield 'stage'
        probs =[jnp.exp(zs[h] - incs[h] - runs[h]) for h in heads]
        if causal is not None:
            probs = [jnp.where(causal, a, 0.0) for a in probs]
        new_accs = [accs[h] + jnp.dot(probs[h].astype(BF16), vbs[h], preferred_element_type=F32)
                    for h in heads]
        new_runs = [runs[h] + incs[h][:, 0:1] for h in heads]
        return tuple(new_accs), tuple(new_runs)

    def live_of(runs):
        return -jnp.min(jnp.minimum(jnp.minimum(runs[0], runs[1]), jnp.minimum(runs[2], runs[3])))

    zeros = lambda r, w: tuple(jnp.zeros((r, w), F32) for _ in range(N_HEADS))
    suffix = suffix_matrix(tk)
    half = tq // 2
    if tq == tk and half % LANES == 0:
        s_diag = pl.multiple_of(j_top * tk, tk)
        tri = (lax.broadcasted_iota(jnp.int32, (half, half), 1)
               < lax.broadcasted_iota(jnp.int32, (half, half), 0))
        suffix_half = suffix_matrix(half)
        q_lo, q_hi = [q[:half] for q in qs], [q[half:] for q in qs]
        acc_lo, run_lo = yield from block_steps(s_diag, half, suffix_half, q_lo, zeros(half, HEAD_V),
                                                zeros(half, 1), tri)
        acc_hi, run_hi = yield from block_steps(s_diag + half, half, suffix_half, q_hi,
                                                zeros(half, HEAD_V), zeros(half, 1), tri)
        acc_hi, run_hi = yield from block_steps(s_diag, half, suffix_half, q_hi, acc_hi, run_hi, None)
        accs = tuple(jnp.concatenate([a, b], axis=0) for a, b in zip(acc_lo, acc_hi))
        runs = tuple(jnp.concatenate([a, b], axis=0) for a, b in zip(run_lo, run_hi))
    else:
        q_pos = q0 + lax.broadcasted_iota(jnp.int32, (tq, tk), 0)
        k_off = lax.broadcasted_iota(jnp.int32, (tq, tk), 1)
        accs, runs = zeros(tq, HEAD_V), zeros(tq, 1)
        for m in range(n_masked):
            s0 = pl.multiple_of((j_top - m) * tk, tk)
            accs, runs = yield from block_steps(s0, tk, suffix, qs, accs, runs, s0 + k_off < q_pos)

    yield 'final'

    def cond(carry):
        j, live, _, _ = carry
        return (j >= 0) & (live > SB_DEAD_LOG)

    def body(carry):
        j, _, accs, runs = carry
        accs, runs = _drain(block_steps(pl.multiple_of(j * tk, tk), tk, suffix, qs, accs, runs, None))
        return j - 1, live_of(runs), accs, runs

    _, _, accs, _ = lax.while_loop(cond, body, (j_top - n_masked, live_of(runs), accs, runs))
    for h in range(N_HEADS):
        o_ref[0, rows, _hs(h)] = accs[h]


def _sb_call(q_src, q_block, k_all, v_all, l, tq, tk, q_offset, kv_layer=None):
    assert (tq % tk == 0 or tk % tq == 0) and q_offset % max(tq, tk) == 0 and tk % LANES == 0
    b = q_src.shape[0]
    n_q = math.gcd(l // tq, SB_BLOCKS_PER_STEP)
    rows = n_q * tq
    if kv_layer is None:
        kv_spec = pl.BlockSpec((1, k_all.shape[1], W_GROUP), lambda i, j: (i, 0, 0))
    else:
        kv_spec = pl.BlockSpec((None, 1, k_all.shape[2], W_GROUP), lambda i, j: (kv_layer, i, 0, 0))
    return pl.pallas_call(
        functools.partial(_sb_kernel, tq=tq, tk=tk, q_offset=q_offset, n_q=n_q),
        grid=(b, l // rows),
        in_specs=[pl.BlockSpec((1, rows, W_GROUP), lambda i, j: (i, j, q_block)), kv_spec, kv_spec],
        out_specs=pl.BlockSpec((1, rows, W_GROUP), lambda i, j: (i, j, 0)),
        out_shape=jax.ShapeDtypeStruct((b, l, W_GROUP), F32),
        compiler_params=pltpu.CompilerParams(dimension_semantics=("parallel", "parallel"),
                                             vmem_limit_bytes=VMEM_LIMIT),
        name="sb_attention",
    )(q_src, k_all, v_all)


def _cummax_rows(x):
    r = x.shape[0]
    row = lax.broadcasted_iota(jnp.int32, x.shape, 0)
    shift = 1
    while shift < r:
        x = jnp.maximum(x, jnp.where(row >= shift, pltpu.roll(x, shift, 0), -jnp.inf))
        shift *= 2
    return x


def _mlstm_steps(q_ref, k_ref, v_ref, og_ref, g_ref, bias_ref, ng_ref, c0_ref, n0_ref, m0_ref,
                 h_ref, c_out, n_out, m_out, cn_s, m_s, *, c, nb):
    j = pl.program_id(1)

    @pl.when(j == 0)
    def _():
        cn_s[...] = jnp.zeros_like(cn_s)
        m_s[...] = jnp.zeros_like(m_s)
        for i in range(nb):
            for h in range(N_HEADS):
                cn_s[i, _hs(h), _hs(h)] = c0_ref[i, h]
                cn_s[i, _hs(h), W_GROUP + h * HEAD_V:W_GROUP + (h + 1) * HEAD_V] = jnp.broadcast_to(
                    n0_ref[i, h], (HEAD_V, HEAD_V))
            m_s[i, :, 0:N_HEADS] = m0_ref[i]

    yield 'stage'

    seqs = range(nb)
    hs = N_HEADS * c
    lane = lax.broadcasted_iota(jnp.int32, (c, LANES), 1)
    is_f = (lane >= G_MF) & (lane < G_MF + N_HEADS)
    head_lane = lane < N_HEADS
    incl_b = _tri(c).astype(BF16)
    t_idx = lax.broadcasted_iota(jnp.int32, (c, hs), 0)
    s_idx = lax.broadcasted_iota(jnp.int32, (c, hs), 1) & (c - 1)
    incl4 = t_idx >= s_idx
    same_hv = _same_head(hs, W_GROUP, c, HEAD_V)
    same_vv = _same_head(W_GROUP, W_GROUP, HEAD_V, HEAD_V)
    group_sum = same_vv.astype(BF16)

    def expand(lanes, group):
        g = lax.broadcasted_iota(jnp.int32, (LANES, lanes), 0)
        h = lax.broadcasted_iota(jnp.int32, (LANES, lanes), 1) >> (group.bit_length() - 1)
        return (g == h).astype(BF16)

    gi = lax.broadcasted_iota(jnp.int32, (LANES, 2 * LANES), 0)
    li = lax.broadcasted_iota(jnp.int32, (LANES, 2 * LANES), 1)
    to_heads = (jnp.where((li < N_HEADS) & (gi == G_MI + li), 1.0, 0.0)
                - jnp.where((li < N_HEADS) & (gi == G_MF + li), 1.0, 0.0)
                + jnp.where((li >= LANES) & (li < LANES + N_HEADS) & (gi == G_MF + li - LANES), 1.0, 0.0)
                ).astype(BF16)
    row_sel = (lax.broadcasted_iota(jnp.int32, (SUBLANES, LANES), 1)
               == lax.broadcasted_iota(jnp.int32, (SUBLANES, LANES), 0)).astype(BF16)
    gates = [g_ref[i] + bias_ref[...] for i in seqs]
    merged = [jnp.where(is_f, _dot01_left(incl_b, jnp.where(is_f, _log_sigmoid(g), 0.0)), g) for g in gates]
    gb = _dot01_right(merged, to_heads)
    yield 'stage'
    g_c = [jnp.where(head_lane, x[:, :LANES], -jnp.inf) for x in gb]
    b_c = [x[:, LANES:] for x in gb]
    m_prev = [m_s[i] for i in seqs]
    r_c = [jnp.maximum(m_prev[i], _cummax_rows(g_c[i])) for i in seqs]
    r_end = [r[c - 1:c, :] for r in r_c]
    si_c = [jnp.exp(m_prev[i] - r_c[i]) for i in seqs]
    emt_c = [jnp.exp(-(b_c[i] + r_c[i])) for i in seqs]
    w_c = [jnp.exp(g_c[i] - r_end[i]) for i in seqs]
    dec_c = [jnp.broadcast_to(jnp.exp(m_prev[i] - r_end[i]), (c, LANES)) for i in seqs]
    yield 'stage'
    zero_pad = lambda x: jnp.where(head_lane, x, 0.0)
    r_hs = _dot01_right([zero_pad(r) for r in r_c], expand(hs, c))
    cols = _dot01_right([zero_pad(x) for i in seqs for x in (si_c[i], emt_c[i], w_c[i], dec_c[i])],
                        expand(W_GROUP, HEAD_V))
    si_hv, emt_hv, w_hv, dec_hv = (cols[n::4] for n in range(4))
    yield 'stage'
    g_rows = [sum(lax.dot_general(row_sel, part, (((1,), (1,)), ((), ())), preferred_element_type=F32)
                  for part in _split3(zero_pad(x))) for x in g_c]
    g_row_hs = [jnp.concatenate([gr[h:h + 1, :] for h in range(N_HEADS)], axis=1) for gr in g_rows]
    d_hs = [jnp.where(incl4, jnp.exp(g_row_hs[i] - r_hs[i]), 0.0) for i in seqs]
    yield 'stage'
    qs = [q_ref[i].astype(BF16) for i in seqs]
    ks = [k_ref[i] * (HEAD_V ** -0.5) for i in seqs]
    vs = [v_ref[i] for i in seqs]
    cns = [cn_s[i] for i in seqs]
    qk = [lax.dot_general(qs[i], _head_blocks(ks[i].astype(BF16), same_hv), (((1,), (1,)), ((), ())),
                          preferred_element_type=F32) for i in seqs]
    q_cn = [jnp.dot(qs[i], cns[i].astype(BF16), preferred_element_type=F32) for i in seqs]
    yield 'stage'
    scs = [(qk[i] * d_hs[i]).astype(BF16) for i in seqs]
    ones_blocks = same_hv.astype(BF16)
    v_ones = [jnp.concatenate([_head_blocks(vs[i].astype(BF16), same_hv), ones_blocks], axis=1)
              for i in seqs]
    sc_v = [jnp.dot(scs[i], v_ones[i], preferred_element_type=F32) for i in seqs]
    yield 'stage'
    nd = [sc_v[i] + jnp.concatenate([si_hv[i], si_hv[i]], axis=1) * q_cn[i] for i in seqs]
    hcs = [nd[i][:, :W_GROUP] / jnp.maximum(jnp.abs(nd[i][:, W_GROUP:]), emt_hv[i]) for i in seqs]
    kws = [ks[i] * w_hv[i] for i in seqs]
    upds = [_dot_tn(kws[i], jnp.concatenate([vs[i], jnp.ones_like(vs[i])], axis=1)) for i in seqs]
    same_cn = jnp.concatenate([same_vv, same_vv], axis=1)
    sum_sq = _group_sums([x * x for x in hcs], group_sum)
    yield 'stage'
    ng = ng_ref[...]
    for i in seqs:
        dec_row = dec_hv[i][0:1, :]
        cn_s[i] = jnp.concatenate([dec_row, dec_row], axis=1) * cns[i] + jnp.where(same_cn, upds[i], 0.0)
        m_s[i] = jnp.where(head_lane[0:1, :], b_c[i][c - 1:c, :] + r_end[i], 0.0)
        h_ref[i] = (hcs[i] * lax.rsqrt(sum_sq[i] * (1.0 / HEAD_V) + EPS) * ng * _sigmoid(og_ref[i]))
    yield 'final'

    @pl.when(j == pl.num_programs(1) - 1)
    def _():
        for i in seqs:
            for h in range(N_HEADS):
                c_out[i, h] = cn_s[i, _hs(h), _hs(h)]
                n_out[i, h] = cn_s[i, _hs(h), W_GROUP + h * HEAD_V:W_GROUP + h * HEAD_V + 1]
            m_out[i] = m_s[i, :, 0:N_HEADS]


def _same_head(rows, lanes, row_group, lane_group):
    rh = lax.broadcasted_iota(jnp.int32, (rows, lanes), 0) >> (row_group.bit_length() - 1)
    lh = lax.broadcasted_iota(jnp.int32, (rows, lanes), 1) >> (lane_group.bit_length() - 1)
    return rh == lh


def _head_blocks(x, same_head):
    tiled = jnp.concatenate([x] * N_HEADS, axis=0)
    return jnp.where(same_head, tiled, jnp.zeros_like(tiled))


def _heads_dot_split(l, r, same_head):
    return _heads_dot_parts(_split2(l), _split2(r), same_head)


def _heads_dot_parts(l_parts, r_parts, same_head):
    l1, l2 = l_parts
    rb1, rb2 = _head_blocks(r_parts[0], same_head), _head_blocks(r_parts[1], same_head)
    m = l1.shape[0]
    y = jnp.dot(jnp.concatenate([l1, l2], axis=0), rb1, preferred_element_type=F32)
    return y[:m] + (y[m:] + jnp.dot(l1, rb2, preferred_element_type=F32))


def _unit_lower_inverses(lows, eye4, same_head, c):
    ts = [eye4 - a for a in lows]
    a_parts = [_split2(a) for a in lows]
    ps = [_heads_dot_parts(ap, ap, same_head) for ap in a_parts]
    yield 'stage'
    for _ in range(c.bit_length() - 3):
        t_parts = [_split2(t) for t in ts]
        p_parts = [_split2(p) for p in ps]
        stacked = [tuple(jnp.concatenate([tp[n], pp[n]], axis=0) for n in range(2))
                   for tp, pp in zip(t_parts, p_parts)]
        ys = [_heads_dot_parts(sp, pp, same_head) for sp, pp in zip(stacked, p_parts)]
        ts = [t + y[:c] for t, y in zip(ts, ys)]
        ps = [y[c:] for y in ys]
        yield 'stage'
    return [t + _heads_dot_split(t, p, same_head) for t, p in zip(ts, ps)]


def _gdn_steps(q_ref, k_ref, v_ref, z_ref, g_ref, bias_ref, alog_ref, ng_ref, s0_ref,
               o_ref, s_out, s_s, *, c, nb):
    j = pl.program_id(1)
    seqs = range(nb)

    @pl.when(j == 0)
    def _():
        s_s[...] = jnp.zeros_like(s_s)
        for i in seqs:
            for h in range(N_HEADS):
                s_s[i, _hs(h), _hs(h)] = s0_ref[i, h]

    yield 'stage'

    hs = N_HEADS * c
    lane = lax.broadcasted_iota(jnp.int32, (c, LANES), 1)
    is_a = (lane >= G_GA) & (lane < G_GA + N_HEADS)
    incl_b = _tri(c).astype(BF16)
    t_idx = lax.broadcasted_iota(jnp.int32, (c, hs), 0)
    s_idx = lax.broadcasted_iota(jnp.int32, (c, hs), 1) & (c - 1)
    incl4, strict4, eye4 = t_idx >= s_idx, t_idx > s_idx, (t_idx == s_idx).astype(F32)
    same_hs = _same_head(hs, hs, c, c)
    same_hv = _same_head(hs, W_GROUP, c, HEAD_V)
    same_vv = _same_head(W_GROUP, W_GROUP, HEAD_V, HEAD_V)
    group_sum = same_vv.astype(BF16)

    def pick(lanes, group):
        g = lax.broadcasted_iota(jnp.int32, (LANES, 2 * lanes), 0)
        l = lax.broadcasted_iota(jnp.int32, (LANES, 2 * lanes), 1)
        h = (l & (lanes - 1)) >> (group.bit_length() - 1)
        return (g == jnp.where(l < lanes, G_GA, G_GB) + h).astype(BF16)

    pick_hs = pick(hs, c)
    row_sel = (lax.broadcasted_iota(jnp.int32, (SUBLANES, LANES), 1)
               == G_GA + lax.broadcasted_iota(jnp.int32, (SUBLANES, LANES), 0)).astype(BF16)
    gates = [g_ref[i] + bias_ref[...] for i in seqs]
    xgs = [jnp.where(is_a, -jnp.exp(alog_ref[...]) * _softplus(g), _sigmoid(g)) for g in gates]
    csums = [jnp.where(is_a, _dot01_left(incl_b, xg), xg) for xg in xgs]
    cols_hs = _dot01_right(csums, pick_hs)
    yield 'stage'
    bc_hs, beta_hs = [x[:, :hs] for x in cols_hs], [x[:, hs:] for x in cols_hs]
    if c == HEAD_V:
        bc_hv, beta_hv = bc_hs, beta_hs
    else:
        cols_hv = _dot01_right(csums, pick(W_GROUP, HEAD_V))
        bc_hv, beta_hv = [x[:, :W_GROUP] for x in cols_hv], [x[:, W_GROUP:] for x in cols_hv]
    b_rows = [sum(lax.dot_general(row_sel, part, (((1,), (1,)), ((), ())), preferred_element_type=F32)
                  for part in _split3(cs)) for cs in csums]
    br_hs = [jnp.concatenate([br[h:h + 1, :] for h in range(N_HEADS)], axis=1) for br in b_rows]
    decays = [jnp.exp(jnp.where(incl4, bc_hs[i] - br_hs[i], 0.0)) for i in seqs]
    yield 'stage'
    qk_raw = [jnp.concatenate([q_ref[i], k_ref[i]], axis=0) for i in seqs]
    norms = _group_sums([x * x for x in qk_raw], group_sum)
    qk_n = [qk_raw[i] * lax.rsqrt(norms[i] + EPS) for i in seqs]
    qs = [x[:c] * (HEAD_V ** -0.5) for x in qk_n]
    ks = [x[c:] for x in qk_n]
    vs = [v_ref[i] for i in seqs]
    kq = [jnp.concatenate([ks[i], qs[i]], axis=0).astype(BF16) for i in seqs]
    yield 'stage'
    states = [s_s[i] for i in seqs]
    scores = [lax.dot_general(kq[i], _head_blocks(ks[i].astype(BF16), same_hv),
                              (((1,), (1,)), ((), ())), preferred_element_type=F32) for i in seqs]
    on_state = [jnp.dot(kq[i], states[i].astype(BF16), preferred_element_type=F32) for i in seqs]
    yield 'stage'
    lows = [jnp.where(strict4, beta_hs[i] * decays[i] * scores[i][:c], 0.0) for i in seqs]
    invs = yield from _unit_lower_inverses(lows, eye4, same_hs, c)
    ebs = [jnp.exp(bc_hv[i]) for i in seqs]
    rhss = [beta_hv[i] * (vs[i] - ebs[i] * on_state[i][:c]) for i in seqs]
    yield 'stage'
    us = [_heads_dot_split(invs[i], rhss[i], same_hv) for i in seqs]
    yield 'stage'
    qkm = [jnp.where(incl4, decays[i] * scores[i][c:], 0.0).astype(BF16) for i in seqs]
    outs = [ebs[i] * on_state[i][c:]
            + jnp.dot(qkm[i], _head_blocks(us[i].astype(BF16), same_hv), preferred_element_type=F32)
            for i in seqs]
    b_ends = [bc_hv[i][c - 1:c, :] for i in seqs]
    upds = [jnp.where(same_vv, _dot_tn(ks[i] * jnp.exp(b_ends[i] - bc_hv[i]), us[i]), 0.0) for i in seqs]
    yield 'stage'
    ng = ng_ref[...]
    sum_sq = _group_sums([o * o for o in outs], group_sum)
    for i in seqs:
        s_s[i] = jnp.exp(b_ends[i]) * states[i] + upds[i]
        o_ref[i] = (outs[i] * lax.rsqrt(sum_sq[i] * (1.0 / HEAD_V) + EPS) * ng * _silu(z_ref[i]))
    yield 'final'

    @pl.when(j == pl.num_programs(1) - 1)
    def _():
        for i in seqs:
            for h in range(N_HEADS):
                s_out[i, h] = s_s[i, _hs(h), _hs(h)]


GLA_SUB = SUBLANES


def _gla_steps(q_ref, k_ref, v_ref, r_ref, g_ref, w2_ref, gb_ref, ng_ref, s0_ref,
               o_ref, s_out, s_s, *, c, nb):
    j = pl.program_id(1)
    dk_all = N_HEADS * GLA_DK
    hs = N_HEADS * c
    nsb = c // GLA_SUB
    seqs = range(nb)

    @pl.when(j == 0)
    def _():
        s_s[...] = jnp.zeros_like(s_s)
        for i in seqs:
            for h in range(N_HEADS):
                s_s[i, h * GLA_DK:(h + 1) * GLA_DK, _hs(h)] = s0_ref[i, h]

    yield 'stage'

    incl_b = _tri(c).astype(BF16)
    er = lax.broadcasted_iota(jnp.int32, (2 * c, c), 0)
    ec = lax.broadcasted_iota(jnp.int32, (2 * c, c), 1)
    et = er & (c - 1)
    edge_sel = (ec == jnp.where(er < c, (et & ~(GLA_SUB - 1)) - 1, et | (GLA_SUB - 1))).astype(BF16)
    t_idx = lax.broadcasted_iota(jnp.int32, (c, hs), 0)
    s_idx = lax.broadcasted_iota(jnp.int32, (c, hs), 1) & (c - 1)
    sub_shift = GLA_SUB.bit_length() - 1
    blk_dist = (t_idx >> sub_shift) - (s_idx >> sub_shift)
    diag_off = t_idx - s_idx
    same_hk = _same_head(hs, dk_all, c, GLA_DK)
    same_hv = _same_head(hs, W_GROUP, c, HEAD_V)
    same_kv = _same_head(dk_all, W_GROUP, GLA_DK, HEAD_V)
    head_rep = _same_head(dk_all, hs, GLA_DK, c).astype(BF16)
    group_sum = _same_head(W_GROUP, W_GROUP, HEAD_V, HEAD_V).astype(BF16)
    row8 = lax.broadcasted_iota(jnp.int32, (c, dk_all), 0) & (GLA_SUB - 1)
    last_row = lax.broadcasted_iota(jnp.int32, (c, dk_all), 0) == c - 1

    log_as = [_log_sigmoid(_dot(g_ref[i], w2_ref[...]) + gb_ref[...]) / GLA_TAU for i in seqs]
    bs = [_dot01_left(incl_b, la) for la in log_as]
    edges = [_dot01_left(edge_sel, b) for b in bs]
    yield 'stage'
    e_prev = [e[:c] for e in edges]
    e_own = [e[c:] for e in edges]
    qs = [q_ref[i] * (GLA_DK ** -0.5) for i in seqs]
    ks = [k_ref[i] for i in seqs]
    vs = [v_ref[i] for i in seqs]
    q_hat = [qs[i] * jnp.exp(bs[i] - e_prev[i]) for i in seqs]
    k_hat = [ks[i] * jnp.exp(e_own[i] - bs[i]) for i in seqs]
    yield 'stage'

    def q_for_distance(i, m):
        if m == 0:
            return q_hat[i]
        rows = GLA_SUB * m
        shifted = jnp.concatenate([jnp.zeros((rows, dk_all), F32), e_prev[i][:c - rows]], axis=0)
        return q_hat[i] * jnp.exp(e_prev[i] - shifted)

    a_parts = []
    for i in seqs:
        lhs = jnp.concatenate([q_for_distance(i, m) for m in range(nsb - 1)], axis=0).astype(BF16)
        prod = lax.dot_general(lhs, _head_blocks(k_hat[i].astype(BF16), same_hk),
                               (((1,), (1,)), ((), ())), preferred_element_type=F32)
        a_parts.append(sum(jnp.where(blk_dist == m + 1, prod[m * c:(m + 1) * c], 0.0)
                           for m in range(nsb - 1)))
        yield 'stage'

    def rot8(x, d):
        return x if d == 0 else pltpu.roll(x, d, 0)

    for i in seqs:
        ws = [jnp.where(row8 >= d, qs[i] * rot8(ks[i], d) * jnp.exp(bs[i] - rot8(bs[i], d)), 0.0)
              for d in range(GLA_SUB)]
        prod = jnp.dot(jnp.concatenate(ws, axis=0).astype(BF16), head_rep, preferred_element_type=F32)
        a_parts[i] = a_parts[i] + sum(jnp.where(diag_off == d, prod[d * c:(d + 1) * c], 0.0)
                                      for d in range(GLA_SUB))
        yield 'stage'
    states = [s_s[i] for i in seqs]
    outs = [jnp.dot(a_parts[i].astype(BF16), _head_blocks(vs[i].astype(BF16), same_hv),
                    preferred_element_type=F32)
            + _dot(qs[i] * jnp.exp(bs[i]), states[i]) for i in seqs]
    yield 'stage'
    b_last = [b[c - 1:c, :] for b in bs]
    upds = [jnp.where(same_kv, _dot_tn(ks[i] * jnp.exp(b_last[i] - bs[i]), vs[i]), 0.0) for i in seqs]
    ones_cv = jnp.ones((c, W_GROUP), BF16)
    decay_rows = [sum(lax.dot_general(part, ones_cv, (((0,), (0,)), ((), ())), preferred_element_type=F32)
                      for part in _split3(jnp.where(last_row, b, 0.0))) for b in bs]
    yield 'stage'
    sum_sq = _group_sums([o * o for o in outs], group_sum)
    ng = ng_ref[...]
    for i in seqs:
        s_s[i] = states[i] * jnp.exp(decay_rows[i]) + upds[i]
        o_ref[i] = outs[i] * lax.rsqrt(sum_sq[i] * (1.0 / HEAD_V) + EPS) * ng * _silu(r_ref[i])
    yield 'final'

    @pl.when(j == pl.num_programs(1) - 1)
    def _():
        for i in seqs:
            for h in range(N_HEADS):
                s_out[i, h] = s_s[i, h * GLA_DK:(h + 1) * GLA_DK, _hs(h)]


def _interleave(steps):
    active, parked = list(steps), []
    while active:
        for g in list(active):
            if next(g) == 'final':
                active.remove(g)
                parked.append(g)
    for g in parked:
        for _ in g:
            pass


def _mixers_kernel(mq, mk, mv, mo, gq, gk, gv, gz, lq, lk, lv, lr, g_ref, bias_ref, alog_ref,
                   ml_ng, gdn_ng, w2_ref, gb_ref, gla_ng, c0, n0, m0, gs0, ls0,
                   ml_h, c_out, n_out, m_out, gdn_o, gs_out, gla_o, ls_out,
                   cn_s, m_s, gs_s, ls_s, *, c, nb):
    _interleave([
        _mlstm_steps(mq, mk, mv, mo, g_ref, bias_ref, ml_ng, c0, n0, m0, ml_h, c_out, n_out, m_out,
                     cn_s, m_s, c=c, nb=nb),
        _gdn_steps(gq, gk, gv, gz, g_ref, bias_ref, alog_ref, gdn_ng, gs0, gdn_o, gs_out, gs_s,
                   c=c, nb=nb),
        _gla_steps(lq, lk, lv, lr, g_ref, w2_ref, gb_ref, gla_ng, ls0, gla_o, ls_out, ls_s,
                   c=c, nb=nb)])


def _mixers_call(p, gate_bias, alog_row, ml_ng, gdn_ng, w2_pad, gla_gb, gla_ng,
                 ml_c, ml_n, ml_m, gdn_s, gla_s, c, nb):
    b, l, _ = p.shape
    nb = math.gcd(nb, b)
    nc = l // c
    dk_all = N_HEADS * GLA_DK
    blk = lambda cb: pl.BlockSpec((nb, c, W_GROUP), lambda i, j, cb=cb: (i, j, cb))
    blk128 = lambda cb: pl.BlockSpec((nb, c, LANES), lambda i, j, cb=cb: (i, j, cb))
    row = lambda w: pl.BlockSpec((1, w), lambda i, j: (0, 0))
    full = lambda a: pl.BlockSpec((nb,) + a.shape[1:], lambda i, j: (i,) + (0,) * (a.ndim - 1))
    tok = pl.BlockSpec((nb, c, W_GROUP), lambda i, j: (i, j, 0))
    ml_n = ml_n[..., None]
    ml_m = ml_m.reshape(b, 1, N_HEADS)
    states = (ml_c, ml_n, ml_m, gdn_s, gla_s)
    sds = lambda a: jax.ShapeDtypeStruct(a.shape, F32)
    act = jax.ShapeDtypeStruct((b, l, W_GROUP), F32)
    o_ml, ml_c, ml_n, ml_m, o_gdn, gdn_s, o_gla, gla_s = pl.pallas_call(
        functools.partial(_mixers_kernel, c=c, nb=nb),
        grid=(b // nb, nc),
        in_specs=[blk(PB_MQ), blk(PB_MK), blk(PB_MV), blk(PB_MO),
                  blk(PB_GQ), blk(PB_GK), blk(PB_GV), blk(PB_GZ),
                  blk128(PB128_LQ), blk128(PB128_LK), blk(PB_LV), blk(PB_LR),
                  blk128(PB128_GATES), row(LANES), row(LANES), row(W_GROUP), row(W_GROUP),
                  pl.BlockSpec((LANES, dk_all), lambda i, j: (0, 0)), row(dk_all), row(W_GROUP)]
                 + [full(a) for a in states],
        out_specs=[tok, full(ml_c), full(ml_n), full(ml_m), tok, full(gdn_s), tok, full(gla_s)],
        out_shape=[act, sds(ml_c), sds(ml_n), sds(ml_m), act, sds(gdn_s), act, sds(gla_s)],
        scratch_shapes=[pltpu.VMEM((nb, W_GROUP, 2 * W_GROUP), F32), pltpu.VMEM((nb, 1, LANES), F32),
                        pltpu.VMEM((nb, W_GROUP, W_GROUP), F32), pltpu.VMEM((nb, dk_all, W_GROUP), F32)],
        compiler_params=pltpu.CompilerParams(dimension_semantics=("parallel", "arbitrary"),
                                             vmem_limit_bytes=VMEM_LIMIT),
        name="mixers",
    )(*([p] * 13), gate_bias, alog_row, ml_ng, gdn_ng, w2_pad, gla_gb, gla_ng, *states)
    return o_ml, ml_c, ml_n[..., 0], ml_m[:, 0, :], o_gdn, gdn_s, o_gla, gla_s


def _outffn_kernel(x_ref, a_ref, b_ref, c_ref, d_ref, mod_ref, g2_ref, gf_ref, wo_ref, w1_ref, w2_ref,
                   o_ref, *, ff_chunk, final):
    ns, tm, d = x_ref.shape
    flat = lambda y: y.reshape(ns * tm, y.shape[-1])
    per_seq = lambda y: y.reshape(ns, tm, y.shape[-1])
    mod = mod_ref[...]
    mixed = flat(jnp.concatenate([a_ref[...], b_ref[...], c_ref[...], d_ref[...]], axis=-1)).astype(BF16)
    x = x_ref[...] + mod[:, 2:3, :] * per_seq(jnp.dot(mixed, wo_ref[...], preferred_element_type=F32))
    h = flat(_modulated_norm(x, g2_ref[...], mod[:, 4:5, :], mod[:, 3:4, :])).astype(BF16)
    d_ff = w1_ref.shape[1]
    acc = jnp.zeros((ns * tm, d), F32)
    for f0 in range(0, d_ff, ff_chunk):
        a = jnp.maximum(jnp.dot(h, w1_ref[:, f0:f0 + ff_chunk], preferred_element_type=F32), 0.0)
        acc = acc + jnp.dot((a * a).astype(BF16), w2_ref[f0:f0 + ff_chunk, :],
                            preferred_element_type=F32)
    x = x + mod[:, 5:6, :] * per_seq(acc)
    if final:
        x = x * lax.rsqrt(jnp.mean(x * x, axis=-1, keepdims=True) + EPS) * gf_ref[...]
    o_ref[...] = x


def _outffn_call(x, mixers, mod, g2, gf, wo, w1, w2, tm, ns, final):
    b, l, d = x.shape
    d_ff = w1.shape[1]
    tok = lambda w: pl.BlockSpec((ns, tm, w), lambda i, j: (i, j, 0))
    const = lambda shape: pl.BlockSpec(shape, lambda i, j: (0,) * len(shape),
                                       pipeline_mode=pl.Buffered(1))
    return pl.pallas_call(
        functools.partial(_outffn_kernel, ff_chunk=1024, final=final),
        grid=(b // ns, l // tm),
        in_specs=[tok(d), tok(W_GROUP), tok(W_GROUP), tok(W_GROUP), tok(W_GROUP),
                  pl.BlockSpec((ns, 6, d), lambda i, j: (i, 0, 0)),
                  pl.BlockSpec((1, d), lambda i, j: (0, 0)),
                  pl.BlockSpec((1, d), lambda i, j: (0, 0)),
                  const((d, d)), const((d, d_ff)), const((d_ff, d))],
        out_specs=tok(d),
        out_shape=jax.ShapeDtypeStruct((b, l, d), F32),
        compiler_params=pltpu.CompilerParams(dimension_semantics=("parallel", "parallel"),
                                             vmem_limit_bytes=VMEM_LIMIT),
        name="outproj_ffn",
    )(x, *mixers, mod, g2, gf, wo, w1, w2)


def _permute_w_in(w_in):
    d = w_in.shape[0]
    sizes = (256, 256, 256, 256, 256, 256, 4, 4, 256, 768, 4, 4, 256, 128, 128, 256, 16, 256)
    offs = [0]
    for s in sizes:
        offs.append(offs[-1] + s)
    (sq, sk, sv, mq, mk, mv, mi, mf, mo, gqkv, gb, ga, gz, lq, lk, lv, lg, lr) = [
        w_in[:, offs[i]:offs[i + 1]] for i in range(len(sizes))]
    pad = jnp.zeros((d, LANES - (4 * N_HEADS + GLA_RANK)), w_in.dtype)
    return jnp.concatenate([sk, sv, sq, mq, mk, mv, mo, gqkv, gz, lv, lr, lq, lk,
                            mi, mf, gb, ga, lg, pad], axis=1)


def _row128(pieces):
    row = jnp.zeros((LANES,), F32)
    for off, vec in pieces:
        row = row.at[off:off + vec.shape[0]].set(vec)
    return row.reshape(1, LANES)


DENSE_TILE_ROWS = 512


class _Tiles(NamedTuple):
    tm: int
    dense_seqs: int
    sb_tq: int
    sb_tk: int
    chunk: int


def _tiles(b, l):
    tm = min(DENSE_TILE_ROWS, l)
    return _Tiles(tm=tm, dense_seqs=math.gcd(b, DENSE_TILE_ROWS // tm), sb_tq=min(256, l),
                  sb_tk=max(min(256, l), LANES), chunk=min(64, l))


def _layer(x, mod, lw, states, tiles, layer, depth, kv_prev, final_g):
    tm, dense_seqs, sb_tq, sb_tk, chunk = tiles
    (n1, n2, w_in_p, gate_bias, ml_ng, conv_w, alog_row, gdn_ng, w2_pad, gla_gb, gla_ng,
     w_out, w_ff1, w_ff2) = lw
    (sb_k_past, sb_v_past, ml_c, ml_n, ml_m, gdn_s, gdn_buf, gla_s) = states
    b, l, _ = x.shape
    final = layer == depth - 1
    k_buf, v_buf, p, gdn_buf = _inproj_call(x, mod, n1, w_in_p, conv_w, gdn_buf, tm, dense_seqs,
                                            layer, depth, kv_prev)
    if sb_k_past is None:
        o_sb = _sb_call(p, PB_SQ, k_buf, v_buf, l, sb_tq, sb_tk, 0, kv_layer=layer)
    else:
        past = sb_k_past.shape[1]
        lk = -(-(past + l) // sb_tk) * sb_tk
        padz = jnp.zeros((b, lk - past - l, W_GROUP), F32)
        k_all = jnp.concatenate([sb_k_past.reshape(b, past, W_GROUP), k_buf[layer], padz], axis=1)
        v_all = jnp.concatenate([sb_v_past.reshape(b, past, W_GROUP), v_buf[layer], padz], axis=1)
        o_sb = _sb_call(p, PB_SQ, k_all, v_all, l, sb_tq, sb_tk, past)
    o_ml, ml_c, ml_n, ml_m, o_gdn, gdn_s, o_gla, gla_s = _mixers_call(
        p, gate_bias, alog_row, ml_ng, gdn_ng, w2_pad, gla_gb, gla_ng, ml_c, ml_n, ml_m, gdn_s, gla_s,
        chunk, MIXER_SEQS_PER_STEP)
    x = _outffn_call(x, (o_sb, o_ml, o_gdn, o_gla), mod, n2, final_g, w_out, w_ff1, w_ff2, tm,
                     dense_seqs, final)
    return x, (k_buf, v_buf), (ml_c, ml_n, ml_m, gdn_s, gdn_buf, gla_s)


def kernel(x_prompt, x_sample, cache_sb_k, cache_sb_v, state_mlstm_C, state_mlstm_n, state_mlstm_m, state_gdn_S, state_gdn_conv, state_gla_S, c_prompt, c_sample, norm1_g, norm2_g, w_ada, b_ada, w_in, mlstm_i_bias, mlstm_f_bias, mlstm_norm_g, gdn_conv_w, gdn_a_log, gdn_dt_bias, gdn_norm_g, gla_w_gate2, gla_gate_bias, gla_norm_g, w_out, w_ff1, w_ff2, final_g):
    depth = w_in.shape[0]
    bp, lp, d = x_prompt.shape
    bs, ls, _ = x_sample.shape
    dk_all = N_HEADS * GLA_DK
    mod_all = _ada_call(jnp.concatenate([c_prompt, c_sample], axis=0), w_ada, b_ada)
    mod_all = mod_all.reshape(depth, bp + bs, 6, d)
    final_row = final_g.reshape(1, d)
    xp, xs = x_prompt, x_sample
    p_list, s_list = [], []
    kv_p = kv_s = None
    for l in range(depth):
        gate_bias = _row128([(G_MI, mlstm_i_bias[l]), (G_MF, mlstm_f_bias[l]), (G_GA, gdn_dt_bias[l])])
        alog_row = _row128([(G_GA, gdn_a_log[l])])
        w2_pad = jnp.zeros((LANES, dk_all), F32).at[G_LG:G_LG + GLA_RANK, :].set(gla_w_gate2[l])
        lw = (norm1_g[l].reshape(1, d), norm2_g[l].reshape(1, d), _permute_w_in(w_in[l]).astype(BF16),
              gate_bias, mlstm_norm_g[l].reshape(1, W_GROUP), gdn_conv_w[l], alog_row,
              gdn_norm_g[l].reshape(1, W_GROUP), w2_pad.astype(BF16),
              gla_gate_bias[l].reshape(1, dk_all), gla_norm_g[l].reshape(1, W_GROUP),
              w_out[l].astype(BF16), w_ff1[l].astype(BF16), w_ff2[l].astype(BF16))
        fresh = (None, None, jnp.zeros((bp,) + state_mlstm_C.shape[2:], F32),
                 jnp.zeros((bp,) + state_mlstm_n.shape[2:], F32),
                 jnp.zeros((bp,) + state_mlstm_m.shape[2:], F32),
                 jnp.zeros((bp,) + state_gdn_S.shape[2:], F32),
                 jnp.zeros((bp,) + state_gdn_conv.shape[2:], F32),
                 jnp.zeros((bp,) + state_gla_S.shape[2:], F32))
        past = (cache_sb_k[l], cache_sb_v[l], state_mlstm_C[l], state_mlstm_n[l], state_mlstm_m[l],
                state_gdn_S[l], state_gdn_conv[l], state_gla_S[l])
        xp, kv_p, st_p = _layer(xp, mod_all[l, :bp], lw, fresh, _tiles(bp, lp), l, depth, kv_p, final_row)
        xs, kv_s, st_s = _layer(xs, mod_all[l, bp:], lw, past, _tiles(bs, ls), l, depth, kv_s, final_row)
        p_list.append(st_p)
        s_list.append(st_s)
    per_head = lambda a: a.reshape(a.shape[:-1] + (N_HEADS, HEAD_V))
    stacked_p = [jnp.stack([st[i] for st in p_list]) for i in range(6)]
    stacked_s = [jnp.stack([st[i] for st in s_list]) for i in range(6)]
    return (xp, xs, per_head(kv_p[0]), per_head(kv_p[1]), *stacked_p,
            per_head(kv_s[0]), per_head(kv_s[1]), *stacked_s)
```

```python
import functools
import math
from typing import NamedTuple

import jax
import jax.numpy as jnp
from jax import lax
from jax.experimental import pallas as pl
from jax.experimental.pallas import tpu as pltpu

F32 = jnp.float32
BF16 = jnp.bfloat16
EPS = 1e-6
N_HEADS = 4
HEAD_V = 64
GLA_DK = 32
CONV_W = 4
GLA_TAU = 16.0
W_GROUP = N_HEADS * HEAD_V
LANES = 128
SUBLANES = 8
VMEM_LIMIT = 56 * 1024 * 1024
MIXER_SEQS_PER_STEP = 8

PB_SQ, PB_MQ, PB_MK, PB_MV, PB_MO, PB_GQ, PB_GK, PB_GV, PB_GZ, PB_LV, PB_LR = range(11)
PB128_LQ, PB128_LK, PB128_GATES = 22, 23, 24
P_WIDTH = 25 * LANES
G_MI, G_MF, G_GB, G_GA, G_LG = 0, 4, 8, 12, 16
GLA_RANK = 16


def _dot(a, b):
    return jnp.dot(a.astype(BF16), b.astype(BF16), preferred_element_type=F32)


def _dot_tn(a, b):
    return lax.dot_general(a.astype(BF16), b.astype(BF16), (((0,), (0,)), ((), ())),
                           preferred_element_type=F32)


def _split3(x):
    x1 = x.astype(BF16)
    r1 = x - x1.astype(F32)
    x2 = r1.astype(BF16)
    x3 = (r1 - x2.astype(F32)).astype(BF16)
    return x1, x2, x3


def _split2(x):
    x1 = x.astype(BF16)
    x2 = (x - x1.astype(F32)).astype(BF16)
    return x1, x2


def _dot01_left(m01, x):
    return sum(jnp.dot(m01, p, preferred_element_type=F32) for p in _split3(x))


def _dot01_right(xs, m01):
    n = xs[0].shape[0]
    stacked = jnp.concatenate([part for x in xs for part in _split3(x)], axis=0)
    y = jnp.dot(stacked, m01, preferred_element_type=F32)
    return [y[3 * i * n:(3 * i + 1) * n] + (y[(3 * i + 1) * n:(3 * i + 2) * n]
                                             + y[(3 * i + 2) * n:(3 * i + 3) * n])
            for i in range(len(xs))]


def _group_sums(xs, group_sum):
    n = xs[0].shape[0]
    y = jnp.dot(jnp.concatenate([x.astype(BF16) for x in xs], axis=0), group_sum,
                preferred_element_type=F32)
    return [y[i * n:(i + 1) * n] for i in range(len(xs))]


def _log_sigmoid(x):
    return jnp.minimum(x, 0.0) - jnp.log(1.0 + jnp.exp(-jnp.abs(x)))


def _sigmoid(x):
    return 1.0 / (1.0 + jnp.exp(-x))


def _silu(x):
    return x * _sigmoid(x)


def _softplus(x):
    return jnp.maximum(x, 0.0) + jnp.log(1.0 + jnp.exp(-jnp.abs(x)))


def _tri(c):
    r = lax.broadcasted_iota(jnp.int32, (c, c), 0)
    s = lax.broadcasted_iota(jnp.int32, (c, c), 1)
    return r >= s


def _hs(h, w=HEAD_V):
    return slice(h * w, (h + 1) * w)


def _ada_kernel(c_ref, w_ref, b_ref, o_ref):
    o_ref[0] = _dot(_silu(c_ref[...]), w_ref[0]) + b_ref[0]


def _ada_call(c_all, w_ada, b_ada, tn=512):
    depth, d, n = w_ada.shape
    rows = c_all.shape[0]
    return pl.pallas_call(
        _ada_kernel,
        grid=(depth, n // tn),
        in_specs=[pl.BlockSpec((rows, d), lambda l, j: (0, 0)),
                  pl.BlockSpec((1, d, tn), lambda l, j: (l, 0, j)),
                  pl.BlockSpec((1, 1, tn), lambda l, j: (l, 0, j))],
        out_specs=pl.BlockSpec((1, rows, tn), lambda l, j: (l, 0, j)),
        out_shape=jax.ShapeDtypeStruct((depth, rows, n), F32),
        compiler_params=pltpu.CompilerParams(dimension_semantics=("parallel", "parallel"),
                                             vmem_limit_bytes=VMEM_LIMIT),
        name="ada_mod",
    )(c_all, w_ada, b_ada.reshape(depth, 1, n))


def _modulated_norm(x, g_row, scale_row, shift_row):
    y = x * lax.rsqrt(jnp.mean(x * x, axis=-1, keepdims=True) + EPS) * g_row
    return y * (1.0 + scale_row) + shift_row


def _inproj_kernel(x_ref, mod_ref, g_ref, w_ref, cw_ref, cb0_ref, *refs, col_chunk):
    k_ref, v_ref, p_ref, cb_out, xw_s = refs[-5:]
    j = pl.program_id(1)
    ns, tm = x_ref.shape[0], x_ref.shape[1]
    conv_lo, conv_hi = PB_GQ * W_GROUP, (PB_GV + 1) * W_GROUP

    @pl.when(j == 0)
    def _():
        xw_s[:, 0:SUBLANES, :] = cb0_ref[...]

    mod = mod_ref[...]
    h = _modulated_norm(x_ref[...], g_ref[...], mod[:, 1:2, :], mod[:, 0:1, :])
    h = h.reshape(ns * tm, h.shape[-1]).astype(BF16)
    per_seq = lambda y: y.reshape(ns, tm, y.shape[-1])
    k_ref[0] = per_seq(jnp.dot(h, w_ref[:, 0:W_GROUP], preferred_element_type=F32))
    v_ref[0] = per_seq(jnp.dot(h, w_ref[:, W_GROUP:2 * W_GROUP], preferred_element_type=F32))
    base = 2 * W_GROUP
    bounds = sorted(set(range(0, conv_lo, col_chunk)) | {conv_lo, conv_hi}
                    | set(range(conv_hi, P_WIDTH, col_chunk)) | {P_WIDTH})
    for n0, n1 in zip(bounds[:-1], bounds[1:]):
        if n0 == conv_lo:
            continue
        p_ref[:, :, n0:n1] = per_seq(jnp.dot(h, w_ref[:, base + n0:base + n1],
                                             preferred_element_type=F32))
    raw = per_seq(jnp.dot(h, w_ref[:, base + conv_lo:base + conv_hi], preferred_element_type=F32))
    xw_s[:, SUBLANES:SUBLANES + tm, :] = raw
    cw = cw_ref[...]
    y = raw * cw[CONV_W - 1:CONV_W, :]
    for d in range(1, CONV_W):
        y = y + xw_s[:, SUBLANES - d:SUBLANES - d + tm, :] * cw[CONV_W - 1 - d:CONV_W - d, :]
    p_ref[:, :, conv_lo:conv_hi] = _silu(y)
    xw_s[:, 0:SUBLANES, :] = raw[:, tm - SUBLANES:tm, :]

    @pl.when(j == pl.num_programs(1) - 1)
    def _():
        cb_out[...] = raw[:, tm - SUBLANES:tm, :]


def _inproj_call(x, mod, g, w_perm, conv_w, conv0, tm, ns, layer, depth, kv_prev):
    b, l, d = x.shape
    n_all = w_perm.shape[1]
    conv_dim = conv_w.shape[1]
    pad_rows = SUBLANES - (CONV_W - 1)
    conv0_pad = jnp.concatenate([jnp.zeros((b, pad_rows, conv_dim), F32), conv0], axis=1)
    carry = pl.BlockSpec((ns, SUBLANES, conv_dim), lambda i, j: (i, 0, 0))
    kv_block = pl.BlockSpec((1, ns, tm, W_GROUP), lambda i, j: (layer, i, j, 0))
    kv_shape = jax.ShapeDtypeStruct((depth, b, l, W_GROUP), F32)
    in_specs = [pl.BlockSpec((ns, tm, d), lambda i, j: (i, j, 0)),
                pl.BlockSpec((ns, 6, d), lambda i, j: (i, 0, 0)),
                pl.BlockSpec((1, d), lambda i, j: (0, 0)),
                pl.BlockSpec((d, n_all), lambda i, j: (0, 0), pipeline_mode=pl.Buffered(1)),
                pl.BlockSpec((CONV_W, conv_dim), lambda i, j: (0, 0)),
                carry]
    operands = [x, mod, g, w_perm, conv_w, conv0_pad]
    aliases = {}
    if kv_prev is not None:
        aliases = {len(operands): 0, len(operands) + 1: 1}
        in_specs += [pl.BlockSpec(memory_space=pl.ANY)] * 2
        operands += list(kv_prev)
    k, v, p, conv_new = pl.pallas_call(
        functools.partial(_inproj_kernel, col_chunk=512),
        grid=(b // ns, l // tm),
        in_specs=in_specs,
        out_specs=[kv_block, kv_block,
                   pl.BlockSpec((ns, tm, P_WIDTH), lambda i, j: (i, j, 0)),
                   carry],
        out_shape=[kv_shape, kv_shape,
                   jax.ShapeDtypeStruct((b, l, P_WIDTH), F32),
                   jax.ShapeDtypeStruct((b, SUBLANES, conv_dim), F32)],
        scratch_shapes=[pltpu.VMEM((ns, tm + SUBLANES, conv_dim), F32)],
        input_output_aliases=aliases,
        compiler_params=pltpu.CompilerParams(dimension_semantics=("parallel", "arbitrary"),
                                             vmem_limit_bytes=VMEM_LIMIT),
        name="norm_inproj",
    )(*operands)
    return k, v, p, conv_new[:, pad_rows:, :]


SB_DEAD_LOG = -104.0
SB_BLOCKS_PER_STEP = 8


def _sb_kernel(q_ref, k_ref, v_ref, o_ref, *, tq, tk, q_offset, n_q):
    _interleave([_sb_query_block(q_ref, k_ref, v_ref, o_ref, pl.program_id(1) * n_q + qb,
                                 slice(qb * tq, (qb + 1) * tq), tq=tq, tk=tk, q_offset=q_offset,
                                 has_earlier_block=qb > 0)
                 for qb in range(n_q)])


def _drain(steps):
    try:
        while True:
            next(steps)
    except StopIteration as done:
        return done.value


def _sb_query_block(q_ref, k_ref, v_ref, o_ref, i, rows, *, tq, tk, q_offset, has_earlier_block):
    q0 = q_offset + i * tq
    j_top = (q0 + tq - 1) // tk
    n_masked = max(tq // tk, 1)
    def suffix_matrix(w):
        return (lax.broadcasted_iota(jnp.int32, (w, w), 0)
                >= lax.broadcasted_iota(jnp.int32, (w, w), 1)).astype(BF16)

    q_all = q_ref[0, rows, :] * (HEAD_V ** -0.5)
    qs = [q_all[:, _hs(h)].astype(BF16) for h in range(N_HEADS)]

    def block_steps(s0, width, suffix, q_rows, accs, runs, causal):
        heads = range(N_HEADS)
        kbs = [k_ref[0, pl.ds(s0, width), _hs(h)].astype(BF16) for h in heads]
        vbs = [v_ref[0, pl.ds(s0, width), _hs(h)].astype(BF16) for h in heads]
        zs = [lax.dot_general(q_rows[h], kbs[h], (((1,), (1,)), ((), ())), preferred_element_type=F32)
              for h in heads]
        yield 'stage'
        sps = [_softplus(z) for z in zs]
        if causal is not None:
            sps = [jnp.where(causal, x, 0.0) for x in sps]
        splits = [_split2(x) for x in sps]
        incs = [jnp.dot(hi, suffix, preferred_element_type=F32)
                + jnp.dot(lo, suffix, preferred_element_type=F32) for hi, lo in splits]
        yield 'stage'
        probs =[jnp.exp(zs[h] - incs[h] - runs[h]) for h in heads]
        if causal is not None:
            probs = [jnp.where(causal, a, 0.0) for a in probs]
        new_accs = [accs[h] + jnp.dot(probs[h].astype(BF16), vbs[h], preferred_element_type=F32)
                    for h in heads]
        new_runs = [runs[h] + incs[h][:, 0:1] for h in heads]
        return tuple(new_accs), tuple(new_runs)

    def live_of(runs):
        return -jnp.min(jnp.minimum(jnp.minimum(runs[0], runs[1]), jnp.minimum(runs[2], runs[3])))

    zeros = lambda r, w: tuple(jnp.zeros((r, w), F32) for _ in range(N_HEADS))
    suffix = suffix_matrix(tk)
    n_visited = n_masked
    half = tq // 2
    if tq == tk and half % LANES == 0:
        s_diag = pl.multiple_of(j_top * tk, tk)
        tri = (lax.broadcasted_iota(jnp.int32, (half, half), 1)
               < lax.broadcasted_iota(jnp.int32, (half, half), 0))
        suffix_half = suffix_matrix(half)
        q_lo, q_hi = [q[:half] for q in qs], [q[half:] for q in qs]
        acc_lo, run_lo = yield from block_steps(s_diag, half, suffix_half, q_lo, zeros(half, HEAD_V),
                                                zeros(half, 1), tri)
        acc_hi, run_hi = yield from block_steps(s_diag + half, half, suffix_half, q_hi,
                                                zeros(half, HEAD_V), zeros(half, 1), tri)
        acc_hi, run_hi = yield from block_steps(s_diag, half, suffix_half, q_hi, acc_hi, run_hi, None)
        accs = tuple(jnp.concatenate([a, b], axis=0) for a, b in zip(acc_lo, acc_hi))
        runs = tuple(jnp.concatenate([a, b], axis=0) for a, b in zip(run_lo, run_hi))
        if has_earlier_block:
            accs, runs = yield from block_steps(pl.multiple_of((j_top - 1) * tk, tk), tk, suffix, qs,
                                                accs, runs, None)
            n_visited += 1
    else:
        q_pos = q0 + lax.broadcasted_iota(jnp.int32, (tq, tk), 0)
        k_off = lax.broadcasted_iota(jnp.int32, (tq, tk), 1)
        accs, runs = zeros(tq, HEAD_V), zeros(tq, 1)
        for m in range(n_masked):
            s0 = pl.multiple_of((j_top - m) * tk, tk)
            accs, runs = yield from block_steps(s0, tk, suffix, qs, accs, runs, s0 + k_off < q_pos)

    yield 'final'

    def cond(carry):
        j, live, _, _ = carry
        return (j >= 0) & (live > SB_DEAD_LOG)

    def body(carry):
        j, _, accs, runs = carry
        accs, runs = _drain(block_steps(pl.multiple_of(j * tk, tk), tk, suffix, qs, accs, runs, None))
        return j - 1, live_of(runs), accs, runs

    _, _, accs, _ = lax.while_loop(cond, body, (j_top - n_visited, live_of(runs), accs, runs))
    for h in range(N_HEADS):
        o_ref[0, rows, _hs(h)] = accs[h]


def _sb_call(q_src, q_block, k_all, v_all, l, tq, tk, q_offset, kv_layer=None):
    assert (tq % tk == 0 or tk % tq == 0) and q_offset % max(tq, tk) == 0 and tk % LANES == 0
    b = q_src.shape[0]
    n_q = math.gcd(l // tq, SB_BLOCKS_PER_STEP)
    rows = n_q * tq
    if kv_layer is None:
        kv_spec = pl.BlockSpec((1, k_all.shape[1], W_GROUP), lambda i, j: (i, 0, 0))
    else:
        kv_spec = pl.BlockSpec((None, 1, k_all.shape[2], W_GROUP), lambda i, j: (kv_layer, i, 0, 0))
    return pl.pallas_call(
        functools.partial(_sb_kernel, tq=tq, tk=tk, q_offset=q_offset, n_q=n_q),
        grid=(b, l // rows),
        in_specs=[pl.BlockSpec((1, rows, W_GROUP), lambda i, j: (i, j, q_block)), kv_spec, kv_spec],
        out_specs=pl.BlockSpec((1, rows, W_GROUP), lambda i, j: (i, j, 0)),
        out_shape=jax.ShapeDtypeStruct((b, l, W_GROUP), F32),
        compiler_params=pltpu.CompilerParams(dimension_semantics=("parallel", "parallel"),
                                             vmem_limit_bytes=VMEM_LIMIT),
        name="sb_attention",
    )(q_src, k_all, v_all)


def _cummax_rows(x):
    r = x.shape[0]
    row = lax.broadcasted_iota(jnp.int32, x.shape, 0)
    shift = 1
    while shift < r:
        x = jnp.maximum(x, jnp.where(row >= shift, pltpu.roll(x, shift, 0), -jnp.inf))
        shift *= 2
    return x


def _mlstm_steps(q_ref, k_ref, v_ref, og_ref, g_ref, bias_ref, ng_ref, c0_ref, n0_ref, m0_ref,
                 h_ref, c_out, n_out, m_out, cn_s, m_s, *, c, nb):
    j = pl.program_id(1)

    @pl.when(j == 0)
    def _():
        cn_s[...] = jnp.zeros_like(cn_s)
        m_s[...] = jnp.zeros_like(m_s)
        for i in range(nb):
            for h in range(N_HEADS):
                cn_s[i, _hs(h), _hs(h)] = c0_ref[i, h]
                cn_s[i, _hs(h), W_GROUP + h * HEAD_V:W_GROUP + (h + 1) * HEAD_V] = jnp.broadcast_to(
                    n0_ref[i, h], (HEAD_V, HEAD_V))
            m_s[i, :, 0:N_HEADS] = m0_ref[i]

    yield 'stage'

    seqs = range(nb)
    hs = N_HEADS * c
    lane = lax.broadcasted_iota(jnp.int32, (c, LANES), 1)
    is_f = (lane >= G_MF) & (lane < G_MF + N_HEADS)
    head_lane = lane < N_HEADS
    incl_b = _tri(c).astype(BF16)
    t_idx = lax.broadcasted_iota(jnp.int32, (c, hs), 0)
    s_idx = lax.broadcasted_iota(jnp.int32, (c, hs), 1) & (c - 1)
    incl4 = t_idx >= s_idx
    same_hv = _same_head(hs, W_GROUP, c, HEAD_V)
    same_vv = _same_head(W_GROUP, W_GROUP, HEAD_V, HEAD_V)
    group_sum = same_vv.astype(BF16)

    def expand(lanes, group):
        g = lax.broadcasted_iota(jnp.int32, (LANES, lanes), 0)
        h = lax.broadcasted_iota(jnp.int32, (LANES, lanes), 1) >> (group.bit_length() - 1)
        return (g == h).astype(BF16)

    gi = lax.broadcasted_iota(jnp.int32, (LANES, 2 * LANES), 0)
    li = lax.broadcasted_iota(jnp.int32, (LANES, 2 * LANES), 1)
    to_heads = (jnp.where((li < N_HEADS) & (gi == G_MI + li), 1.0, 0.0)
                - jnp.where((li < N_HEADS) & (gi == G_MF + li), 1.0, 0.0)
                + jnp.where((li >= LANES) & (li < LANES + N_HEADS) & (gi == G_MF + li - LANES), 1.0, 0.0)
                ).astype(BF16)
    row_sel = (lax.broadcasted_iota(jnp.int32, (SUBLANES, LANES), 1)
               == lax.broadcasted_iota(jnp.int32, (SUBLANES, LANES), 0)).astype(BF16)
    gates = [g_ref[i] + bias_ref[...] for i in seqs]
    merged = [jnp.where(is_f, _dot01_left(incl_b, jnp.where(is_f, _log_sigmoid(g), 0.0)), g) for g in gates]
    gb = _dot01_right(merged, to_heads)
    yield 'stage'
    g_c = [jnp.where(head_lane, x[:, :LANES], -jnp.inf) for x in gb]
    b_c = [x[:, LANES:] for x in gb]
    m_prev = [m_s[i] for i in seqs]
    r_c = [jnp.maximum(m_prev[i], _cummax_rows(g_c[i])) for i in seqs]
    r_end = [r[c - 1:c, :] for r in r_c]
    si_c = [jnp.exp(m_prev[i] - r_c[i]) for i in seqs]
    emt_c = [jnp.exp(-(b_c[i] + r_c[i])) for i in seqs]
    w_c = [jnp.exp(g_c[i] - r_end[i]) for i in seqs]
    dec_c = [jnp.broadcast_to(jnp.exp(m_prev[i] - r_end[i]), (c, LANES)) for i in seqs]
    yield 'stage'
    zero_pad = lambda x: jnp.where(head_lane, x, 0.0)
    r_hs = _dot01_right([zero_pad(r) for r in r_c], expand(hs, c))
    cols = _dot01_right([zero_pad(x) for i in seqs for x in (si_c[i], emt_c[i], w_c[i], dec_c[i])],
                        expand(W_GROUP, HEAD_V))
    si_hv, emt_hv, w_hv, dec_hv = (cols[n::4] for n in range(4))
    yield 'stage'
    g_rows = [sum(lax.dot_general(row_sel, part, (((1,), (1,)), ((), ())), preferred_element_type=F32)
                  for part in _split3(zero_pad(x))) for x in g_c]
    g_row_hs = [jnp.concatenate([gr[h:h + 1, :] for h in range(N_HEADS)], axis=1) for gr in g_rows]
    d_hs = [jnp.where(incl4, jnp.exp(g_row_hs[i] - r_hs[i]), 0.0) for i in seqs]
    yield 'stage'
    qs = [q_ref[i].astype(BF16) for i in seqs]
    ks = [k_ref[i] * (HEAD_V ** -0.5) for i in seqs]
    vs = [v_ref[i] for i in seqs]
    cns = [cn_s[i] for i in seqs]
    qk = [lax.dot_general(qs[i], _head_blocks(ks[i].astype(BF16), same_hv), (((1,), (1,)), ((), ())),
                          preferred_element_type=F32) for i in seqs]
    q_cn = [jnp.dot(qs[i], cns[i].astype(BF16), preferred_element_type=F32) for i in seqs]
    yield 'stage'
    scs = [(qk[i] * d_hs[i]).astype(BF16) for i in seqs]
    ones_blocks = same_hv.astype(BF16)
    v_ones = [jnp.concatenate([_head_blocks(vs[i].astype(BF16), same_hv), ones_blocks], axis=1)
              for i in seqs]
    sc_v = [jnp.dot(scs[i], v_ones[i], preferred_element_type=F32) for i in seqs]
    yield 'stage'
    nd = [sc_v[i] + jnp.concatenate([si_hv[i], si_hv[i]], axis=1) * q_cn[i] for i in seqs]
    hcs = [nd[i][:, :W_GROUP] / jnp.maximum(jnp.abs(nd[i][:, W_GROUP:]), emt_hv[i]) for i in seqs]
    kws = [ks[i] * w_hv[i] for i in seqs]
    upds = [_dot_tn(kws[i], jnp.concatenate([vs[i], jnp.ones_like(vs[i])], axis=1)) for i in seqs]
    same_cn = jnp.concatenate([same_vv, same_vv], axis=1)
    sum_sq = _group_sums([x * x for x in hcs], group_sum)
    yield 'stage'
    ng = ng_ref[...]
    for i in seqs:
        dec_row = dec_hv[i][0:1, :]
        cn_s[i] = jnp.concatenate([dec_row, dec_row], axis=1) * cns[i] + jnp.where(same_cn, upds[i], 0.0)
        m_s[i] = jnp.where(head_lane[0:1, :], b_c[i][c - 1:c, :] + r_end[i], 0.0)
        h_ref[i] = (hcs[i] * lax.rsqrt(sum_sq[i] * (1.0 / HEAD_V) + EPS) * ng * _sigmoid(og_ref[i]))
    yield 'final'

    @pl.when(j == pl.num_programs(1) - 1)
    def _():
        for i in seqs:
            for h in range(N_HEADS):
                c_out[i, h] = cn_s[i, _hs(h), _hs(h)]
                n_out[i, h] = cn_s[i, _hs(h), W_GROUP + h * HEAD_V:W_GROUP + h * HEAD_V + 1]
            m_out[i] = m_s[i, :, 0:N_HEADS]


def _same_head(rows, lanes, row_group, lane_group):
    rh = lax.broadcasted_iota(jnp.int32, (rows, lanes), 0) >> (row_group.bit_length() - 1)
    lh = lax.broadcasted_iota(jnp.int32, (rows, lanes), 1) >> (lane_group.bit_length() - 1)
    return rh == lh


def _head_blocks(x, same_head):
    tiled = jnp.concatenate([x] * N_HEADS, axis=0)
    return jnp.where(same_head, tiled, jnp.zeros_like(tiled))


def _heads_dot_split(l, r, same_head):
    return _heads_dot_parts(_split2(l), _split2(r), same_head)


def _heads_dot_parts(l_parts, r_parts, same_head):
    l1, l2 = l_parts
    rb1, rb2 = _head_blocks(r_parts[0], same_head), _head_blocks(r_parts[1], same_head)
    m = l1.shape[0]
    y = jnp.dot(jnp.concatenate([l1, l2], axis=0), rb1, preferred_element_type=F32)
    return y[:m] + (y[m:] + jnp.dot(l1, rb2, preferred_element_type=F32))


def _unit_lower_inverses(lows, eye4, same_head, c):
    ts = [eye4 - a for a in lows]
    a_parts = [_split2(a) for a in lows]
    ps = [_heads_dot_parts(ap, ap, same_head) for ap in a_parts]
    yield 'stage'
    for _ in range(c.bit_length() - 3):
        t_parts = [_split2(t) for t in ts]
        p_parts = [_split2(p) for p in ps]
        stacked = [tuple(jnp.concatenate([tp[n], pp[n]], axis=0) for n in range(2))
                   for tp, pp in zip(t_parts, p_parts)]
        ys = [_heads_dot_parts(sp, pp, same_head) for sp, pp in zip(stacked, p_parts)]
        ts = [t + y[:c] for t, y in zip(ts, ys)]
        ps = [y[c:] for y in ys]
        yield 'stage'
    return [t + _heads_dot_split(t, p, same_head) for t, p in zip(ts, ps)]


def _gdn_steps(q_ref, k_ref, v_ref, z_ref, g_ref, bias_ref, alog_ref, ng_ref, s0_ref,
               o_ref, s_out, s_s, *, c, nb):
    j = pl.program_id(1)
    seqs = range(nb)

    @pl.when(j == 0)
    def _():
        s_s[...] = jnp.zeros_like(s_s)
        for i in seqs:
            for h in range(N_HEADS):
                s_s[i, _hs(h), _hs(h)] = s0_ref[i, h]

    yield 'stage'

    hs = N_HEADS * c
    lane = lax.broadcasted_iota(jnp.int32, (c, LANES), 1)
    is_a = (lane >= G_GA) & (lane < G_GA + N_HEADS)
    incl_b = _tri(c).astype(BF16)
    t_idx = lax.broadcasted_iota(jnp.int32, (c, hs), 0)
    s_idx = lax.broadcasted_iota(jnp.int32, (c, hs), 1) & (c - 1)
    incl4, strict4, eye4 = t_idx >= s_idx, t_idx > s_idx, (t_idx == s_idx).astype(F32)
    same_hs = _same_head(hs, hs, c, c)
    same_hv = _same_head(hs, W_GROUP, c, HEAD_V)
    same_vv = _same_head(W_GROUP, W_GROUP, HEAD_V, HEAD_V)
    group_sum = same_vv.astype(BF16)

    def pick(lanes, group):
        g = lax.broadcasted_iota(jnp.int32, (LANES, 2 * lanes), 0)
        l = lax.broadcasted_iota(jnp.int32, (LANES, 2 * lanes), 1)
        h = (l & (lanes - 1)) >> (group.bit_length() - 1)
        return (g == jnp.where(l < lanes, G_GA, G_GB) + h).astype(BF16)

    pick_hs = pick(hs, c)
    row_sel = (lax.broadcasted_iota(jnp.int32, (SUBLANES, LANES), 1)
               == G_GA + lax.broadcasted_iota(jnp.int32, (SUBLANES, LANES), 0)).astype(BF16)
    gates = [g_ref[i] + bias_ref[...] for i in seqs]
    xgs = [jnp.where(is_a, -jnp.exp(alog_ref[...]) * _softplus(g), _sigmoid(g)) for g in gates]
    csums = [jnp.where(is_a, _dot01_left(incl_b, xg), xg) for xg in xgs]
    cols_hs = _dot01_right(csums, pick_hs)
    yield 'stage'
    bc_hs, beta_hs = [x[:, :hs] for x in cols_hs], [x[:, hs:] for x in cols_hs]
    if c == HEAD_V:
        bc_hv, beta_hv = bc_hs, beta_hs
    else:
        cols_hv = _dot01_right(csums, pick(W_GROUP, HEAD_V))
        bc_hv, beta_hv = [x[:, :W_GROUP] for x in cols_hv], [x[:, W_GROUP:] for x in cols_hv]
    b_rows = [sum(lax.dot_general(row_sel, part, (((1,), (1,)), ((), ())), preferred_element_type=F32)
                  for part in _split3(cs)) for cs in csums]
    br_hs = [jnp.concatenate([br[h:h + 1, :] for h in range(N_HEADS)], axis=1) for br in b_rows]
    decays = [jnp.exp(jnp.where(incl4, bc_hs[i] - br_hs[i], 0.0)) for i in seqs]
    yield 'stage'
    qk_raw = [jnp.concatenate([q_ref[i], k_ref[i]], axis=0) for i in seqs]
    norms = _group_sums([x * x for x in qk_raw], group_sum)
    qk_n = [qk_raw[i] * lax.rsqrt(norms[i] + EPS) for i in seqs]
    qs = [x[:c] * (HEAD_V ** -0.5) for x in qk_n]
    ks = [x[c:] for x in qk_n]
    vs = [v_ref[i] for i in seqs]
    kq = [jnp.concatenate([ks[i], qs[i]], axis=0).astype(BF16) for i in seqs]
    yield 'stage'
    states = [s_s[i] for i in seqs]
    scores = [lax.dot_general(kq[i], _head_blocks(ks[i].astype(BF16), same_hv),
                              (((1,), (1,)), ((), ())), preferred_element_type=F32) for i in seqs]
    on_state = [jnp.dot(kq[i], states[i].astype(BF16), preferred_element_type=F32) for i in seqs]
    yield 'stage'
    lows = [jnp.where(strict4, beta_hs[i] * decays[i] * scores[i][:c], 0.0) for i in seqs]
    invs = yield from _unit_lower_inverses(lows, eye4, same_hs, c)
    ebs = [jnp.exp(bc_hv[i]) for i in seqs]
    rhss = [beta_hv[i] * (vs[i] - ebs[i] * on_state[i][:c]) for i in seqs]
    yield 'stage'
    us = [_heads_dot_split(invs[i], rhss[i], same_hv) for i in seqs]
    yield 'stage'
    qkm = [jnp.where(incl4, decays[i] * scores[i][c:], 0.0).astype(BF16) for i in seqs]
    outs = [ebs[i] * on_state[i][c:]
            + jnp.dot(qkm[i], _head_blocks(us[i].astype(BF16), same_hv), preferred_element_type=F32)
            for i in seqs]
    b_ends = [bc_hv[i][c - 1:c, :] for i in seqs]
    upds = [jnp.where(same_vv, _dot_tn(ks[i] * jnp.exp(b_ends[i] - bc_hv[i]), us[i]), 0.0) for i in seqs]
    yield 'stage'
    ng = ng_ref[...]
    sum_sq = _group_sums([o * o for o in outs], group_sum)
    for i in seqs:
        s_s[i] = jnp.exp(b_ends[i]) * states[i] + upds[i]
        o_ref[i] = (outs[i] * lax.rsqrt(sum_sq[i] * (1.0 / HEAD_V) + EPS) * ng * _silu(z_ref[i]))
    yield 'final'

    @pl.when(j == pl.num_programs(1) - 1)
    def _():
        for i in seqs:
            for h in range(N_HEADS):
                s_out[i, h] = s_s[i, _hs(h), _hs(h)]


GLA_SUB = SUBLANES


def _gla_steps(q_ref, k_ref, v_ref, r_ref, g_ref, w2_ref, gb_ref, ng_ref, s0_ref,
               o_ref, s_out, s_s, *, c, nb):
    j = pl.program_id(1)
    dk_all = N_HEADS * GLA_DK
    hs = N_HEADS * c
    nsb = c // GLA_SUB
    seqs = range(nb)

    @pl.when(j == 0)
    def _():
        s_s[...] = jnp.zeros_like(s_s)
        for i in seqs:
            for h in range(N_HEADS):
                s_s[i, h * GLA_DK:(h + 1) * GLA_DK, _hs(h)] = s0_ref[i, h]

    yield 'stage'

    incl_b = _tri(c).astype(BF16)
    er = lax.broadcasted_iota(jnp.int32, (2 * c, c), 0)
    ec = lax.broadcasted_iota(jnp.int32, (2 * c, c), 1)
    et = er & (c - 1)
    edge_sel = (ec == jnp.where(er < c, (et & ~(GLA_SUB - 1)) - 1, et | (GLA_SUB - 1))).astype(BF16)
    t_idx = lax.broadcasted_iota(jnp.int32, (c, hs), 0)
    s_idx = lax.broadcasted_iota(jnp.int32, (c, hs), 1) & (c - 1)
    sub_shift = GLA_SUB.bit_length() - 1
    blk_dist = (t_idx >> sub_shift) - (s_idx >> sub_shift)
    diag_off = t_idx - s_idx
    same_hk = _same_head(hs, dk_all, c, GLA_DK)
    same_hv = _same_head(hs, W_GROUP, c, HEAD_V)
    same_kv = _same_head(dk_all, W_GROUP, GLA_DK, HEAD_V)
    head_rep = _same_head(dk_all, hs, GLA_DK, c).astype(BF16)
    group_sum = _same_head(W_GROUP, W_GROUP, HEAD_V, HEAD_V).astype(BF16)
    row8 = lax.broadcasted_iota(jnp.int32, (c, dk_all), 0) & (GLA_SUB - 1)
    last_row = lax.broadcasted_iota(jnp.int32, (c, dk_all), 0) == c - 1

    log_as = [_log_sigmoid(_dot(g_ref[i], w2_ref[...]) + gb_ref[...]) / GLA_TAU for i in seqs]
    bs = [_dot01_left(incl_b, la) for la in log_as]
    edges = [_dot01_left(edge_sel, b) for b in bs]
    yield 'stage'
    e_prev = [e[:c] for e in edges]
    e_own = [e[c:] for e in edges]
    qs = [q_ref[i] * (GLA_DK ** -0.5) for i in seqs]
    ks = [k_ref[i] for i in seqs]
    vs = [v_ref[i] for i in seqs]
    q_hat = [qs[i] * jnp.exp(bs[i] - e_prev[i]) for i in seqs]
    k_hat = [ks[i] * jnp.exp(e_own[i] - bs[i]) for i in seqs]
    yield 'stage'

    def q_for_distance(i, m):
        if m == 0:
            return q_hat[i]
        rows = GLA_SUB * m
        shifted = jnp.concatenate([jnp.zeros((rows, dk_all), F32), e_prev[i][:c - rows]], axis=0)
        return q_hat[i] * jnp.exp(e_prev[i] - shifted)

    a_parts = []
    for i in seqs:
        lhs = jnp.concatenate([q_for_distance(i, m) for m in range(nsb - 1)], axis=0).astype(BF16)
        prod = lax.dot_general(lhs, _head_blocks(k_hat[i].astype(BF16), same_hk),
                               (((1,), (1,)), ((), ())), preferred_element_type=F32)
        a_parts.append(sum(jnp.where(blk_dist == m + 1, prod[m * c:(m + 1) * c], 0.0)
                           for m in range(nsb - 1)))
        yield 'stage'

    def rot8(x, d):
        return x if d == 0 else pltpu.roll(x, d, 0)

    for i in seqs:
        ws = [jnp.where(row8 >= d, qs[i] * rot8(ks[i], d) * jnp.exp(bs[i] - rot8(bs[i], d)), 0.0)
              for d in range(GLA_SUB)]
        prod = jnp.dot(jnp.concatenate(ws, axis=0).astype(BF16), head_rep, preferred_element_type=F32)
        a_parts[i] = a_parts[i] + sum(jnp.where(diag_off == d, prod[d * c:(d + 1) * c], 0.0)
                                      for d in range(GLA_SUB))
        yield 'stage'
    states = [s_s[i] for i in seqs]
    outs = [jnp.dot(a_parts[i].astype(BF16), _head_blocks(vs[i].astype(BF16), same_hv),
                    preferred_element_type=F32)
            + _dot(qs[i] * jnp.exp(bs[i]), states[i]) for i in seqs]
    yield 'stage'
    b_last = [b[c - 1:c, :] for b in bs]
    upds = [jnp.where(same_kv, _dot_tn(ks[i] * jnp.exp(b_last[i] - bs[i]), vs[i]), 0.0) for i in seqs]
    ones_cv = jnp.ones((c, W_GROUP), BF16)
    decay_rows = [sum(lax.dot_general(part, ones_cv, (((0,), (0,)), ((), ())), preferred_element_type=F32)
                      for part in _split3(jnp.where(last_row, b, 0.0))) for b in bs]
    yield 'stage'
    sum_sq = _group_sums([o * o for o in outs], group_sum)
    ng = ng_ref[...]
    for i in seqs:
        s_s[i] = states[i] * jnp.exp(decay_rows[i]) + upds[i]
        o_ref[i] = outs[i] * lax.rsqrt(sum_sq[i] * (1.0 / HEAD_V) + EPS) * ng * _silu(r_ref[i])
    yield 'final'

    @pl.when(j == pl.num_programs(1) - 1)
    def _():
        for i in seqs:
            for h in range(N_HEADS):
                s_out[i, h] = s_s[i, h * GLA_DK:(h + 1) * GLA_DK, _hs(h)]


def _interleave(steps):
    active, parked = list(steps), []
    while active:
        for g in list(active):
            if next(g) == 'final':
                active.remove(g)
                parked.append(g)
    for g in parked:
        for _ in g:
            pass


def _mixers_kernel(mq, mk, mv, mo, gq, gk, gv, gz, lq, lk, lv, lr, g_ref, bias_ref, alog_ref,
                   ml_ng, gdn_ng, w2_ref, gb_ref, gla_ng, c0, n0, m0, gs0, ls0,
                   ml_h, c_out, n_out, m_out, gdn_o, gs_out, gla_o, ls_out,
                   cn_s, m_s, gs_s, ls_s, *, c, nb):
    _interleave([
        _mlstm_steps(mq, mk, mv, mo, g_ref, bias_ref, ml_ng, c0, n0, m0, ml_h, c_out, n_out, m_out,
                     cn_s, m_s, c=c, nb=nb),
        _gdn_steps(gq, gk, gv, gz, g_ref, bias_ref, alog_ref, gdn_ng, gs0, gdn_o, gs_out, gs_s,
                   c=c, nb=nb),
        _gla_steps(lq, lk, lv, lr, g_ref, w2_ref, gb_ref, gla_ng, ls0, gla_o, ls_out, ls_s,
                   c=c, nb=nb)])


def _mixers_call(p, gate_bias, alog_row, ml_ng, gdn_ng, w2_pad, gla_gb, gla_ng,
                 ml_c, ml_n, ml_m, gdn_s, gla_s, c, nb):
    b, l, _ = p.shape
    nb = math.gcd(nb, b)
    nc = l // c
    dk_all = N_HEADS * GLA_DK
    blk = lambda cb: pl.BlockSpec((nb, c, W_GROUP), lambda i, j, cb=cb: (i, j, cb))
    blk128 = lambda cb: pl.BlockSpec((nb, c, LANES), lambda i, j, cb=cb: (i, j, cb))
    row = lambda w: pl.BlockSpec((1, w), lambda i, j: (0, 0))
    full = lambda a: pl.BlockSpec((nb,) + a.shape[1:], lambda i, j: (i,) + (0,) * (a.ndim - 1))
    tok = pl.BlockSpec((nb, c, W_GROUP), lambda i, j: (i, j, 0))
    ml_n = ml_n[..., None]
    ml_m = ml_m.reshape(b, 1, N_HEADS)
    states = (ml_c, ml_n, ml_m, gdn_s, gla_s)
    sds = lambda a: jax.ShapeDtypeStruct(a.shape, F32)
    act = jax.ShapeDtypeStruct((b, l, W_GROUP), F32)
    o_ml, ml_c, ml_n, ml_m, o_gdn, gdn_s, o_gla, gla_s = pl.pallas_call(
        functools.partial(_mixers_kernel, c=c, nb=nb),
        grid=(b // nb, nc),
        in_specs=[blk(PB_MQ), blk(PB_MK), blk(PB_MV), blk(PB_MO),
                  blk(PB_GQ), blk(PB_GK), blk(PB_GV), blk(PB_GZ),
                  blk128(PB128_LQ), blk128(PB128_LK), blk(PB_LV), blk(PB_LR),
                  blk128(PB128_GATES), row(LANES), row(LANES), row(W_GROUP), row(W_GROUP),
                  pl.BlockSpec((LANES, dk_all), lambda i, j: (0, 0)), row(dk_all), row(W_GROUP)]
                 + [full(a) for a in states],
        out_specs=[tok, full(ml_c), full(ml_n), full(ml_m), tok, full(gdn_s), tok, full(gla_s)],
        out_shape=[act, sds(ml_c), sds(ml_n), sds(ml_m), act, sds(gdn_s), act, sds(gla_s)],
        scratch_shapes=[pltpu.VMEM((nb, W_GROUP, 2 * W_GROUP), F32), pltpu.VMEM((nb, 1, LANES), F32),
                        pltpu.VMEM((nb, W_GROUP, W_GROUP), F32), pltpu.VMEM((nb, dk_all, W_GROUP), F32)],
        compiler_params=pltpu.CompilerParams(dimension_semantics=("parallel", "arbitrary"),
                                             vmem_limit_bytes=VMEM_LIMIT),
        name="mixers",
    )(*([p] * 13), gate_bias, alog_row, ml_ng, gdn_ng, w2_pad, gla_gb, gla_ng, *states)
    return o_ml, ml_c, ml_n[..., 0], ml_m[:, 0, :], o_gdn, gdn_s, o_gla, gla_s


def _outffn_kernel(x_ref, a_ref, b_ref, c_ref, d_ref, mod_ref, g2_ref, gf_ref, wo_ref, w1_ref, w2_ref,
                   o_ref, *, ff_chunk, final):
    ns, tm, d = x_ref.shape
    flat = lambda y: y.reshape(ns * tm, y.shape[-1])
    per_seq = lambda y: y.reshape(ns, tm, y.shape[-1])
    mod = mod_ref[...]
    mixed = flat(jnp.concatenate([a_ref[...], b_ref[...], c_ref[...], d_ref[...]], axis=-1)).astype(BF16)
    x = x_ref[...] + mod[:, 2:3, :] * per_seq(jnp.dot(mixed, wo_ref[...], preferred_element_type=F32))
    h = flat(_modulated_norm(x, g2_ref[...], mod[:, 4:5, :], mod[:, 3:4, :])).astype(BF16)
    d_ff = w1_ref.shape[1]
    acc = jnp.zeros((ns * tm, d), F32)
    for f0 in range(0, d_ff, ff_chunk):
        a = jnp.maximum(jnp.dot(h, w1_ref[:, f0:f0 + ff_chunk], preferred_element_type=F32), 0.0)
        acc = acc + jnp.dot((a * a).astype(BF16), w2_ref[f0:f0 + ff_chunk, :],
                            preferred_element_type=F32)
    x = x + mod[:, 5:6, :] * per_seq(acc)
    if final:
        x = x * lax.rsqrt(jnp.mean(x * x, axis=-1, keepdims=True) + EPS) * gf_ref[...]
    o_ref[...] = x


def _outffn_call(x, mixers, mod, g2, gf, wo, w1, w2, tm, ns, final):
    b, l, d = x.shape
    d_ff = w1.shape[1]
    tok = lambda w: pl.BlockSpec((ns, tm, w), lambda i, j: (i, j, 0))
    const = lambda shape: pl.BlockSpec(shape, lambda i, j: (0,) * len(shape),
                                       pipeline_mode=pl.Buffered(1))
    return pl.pallas_call(
        functools.partial(_outffn_kernel, ff_chunk=1024, final=final),
        grid=(b // ns, l // tm),
        in_specs=[tok(d), tok(W_GROUP), tok(W_GROUP), tok(W_GROUP), tok(W_GROUP),
                  pl.BlockSpec((ns, 6, d), lambda i, j: (i, 0, 0)),
                  pl.BlockSpec((1, d), lambda i, j: (0, 0)),
                  pl.BlockSpec((1, d), lambda i, j: (0, 0)),
                  const((d, d)), const((d, d_ff)), const((d_ff, d))],
        out_specs=tok(d),
        out_shape=jax.ShapeDtypeStruct((b, l, d), F32),
        compiler_params=pltpu.CompilerParams(dimension_semantics=("parallel", "parallel"),
                                             vmem_limit_bytes=VMEM_LIMIT),
        name="outproj_ffn",
    )(x, *mixers, mod, g2, gf, wo, w1, w2)


def _permute_w_in(w_in):
    d = w_in.shape[0]
    sizes = (256, 256, 256, 256, 256, 256, 4, 4, 256, 768, 4, 4, 256, 128, 128, 256, 16, 256)
    offs = [0]
    for s in sizes:
        offs.append(offs[-1] + s)
    (sq, sk, sv, mq, mk, mv, mi, mf, mo, gqkv, gb, ga, gz, lq, lk, lv, lg, lr) = [
        w_in[:, offs[i]:offs[i + 1]] for i in range(len(sizes))]
    pad = jnp.zeros((d, LANES - (4 * N_HEADS + GLA_RANK)), w_in.dtype)
    return jnp.concatenate([sk, sv, sq, mq, mk, mv, mo, gqkv, gz, lv, lr, lq, lk,
                            mi, mf, gb, ga, lg, pad], axis=1)


def _row128(pieces):
    row = jnp.zeros((LANES,), F32)
    for off, vec in pieces:
        row = row.at[off:off + vec.shape[0]].set(vec)
    return row.reshape(1, LANES)


DENSE_TILE_ROWS = 512


class _Tiles(NamedTuple):
    tm: int
    dense_seqs: int
    sb_tq: int
    sb_tk: int
    chunk: int


def _tiles(b, l):
    tm = min(DENSE_TILE_ROWS, l)
    return _Tiles(tm=tm, dense_seqs=math.gcd(b, DENSE_TILE_ROWS // tm), sb_tq=min(256, l),
                  sb_tk=max(min(256, l), LANES), chunk=min(64, l))


def _layer(x, mod, lw, states, tiles, layer, depth, kv_prev, final_g):
    tm, dense_seqs, sb_tq, sb_tk, chunk = tiles
    (n1, n2, w_in_p, gate_bias, ml_ng, conv_w, alog_row, gdn_ng, w2_pad, gla_gb, gla_ng,
     w_out, w_ff1, w_ff2) = lw
    (sb_k_past, sb_v_past, ml_c, ml_n, ml_m, gdn_s, gdn_buf, gla_s) = states
    b, l, _ = x.shape
    final = layer == depth - 1
    k_buf, v_buf, p, gdn_buf = _inproj_call(x, mod, n1, w_in_p, conv_w, gdn_buf, tm, dense_seqs,
                                            layer, depth, kv_prev)
    if sb_k_past is None:
        o_sb = _sb_call(p, PB_SQ, k_buf, v_buf, l, sb_tq, sb_tk, 0, kv_layer=layer)
    else:
        past = sb_k_past.shape[1]
        lk = -(-(past + l) // sb_tk) * sb_tk
        padz = jnp.zeros((b, lk - past - l, W_GROUP), F32)
        k_all = jnp.concatenate([sb_k_past.reshape(b, past, W_GROUP), k_buf[layer], padz], axis=1)
        v_all = jnp.concatenate([sb_v_past.reshape(b, past, W_GROUP), v_buf[layer], padz], axis=1)
        o_sb = _sb_call(p, PB_SQ, k_all, v_all, l, sb_tq, sb_tk, past)
    o_ml, ml_c, ml_n, ml_m, o_gdn, gdn_s, o_gla, gla_s = _mixers_call(
        p, gate_bias, alog_row, ml_ng, gdn_ng, w2_pad, gla_gb, gla_ng, ml_c, ml_n, ml_m, gdn_s, gla_s,
        chunk, MIXER_SEQS_PER_STEP)
    x = _outffn_call(x, (o_sb, o_ml, o_gdn, o_gla), mod, n2, final_g, w_out, w_ff1, w_ff2, tm,
                     dense_seqs, final)
    return x, (k_buf, v_buf), (ml_c, ml_n, ml_m, gdn_s, gdn_buf, gla_s)


def kernel(x_prompt, x_sample, cache_sb_k, cache_sb_v, state_mlstm_C, state_mlstm_n, state_mlstm_m, state_gdn_S, state_gdn_conv, state_gla_S, c_prompt, c_sample, norm1_g, norm2_g, w_ada, b_ada, w_in, mlstm_i_bias, mlstm_f_bias, mlstm_norm_g, gdn_conv_w, gdn_a_log, gdn_dt_bias, gdn_norm_g, gla_w_gate2, gla_gate_bias, gla_norm_g, w_out, w_ff1, w_ff2, final_g):
    depth = w_in.shape[0]
    bp, lp, d = x_prompt.shape
    bs, ls, _ = x_sample.shape
    dk_all = N_HEADS * GLA_DK
    mod_all = _ada_call(jnp.concatenate([c_prompt, c_sample], axis=0), w_ada, b_ada)
    mod_all = mod_all.reshape(depth, bp + bs, 6, d)
    final_row = final_g.reshape(1, d)
    xp, xs = x_prompt, x_sample
    p_list, s_list = [], []
    kv_p = kv_s = None
    for l in range(depth):
        gate_bias = _row128([(G_MI, mlstm_i_bias[l]), (G_MF, mlstm_f_bias[l]), (G_GA, gdn_dt_bias[l])])
        alog_row = _row128([(G_GA, gdn_a_log[l])])
        w2_pad = jnp.zeros((LANES, dk_all), F32).at[G_LG:G_LG + GLA_RANK, :].set(gla_w_gate2[l])
        lw = (norm1_g[l].reshape(1, d), norm2_g[l].reshape(1, d), _permute_w_in(w_in[l]).astype(BF16),
              gate_bias, mlstm_norm_g[l].reshape(1, W_GROUP), gdn_conv_w[l], alog_row,
              gdn_norm_g[l].reshape(1, W_GROUP), w2_pad.astype(BF16),
              gla_gate_bias[l].reshape(1, dk_all), gla_norm_g[l].reshape(1, W_GROUP),
              w_out[l].astype(BF16), w_ff1[l].astype(BF16), w_ff2[l].astype(BF16))
        fresh = (None, None, jnp.zeros((bp,) + state_mlstm_C.shape[2:], F32),
                 jnp.zeros((bp,) + state_mlstm_n.shape[2:], F32),
                 jnp.zeros((bp,) + state_mlstm_m.shape[2:], F32),
                 jnp.zeros((bp,) + state_gdn_S.shape[2:], F32),
                 jnp.zeros((bp,) + state_gdn_conv.shape[2:], F32),
                 jnp.zeros((bp,) + state_gla_S.shape[2:], F32))
        past = (cache_sb_k[l], cache_sb_v[l], state_mlstm_C[l], state_mlstm_n[l], state_mlstm_m[l],
                state_gdn_S[l], state_gdn_conv[l], state_gla_S[l])
        xp, kv_p, st_p = _layer(xp, mod_all[l, :bp], lw, fresh, _tiles(bp, lp), l, depth, kv_p, final_row)
        xs, kv_s, st_s = _layer(xs, mod_all[l, bp:], lw, past, _tiles(bs, ls), l, depth, kv_s, final_row)
        p_list.append(st_p)
        s_list.append(st_s)
    per_head = lambda a: a.reshape(a.shape[:-1] + (N_HEADS, HEAD_V))
    stacked_p = [jnp.stack([st[i] for st in p_list]) for i in range(6)]
    stacked_s = [jnp.stack([st[i] for st in s_list]) for i in range(6)]
    return (xp, xs, per_head(kv_p[0]), per_head(kv_p[1]), *stacked_p,
            per_head(kv_s[0]), per_head(kv_s[1]), *stacked_s)
```
